```python
import jax, jax.numpy as jnp
from jax import lax
import numpy as np

D_MODEL = 1024
BATCH = 8
SEQ = 4096
DEPTH = 1

CHUNK = 64
Q_BLOCK = 128
POOL_WINDOWS = (2, 4, 8, 16)
POOL_GROUPS = len(POOL_WINDOWS)
POOL_WIDTH = D_MODEL
POOL_GROUP_W = POOL_WIDTH // POOL_GROUPS
MLA_HEADS = 8
QK_NOPE = 128
QK_ROPE = 64
V_HEAD = 128
Q_LORA = 3 * D_MODEL // 4
KV_LORA = D_MODEL // 4
ROPE_THETA = 10000.0
N_BRANCHES = 2
IN_WIDTH = POOL_WIDTH + Q_LORA + KV_LORA + QK_ROPE + N_BRANCHES * D_MODEL
N_EXPERTS = 32
TOP_K = 4
D_EXPERT = D_MODEL
SWIGLU_LIMIT = 7.0
SWIGLU_ALPHA = 1.702
EXPERT_BLOCK = 128
N_MOD = 6
NORM_EPS = 1e-6
NEG_INF = -1e30

kernel_name = 'hybrid_pool_mla_moe_adaln_block'


def rms_norm(x, g):
    xf = x.astype(jnp.float32)
    y = xf * lax.rsqrt(jnp.mean(xf * xf, axis=-1, keepdims=True) + NORM_EPS)
    return (y * g.astype(jnp.float32)).astype(x.dtype)


def modulate(h, shift, scale):
    return h * (1 + scale[:, None, :]) + shift[:, None, :]


def pool_mixer(u, w_grp, scale, w_proj):
    B, S, _ = u.shape
    uf = u.astype(jnp.float32).reshape(B, S, POOL_GROUPS, POOL_GROUP_W)
    csum = jnp.concatenate([jnp.zeros_like(uf[:, :1]), jnp.cumsum(uf, axis=1)], axis=1)
    t = np.arange(S)
    pooled = []
    for g, w in enumerate(POOL_WINDOWS):
        lo = np.maximum(t + 1 - w, 0)
        cnt = np.minimum(t + 1, w).astype(np.float32)
        win = csum[:, 1:, g] - csum[:, lo, g]
        pooled.append(win / cnt[None, :, None])
    pooled = jnp.stack(pooled, axis=2)
    mixed = (pooled - uf).astype(u.dtype)
    y = jnp.einsum('bsgc,gcd->bsgd', mixed, w_grp).reshape(B, S, POOL_WIDTH) * scale
    return y @ w_proj


def rotate(x, cos, sin):
    x1, x2 = jnp.split(x, 2, axis=-1)
    return jnp.concatenate([x1 * cos - x2 * sin, x2 * cos + x1 * sin], axis=-1)


def mla_mixer(q_lat, kv_lat, k_pe_raw, positions, g_q_a, w_q_b, g_kv_a, w_kv_b, w_o):
    B, S, _ = q_lat.shape
    dt = q_lat.dtype
    q = (rms_norm(q_lat, g_q_a) @ w_q_b).reshape(B, S, MLA_HEADS, QK_NOPE + QK_ROPE)
    q_nope, q_pe = q[..., :QK_NOPE], q[..., QK_NOPE:]
    kv = (rms_norm(kv_lat, g_kv_a) @ w_kv_b).reshape(B, S, MLA_HEADS, QK_NOPE + V_HEAD)
    k_nope, v = kv[..., :QK_NOPE], kv[..., QK_NOPE:]
    inv_freq = 1.0 / (ROPE_THETA ** (jnp.arange(0, QK_ROPE, 2, dtype=jnp.float32) / QK_ROPE))
    ang = positions.astype(jnp.float32)[..., None] * inv_freq
    cos, sin = jnp.cos(ang), jnp.sin(ang)
    q_pe = rotate(q_pe.astype(jnp.float32), cos[:, :, None], sin[:, :, None]).astype(dt)
    k_pe = rotate(k_pe_raw.astype(jnp.float32), cos, sin).astype(dt)
    sm_scale = (QK_NOPE + QK_ROPE) ** -0.5
    chunk_id = np.arange(S) // CHUNK
    outs = []
    for qb in range(S // Q_BLOCK):
        q0, q1 = qb * Q_BLOCK, (qb + 1) * Q_BLOCK
        s = (jnp.einsum('bqhd,bkhd->bhqk', q_nope[:, q0:q1], k_nope[:, :q1])
             + jnp.einsum('bqhr,bkr->bhqk', q_pe[:, q0:q1], k_pe[:, :q1])).astype(jnp.float32) * sm_scale
        mask = chunk_id[q0:q1, None] >= chunk_id[None, :q1]
        s = jnp.where(mask, s, NEG_INF)
        p = jax.nn.softmax(s, axis=-1).astype(dt)
        outs.append(jnp.einsum('bhqk,bkhd->bqhd', p, v[:, :q1]))
    o = jnp.concatenate(outs, axis=1).reshape(B, S, MLA_HEADS * V_HEAD)
    return o @ w_o


def clamped_swiglu(gu):
    gate, up = gu[..., 0::2], gu[..., 1::2]
    gate = jnp.minimum(gate, SWIGLU_LIMIT)
    up = jnp.clip(up, -SWIGLU_LIMIT, SWIGLU_LIMIT)
    return (up + 1) * (gate * jax.nn.sigmoid(SWIGLU_ALPHA * gate))


def moe_ffn(h, w_router, b_router, w_gu, b_gu, w_down, b_down):
    B, S, D = h.shape
    T = B * S
    ht = h.reshape(T, D)
    logits = (ht @ w_router + b_router).astype(jnp.float32)
    top_val, top_idx = lax.top_k(logits, TOP_K)
    weights = jax.nn.softmax(top_val, axis=-1)
    flat_e = top_idx.reshape(-1)
    n_slots = T * TOP_K
    order = jnp.argsort(flat_e, stable=True)
    sorted_e = flat_e[order]
    counts = jnp.bincount(flat_e, length=N_EXPERTS)
    padded = (counts + EXPERT_BLOCK - 1) // EXPERT_BLOCK * EXPERT_BLOCK
    pad_end = jnp.cumsum(padded)
    pad_start = pad_end - padded
    grp_start = jnp.cumsum(counts) - counts
    dest = pad_start[sorted_e] + jnp.arange(n_slots, dtype=jnp.int32) - grp_start[sorted_e]
    n_rows = n_slots + N_EXPERTS * EXPERT_BLOCK
    n_blocks = n_rows // EXPERT_BLOCK
    row_tok = jnp.zeros((n_rows,), jnp.int32).at[dest].set((order // TOP_K).astype(jnp.int32))
    xs = ht[row_tok].reshape(n_blocks, EXPERT_BLOCK, D)
    blk_start = jnp.arange(n_blocks, dtype=jnp.int32) * EXPERT_BLOCK
    blk_e = jnp.minimum(jnp.searchsorted(pad_end, blk_start, side='right'), N_EXPERTS - 1)

    def expert_block(args):
        xb, e = args
        gu = xb @ w_gu[e] + b_gu[e]
        return clamped_swiglu(gu) @ w_down[e] + b_down[e]

    ys = lax.map(expert_block, (xs, blk_e)).reshape(n_rows, D)
    slot_row = jnp.zeros((n_slots,), jnp.int32).at[order].set(dest.astype(jnp.int32))
    y_slots = ys[slot_row].reshape(T, TOP_K, D)
    out = jnp.einsum('tk,tkd->td', weights.astype(ys.dtype), y_slots)
    return out.reshape(B, S, D)


def setup_inputs(seed: int = 0) -> dict:
    key = jax.random.key(seed)
    ks = jax.random.split(key, 32)
    f32 = jnp.float32
    L, D = DEPTH, D_MODEL

    def nrm(k, shape, fan_in, mult=1.0):
        return jax.random.normal(k, shape, f32) * (mult * fan_in ** -0.5)

    def gain(k, shape):
        return 1.0 + 0.1 * jax.random.normal(k, shape, f32)

    def bias(k, shape, s=0.01):
        return s * jax.random.normal(k, shape, f32)

    x = jax.random.normal(ks[0], (BATCH, SEQ, D), f32)
    c = jax.random.normal(ks[1], (BATCH, D), f32)
    offsets = jax.random.randint(ks[2], (BATCH, 1), 0, 8192, dtype=jnp.int32)
    positions = offsets + jnp.arange(SEQ, dtype=jnp.int32)[None, :]
    return {
        'x': x,
        'c': c,
        'positions': positions,
        'w_mod': nrm(ks[3], (L, D, N_MOD * D), D, 0.5),
        'b_mod': bias(ks[4], (L, N_MOD * D), 0.05),
        'g_mix': gain(ks[5], (L, D)),
        'w_in': nrm(ks[6], (L, D, IN_WIDTH), D),
        'b_gate': bias(ks[7], (L, N_BRANCHES * D), 0.1),
        'w_pool_grp': nrm(ks[8], (L, POOL_GROUPS, POOL_GROUP_W, POOL_GROUP_W), POOL_GROUP_W),
        'pool_scale': gain(ks[9], (L, POOL_WIDTH)),
        'w_pool_out': nrm(ks[10], (L, POOL_WIDTH, D), POOL_WIDTH),
        'g_q_a': gain(ks[11], (L, Q_LORA)),
        'w_q_b': nrm(ks[12], (L, Q_LORA, MLA_HEADS * (QK_NOPE + QK_ROPE)), Q_LORA),
        'g_kv_a': gain(ks[13], (L, KV_LORA)),
        'w_kv_b': nrm(ks[14], (L, KV_LORA, MLA_HEADS * (QK_NOPE + V_HEAD)), KV_LORA),
        'w_mla_out': nrm(ks[15], (L, MLA_HEADS * V_HEAD, D), MLA_HEADS * V_HEAD),
        'w_out': nrm(ks[16], (L, D, D), D),
        'g_ffn': gain(ks[17], (L, D)),
        'w_router': nrm(ks[18], (L, D, N_EXPERTS), D),
        'b_router': bias(ks[19], (L, N_EXPERTS)),
        'w_gu': nrm(ks[20], (L, N_EXPERTS, D, 2 * D_EXPERT), D),
        'b_gu': bias(ks[21], (L, N_EXPERTS, 2 * D_EXPERT)),
        'w_down': nrm(ks[22], (L, N_EXPERTS, D_EXPERT, D), D_EXPERT),
        'b_down': bias(ks[23], (L, N_EXPERTS, D)),
        'g_final': gain(ks[24], (D,)),
        'w_fmod': nrm(ks[25], (D, 2 * D), D, 0.5),
        'b_fmod': bias(ks[26], (2 * D,), 0.05),
    }


def reference(x, c, positions, w_mod, b_mod, g_mix, w_in, b_gate, w_pool_grp, pool_scale,
              w_pool_out, g_q_a, w_q_b, g_kv_a, w_kv_b, w_mla_out, w_out, g_ffn,
              w_router, b_router, w_gu, b_gu, w_down, b_down, g_final, w_fmod, b_fmod):
    B, S, D = x.shape
    c_act = jax.nn.silu(c)
    splits = np.cumsum([POOL_WIDTH, Q_LORA, KV_LORA, QK_ROPE]).tolist()
    for l in range(DEPTH):
        mod = (c_act @ w_mod[l] + b_mod[l]).reshape(B, N_MOD, D)
        shift1, scale1, gate1 = mod[:, 0], mod[:, 1], mod[:, 2]
        shift2, scale2, gate2 = mod[:, 3], mod[:, 4], mod[:, 5]

        h = modulate(rms_norm(x, g_mix[l]), shift1, scale1)
        z = h @ w_in[l]
        u, q_lat, kv_lat, k_pe, gate_logits = jnp.split(z, splits, axis=-1)
        a = pool_mixer(u, w_pool_grp[l], pool_scale[l], w_pool_out[l])
        m = mla_mixer(q_lat, kv_lat, k_pe, positions, g_q_a[l], w_q_b[l], g_kv_a[l], w_kv_b[l], w_mla_out[l])
        g = jax.nn.sigmoid((gate_logits + b_gate[l]).astype(jnp.float32)).astype(x.dtype)
        g = g.reshape(B, S, N_BRANCHES, D)
        merged = g[:, :, 0] * a + g[:, :, 1] * m
        x = x + gate1[:, None, :] * (merged @ w_out[l])

        h2 = modulate(rms_norm(x, g_ffn[l]), shift2, scale2)
        x = x + gate2[:, None, :] * moe_ffn(h2, w_router[l], b_router[l], w_gu[l], b_gu[l], w_down[l], b_down[l])

    fmod = c_act @ w_fmod + b_fmod
    fshift, fscale = fmod[:, :D], fmod[:, D:]
    return modulate(rms_norm(x, g_final), fshift, fscale)
```

```python
import functools

import jax
import jax.numpy as jnp
import numpy as np
from jax import lax
from jax.experimental import pallas as pl
from jax.experimental.pallas import tpu as pltpu

F32 = jnp.float32
BF16 = jnp.bfloat16

CHUNK = 64
POOL_WINDOWS = (2, 4, 8, 16)
POOL_HALO = 16
N_HEADS = 8
QK_NOPE = 128
QK_ROPE = 64
V_HEAD = 128
HEAD_PAD = 256
ROPE_THETA = 10000.0
N_EXPERTS = 32
TOP_K = 4
SWIGLU_LIMIT = 7.0
SWIGLU_ALPHA = 1.702
NORM_EPS = 1e-6
NEG_INF = -1e30

VMEM_LIMIT = 56 * 1024 * 1024

TM_IN = 256
TQ = 512
TK = 512
TM_POST = 256
MOE_BLK = 256
TM_FIN = 256


def _const_spec(shape):
    nd = len(shape)
    return pl.BlockSpec(shape, lambda *_: (0,) * nd, pipeline_mode=pl.Buffered(1))


def _rms(xf, g):
    return xf * lax.rsqrt(jnp.mean(xf * xf, axis=-1, keepdims=True) + NORM_EPS) * g


def _pack_bf16_pair(lo, hi):
    lo_b = lax.bitcast_convert_type(lo.astype(BF16).astype(F32), jnp.uint32)
    hi_b = lax.bitcast_convert_type(hi.astype(BF16).astype(F32), jnp.uint32)
    return (hi_b & jnp.uint32(0xFFFF0000)) | (lo_b >> 16)


def _unpack_bf16_pair(p):
    lo = lax.bitcast_convert_type(p << 16, F32)
    hi = lax.bitcast_convert_type(p & jnp.uint32(0xFFFF0000), F32)
    return lo, hi


def _mod_kernel(c_ref, w_ref, b_ref, o_ref):
    c = c_ref[...]
    c_act = c * jax.nn.sigmoid(c)
    o_ref[...] = jnp.dot(c_act, w_ref[...], preferred_element_type=F32,
                         precision=lax.Precision.HIGHEST) + b_ref[...]


def _mod_call(c, w, b, tn=1024):
    bsz, d = c.shape
    n = w.shape[1]
    return pl.pallas_call(
        _mod_kernel,
        grid=(n // tn,),
        in_specs=[pl.BlockSpec((bsz, d), lambda j: (0, 0)),
                  pl.BlockSpec((d, tn), lambda j: (0, j)),
                  pl.BlockSpec((1, tn), lambda j: (0, j))],
        out_specs=pl.BlockSpec((bsz, tn), lambda j: (0, j)),
        out_shape=jax.ShapeDtypeStruct((bsz, n), F32),
        compiler_params=pltpu.CompilerParams(dimension_semantics=("arbitrary",),
                                             vmem_limit_bytes=VMEM_LIMIT),
        name="adaln_mod",
    )(c, w, b.reshape(1, n))


def _mixer_in_kernel(x_ref, xh_ref, pos_ref, mod_ref, gmix_ref, win_ref, bgate_ref, wgrp_ref,
                     pscale_ref, wpo_ref, gq_ref, wq_ref, gkv_ref, wkn_ref, wvt_ref, invf_ref,
                     ga_ref, g1_ref, q_ref, k_ref, vt_ref, u_scr, *, tm, tiles_per_seq, d, q_lora,
                     kv_lora):
    i = pl.program_id(0)
    t_in_seq = i % tiles_per_seq
    is_start = t_in_seq == 0
    shift1 = mod_ref[0, 0:1, :]
    scale1 = mod_ref[0, 1:2, :]
    gmix = gmix_ref[...]

    def prenorm(xf):
        return (_rms(xf, gmix) * (1.0 + scale1) + shift1).astype(BF16)

    h = prenorm(x_ref[...])
    hh = prenorm(xh_ref[...])
    u = jnp.dot(h, win_ref[:, 0:d], preferred_element_type=F32)
    uh = jnp.dot(hh, win_ref[:, 0:d], preferred_element_type=F32)
    u_scr[0:POOL_HALO, :] = jnp.where(is_start, 0.0, uh)
    u_scr[POOL_HALO:POOL_HALO + tm, :] = u
    rest = jnp.dot(h, win_ref[:, d:], preferred_element_type=F32)

    gw = d // len(POOL_WINDOWS)
    tseq = t_in_seq * tm + lax.broadcasted_iota(jnp.int32, (tm, 1), 0)
    ys = []
    for g, w in enumerate(POOL_WINDOWS):
        c0 = g * gw
        ug = u_scr[POOL_HALO:POOL_HALO + tm, c0:c0 + gw]
        acc = ug
        for j in range(1, w):
            acc = acc + u_scr[POOL_HALO - j:POOL_HALO - j + tm, c0:c0 + gw]
        cnt = jnp.minimum(tseq + 1, w).astype(F32)
        mixed = (acc / cnt - ug).astype(BF16)
        ys.append(jnp.dot(mixed, wgrp_ref[g], preferred_element_type=F32))
    y = (jnp.concatenate(ys, axis=1) * pscale_ref[...]).astype(BF16)
    a = jnp.dot(y, wpo_ref[...], preferred_element_type=F32)

    o_q = 0
    o_kv = q_lora
    o_kpe = q_lora + kv_lora
    o_ksw = o_kpe + 128
    o_g0 = o_ksw + 128
    o_g1 = o_g0 + d
    gates0 = jax.nn.sigmoid(rest[:, o_g0:o_g0 + d] + bgate_ref[:, 0:d])
    gates1 = jax.nn.sigmoid(rest[:, o_g1:o_g1 + d] + bgate_ref[:, d:2 * d])
    ga_ref[...] = (gates0 * a).astype(BF16)
    g1_ref[...] = gates1.astype(BF16)

    ang = pos_ref[...] * invf_ref[...]
    cos2 = jnp.cos(ang)
    sin2 = jnp.sin(ang)
    sm_scale = float(QK_NOPE + QK_ROPE) ** -0.5

    qn = _rms(rest[:, o_q:o_q + q_lora], gq_ref[...]).astype(BF16)
    qall = jnp.dot(qn, wq_ref[...], preferred_element_type=F32)
    kvn = _rms(rest[:, o_kv:o_kv + kv_lora], gkv_ref[...]).astype(BF16)
    kn = jnp.dot(kvn, wkn_ref[...], preferred_element_type=F32)
    vt = lax.dot_general(wvt_ref[...], kvn, (((1,), (1,)), ((), ())),
                         preferred_element_type=F32)
    vt_ref[0] = vt.astype(BF16)
    kpe = (rest[:, o_kpe:o_kpe + 128] * cos2 + rest[:, o_ksw:o_ksw + 128] * sin2).astype(BF16)
    hn = N_HEADS * 128
    for hd in range(N_HEADS):
        c0 = hd * 128
        qpe = qall[:, hn + c0:hn + c0 + 128] * cos2 + qall[:, 2 * hn + c0:2 * hn + c0 + 128] * sin2
        q_ref[:, hd * HEAD_PAD:hd * HEAD_PAD + 128] = (qall[:, c0:c0 + 128] * sm_scale).astype(BF16)
        q_ref[:, hd * HEAD_PAD + 128:(hd + 1) * HEAD_PAD] = (qpe * sm_scale).astype(BF16)
        k_ref[:, hd * HEAD_PAD:hd * HEAD_PAD + 128] = kn[:, c0:c0 + 128].astype(BF16)
        k_ref[:, hd * HEAD_PAD + 128:(hd + 1) * HEAD_PAD] = kpe


def _mixer_in_call(x2d, pos_col, mod, g_mix, w_in_p, b_gate, w_grp, pool_scale, w_po, g_q_a, w_q_p,
                   g_kv_a, w_kn, w_vt, invf2, *, bsz, seq):
    t, d = x2d.shape
    tm = TM_IN
    tps = seq // tm
    q_lora = g_q_a.shape[-1]
    kv_lora = g_kv_a.shape[-1]
    hp = N_HEADS * HEAD_PAD
    halo_blocks = tm // POOL_HALO
    kern = functools.partial(_mixer_in_kernel, tm=tm, tiles_per_seq=tps, d=d, q_lora=q_lora,
                             kv_lora=kv_lora)
    row = lambda i: (i, 0)
    return pl.pallas_call(
        kern,
        grid=(t // tm,),
        in_specs=[
            pl.BlockSpec((tm, d), row),
            pl.BlockSpec((POOL_HALO, d), lambda i: (jnp.maximum(i * halo_blocks - 1, 0), 0)),
            pl.BlockSpec((tm, 1), row),
            pl.BlockSpec((1, 6, d), lambda i: (i // tps, 0, 0)),
            _const_spec((1, d)),
            _const_spec(w_in_p.shape),
            _const_spec((1, 2 * d)),
            _const_spec(w_grp.shape),
            _const_spec((1, d)),
            _const_spec(w_po.shape),
            _const_spec((1, q_lora)),
            _const_spec(w_q_p.shape),
            _const_spec((1, kv_lora)),
            _const_spec(w_kn.shape),
            _const_spec(w_vt.shape),
            _const_spec((1, 128)),
        ],
        out_specs=[
            pl.BlockSpec((tm, d), row),
            pl.BlockSpec((tm, d), row),
            pl.BlockSpec((tm, hp), row),
            pl.BlockSpec((tm, hp), row),
            pl.BlockSpec((1, N_HEADS * V_HEAD, tm), lambda i: (i // tps, 0, i % tps)),
        ],
        out_shape=[
            jax.ShapeDtypeStruct((t, d), BF16),
            jax.ShapeDtypeStruct((t, d), BF16),
            jax.ShapeDtypeStruct((t, hp), BF16),
            jax.ShapeDtypeStruct((t, hp), BF16),
            jax.ShapeDtypeStruct((bsz, N_HEADS * V_HEAD, seq), BF16),
        ],
        scratch_shapes=[pltpu.VMEM((tm + POOL_HALO, d), F32)],
        compiler_params=pltpu.CompilerParams(dimension_semantics=("arbitrary",),
                                             vmem_limit_bytes=VMEM_LIMIT),
        name="mixer_in",
    )(x2d, x2d, pos_col, mod, g_mix, w_in_p, b_gate, w_grp, pool_scale, w_po, g_q_a, w_q_p, g_kv_a,
      w_kn, w_vt, invf2)


def _attn_kernel(q_ref, k_ref, vt_ref, o_ref, *, tq, tk, nq):
    def q_body(qi, carry):
        q0 = pl.multiple_of(qi * tq, tq)
        q = q_ref[0, pl.ds(q0, tq), :]

        def step(k0, m, l, acc, masked):
            k = k_ref[0, pl.ds(k0, tk), :]
            s = lax.dot_general(k, q, (((1,), (1,)), ((), ())), preferred_element_type=F32)
            if masked:
                kc = (k0 + lax.broadcasted_iota(jnp.int32, (tk, tq), 0)) // CHUNK
                qc = (q0 + lax.broadcasted_iota(jnp.int32, (tk, tq), 1)) // CHUNK
                s = jnp.where(qc >= kc, s, NEG_INF)
            m_new = jnp.maximum(m, jnp.max(s, axis=0, keepdims=True))
            p = jnp.exp(s - m_new)
            alpha = jnp.exp(m - m_new)
            l_new = alpha * l + jnp.sum(p, axis=0, keepdims=True)
            vt = vt_ref[0, :, pl.ds(k0, tk)]
            acc_new = alpha * acc + jnp.dot(vt, p.astype(BF16), preferred_element_type=F32)
            return m_new, l_new, acc_new

        def kv_body(kj, c):
            k0 = pl.multiple_of(kj * tk, tk)
            return step(k0, *c, masked=False)

        init = (jnp.full((1, tq), NEG_INF, F32), jnp.zeros((1, tq), F32),
                jnp.zeros((V_HEAD, tq), F32))
        m, l, acc = lax.fori_loop(0, qi * (tq // tk), kv_body, init)
        for dj in range(tq // tk):
            k0 = pl.multiple_of(q0 + dj * tk, tk)
            m, l, acc = step(k0, m, l, acc, masked=True)
        o = (acc / l).T
        o_ref[0, pl.ds(q0, tq), :] = o.astype(BF16)
        return carry

    lax.fori_loop(0, nq, q_body, 0)


def _attn_call(q3, k3, vt3):
    bsz, seq, _ = q3.shape
    kern = functools.partial(_attn_kernel, tq=TQ, tk=TK, nq=seq // TQ)
    return pl.pallas_call(
        kern,
        grid=(bsz, N_HEADS),
        in_specs=[pl.BlockSpec((1, seq, HEAD_PAD), lambda b, h: (b, 0, h)),
                  pl.BlockSpec((1, seq, HEAD_PAD), lambda b, h: (b, 0, h)),
                  pl.BlockSpec((1, V_HEAD, seq), lambda b, h: (b, h, 0))],
        out_specs=pl.BlockSpec((1, seq, V_HEAD), lambda b, h: (b, 0, h)),
        out_shape=jax.ShapeDtypeStruct((bsz, seq, N_HEADS * V_HEAD), BF16),
        compiler_params=pltpu.CompilerParams(dimension_semantics=("arbitrary", "arbitrary"),
                                             vmem_limit_bytes=VMEM_LIMIT),
        name="mla_attn",
    )(q3, k3, vt3)


def _post_kernel(o_ref, ga_ref, g1_ref, x_ref, mod_ref, wmo_ref, wout_ref, gffn_ref, wrt_ref, br_ref,
                 x1_ref, h2_ref, idx_ref, wgt_ref, *, d):
    m = jnp.dot(o_ref[...], wmo_ref[...], preferred_element_type=F32)
    merged = ga_ref[...].astype(F32) + g1_ref[...].astype(F32) * m
    gate1 = mod_ref[0, 2:3, :]
    x1 = x_ref[...] + gate1 * jnp.dot(merged.astype(BF16), wout_ref[...],
                                      preferred_element_type=F32)
    x1_ref[...] = x1
    shift2 = mod_ref[0, 3:4, :]
    scale2 = mod_ref[0, 4:5, :]
    h2 = _rms(x1, gffn_ref[...]) * (1.0 + scale2) + shift2
    h2_ref[...] = _pack_bf16_pair(h2[:, 0:d // 2], h2[:, d // 2:d])

    logits = lax.dot_general(wrt_ref[...], h2, (((1,), (1,)), ((), ())),
                             preferred_element_type=F32,
                             precision=lax.Precision.HIGHEST) + br_ref[...]
    ne, tm = logits.shape
    eid = lax.broadcasted_iota(jnp.int32, (ne, tm), 0)
    vals, idxs = [], []
    cur = logits
    for _ in range(TOP_K):
        mx = jnp.max(cur, axis=0, keepdims=True)
        ix = jnp.min(jnp.where(cur == mx, eid, ne), axis=0, keepdims=True)
        vals.append(mx)
        idxs.append(ix)
        cur = jnp.where(eid == ix, -jnp.inf, cur)
    es = [jnp.exp(v - vals[0]) for v in vals]
    den = es[0] + es[1] + es[2] + es[3]
    idx_ref[...] = jnp.concatenate(idxs, axis=0)
    wgt_ref[...] = jnp.concatenate([e / den for e in es], axis=0)


def _post_call(o2d, ga, g1, x2d, mod, w_mo, w_out, g_ffn, w_rt, b_r, *, seq):
    t, d = x2d.shape
    tm = TM_POST
    tps = seq // tm
    row = lambda i: (i, 0)
    kern = functools.partial(_post_kernel, d=d)
    return pl.pallas_call(
        kern,
        grid=(t // tm,),
        in_specs=[
            pl.BlockSpec((tm, d), row),
            pl.BlockSpec((tm, d), row),
            pl.BlockSpec((tm, d), row),
            pl.BlockSpec((tm, d), row),
            pl.BlockSpec((1, 6, d), lambda i: (i // tps, 0, 0)),
            _const_spec(w_mo.shape),
            _const_spec(w_out.shape),
            _const_spec((1, d)),
            _const_spec(w_rt.shape),
            _const_spec(b_r.shape),
        ],
        out_specs=[
            pl.BlockSpec((tm, d), row),
            pl.BlockSpec((tm, d // 2), row),
            pl.BlockSpec((TOP_K, tm), lambda i: (0, i)),
            pl.BlockSpec((TOP_K, tm), lambda i: (0, i)),
        ],
        out_shape=[
            jax.ShapeDtypeStruct((t, d), F32),
            jax.ShapeDtypeStruct((t, d // 2), jnp.uint32),
            jax.ShapeDtypeStruct((TOP_K, t), jnp.int32),
            jax.ShapeDtypeStruct((TOP_K, t), F32),
        ],
        compiler_params=pltpu.CompilerParams(dimension_semantics=("arbitrary",),
                                             vmem_limit_bytes=VMEM_LIMIT),
        name="post_attn_router",
    )(o2d, ga, g1, x2d, mod, w_mo, w_out, g_ffn, w_rt, b_r)


def _moe_kernel(blk_e_ref, nvalid_ref, rs_cur_ref, rs_nxt_ref, h2_hbm, wgu_ref, bgu_ref, wd_ref,
                bd_ref, ys_hbm, xs_scr, y_scr, gsem, ssem, *, blk, n_tok, f):
    del blk_e_ref
    i = pl.program_id(0)
    n = pl.num_programs(0)
    slot = i % 2

    def gather_copy(tok, j, slot_):
        return pltpu.make_async_copy(h2_hbm.at[pl.ds(tok, 1)], xs_scr.at[slot_, pl.ds(j, 1)],
                                     gsem.at[slot_])

    def scatter_copy(dst, j, slot_):
        return pltpu.make_async_copy(y_scr.at[slot_, pl.ds(j, 1)], ys_hbm.at[pl.ds(dst, 1)],
                                     ssem.at[slot_])

    def start_gather(rs_ref, slot_):
        def body(j, c):
            s = rs_ref[0, 0, j]
            tok = jnp.where(s >= 0, s & (n_tok - 1), 0)
            gather_copy(tok, j, slot_).start()
            return c
        lax.fori_loop(0, blk, body, 0, unroll=8)

    @pl.when(i == 0)
    def _():
        start_gather(rs_cur_ref, 0)

    @pl.when(i + 1 < n)
    def _():
        start_gather(rs_nxt_ref, 1 - slot)

    def wait_gather(j, c):
        gather_copy(0, j, slot).wait()
        return c
    lax.fori_loop(0, blk, wait_gather, 0, unroll=8)

    @pl.when(i >= 2)
    def _():
        def wait_scatter(j, c):
            scatter_copy(0, j, slot).wait()
            return c
        lax.fori_loop(0, nvalid_ref[jnp.maximum(i - 2, 0)], wait_scatter, 0)

    nv = nvalid_ref[i]

    @pl.when(nv > 0)
    def _():
        lo, hi = _unpack_bf16_pair(xs_scr[slot])
        x = jnp.concatenate([lo, hi], axis=1).astype(BF16)
        gu = jnp.dot(x, wgu_ref[0], preferred_element_type=F32) + bgu_ref[0]
        gate = jnp.minimum(gu[:, 0:f], SWIGLU_LIMIT)
        up = jnp.clip(gu[:, f:2 * f], -SWIGLU_LIMIT, SWIGLU_LIMIT)
        act = (up + 1.0) * (gate * jax.nn.sigmoid(SWIGLU_ALPHA * gate))
        y = jnp.dot(act.astype(BF16), wd_ref[0], preferred_element_type=F32) + bd_ref[0]
        dh = y.shape[1] // 2
        y_scr[slot] = _pack_bf16_pair(y[:, 0:dh], y[:, dh:])

        def start_scatter(j, c):
            scatter_copy(rs_cur_ref[0, 0, j], j, slot).start()
            return c
        lax.fori_loop(0, nv, start_scatter, 0)

    @pl.when(i == n - 1)
    def _():
        def wait_cur(j, c):
            scatter_copy(0, j, slot).wait()
            return c
        lax.fori_loop(0, nv, wait_cur, 0)

        def wait_prev(j, c):
            scatter_copy(0, j, 1 - slot).wait()
            return c
        lax.fori_loop(0, nvalid_ref[jnp.maximum(i - 1, 0)], wait_prev, 0)


def _moe_call(blk_e, nvalid, row_slot3, h2p, w_gu_p, b_gu_p, w_down, b_down, *, n_slots, n_tok):
    n_blocks = row_slot3.shape[0]
    blk = row_slot3.shape[2]
    e, d, f2 = w_gu_p.shape
    f = f2 // 2
    dp = h2p.shape[1]
    kern = functools.partial(_moe_kernel, blk=blk, n_tok=n_tok, f=f)
    grid_spec = pltpu.PrefetchScalarGridSpec(
        num_scalar_prefetch=2,
        grid=(n_blocks,),
        in_specs=[
            pl.BlockSpec((1, 1, blk), lambda i, be, nv: (i, 0, 0), memory_space=pltpu.SMEM),
            pl.BlockSpec((1, 1, blk), lambda i, be, nv: (jnp.minimum(i + 1, n_blocks - 1), 0, 0),
                         memory_space=pltpu.SMEM),
            pl.BlockSpec(memory_space=pl.ANY),
            pl.BlockSpec((1, d, f2), lambda i, be, nv: (be[i], 0, 0)),
            pl.BlockSpec((1, 1, f2), lambda i, be, nv: (be[i], 0, 0)),
            pl.BlockSpec((1, f, d), lambda i, be, nv: (be[i], 0, 0)),
            pl.BlockSpec((1, 1, d), lambda i, be, nv: (be[i], 0, 0)),
        ],
        out_specs=pl.BlockSpec(memory_space=pl.ANY),
        scratch_shapes=[
            pltpu.VMEM((2, blk, dp), jnp.uint32),
            pltpu.VMEM((2, blk, dp), jnp.uint32),
            pltpu.SemaphoreType.DMA((2,)),
            pltpu.SemaphoreType.DMA((2,)),
        ],
    )
    return pl.pallas_call(
        kern,
        grid_spec=grid_spec,
        out_shape=jax.ShapeDtypeStruct((n_slots, dp), jnp.uint32),
        compiler_params=pltpu.CompilerParams(dimension_semantics=("arbitrary",),
                                             vmem_limit_bytes=VMEM_LIMIT),
        name="moe_experts",
    )(blk_e, nvalid, row_slot3, row_slot3, h2p, w_gu_p, b_gu_p, w_down, b_down)


def _final_kernel(x1_ref, ys_ref, w_ref, mod_ref, fmod_ref, gfin_ref, o_ref, *, d):
    w = w_ref[...]
    moe = None
    for k in range(TOP_K):
        lo, hi = _unpack_bf16_pair(ys_ref[k])
        yk = jnp.concatenate([lo, hi], axis=1) * w[:, k:k + 1]
        moe = yk if moe is None else moe + yk
    gate2 = mod_ref[0, 5:6, :]
    x2 = x1_ref[...] + gate2 * moe
    fshift = fmod_ref[0, 0:1, :]
    fscale = fmod_ref[0, 1:2, :]
    o_ref[...] = _rms(x2, gfin_ref[...]) * (1.0 + fscale) + fshift


def _final_call(x1, ys3, w_tok, mod, fmod, g_final, *, seq):
    t, d = x1.shape
    tm = TM_FIN
    tps = seq // tm
    kern = functools.partial(_final_kernel, d=d)
    return pl.pallas_call(
        kern,
        grid=(t // tm,),
        in_specs=[
            pl.BlockSpec((tm, d), lambda i: (i, 0)),
            pl.BlockSpec((TOP_K, tm, d // 2), lambda i: (0, i, 0)),
            pl.BlockSpec((tm, TOP_K), lambda i: (i, 0)),
            pl.BlockSpec((1, 6, d), lambda i: (i // tps, 0, 0)),
            pl.BlockSpec((1, 2, d), lambda i: (i // tps, 0, 0)),
            _const_spec((1, d)),
        ],
        out_specs=pl.BlockSpec((tm, d), lambda i: (i, 0)),
        out_shape=jax.ShapeDtypeStruct((t, d), F32),
        compiler_params=pltpu.CompilerParams(dimension_semantics=("arbitrary",),
                                             vmem_limit_bytes=VMEM_LIMIT),
        name="combine_final",
    )(x1, ys3, w_tok, mod, fmod, g_final)


def _prep_w_in(w_in, d, q_lora, kv_lora):
    o_kpe = d + q_lora + kv_lora
    o_g = o_kpe + QK_ROPE
    half = QK_ROPE // 2
    kpe = w_in[:, o_kpe:o_kpe + QK_ROPE]
    zpad = jnp.zeros((d, 128 - QK_ROPE), w_in.dtype)
    ksw = jnp.concatenate([-kpe[:, half:], kpe[:, :half]], axis=1)
    return jnp.concatenate([w_in[:, :o_kpe], kpe, zpad, ksw, zpad, w_in[:, o_g:]], axis=1).astype(BF16)


def _prep_w_q(w_q_b):
    ql = w_q_b.shape[0]
    hd = QK_NOPE + QK_ROPE
    half = QK_ROPE // 2
    w = w_q_b.reshape(ql, N_HEADS, hd)
    nope = w[:, :, :QK_NOPE]
    pe = w[:, :, QK_NOPE:]
    zpad = jnp.zeros((ql, N_HEADS, 128 - QK_ROPE), w.dtype)
    pe_p = jnp.concatenate([pe, zpad], axis=2)
    sw_p = jnp.concatenate([-pe[:, :, half:], pe[:, :, :half], zpad], axis=2)
    return jnp.concatenate([nope.reshape(ql, -1), pe_p.reshape(ql, -1), sw_p.reshape(ql, -1)],
                           axis=1).astype(BF16)


def _prep_w_kv(w_kv_b):
    kvl = w_kv_b.shape[0]
    w = w_kv_b.reshape(kvl, N_HEADS, QK_NOPE + V_HEAD)
    w_kn = w[:, :, :QK_NOPE].reshape(kvl, -1).astype(BF16)
    w_vt = w[:, :, QK_NOPE:].reshape(kvl, -1).T.astype(BF16)
    return w_kn, w_vt


def kernel(x, c, positions, w_mod, b_mod, g_mix, w_in, b_gate, w_pool_grp, pool_scale, w_pool_out,
           g_q_a, w_q_b, g_kv_a, w_kv_b, w_mla_out, w_out, g_ffn, w_router, b_router, w_gu, b_gu,
           w_down, b_down, g_final, w_fmod, b_fmod):
    bsz, seq, d = x.shape
    t = bsz * seq
    depth = w_mod.shape[0]
    assert depth == 1
    assert seq % TQ == 0 and seq % TM_IN == 0 and seq % TM_POST == 0 and seq % TM_FIN == 0
    assert t & (t - 1) == 0, "token count must be a power of two"
    q_lora = g_q_a.shape[-1]
    kv_lora = g_kv_a.shape[-1]
    n_exp = w_gu.shape[1]
    f = w_gu.shape[-1] // 2
    blk = MOE_BLK

    x2d = x.reshape(t, d)
    pos_col = positions.astype(F32).reshape(t, 1)
    inv_freq = 1.0 / (ROPE_THETA ** (jnp.arange(0, QK_ROPE, 2, dtype=F32) / QK_ROPE))
    invf2 = jnp.concatenate([inv_freq, inv_freq, jnp.zeros((128 - QK_ROPE,), F32)]).reshape(1, 128)

    mod = _mod_call(c, w_mod[0], b_mod[0]).reshape(bsz, 6, d)
    fmod = _mod_call(c, w_fmod, b_fmod).reshape(bsz, 2, d)

    w_in_p = _prep_w_in(w_in[0], d, q_lora, kv_lora)
    w_q_p = _prep_w_q(w_q_b[0])
    w_kn, w_vt = _prep_w_kv(w_kv_b[0])
    ga, g1, q2, k2, vt3 = _mixer_in_call(
        x2d, pos_col, mod, g_mix[0].reshape(1, d), w_in_p, b_gate[0].reshape(1, 2 * d),
        w_pool_grp[0].astype(BF16), pool_scale[0].reshape(1, d), w_pool_out[0].astype(BF16),
        g_q_a[0].reshape(1, q_lora), w_q_p, g_kv_a[0].reshape(1, kv_lora), w_kn, w_vt, invf2,
        bsz=bsz, seq=seq)

    hp = N_HEADS * HEAD_PAD
    o = _attn_call(q2.reshape(bsz, seq, hp), k2.reshape(bsz, seq, hp), vt3)

    x1, h2p, idx_t, wgt_t = _post_call(
        o.reshape(t, d), ga, g1, x2d, mod, w_mla_out[0].astype(BF16), w_out[0].astype(BF16),
        g_ffn[0].reshape(1, d), w_router[0].T, b_router[0].reshape(n_exp, 1), seq=seq)

    n_slots = t * TOP_K
    n_rows = n_slots + n_exp * blk
    n_blocks = n_rows // blk
    flat_e = idx_t.reshape(-1)
    order = jnp.argsort(flat_e, stable=True).astype(jnp.int32)
    counts = jnp.sum(flat_e[:, None] == jnp.arange(n_exp, dtype=jnp.int32)[None, :], axis=0,
                     dtype=jnp.int32)
    padded = (counts + blk - 1) // blk * blk
    pad_end = jnp.cumsum(padded)
    pad_start = pad_end - padded
    grp_start = jnp.cumsum(counts) - counts
    blk_start = jnp.arange(n_blocks, dtype=jnp.int32) * blk
    blk_e = jnp.minimum(jnp.searchsorted(pad_end, blk_start, side='right'), n_exp - 1).astype(jnp.int32)
    off_in_e = blk_start - pad_start[blk_e]
    nvalid = jnp.clip(counts[blk_e] - off_in_e, 0, blk).astype(jnp.int32)
    nvalid = jnp.where(blk_start < pad_end[-1], nvalid, 0)
    j = jnp.arange(blk, dtype=jnp.int32)[None, :]
    src = (grp_start[blk_e] + off_in_e)[:, None] + j
    row_slot = jnp.where(j < nvalid[:, None], order[jnp.clip(src, 0, n_slots - 1)], -1)
    row_slot3 = row_slot.reshape(n_blocks, 1, blk).astype(jnp.int32)

    w_gu_p = jnp.concatenate([w_gu[0][:, :, 0::2], w_gu[0][:, :, 1::2]], axis=-1).astype(BF16)
    b_gu_p = jnp.concatenate([b_gu[0][:, 0::2], b_gu[0][:, 1::2]], axis=-1).reshape(n_exp, 1, 2 * f)
    ys = _moe_call(blk_e, nvalid, row_slot3, h2p, w_gu_p, b_gu_p, w_down[0].astype(BF16),
                   b_down[0].reshape(n_exp, 1, d), n_slots=n_slots, n_tok=t)

    out = _final_call(x1, ys.reshape(TOP_K, t, d // 2), wgt_t.T, mod, fmod,
                      g_final.reshape(1, d), seq=seq)
    return out.reshape(bsz, seq, d)
```

```python
import functools

import jax
import jax.numpy as jnp
from jax import lax
from jax.experimental import pallas as pl
from jax.experimental.pallas import tpu as pltpu

F32 = jnp.float32
BF16 = jnp.bfloat16

CHUNK = 64
POOL_WINDOWS = (2, 4, 8, 16)
POOL_HALO = 16
N_HEADS = 8
QK_NOPE = 128
QK_ROPE = 64
V_HEAD = 128
HEAD_PAD = 256
ROPE_THETA = 10000.0
TOP_K = 4
SWIGLU_LIMIT = 7.0
SWIGLU_ALPHA = 1.702
NORM_EPS = 1e-6
NEG_INF = -1e30

VMEM_LIMIT = 56 * 1024 * 1024

TM_IN = 256
TQ = 512
TK = 512
TM_POST = 256
MOE_BLK = 256
TM_FIN = 256
ROW_SUB = 4


def _const_spec(shape):
    nd = len(shape)
    return pl.BlockSpec(shape, lambda *_: (0,) * nd, pipeline_mode=pl.Buffered(1))


def _rms(xf, g):
    return xf * lax.rsqrt(jnp.mean(xf * xf, axis=-1, keepdims=True) + NORM_EPS) * g


def _pack_bf16_pair(lo, hi):
    lo_b = lax.bitcast_convert_type(lo.astype(BF16).astype(F32), jnp.uint32)
    hi_b = lax.bitcast_convert_type(hi.astype(BF16).astype(F32), jnp.uint32)
    return (hi_b & jnp.uint32(0xFFFF0000)) | (lo_b >> 16)


def _unpack_bf16_pair(p):
    lo = lax.bitcast_convert_type(p << 16, F32)
    hi = lax.bitcast_convert_type(p & jnp.uint32(0xFFFF0000), F32)
    return lo, hi


def _store_rows(ref, packed):
    n = packed.shape[0]
    for q in range(ROW_SUB):
        ref[pl.ds(q, n, stride=ROW_SUB), :] = packed[:, q * 128:(q + 1) * 128]


def _load_rows(ref, n):
    los, his = [], []
    for q in range(ROW_SUB):
        lo, hi = _unpack_bf16_pair(ref[pl.ds(q, n, stride=ROW_SUB), :])
        los.append(lo)
        his.append(hi)
    return jnp.concatenate(los + his, axis=1)


def _mod_kernel(c_ref, w_ref, b_ref, o_ref):
    c = c_ref[...]
    c_act = c * jax.nn.sigmoid(c)
    o_ref[...] = jnp.dot(c_act, w_ref[...], preferred_element_type=F32,
                         precision=lax.Precision.HIGHEST) + b_ref[...]


def _mod_call(c, w, b, tn=1024):
    bsz, d = c.shape
    n = w.shape[1]
    return pl.pallas_call(
        _mod_kernel,
        grid=(n // tn,),
        in_specs=[pl.BlockSpec((bsz, d), lambda j: (0, 0)),
                  pl.BlockSpec((d, tn), lambda j: (0, j)),
                  pl.BlockSpec((1, tn), lambda j: (0, j))],
        out_specs=pl.BlockSpec((bsz, tn), lambda j: (0, j)),
        out_shape=jax.ShapeDtypeStruct((bsz, n), F32),
        compiler_params=pltpu.CompilerParams(dimension_semantics=("arbitrary",),
                                             vmem_limit_bytes=VMEM_LIMIT),
        name="adaln_mod",
    )(c, w, b.reshape(1, n))


def _mixer_in_kernel(x_ref, xh_ref, pos_ref, mod_ref, gmix_ref, win_ref, bgate_ref, wgrp_ref,
                     pscale_ref, wpo_ref, gq_ref, wq_ref, gkv_ref, wkn_ref, wvt_ref, invf_ref,
                     ga_ref, g1_ref, q_ref, k_ref, vt_ref, u_scr, *, tm, tiles_per_seq, d, q_lora,
                     kv_lora):
    i = pl.program_id(0)
    t_in_seq = i % tiles_per_seq
    is_start = t_in_seq == 0
    shift1 = mod_ref[0, 0:1, :]
    scale1 = mod_ref[0, 1:2, :]
    gmix = gmix_ref[...]

    def prenorm(xf):
        return (_rms(xf, gmix) * (1.0 + scale1) + shift1).astype(BF16)

    h = prenorm(x_ref[...])
    hh = prenorm(xh_ref[...])
    u = jnp.dot(h, win_ref[:, 0:d], preferred_element_type=F32)
    uh = jnp.dot(hh, win_ref[:, 0:d], preferred_element_type=F32)
    u_scr[0:POOL_HALO, :] = jnp.where(is_start, 0.0, uh)
    u_scr[POOL_HALO:POOL_HALO + tm, :] = u
    rest = jnp.dot(h, win_ref[:, d:], preferred_element_type=F32)

    gw = d // len(POOL_WINDOWS)
    tseq = t_in_seq * tm + lax.broadcasted_iota(jnp.int32, (tm, 1), 0)
    ys = []
    for g, w in enumerate(POOL_WINDOWS):
        c0 = g * gw
        ug = u_scr[POOL_HALO:POOL_HALO + tm, c0:c0 + gw]
        acc = ug
        for j in range(1, w):
            acc = acc + u_scr[POOL_HALO - j:POOL_HALO - j + tm, c0:c0 + gw]
        cnt = jnp.minimum(tseq + 1, w).astype(F32)
        mixed = (acc / cnt - ug).astype(BF16)
        ys.append(jnp.dot(mixed, wgrp_ref[g], preferred_element_type=F32))
    y = (jnp.concatenate(ys, axis=1) * pscale_ref[...]).astype(BF16)
    a = jnp.dot(y, wpo_ref[...], preferred_element_type=F32)

    o_q = 0
    o_kv = q_lora
    o_kpe = q_lora + kv_lora
    o_ksw = o_kpe + 128
    o_g0 = o_ksw + 128
    o_g1 = o_g0 + d
    gates0 = jax.nn.sigmoid(rest[:, o_g0:o_g0 + d] + bgate_ref[:, 0:d])
    gates1 = jax.nn.sigmoid(rest[:, o_g1:o_g1 + d] + bgate_ref[:, d:2 * d])
    ga_ref[...] = (gates0 * a).astype(BF16)
    g1_ref[...] = gates1.astype(BF16)

    ang = pos_ref[...] * invf_ref[...]
    cos2 = jnp.cos(ang)
    sin2 = jnp.sin(ang)
    sm_scale = float(QK_NOPE + QK_ROPE) ** -0.5

    qn = _rms(rest[:, o_q:o_q + q_lora], gq_ref[...]).astype(BF16)
    qall = jnp.dot(qn, wq_ref[...], preferred_element_type=F32)
    kvn = _rms(rest[:, o_kv:o_kv + kv_lora], gkv_ref[...]).astype(BF16)
    kn = jnp.dot(kvn, wkn_ref[...], preferred_element_type=F32)
    vt = lax.dot_general(wvt_ref[...], kvn, (((1,), (1,)), ((), ())),
                         preferred_element_type=F32)
    vt_ref[0] = vt.astype(BF16)
    kpe = (rest[:, o_kpe:o_kpe + 128] * cos2 + rest[:, o_ksw:o_ksw + 128] * sin2).astype(BF16)
    hn = N_HEADS * 128
    for hd in range(N_HEADS):
        c0 = hd * 128
        qpe = qall[:, hn + c0:hn + c0 + 128] * cos2 + qall[:, 2 * hn + c0:2 * hn + c0 + 128] * sin2
        q_ref[:, hd * HEAD_PAD:hd * HEAD_PAD + 128] = (qall[:, c0:c0 + 128] * sm_scale).astype(BF16)
        q_ref[:, hd * HEAD_PAD + 128:(hd + 1) * HEAD_PAD] = (qpe * sm_scale).astype(BF16)
        k_ref[:, hd * HEAD_PAD:hd * HEAD_PAD + 128] = kn[:, c0:c0 + 128].astype(BF16)
        k_ref[:, hd * HEAD_PAD + 128:(hd + 1) * HEAD_PAD] = kpe


def _mixer_in_call(x2d, pos_col, mod, g_mix, w_in_p, b_gate, w_grp, pool_scale, w_po, g_q_a, w_q_p,
                   g_kv_a, w_kn, w_vt, invf2, *, bsz, seq):
    t, d = x2d.shape
    tm = TM_IN
    tps = seq // tm
    q_lora = g_q_a.shape[-1]
    kv_lora = g_kv_a.shape[-1]
    hp = N_HEADS * HEAD_PAD
    halo_blocks = tm // POOL_HALO
    kern = functools.partial(_mixer_in_kernel, tm=tm, tiles_per_seq=tps, d=d, q_lora=q_lora,
                             kv_lora=kv_lora)
    row = lambda i: (i, 0)
    return pl.pallas_call(
        kern,
        grid=(t // tm,),
        in_specs=[
            pl.BlockSpec((tm, d), row),
            pl.BlockSpec((POOL_HALO, d), lambda i: (jnp.maximum(i * halo_blocks - 1, 0), 0)),
            pl.BlockSpec((tm, 1), row),
            pl.BlockSpec((1, 6, d), lambda i: (i // tps, 0, 0)),
            _const_spec((1, d)),
            _const_spec(w_in_p.shape),
            _const_spec((1, 2 * d)),
            _const_spec(w_grp.shape),
            _const_spec((1, d)),
            _const_spec(w_po.shape),
            _const_spec((1, q_lora)),
            _const_spec(w_q_p.shape),
            _const_spec((1, kv_lora)),
            _const_spec(w_kn.shape),
            _const_spec(w_vt.shape),
            _const_spec((1, 128)),
        ],
        out_specs=[
            pl.BlockSpec((tm, d), row),
            pl.BlockSpec((tm, d), row),
            pl.BlockSpec((tm, hp), row),
            pl.BlockSpec((tm, hp), row),
            pl.BlockSpec((1, N_HEADS * V_HEAD, tm), lambda i: (i // tps, 0, i % tps)),
        ],
        out_shape=[
            jax.ShapeDtypeStruct((t, d), BF16),
            jax.ShapeDtypeStruct((t, d), BF16),
            jax.ShapeDtypeStruct((t, hp), BF16),
            jax.ShapeDtypeStruct((t, hp), BF16),
            jax.ShapeDtypeStruct((bsz, N_HEADS * V_HEAD, seq), BF16),
        ],
        scratch_shapes=[pltpu.VMEM((tm + POOL_HALO, d), F32)],
        compiler_params=pltpu.CompilerParams(dimension_semantics=("arbitrary",),
                                             vmem_limit_bytes=VMEM_LIMIT),
        name="mixer_in",
    )(x2d, x2d, pos_col, mod, g_mix, w_in_p, b_gate, w_grp, pool_scale, w_po, g_q_a, w_q_p, g_kv_a,
      w_kn, w_vt, invf2)


def _attn_kernel(q_ref, k_ref, vt_ref, o_ref, *, tq, tk, nq):
    def q_body(qi, carry):
        q0 = pl.multiple_of(qi * tq, tq)
        q = q_ref[0, pl.ds(q0, tq), :]

        def step(k0, m, l, acc, masked):
            k = k_ref[0, pl.ds(k0, tk), :]
            s = lax.dot_general(k, q, (((1,), (1,)), ((), ())), preferred_element_type=F32)
            if masked:
                kc = (k0 + lax.broadcasted_iota(jnp.int32, (tk, tq), 0)) // CHUNK
                qc = (q0 + lax.broadcasted_iota(jnp.int32, (tk, tq), 1)) // CHUNK
                s = jnp.where(qc >= kc, s, NEG_INF)
            m_new = jnp.maximum(m, jnp.max(s, axis=0, keepdims=True))
            p = jnp.exp(s - m_new)
            alpha = jnp.exp(m - m_new)
            l_new = alpha * l + jnp.sum(p, axis=0, keepdims=True)
            vt = vt_ref[0, :, pl.ds(k0, tk)]
            acc_new = alpha * acc + jnp.dot(vt, p.astype(BF16), preferred_element_type=F32)
            return m_new, l_new, acc_new

        def kv_body(kj, c):
            k0 = pl.multiple_of(kj * tk, tk)
            return step(k0, *c, masked=False)

        init = (jnp.full((1, tq), NEG_INF, F32), jnp.zeros((1, tq), F32),
                jnp.zeros((V_HEAD, tq), F32))
        m, l, acc = lax.fori_loop(0, qi * (tq // tk), kv_body, init)
        for dj in range(tq // tk):
            k0 = pl.multiple_of(q0 + dj * tk, tk)
            m, l, acc = step(k0, m, l, acc, masked=True)
        o = (acc / l).T
        o_ref[0, pl.ds(q0, tq), :] = o.astype(BF16)
        return carry

    lax.fori_loop(0, nq, q_body, 0)


def _attn_call(q3, k3, vt3):
    bsz, seq, _ = q3.shape
    kern = functools.partial(_attn_kernel, tq=TQ, tk=TK, nq=seq // TQ)
    return pl.pallas_call(
        kern,
        grid=(bsz, N_HEADS),
        in_specs=[pl.BlockSpec((1, seq, HEAD_PAD), lambda b, h: (b, 0, h)),
                  pl.BlockSpec((1, seq, HEAD_PAD), lambda b, h: (b, 0, h)),
                  pl.BlockSpec((1, V_HEAD, seq), lambda b, h: (b, h, 0))],
        out_specs=pl.BlockSpec((1, seq, V_HEAD), lambda b, h: (b, 0, h)),
        out_shape=jax.ShapeDtypeStruct((bsz, seq, N_HEADS * V_HEAD), BF16),
        compiler_params=pltpu.CompilerParams(dimension_semantics=("arbitrary", "arbitrary"),
                                             vmem_limit_bytes=VMEM_LIMIT),
        name="mla_attn",
    )(q3, k3, vt3)


def _post_kernel(o_ref, ga_ref, g1_ref, x_ref, mod_ref, wmo_ref, wout_ref, gffn_ref, wrt_ref, br_ref,
                 x1_ref, h2_ref, idx_ref, wgt_ref, *, d):
    m = jnp.dot(o_ref[...], wmo_ref[...], preferred_element_type=F32)
    merged = ga_ref[...].astype(F32) + g1_ref[...].astype(F32) * m
    gate1 = mod_ref[0, 2:3, :]
    x1 = x_ref[...] + gate1 * jnp.dot(merged.astype(BF16), wout_ref[...],
                                      preferred_element_type=F32)
    x1_ref[...] = x1
    shift2 = mod_ref[0, 3:4, :]
    scale2 = mod_ref[0, 4:5, :]
    h2 = _rms(x1, gffn_ref[...]) * (1.0 + scale2) + shift2
    _store_rows(h2_ref, _pack_bf16_pair(h2[:, 0:d // 2], h2[:, d // 2:d]))

    logits = lax.dot_general(wrt_ref[...], h2, (((1,), (1,)), ((), ())),
                             preferred_element_type=F32,
                             precision=lax.Precision.HIGHEST) + br_ref[...]
    ne, tm = logits.shape
    eid = lax.broadcasted_iota(jnp.int32, (ne, tm), 0)
    vals, idxs = [], []
    cur = logits
    for _ in range(TOP_K):
        mx = jnp.max(cur, axis=0, keepdims=True)
        ix = jnp.min(jnp.where(cur == mx, eid, ne), axis=0, keepdims=True)
        vals.append(mx)
        idxs.append(ix)
        cur = jnp.where(eid == ix, -jnp.inf, cur)
    es = [jnp.exp(v - vals[0]) for v in vals]
    den = es[0] + es[1] + es[2] + es[3]
    idx_ref[...] = jnp.concatenate(idxs, axis=0)
    wgt_ref[...] = jnp.concatenate([e / den for e in es], axis=0)


def _post_call(o2d, ga, g1, x2d, mod, w_mo, w_out, g_ffn, w_rt, b_r, *, seq):
    t, d = x2d.shape
    tm = TM_POST
    tps = seq // tm
    row = lambda i: (i, 0)
    kern = functools.partial(_post_kernel, d=d)
    return pl.pallas_call(
        kern,
        grid=(t // tm,),
        in_specs=[
            pl.BlockSpec((tm, d), row),
            pl.BlockSpec((tm, d), row),
            pl.BlockSpec((tm, d), row),
            pl.BlockSpec((tm, d), row),
            pl.BlockSpec((1, 6, d), lambda i: (i // tps, 0, 0)),
            _const_spec(w_mo.shape),
            _const_spec(w_out.shape),
            _const_spec((1, d)),
            _const_spec(w_rt.shape),
            _const_spec(b_r.shape),
        ],
        out_specs=[
            pl.BlockSpec((tm, d), row),
            pl.BlockSpec((tm * ROW_SUB, 128), row),
            pl.BlockSpec((TOP_K, tm), lambda i: (0, i)),
            pl.BlockSpec((TOP_K, tm), lambda i: (0, i)),
        ],
        out_shape=[
            jax.ShapeDtypeStruct((t, d), F32),
            jax.ShapeDtypeStruct((t * ROW_SUB, 128), jnp.uint32),
            jax.ShapeDtypeStruct((TOP_K, t), jnp.int32),
            jax.ShapeDtypeStruct((TOP_K, t), F32),
        ],
        compiler_params=pltpu.CompilerParams(dimension_semantics=("arbitrary",),
                                             vmem_limit_bytes=VMEM_LIMIT),
        name="post_attn_router",
    )(o2d, ga, g1, x2d, mod, w_mo, w_out, g_ffn, w_rt, b_r)


def _moe_kernel(blk_e_ref, src_cur_ref, src_nxt_ref, dst_ref, h2_hbm, perm_ref, wgu_ref, bgu_ref,
                wd_ref, bd_ref, ys_hbm, xs_scr, y_scr, xb_scr, wgu_scr, wd_scr, gsem, ssem, *, blk, f):
    s = pl.program_id(0)
    last = pl.num_programs(0) - 1
    slot = s % 2
    rows = blk * ROW_SUB

    def gather_copy(src_row, j, slot_):
        return pltpu.make_async_copy(h2_hbm.at[pl.ds(src_row, ROW_SUB)],
                                     xs_scr.at[slot_, pl.ds(j * ROW_SUB, ROW_SUB)], gsem.at[slot_])

    def scatter_copy(dst_row, j, slot_):
        return pltpu.make_async_copy(y_scr.at[slot_, pl.ds(j * ROW_SUB, ROW_SUB)],
                                     ys_hbm.at[pl.ds(dst_row, ROW_SUB)], ssem.at[slot_])

    def wait_gathers(slot_):
        pltpu.make_async_copy(h2_hbm.at[pl.ds(0, rows)], xs_scr.at[slot_], gsem.at[slot_]).wait()

    def wait_scatters(slot_):
        pltpu.make_async_copy(y_scr.at[slot_], ys_hbm.at[pl.ds(0, rows)], ssem.at[slot_]).wait()

    @pl.when(s == 0)
    def _():
        y_scr[...] = jnp.zeros(y_scr.shape, y_scr.dtype)

        def body(j, c):
            gather_copy(pl.multiple_of(src_cur_ref[0, 0, j], ROW_SUB), j, 0).start()
            return c
        lax.fori_loop(0, blk, body, 0, unroll=8)

    wait_gathers(slot)

    @pl.when(s >= 1)
    def _():
        wait_scatters(slot)

    e_cur = blk_e_ref[s]
    e_prev = blk_e_ref[jnp.maximum(s - 1, 0)]

    @pl.when(jnp.logical_or(s == 0, e_cur != e_prev))
    def _():
        perm = perm_ref[...]
        for c in range(2 * f // 256):
            r = jnp.dot(wgu_ref[0, :, c * 256:(c + 1) * 256].astype(BF16), perm,
                        preferred_element_type=F32)
            wgu_scr[:, c * 128:(c + 1) * 128] = r[:, 0:128].astype(BF16)
            wgu_scr[:, f + c * 128:f + (c + 1) * 128] = r[:, 128:256].astype(BF16)
        wd_scr[...] = wd_ref[0].astype(BF16)

    xb_scr[...] = _load_rows(xs_scr.at[slot], blk).astype(BF16)
    for j in range(blk):
        gather_copy(pl.multiple_of(src_nxt_ref[0, 0, j], ROW_SUB), j, 1 - slot).start()
    for j in range(blk):
        scatter_copy(pl.multiple_of(dst_ref[0, 0, j], ROW_SUB), j, 1 - slot).start(priority=1)
    gu = jnp.dot(xb_scr[...], wgu_scr[...], preferred_element_type=F32) + bgu_ref[0]
    gate = jnp.minimum(gu[:, 0:f], SWIGLU_LIMIT)
    up = jnp.clip(gu[:, f:2 * f], -SWIGLU_LIMIT, SWIGLU_LIMIT)
    act = (up + 1.0) * (gate * jax.nn.sigmoid(SWIGLU_ALPHA * gate))
    y = jnp.dot(act.astype(BF16), wd_scr[...], preferred_element_type=F32) + bd_ref[0]
    dh = y.shape[1] // 2
    _store_rows(y_scr.at[slot], _pack_bf16_pair(y[:, 0:dh], y[:, dh:]))

    @pl.when(s == last)
    def _():
        wait_scatters(1 - slot)
        wait_gathers(1 - slot)


def _moe_call(blk_e, src3, dst3, h2p, perm, w_gu, b_gu_p, w_down, b_down, *, n_out_rows):
    n_steps = src3.shape[0]
    blk = src3.shape[2]
    _, d, f2 = w_gu.shape
    f = f2 // 2
    rows = blk * ROW_SUB
    kern = functools.partial(_moe_kernel, blk=blk, f=f)
    smem_blk = lambda imap: pl.BlockSpec((1, 1, blk), imap, memory_space=pltpu.SMEM)
    grid_spec = pltpu.PrefetchScalarGridSpec(
        num_scalar_prefetch=1,
        grid=(n_steps,),
        in_specs=[
            smem_blk(lambda i, be: (i, 0, 0)),
            smem_blk(lambda i, be: (jnp.minimum(i + 1, n_steps - 1), 0, 0)),
            smem_blk(lambda i, be: (i, 0, 0)),
            pl.BlockSpec(memory_space=pl.ANY),
            pl.BlockSpec((256, 256), lambda i, be: (0, 0)),
            pl.BlockSpec((1, d, f2), lambda i, be: (be[i], 0, 0)),
            pl.BlockSpec((1, 1, f2), lambda i, be: (be[i], 0, 0)),
            pl.BlockSpec((1, f, d), lambda i, be: (be[i], 0, 0)),
            pl.BlockSpec((1, 1, d), lambda i, be: (be[i], 0, 0)),
        ],
        out_specs=pl.BlockSpec(memory_space=pl.ANY),
        scratch_shapes=[
            pltpu.VMEM((2, rows, 128), jnp.uint32),
            pltpu.VMEM((2, rows, 128), jnp.uint32),
            pltpu.VMEM((blk, d), BF16),
            pltpu.VMEM((d, f2), BF16),
            pltpu.VMEM((f, d), BF16),
            pltpu.SemaphoreType.DMA((2,)),
            pltpu.SemaphoreType.DMA((2,)),
        ],
    )
    return pl.pallas_call(
        kern,
        grid_spec=grid_spec,
        out_shape=jax.ShapeDtypeStruct((n_out_rows, 128), jnp.uint32),
        compiler_params=pltpu.CompilerParams(dimension_semantics=("arbitrary",),
                                             vmem_limit_bytes=VMEM_LIMIT),
        name="moe_experts",
    )(blk_e, src3, src3, dst3, h2p, perm, w_gu, b_gu_p, w_down, b_down)


def _final_kernel(x1_ref, y0_ref, y1_ref, y2_ref, y3_ref, w_ref, mod_ref, fmod_ref, gfin_ref, o_ref,
                  *, tm):
    w = w_ref[...]
    moe = None
    for k, y_ref in enumerate((y0_ref, y1_ref, y2_ref, y3_ref)):
        yk = _load_rows(y_ref, tm) * w[:, k:k + 1]
        moe = yk if moe is None else moe + yk
    gate2 = mod_ref[0, 5:6, :]
    x2 = x1_ref[...] + gate2 * moe
    fshift = fmod_ref[0, 0:1, :]
    fscale = fmod_ref[0, 1:2, :]
    o_ref[...] = _rms(x2, gfin_ref[...]) * (1.0 + fscale) + fshift


def _final_call(x1, ys, w_tok, mod, fmod, g_final, *, seq):
    t, d = x1.shape
    tm = TM_FIN
    tps = seq // tm
    nt = t // tm
    kern = functools.partial(_final_kernel, tm=tm)
    y_specs = [pl.BlockSpec((tm * ROW_SUB, 128), functools.partial(lambda i, k: (k * nt + i, 0), k=k))
               for k in range(TOP_K)]
    return pl.pallas_call(
        kern,
        grid=(nt,),
        in_specs=[pl.BlockSpec((tm, d), lambda i: (i, 0))] + y_specs + [
            pl.BlockSpec((tm, TOP_K), lambda i: (i, 0)),
            pl.BlockSpec((1, 6, d), lambda i: (i // tps, 0, 0)),
            pl.BlockSpec((1, 2, d), lambda i: (i // tps, 0, 0)),
            _const_spec((1, d)),
        ],
        out_specs=pl.BlockSpec((tm, d), lambda i: (i, 0)),
        out_shape=jax.ShapeDtypeStruct((t, d), F32),
        compiler_params=pltpu.CompilerParams(dimension_semantics=("arbitrary",),
                                             vmem_limit_bytes=VMEM_LIMIT),
        name="combine_final",
    )(x1, ys, ys, ys, ys, w_tok, mod, fmod, g_final)


def _prep_w_in(w_in, d, q_lora, kv_lora):
    o_kpe = d + q_lora + kv_lora
    o_g = o_kpe + QK_ROPE
    half = QK_ROPE // 2
    kpe = w_in[:, o_kpe:o_kpe + QK_ROPE]
    zpad = jnp.zeros((d, 128 - QK_ROPE), w_in.dtype)
    ksw = jnp.concatenate([-kpe[:, half:], kpe[:, :half]], axis=1)
    return jnp.concatenate([w_in[:, :o_kpe], kpe, zpad, ksw, zpad, w_in[:, o_g:]], axis=1).astype(BF16)


def _prep_w_q(w_q_b):
    ql = w_q_b.shape[0]
    hd = QK_NOPE + QK_ROPE
    half = QK_ROPE // 2
    w = w_q_b.reshape(ql, N_HEADS, hd)
    nope = w[:, :, :QK_NOPE]
    pe = w[:, :, QK_NOPE:]
    zpad = jnp.zeros((ql, N_HEADS, 128 - QK_ROPE), w.dtype)
    pe_p = jnp.concatenate([pe, zpad], axis=2)
    sw_p = jnp.concatenate([-pe[:, :, half:], pe[:, :, :half], zpad], axis=2)
    return jnp.concatenate([nope.reshape(ql, -1), pe_p.reshape(ql, -1), sw_p.reshape(ql, -1)],
                           axis=1).astype(BF16)


def _prep_w_kv(w_kv_b):
    kvl = w_kv_b.shape[0]
    w = w_kv_b.reshape(kvl, N_HEADS, QK_NOPE + V_HEAD)
    w_kn = w[:, :, :QK_NOPE].reshape(kvl, -1).astype(BF16)
    w_vt = w[:, :, QK_NOPE:].reshape(kvl, -1).T.astype(BF16)
    return w_kn, w_vt


def kernel(x, c, positions, w_mod, b_mod, g_mix, w_in, b_gate, w_pool_grp, pool_scale, w_pool_out,
           g_q_a, w_q_b, g_kv_a, w_kv_b, w_mla_out, w_out, g_ffn, w_router, b_router, w_gu, b_gu,
           w_down, b_down, g_final, w_fmod, b_fmod):
    bsz, seq, d = x.shape
    t = bsz * seq
    depth = w_mod.shape[0]
    assert depth == 1
    assert seq % TQ == 0 and seq % TM_IN == 0 and seq % TM_POST == 0 and seq % TM_FIN == 0
    assert t & (t - 1) == 0, "token count must be a power of two"
    q_lora = g_q_a.shape[-1]
    kv_lora = g_kv_a.shape[-1]
    n_exp = w_gu.shape[1]
    f = w_gu.shape[-1] // 2
    blk = MOE_BLK

    x2d = x.reshape(t, d)
    pos_col = positions.astype(F32).reshape(t, 1)
    inv_freq = 1.0 / (ROPE_THETA ** (jnp.arange(0, QK_ROPE, 2, dtype=F32) / QK_ROPE))
    invf2 = jnp.concatenate([inv_freq, inv_freq, jnp.zeros((128 - QK_ROPE,), F32)]).reshape(1, 128)

    mod = _mod_call(c, w_mod[0], b_mod[0]).reshape(bsz, 6, d)
    fmod = _mod_call(c, w_fmod, b_fmod).reshape(bsz, 2, d)

    w_in_p = _prep_w_in(w_in[0], d, q_lora, kv_lora)
    w_q_p = _prep_w_q(w_q_b[0])
    w_kn, w_vt = _prep_w_kv(w_kv_b[0])
    ga, g1, q2, k2, vt3 = _mixer_in_call(
        x2d, pos_col, mod, g_mix[0].reshape(1, d), w_in_p, b_gate[0].reshape(1, 2 * d),
        w_pool_grp[0].astype(BF16), pool_scale[0].reshape(1, d), w_pool_out[0].astype(BF16),
        g_q_a[0].reshape(1, q_lora), w_q_p, g_kv_a[0].reshape(1, kv_lora), w_kn, w_vt, invf2,
        bsz=bsz, seq=seq)

    hp = N_HEADS * HEAD_PAD
    o = _attn_call(q2.reshape(bsz, seq, hp), k2.reshape(bsz, seq, hp), vt3)

    x1, h2p, idx_t, wgt_t = _post_call(
        o.reshape(t, d), ga, g1, x2d, mod, w_mla_out[0].astype(BF16), w_out[0].astype(BF16),
        g_ffn[0].reshape(1, d), w_router[0].T, b_router[0].reshape(n_exp, 1), seq=seq)

    n_slots = t * TOP_K
    n_rows = n_slots + n_exp * blk
    n_blocks = n_rows // blk
    flat_e = idx_t.reshape(-1)
    order = jnp.argsort(flat_e, stable=True).astype(jnp.int32)
    counts = jnp.sum(flat_e[:, None] == jnp.arange(n_exp, dtype=jnp.int32)[None, :], axis=0,
                     dtype=jnp.int32)
    padded = (counts + blk - 1) // blk * blk
    pad_end = jnp.cumsum(padded)
    pad_start = pad_end - padded
    grp_start = jnp.cumsum(counts) - counts
    blk_id = jnp.arange(n_blocks, dtype=jnp.int32)
    blk_start = blk_id * blk
    blk_e = jnp.minimum(jnp.sum(pad_end[None, :] <= blk_start[:, None], axis=1, dtype=jnp.int32),
                        n_exp - 1)
    off_in_e = blk_start - pad_start[blk_e]
    nvalid = jnp.clip(counts[blk_e] - off_in_e, 0, blk)
    j = jnp.arange(blk, dtype=jnp.int32)[None, :]
    src = (grp_start[blk_e] + off_in_e)[:, None] + j
    valid = j < nvalid[:, None]
    row_slot = order[jnp.clip(src, 0, n_slots - 1)]
    src_rows = jnp.where(valid, row_slot & (t - 1), 0) * ROW_SUB
    spare = n_slots + (blk_id[:, None] % 2) * blk + j
    dst_rows = jnp.where(valid, row_slot, spare) * ROW_SUB
    zero_blk = jnp.zeros((1, blk), jnp.int32)
    src3 = jnp.concatenate([src_rows, zero_blk], axis=0).reshape(n_blocks + 1, 1, blk)
    first_dst = (n_slots + blk + j) * ROW_SUB
    dst3 = jnp.concatenate([first_dst, dst_rows], axis=0).reshape(n_blocks + 1, 1, blk)
    blk_e_steps = jnp.concatenate([blk_e, jnp.full((1,), n_exp - 1, jnp.int32)])

    col = jnp.arange(256, dtype=jnp.int32)[None, :]
    row = jnp.arange(256, dtype=jnp.int32)[:, None]
    perm = (row == jnp.where(col < 128, 2 * col, 2 * (col - 128) + 1)).astype(BF16)
    b_gu_p = jnp.concatenate([b_gu[0][:, 0::2], b_gu[0][:, 1::2]], axis=-1).reshape(n_exp, 1, 2 * f)
    ys = _moe_call(blk_e_steps, src3, dst3, h2p, perm, w_gu[0], b_gu_p, w_down[0],
                   b_down[0].reshape(n_exp, 1, d), n_out_rows=(n_slots + 2 * blk) * ROW_SUB)

    out = _final_call(x1, ys, wgt_t.T, mod, fmod, g_final.reshape(1, d), seq=seq)
    return out.reshape(bsz, seq, d)
```

```python
import functools

import jax
import jax.numpy as jnp
from jax import lax
from jax.experimental import pallas as pl
from jax.experimental.pallas import tpu as pltpu

F32 = jnp.float32
BF16 = jnp.bfloat16

CHUNK = 64
POOL_WINDOWS = (2, 4, 8, 16)
POOL_HALO = 16
N_HEADS = 8
QK_NOPE = 128
QK_ROPE = 64
V_HEAD = 128
HEAD_PAD = 256
ROPE_THETA = 10000.0
TOP_K = 4
SWIGLU_LIMIT = 7.0
SWIGLU_ALPHA = 1.702
NORM_EPS = 1e-6
NEG_INF = -1e30

VMEM_LIMIT = 56 * 1024 * 1024

TM_IN = 256
TQ = 512
TM_POST = 512
MOE_BLK = 256
TM_FIN = 256
ROW_SUB = 4


def _const_spec(shape):
    nd = len(shape)
    return pl.BlockSpec(shape, lambda *_: (0,) * nd, pipeline_mode=pl.Buffered(1))


def _rms(xf, g):
    return xf * lax.rsqrt(jnp.mean(xf * xf, axis=-1, keepdims=True) + NORM_EPS) * g


def _pack_bf16_pair(lo, hi):
    lo_b = lax.bitcast_convert_type(lo.astype(BF16).astype(F32), jnp.uint32)
    hi_b = lax.bitcast_convert_type(hi.astype(BF16).astype(F32), jnp.uint32)
    return (hi_b & jnp.uint32(0xFFFF0000)) | (lo_b >> 16)


def _unpack_bf16_pair(p):
    lo = lax.bitcast_convert_type(p << 16, F32)
    hi = lax.bitcast_convert_type(p & jnp.uint32(0xFFFF0000), F32)
    return lo, hi


def _store_rows(ref, packed):
    n = packed.shape[0]
    for q in range(ROW_SUB):
        ref[pl.ds(q, n, stride=ROW_SUB), :] = packed[:, q * 128:(q + 1) * 128]


def _load_rows(ref, n):
    los, his = [], []
    for q in range(ROW_SUB):
        lo, hi = _unpack_bf16_pair(ref[pl.ds(q, n, stride=ROW_SUB), :])
        los.append(lo)
        his.append(hi)
    return jnp.concatenate(los + his, axis=1)


def _mod_kernel(c_ref, w_ref, b_ref, o_ref):
    c = c_ref[...]
    c_act = c * jax.nn.sigmoid(c)
    o_ref[...] = jnp.dot(c_act, w_ref[...], preferred_element_type=F32,
                         precision=lax.Precision.HIGHEST) + b_ref[...]


def _mod_call(c, w, b, tn=1024):
    bsz, d = c.shape
    n = w.shape[1]
    return pl.pallas_call(
        _mod_kernel,
        grid=(n // tn,),
        in_specs=[pl.BlockSpec((bsz, d), lambda j: (0, 0)),
                  pl.BlockSpec((d, tn), lambda j: (0, j)),
                  pl.BlockSpec((1, tn), lambda j: (0, j))],
        out_specs=pl.BlockSpec((bsz, tn), lambda j: (0, j)),
        out_shape=jax.ShapeDtypeStruct((bsz, n), F32),
        compiler_params=pltpu.CompilerParams(dimension_semantics=("arbitrary",),
                                             vmem_limit_bytes=VMEM_LIMIT),
        name="adaln_mod",
    )(c, w, b.reshape(1, n))


def _mixer_in_kernel(x_ref, xh_ref, pos_ref, mod_ref, gmix_ref, win_ref, bgate_ref, wgrp_ref,
                     pscale_ref, wpo_ref, gq_ref, wq_ref, gkv_ref, wkn_ref, wvt_ref, invf_ref,
                     ga_ref, g1_ref, q_ref, k_ref, vt_ref, u_scr, *, tm, tiles_per_seq, d, q_lora,
                     kv_lora):
    i = pl.program_id(0)
    t_in_seq = i % tiles_per_seq
    is_start = t_in_seq == 0
    shift1 = mod_ref[0, 0:1, :]
    scale1 = mod_ref[0, 1:2, :]
    gmix = gmix_ref[...]

    def prenorm(xf):
        return (_rms(xf, gmix) * (1.0 + scale1) + shift1).astype(BF16)

    h = prenorm(x_ref[...])
    hh = prenorm(xh_ref[...])
    u = jnp.dot(h, win_ref[:, 0:d], preferred_element_type=F32)
    uh = jnp.dot(hh, win_ref[:, 0:d], preferred_element_type=F32)
    u_scr[0:POOL_HALO, :] = jnp.where(is_start, 0.0, uh)
    u_scr[POOL_HALO:POOL_HALO + tm, :] = u
    rest = jnp.dot(h, win_ref[:, d:], preferred_element_type=F32)

    gw = d // len(POOL_WINDOWS)
    tseq = t_in_seq * tm + lax.broadcasted_iota(jnp.int32, (tm, 1), 0)
    ys = []
    for g, w in enumerate(POOL_WINDOWS):
        c0 = g * gw
        ug = u_scr[POOL_HALO:POOL_HALO + tm, c0:c0 + gw]
        acc = ug
        for j in range(1, w):
            acc = acc + u_scr[POOL_HALO - j:POOL_HALO - j + tm, c0:c0 + gw]
        cnt = jnp.minimum(tseq + 1, w).astype(F32)
        mixed = (acc / cnt - ug).astype(BF16)
        ys.append(jnp.dot(mixed, wgrp_ref[g], preferred_element_type=F32))
    y = (jnp.concatenate(ys, axis=1) * pscale_ref[...]).astype(BF16)
    a = jnp.dot(y, wpo_ref[...], preferred_element_type=F32)

    o_q = 0
    o_kv = q_lora
    o_kpe = q_lora + kv_lora
    o_ksw = o_kpe + 128
    o_g0 = o_ksw + 128
    o_g1 = o_g0 + d
    gates0 = jax.nn.sigmoid(rest[:, o_g0:o_g0 + d] + bgate_ref[:, 0:d])
    gates1 = jax.nn.sigmoid(rest[:, o_g1:o_g1 + d] + bgate_ref[:, d:2 * d])
    ga_ref[...] = (gates0 * a).astype(BF16)
    g1_ref[...] = gates1.astype(BF16)

    ang = pos_ref[...] * invf_ref[...]
    cos2 = jnp.cos(ang)
    sin2 = jnp.sin(ang)
    sm_scale = float(QK_NOPE + QK_ROPE) ** -0.5

    qn = _rms(rest[:, o_q:o_q + q_lora], gq_ref[...]).astype(BF16)
    qall = jnp.dot(qn, wq_ref[...], preferred_element_type=F32)
    kvn = _rms(rest[:, o_kv:o_kv + kv_lora], gkv_ref[...]).astype(BF16)
    kn = jnp.dot(kvn, wkn_ref[...], preferred_element_type=F32)
    vt = lax.dot_general(wvt_ref[...], kvn, (((1,), (1,)), ((), ())),
                         preferred_element_type=F32)
    vt_ref[0] = vt.astype(BF16)
    kpe = (rest[:, o_kpe:o_kpe + 128] * cos2 + rest[:, o_ksw:o_ksw + 128] * sin2).astype(BF16)
    hn = N_HEADS * 128
    for hd in range(N_HEADS):
        c0 = hd * 128
        qpe = qall[:, hn + c0:hn + c0 + 128] * cos2 + qall[:, 2 * hn + c0:2 * hn + c0 + 128] * sin2
        q_ref[:, hd * HEAD_PAD:hd * HEAD_PAD + 128] = (qall[:, c0:c0 + 128] * sm_scale).astype(BF16)
        q_ref[:, hd * HEAD_PAD + 128:(hd + 1) * HEAD_PAD] = (qpe * sm_scale).astype(BF16)
        k_ref[:, hd * HEAD_PAD:hd * HEAD_PAD + 128] = kn[:, c0:c0 + 128].astype(BF16)
        k_ref[:, hd * HEAD_PAD + 128:(hd + 1) * HEAD_PAD] = kpe


def _mixer_in_call(x2d, pos_col, mod, g_mix, w_in_p, b_gate, w_grp, pool_scale, w_po, g_q_a, w_q_p,
                   g_kv_a, w_kn, w_vt, invf2, *, bsz, seq):
    t, d = x2d.shape
    tm = TM_IN
    tps = seq // tm
    q_lora = g_q_a.shape[-1]
    kv_lora = g_kv_a.shape[-1]
    hp = N_HEADS * HEAD_PAD
    halo_blocks = tm // POOL_HALO
    kern = functools.partial(_mixer_in_kernel, tm=tm, tiles_per_seq=tps, d=d, q_lora=q_lora,
                             kv_lora=kv_lora)
    row = lambda i: (i, 0)
    return pl.pallas_call(
        kern,
        grid=(t // tm,),
        in_specs=[
            pl.BlockSpec((tm, d), row),
            pl.BlockSpec((POOL_HALO, d), lambda i: (jnp.maximum(i * halo_blocks - 1, 0), 0)),
            pl.BlockSpec((tm, 1), row),
            pl.BlockSpec((1, 6, d), lambda i: (i // tps, 0, 0)),
            _const_spec((1, d)),
            _const_spec(w_in_p.shape),
            _const_spec((1, 2 * d)),
            _const_spec(w_grp.shape),
            _const_spec((1, d)),
            _const_spec(w_po.shape),
            _const_spec((1, q_lora)),
            _const_spec(w_q_p.shape),
            _const_spec((1, kv_lora)),
            _const_spec(w_kn.shape),
            _const_spec(w_vt.shape),
            _const_spec((1, 128)),
        ],
        out_specs=[
            pl.BlockSpec((tm, d), row),
            pl.BlockSpec((tm, d), row),
            pl.BlockSpec((tm, hp), row),
            pl.BlockSpec((tm, hp), row),
            pl.BlockSpec((1, N_HEADS * V_HEAD, tm), lambda i: (i // tps, 0, i % tps)),
        ],
        out_shape=[
            jax.ShapeDtypeStruct((t, d), BF16),
            jax.ShapeDtypeStruct((t, d), BF16),
            jax.ShapeDtypeStruct((t, hp), BF16),
            jax.ShapeDtypeStruct((t, hp), BF16),
            jax.ShapeDtypeStruct((bsz, N_HEADS * V_HEAD, seq), BF16),
        ],
        scratch_shapes=[pltpu.VMEM((tm + POOL_HALO, d), F32)],
        compiler_params=pltpu.CompilerParams(dimension_semantics=("arbitrary",),
                                             vmem_limit_bytes=VMEM_LIMIT),
        name="mixer_in",
    )(x2d, x2d, pos_col, mod, g_mix, w_in_p, b_gate, w_grp, pool_scale, w_po, g_q_a, w_q_p, g_kv_a,
      w_kn, w_vt, invf2)


def _attn_kernel(q_ref, k_ref, vt_ref, o_ref, *, tq, nq):
    for qi in range(nq):
        q0 = qi * tq
        ln = q0 + tq
        q = q_ref[0, q0:q0 + tq, :]
        s = lax.dot_general(k_ref[0, 0:ln, :], q, (((1,), (1,)), ((), ())),
                            preferred_element_type=F32)
        kc = (q0 + lax.broadcasted_iota(jnp.int32, (tq, tq), 0)) // CHUNK
        qc = (q0 + lax.broadcasted_iota(jnp.int32, (tq, tq), 1)) // CHUNK
        s_diag = jnp.where(qc >= kc, s[q0:ln, :], NEG_INF)
        m = jnp.max(s_diag, axis=0, keepdims=True)
        if qi > 0:
            s_main = s[0:q0, :]
            m = jnp.maximum(m, jnp.max(s_main, axis=0, keepdims=True))
        p_diag = jnp.exp(s_diag - m)
        l = jnp.sum(p_diag, axis=0, keepdims=True)
        acc = jnp.dot(vt_ref[0, :, q0:ln], p_diag.astype(BF16), preferred_element_type=F32)
        if qi > 0:
            p_main = jnp.exp(s_main - m)
            l = l + jnp.sum(p_main, axis=0, keepdims=True)
            acc = acc + jnp.dot(vt_ref[0, :, 0:q0], p_main.astype(BF16), preferred_element_type=F32)
        o_ref[0, q0:q0 + tq, :] = (acc / l).T.astype(BF16)


def _attn_call(q3, k3, vt3):
    bsz, seq, _ = q3.shape
    kern = functools.partial(_attn_kernel, tq=TQ, nq=seq // TQ)
    return pl.pallas_call(
        kern,
        grid=(bsz, N_HEADS),
        in_specs=[pl.BlockSpec((1, seq, HEAD_PAD), lambda b, h: (b, 0, h)),
                  pl.BlockSpec((1, seq, HEAD_PAD), lambda b, h: (b, 0, h)),
                  pl.BlockSpec((1, V_HEAD, seq), lambda b, h: (b, h, 0))],
        out_specs=pl.BlockSpec((1, seq, V_HEAD), lambda b, h: (b, 0, h)),
        out_shape=jax.ShapeDtypeStruct((bsz, seq, N_HEADS * V_HEAD), BF16),
        compiler_params=pltpu.CompilerParams(dimension_semantics=("arbitrary", "arbitrary"),
                                             vmem_limit_bytes=VMEM_LIMIT),
        name="mla_attn",
    )(q3, k3, vt3)


def _post_kernel(o_ref, ga_ref, g1_ref, x_ref, mod_ref, wmo_ref, wout_ref, gffn_ref, wrt_ref, br_ref,
                 x1_ref, h2_ref, idx_ref, wgt_ref, *, d):
    m = jnp.dot(o_ref[...], wmo_ref[...], preferred_element_type=F32)
    merged = ga_ref[...].astype(F32) + g1_ref[...].astype(F32) * m
    gate1 = mod_ref[0, 2:3, :]
    x1 = x_ref[...] + gate1 * jnp.dot(merged.astype(BF16), wout_ref[...],
                                      preferred_element_type=F32)
    x1_ref[...] = x1
    shift2 = mod_ref[0, 3:4, :]
    scale2 = mod_ref[0, 4:5, :]
    h2 = _rms(x1, gffn_ref[...]) * (1.0 + scale2) + shift2
    _store_rows(h2_ref, _pack_bf16_pair(h2[:, 0:d // 2], h2[:, d // 2:d]))

    ne = br_ref.shape[0]
    h_hi = h2.astype(BF16)
    h_lo = (h2 - h_hi.astype(F32)).astype(BF16)
    hcat = jnp.concatenate([h_hi, h_lo], axis=1)
    half = hcat.shape[0] // 2
    lg2 = jnp.concatenate(
        [jnp.dot(hcat[0:half], wrt_ref[...], preferred_element_type=F32),
         jnp.dot(hcat[half:], wrt_ref[...], preferred_element_type=F32)], axis=0)
    lg = lg2[:, 0:128] + lg2[:, 128:256]
    logits = lg.T[0:ne, :] + br_ref[...]
    tm = logits.shape[1]
    eid = lax.broadcasted_iota(jnp.int32, (ne, tm), 0)
    vals, idxs = [], []
    cur = logits
    for _ in range(TOP_K):
        mx = jnp.max(cur, axis=0, keepdims=True)
        ix = jnp.min(jnp.where(cur == mx, eid, ne), axis=0, keepdims=True)
        vals.append(mx)
        idxs.append(ix)
        cur = jnp.where(eid == ix, -jnp.inf, cur)
    es = [jnp.exp(v - vals[0]) for v in vals]
    den = es[0] + es[1] + es[2] + es[3]
    idx_ref[...] = jnp.concatenate(idxs, axis=0)
    wgt_ref[...] = jnp.concatenate([e / den for e in es], axis=0)


def _post_call(o2d, ga, g1, x2d, mod, w_mo, w_out, g_ffn, w_rt, b_r, *, seq):
    t, d = x2d.shape
    tm = TM_POST
    tps = seq // tm
    row = lambda i: (i, 0)
    kern = functools.partial(_post_kernel, d=d)
    return pl.pallas_call(
        kern,
        grid=(t // tm,),
        in_specs=[
            pl.BlockSpec((tm, d), row),
            pl.BlockSpec((tm, d), row),
            pl.BlockSpec((tm, d), row),
            pl.BlockSpec((tm, d), row),
            pl.BlockSpec((1, 6, d), lambda i: (i // tps, 0, 0)),
            _const_spec(w_mo.shape),
            _const_spec(w_out.shape),
            _const_spec((1, d)),
            _const_spec(w_rt.shape),
            _const_spec(b_r.shape),
        ],
        out_specs=[
            pl.BlockSpec((tm, d), row),
            pl.BlockSpec((tm * ROW_SUB, 128), row),
            pl.BlockSpec((TOP_K, tm), lambda i: (0, i)),
            pl.BlockSpec((TOP_K, tm), lambda i: (0, i)),
        ],
        out_shape=[
            jax.ShapeDtypeStruct((t, d), F32),
            jax.ShapeDtypeStruct((t * ROW_SUB, 128), jnp.uint32),
            jax.ShapeDtypeStruct((TOP_K, t), jnp.int32),
            jax.ShapeDtypeStruct((TOP_K, t), F32),
        ],
        compiler_params=pltpu.CompilerParams(dimension_semantics=("arbitrary",),
                                             vmem_limit_bytes=VMEM_LIMIT),
        name="post_attn_router",
    )(o2d, ga, g1, x2d, mod, w_mo, w_out, g_ffn, w_rt, b_r)


def _moe_kernel(blk_e_ref, src_cur_ref, src_nx1_ref, src_nx2_ref, dst_ref, h2_hbm, perm_ref, wgu_ref,
                bgu_ref, wd_ref, bd_ref, ys_hbm, xs_scr, y_scr, xb_scr, wgu_scr, wd_scr, gsem, ssem,
                *, blk, f):
    s = pl.program_id(0)
    last = pl.num_programs(0) - 1
    xslot = s % 3
    yslot = s % 2
    rows = blk * ROW_SUB

    def gather_copy(src_row, j, slot_):
        return pltpu.make_async_copy(h2_hbm.at[pl.ds(src_row, ROW_SUB)],
                                     xs_scr.at[slot_, pl.ds(j * ROW_SUB, ROW_SUB)], gsem.at[slot_])

    def scatter_copy(dst_row, j, slot_):
        return pltpu.make_async_copy(y_scr.at[slot_, pl.ds(j * ROW_SUB, ROW_SUB)],
                                     ys_hbm.at[pl.ds(dst_row, ROW_SUB)], ssem.at[slot_])

    def wait_gathers(slot_):
        pltpu.make_async_copy(h2_hbm.at[pl.ds(0, rows)], xs_scr.at[slot_], gsem.at[slot_]).wait()

    def wait_scatters(slot_):
        pltpu.make_async_copy(y_scr.at[slot_], ys_hbm.at[pl.ds(0, rows)], ssem.at[slot_]).wait()

    def gather_one(idx_ref, j, slot_, priority):
        gather_copy(pl.multiple_of(idx_ref[0, 0, j], ROW_SUB), j, slot_).start(priority=priority)

    def gather_looped(idx_ref, slot_):
        def body(jj, c):
            for r in range(2):
                gather_one(idx_ref, 2 * jj + r, slot_, r)
            return c
        lax.fori_loop(0, blk // 2, body, 0, unroll=4)

    @pl.when(s == 0)
    def _():
        gather_looped(src_cur_ref, 0)
        gather_looped(src_nx1_ref, 1)

    wait_gathers(xslot)

    @pl.when(s >= 2)
    def _():
        wait_scatters(yslot)

    e_cur = blk_e_ref[s]
    e_prev = blk_e_ref[jnp.maximum(s - 1, 0)]

    @pl.when(jnp.logical_or(s == 0, e_cur != e_prev))
    def _():
        perm = perm_ref[...]
        for c in range(2 * f // 256):
            r = jnp.dot(wgu_ref[0, :, c * 256:(c + 1) * 256].astype(BF16), perm,
                        preferred_element_type=F32)
            wgu_scr[:, c * 128:(c + 1) * 128] = r[:, 0:128].astype(BF16)
            wgu_scr[:, f + c * 128:f + (c + 1) * 128] = r[:, 128:256].astype(BF16)
        wd_scr[...] = wd_ref[0].astype(BF16)

    xb_scr[...] = _load_rows(xs_scr.at[xslot], blk).astype(BF16)
    gu = jnp.dot(xb_scr[...], wgu_scr[...], preferred_element_type=F32) + bgu_ref[0]
    gate = jnp.minimum(gu[:, 0:f], SWIGLU_LIMIT)
    up = jnp.clip(gu[:, f:2 * f], -SWIGLU_LIMIT, SWIGLU_LIMIT)
    act = (up + 1.0) * (gate * jax.nn.sigmoid(SWIGLU_ALPHA * gate))
    y = jnp.dot(act.astype(BF16), wd_scr[...], preferred_element_type=F32) + bd_ref[0]
    dh = y.shape[1] // 2
    _store_rows(y_scr.at[yslot], _pack_bf16_pair(y[:, 0:dh], y[:, dh:]))
    for j in range(blk):
        scatter_copy(pl.multiple_of(dst_ref[0, 0, j], ROW_SUB), j, yslot).start(priority=j % 2)
    for j in range(blk):
        gather_one(src_nx2_ref, j, (s + 2) % 3, j % 2)

    @pl.when(s == last)
    def _():
        wait_scatters(yslot)
        wait_scatters(1 - yslot)
        wait_gathers((s + 1) % 3)
        wait_gathers((s + 2) % 3)


def _moe_call(blk_e, src3, dst3, h2p, perm, w_gu, b_gu_p, w_down, b_down, *, n_out_rows):
    n_steps = src3.shape[0]
    assert n_steps >= 3
    blk = src3.shape[2]
    _, d, f2 = w_gu.shape
    f = f2 // 2
    rows = blk * ROW_SUB
    kern = functools.partial(_moe_kernel, blk=blk, f=f)
    smem_blk = lambda ahead: pl.BlockSpec(
        (1, 1, blk), lambda i, be: (jnp.minimum(i + ahead, n_steps - 1), 0, 0),
        memory_space=pltpu.SMEM)
    grid_spec = pltpu.PrefetchScalarGridSpec(
        num_scalar_prefetch=1,
        grid=(n_steps,),
        in_specs=[
            smem_blk(0),
            smem_blk(1),
            smem_blk(2),
            smem_blk(0),
            pl.BlockSpec(memory_space=pl.ANY),
            pl.BlockSpec((256, 256), lambda i, be: (0, 0)),
            pl.BlockSpec((1, d, f2), lambda i, be: (be[i], 0, 0)),
            pl.BlockSpec((1, 1, f2), lambda i, be: (be[i], 0, 0)),
            pl.BlockSpec((1, f, d), lambda i, be: (be[i], 0, 0)),
            pl.BlockSpec((1, 1, d), lambda i, be: (be[i], 0, 0)),
        ],
        out_specs=pl.BlockSpec(memory_space=pl.ANY),
        scratch_shapes=[
            pltpu.VMEM((3, rows, 128), jnp.uint32),
            pltpu.VMEM((2, rows, 128), jnp.uint32),
            pltpu.VMEM((blk, d), BF16),
            pltpu.VMEM((d, f2), BF16),
            pltpu.VMEM((f, d), BF16),
            pltpu.SemaphoreType.DMA((3,)),
            pltpu.SemaphoreType.DMA((2,)),
        ],
    )
    return pl.pallas_call(
        kern,
        grid_spec=grid_spec,
        out_shape=jax.ShapeDtypeStruct((n_out_rows, 128), jnp.uint32),
        compiler_params=pltpu.CompilerParams(dimension_semantics=("arbitrary",),
                                             vmem_limit_bytes=VMEM_LIMIT),
        name="moe_experts",
    )(blk_e, src3, src3, src3, dst3, h2p, perm, w_gu, b_gu_p, w_down, b_down)


def _final_kernel(x1_ref, y0_ref, y1_ref, y2_ref, y3_ref, w_ref, mod_ref, fmod_ref, gfin_ref, o_ref,
                  *, tm):
    w = w_ref[...]
    moe = None
    for k, y_ref in enumerate((y0_ref, y1_ref, y2_ref, y3_ref)):
        yk = _load_rows(y_ref, tm) * w[:, k:k + 1]
        moe = yk if moe is None else moe + yk
    gate2 = mod_ref[0, 5:6, :]
    x2 = x1_ref[...] + gate2 * moe
    fshift = fmod_ref[0, 0:1, :]
    fscale = fmod_ref[0, 1:2, :]
    o_ref[...] = _rms(x2, gfin_ref[...]) * (1.0 + fscale) + fshift


def _final_call(x1, ys, w_tok, mod, fmod, g_final, *, seq):
    t, d = x1.shape
    tm = TM_FIN
    tps = seq // tm
    nt = t // tm
    kern = functools.partial(_final_kernel, tm=tm)
    y_specs = [pl.BlockSpec((tm * ROW_SUB, 128), functools.partial(lambda i, k: (k * nt + i, 0), k=k))
               for k in range(TOP_K)]
    return pl.pallas_call(
        kern,
        grid=(nt,),
        in_specs=[pl.BlockSpec((tm, d), lambda i: (i, 0))] + y_specs + [
            pl.BlockSpec((tm, TOP_K), lambda i: (i, 0)),
            pl.BlockSpec((1, 6, d), lambda i: (i // tps, 0, 0)),
            pl.BlockSpec((1, 2, d), lambda i: (i // tps, 0, 0)),
            _const_spec((1, d)),
        ],
        out_specs=pl.BlockSpec((tm, d), lambda i: (i, 0)),
        out_shape=jax.ShapeDtypeStruct((t, d), F32),
        compiler_params=pltpu.CompilerParams(dimension_semantics=("arbitrary",),
                                             vmem_limit_bytes=VMEM_LIMIT),
        name="combine_final",
    )(x1, ys, ys, ys, ys, w_tok, mod, fmod, g_final)


def _prep_w_in(w_in, d, q_lora, kv_lora):
    o_kpe = d + q_lora + kv_lora
    o_g = o_kpe + QK_ROPE
    half = QK_ROPE // 2
    kpe = w_in[:, o_kpe:o_kpe + QK_ROPE]
    zpad = jnp.zeros((d, 128 - QK_ROPE), w_in.dtype)
    ksw = jnp.concatenate([-kpe[:, half:], kpe[:, :half]], axis=1)
    return jnp.concatenate([w_in[:, :o_kpe], kpe, zpad, ksw, zpad, w_in[:, o_g:]], axis=1).astype(BF16)


def _prep_w_q(w_q_b):
    ql = w_q_b.shape[0]
    hd = QK_NOPE + QK_ROPE
    half = QK_ROPE // 2
    w = w_q_b.reshape(ql, N_HEADS, hd)
    nope = w[:, :, :QK_NOPE]
    pe = w[:, :, QK_NOPE:]
    zpad = jnp.zeros((ql, N_HEADS, 128 - QK_ROPE), w.dtype)
    pe_p = jnp.concatenate([pe, zpad], axis=2)
    sw_p = jnp.concatenate([-pe[:, :, half:], pe[:, :, :half], zpad], axis=2)
    return jnp.concatenate([nope.reshape(ql, -1), pe_p.reshape(ql, -1), sw_p.reshape(ql, -1)],
                           axis=1).astype(BF16)


def _prep_w_kv(w_kv_b):
    kvl = w_kv_b.shape[0]
    w = w_kv_b.reshape(kvl, N_HEADS, QK_NOPE + V_HEAD)
    w_kn = w[:, :, :QK_NOPE].reshape(kvl, -1).astype(BF16)
    w_vt = w[:, :, QK_NOPE:].reshape(kvl, -1).T.astype(BF16)
    return w_kn, w_vt


def kernel(x, c, positions, w_mod, b_mod, g_mix, w_in, b_gate, w_pool_grp, pool_scale, w_pool_out,
           g_q_a, w_q_b, g_kv_a, w_kv_b, w_mla_out, w_out, g_ffn, w_router, b_router, w_gu, b_gu,
           w_down, b_down, g_final, w_fmod, b_fmod):
    bsz, seq, d = x.shape
    t = bsz * seq
    depth = w_mod.shape[0]
    assert depth == 1
    assert seq % TQ == 0 and seq % TM_IN == 0 and seq % TM_POST == 0 and seq % TM_FIN == 0
    assert t & (t - 1) == 0, "token count must be a power of two"
    q_lora = g_q_a.shape[-1]
    kv_lora = g_kv_a.shape[-1]
    n_exp = w_gu.shape[1]
    f = w_gu.shape[-1] // 2
    blk = MOE_BLK

    x2d = x.reshape(t, d)
    pos_col = positions.astype(F32).reshape(t, 1)
    inv_freq = 1.0 / (ROPE_THETA ** (jnp.arange(0, QK_ROPE, 2, dtype=F32) / QK_ROPE))
    invf2 = jnp.concatenate([inv_freq, inv_freq, jnp.zeros((128 - QK_ROPE,), F32)]).reshape(1, 128)

    mod = _mod_call(c, w_mod[0], b_mod[0]).reshape(bsz, 6, d)
    fmod = _mod_call(c, w_fmod, b_fmod).reshape(bsz, 2, d)

    w_in_p = _prep_w_in(w_in[0], d, q_lora, kv_lora)
    w_q_p = _prep_w_q(w_q_b[0])
    w_kn, w_vt = _prep_w_kv(w_kv_b[0])
    ga, g1, q2, k2, vt3 = _mixer_in_call(
        x2d, pos_col, mod, g_mix[0].reshape(1, d), w_in_p, b_gate[0].reshape(1, 2 * d),
        w_pool_grp[0].astype(BF16), pool_scale[0].reshape(1, d), w_pool_out[0].astype(BF16),
        g_q_a[0].reshape(1, q_lora), w_q_p, g_kv_a[0].reshape(1, kv_lora), w_kn, w_vt, invf2,
        bsz=bsz, seq=seq)

    hp = N_HEADS * HEAD_PAD
    o = _attn_call(q2.reshape(bsz, seq, hp), k2.reshape(bsz, seq, hp), vt3)

    w_r = w_router[0]
    w_r_hi = w_r.astype(BF16)
    w_r_lo = (w_r - w_r_hi.astype(F32)).astype(BF16)
    lane_pad = lambda a: jnp.pad(a, ((0, 0), (0, 128 - n_exp)))
    w_rt = jnp.concatenate(
        [jnp.concatenate([lane_pad(w_r_hi), lane_pad(w_r_lo)], axis=1),
         jnp.concatenate([lane_pad(w_r_hi), jnp.zeros((d, 128), BF16)], axis=1)], axis=0)
    x1, h2p, idx_t, wgt_t = _post_call(
        o.reshape(t, d), ga, g1, x2d, mod, w_mla_out[0].astype(BF16), w_out[0].astype(BF16),
        g_ffn[0].reshape(1, d), w_rt, b_router[0].reshape(n_exp, 1), seq=seq)

    n_slots = t * TOP_K
    n_rows = n_slots + n_exp * blk
    n_blocks = n_rows // blk
    flat_e = idx_t.reshape(-1)
    order = jnp.argsort(flat_e, stable=True).astype(jnp.int32)
    counts = jnp.sum(flat_e[:, None] == jnp.arange(n_exp, dtype=jnp.int32)[None, :], axis=0,
                     dtype=jnp.int32)
    padded = (counts + blk - 1) // blk * blk
    pad_end = jnp.cumsum(padded)
    pad_start = pad_end - padded
    grp_start = jnp.cumsum(counts) - counts
    blk_id = jnp.arange(n_blocks, dtype=jnp.int32)
    blk_start = blk_id * blk
    blk_e = jnp.minimum(jnp.sum(pad_end[None, :] <= blk_start[:, None], axis=1, dtype=jnp.int32),
                        n_exp - 1)
    off_in_e = blk_start - pad_start[blk_e]
    nvalid = jnp.clip(counts[blk_e] - off_in_e, 0, blk)
    j = jnp.arange(blk, dtype=jnp.int32)[None, :]
    src = (grp_start[blk_e] + off_in_e)[:, None] + j
    valid = j < nvalid[:, None]
    row_slot = order[jnp.clip(src, 0, n_slots - 1)]
    src_rows = jnp.where(valid, row_slot & (t - 1), 0) * ROW_SUB
    spare = n_slots + (blk_id[:, None] % 2) * blk + j
    dst_rows = jnp.where(valid, row_slot, spare) * ROW_SUB
    src3 = src_rows.reshape(n_blocks, 1, blk)
    dst3 = dst_rows.reshape(n_blocks, 1, blk)

    col = jnp.arange(256, dtype=jnp.int32)[None, :]
    row = jnp.arange(256, dtype=jnp.int32)[:, None]
    perm = (row == jnp.where(col < 128, 2 * col, 2 * (col - 128) + 1)).astype(BF16)
    b_gu_p = jnp.concatenate([b_gu[0][:, 0::2], b_gu[0][:, 1::2]], axis=-1).reshape(n_exp, 1, 2 * f)
    ys = _moe_call(blk_e, src3, dst3, h2p, perm, w_gu[0], b_gu_p, w_down[0],
                   b_down[0].reshape(n_exp, 1, d), n_out_rows=(n_slots + 2 * blk) * ROW_SUB)

    out = _final_call(x1, ys, wgt_t.T, mod, fmod, g_final.reshape(1, d), seq=seq)
    return out.reshape(bsz, seq, d)
```

```python
import functools

import jax
import jax.numpy as jnp
from jax import lax
from jax.experimental import pallas as pl
from jax.experimental.pallas import tpu as pltpu

F32 = jnp.float32
BF16 = jnp.bfloat16

CHUNK = 64
POOL_WINDOWS = (2, 4, 8, 16)
POOL_HALO = 16
N_HEADS = 8
QK_NOPE = 128
QK_ROPE = 64
V_HEAD = 128
HEAD_PAD = 256
ROPE_THETA = 10000.0
TOP_K = 4
SWIGLU_LIMIT = 7.0
SWIGLU_ALPHA = 1.702
NORM_EPS = 1e-6
NEG_INF = -1e30

VMEM_LIMIT = 56 * 1024 * 1024

TM_IN = 256
TQ = 512
TM_POST = 512
MOE_BLK = 256
TM_FIN = 256
ROW_SUB = 4


def _const_spec(shape):
    nd = len(shape)
    return pl.BlockSpec(shape, lambda *_: (0,) * nd, pipeline_mode=pl.Buffered(1))


def _rms(xf, g):
    return xf * lax.rsqrt(jnp.mean(xf * xf, axis=-1, keepdims=True) + NORM_EPS) * g


def _pack_bf16_pair(lo, hi):
    lo_b = lax.bitcast_convert_type(lo.astype(BF16).astype(F32), jnp.uint32)
    hi_b = lax.bitcast_convert_type(hi.astype(BF16).astype(F32), jnp.uint32)
    return (hi_b & jnp.uint32(0xFFFF0000)) | (lo_b >> 16)


def _unpack_bf16_pair(p):
    lo = lax.bitcast_convert_type(p << 16, F32)
    hi = lax.bitcast_convert_type(p & jnp.uint32(0xFFFF0000), F32)
    return lo, hi


def _store_rows(ref, packed):
    n = packed.shape[0]
    for q in range(ROW_SUB):
        ref[pl.ds(q, n, stride=ROW_SUB), :] = packed[:, q * 128:(q + 1) * 128]


def _load_rows(ref, n):
    los, his = [], []
    for q in range(ROW_SUB):
        lo, hi = _unpack_bf16_pair(ref[pl.ds(q, n, stride=ROW_SUB), :])
        los.append(lo)
        his.append(hi)
    return jnp.concatenate(los + his, axis=1)


def _mod_kernel(c_ref, w_ref, b_ref, o_ref):
    c = c_ref[...]
    c_act = c * jax.nn.sigmoid(c)
    o_ref[...] = jnp.dot(c_act, w_ref[...], preferred_element_type=F32,
                         precision=lax.Precision.HIGHEST) + b_ref[...]


def _mod_call(c, w, b, tn=1024):
    bsz, d = c.shape
    n = w.shape[1]
    return pl.pallas_call(
        _mod_kernel,
        grid=(n // tn,),
        in_specs=[pl.BlockSpec((bsz, d), lambda j: (0, 0)),
                  pl.BlockSpec((d, tn), lambda j: (0, j)),
                  pl.BlockSpec((1, tn), lambda j: (0, j))],
        out_specs=pl.BlockSpec((bsz, tn), lambda j: (0, j)),
        out_shape=jax.ShapeDtypeStruct((bsz, n), F32),
        compiler_params=pltpu.CompilerParams(dimension_semantics=("arbitrary",),
                                             vmem_limit_bytes=VMEM_LIMIT),
        name="adaln_mod",
    )(c, w, b.reshape(1, n))


def _mixer_in_kernel(x_ref, xh_ref, pos_ref, mod_ref, gmix_ref, win_ref, bgate_ref, wgrp_ref,
                     pscale_ref, wpo_ref, gq_ref, wq_ref, gkv_ref, wkn_ref, wvt_ref, invf_ref,
                     ga_ref, g1_ref, q_ref, k_ref, vt_ref, u_scr, *, tm, tiles_per_seq, d, q_lora,
                     kv_lora):
    i = pl.program_id(0)
    t_in_seq = i % tiles_per_seq
    is_start = t_in_seq == 0
    shift1 = mod_ref[0, 0:1, :]
    scale1 = mod_ref[0, 1:2, :]
    gmix = gmix_ref[...]

    def prenorm(xf):
        return (_rms(xf, gmix) * (1.0 + scale1) + shift1).astype(BF16)

    h = prenorm(x_ref[...])
    hh = prenorm(xh_ref[...])
    u = jnp.dot(h, win_ref[:, 0:d], preferred_element_type=F32)
    uh = jnp.dot(hh, win_ref[:, 0:d], preferred_element_type=F32)
    u_scr[0:POOL_HALO, :] = jnp.where(is_start, 0.0, uh)
    u_scr[POOL_HALO:POOL_HALO + tm, :] = u
    rest = jnp.dot(h, win_ref[:, d:], preferred_element_type=F32)

    gw = d // len(POOL_WINDOWS)
    tseq = t_in_seq * tm + lax.broadcasted_iota(jnp.int32, (tm, 1), 0)
    ys = []
    for g, w in enumerate(POOL_WINDOWS):
        c0 = g * gw
        ug = u_scr[POOL_HALO:POOL_HALO + tm, c0:c0 + gw]
        acc = ug
        for j in range(1, w):
            acc = acc + u_scr[POOL_HALO - j:POOL_HALO - j + tm, c0:c0 + gw]
        cnt = jnp.minimum(tseq + 1, w).astype(F32)
        mixed = (acc / cnt - ug).astype(BF16)
        ys.append(jnp.dot(mixed, wgrp_ref[g], preferred_element_type=F32))
    y = (jnp.concatenate(ys, axis=1) * pscale_ref[...]).astype(BF16)
    a = jnp.dot(y, wpo_ref[...], preferred_element_type=F32)

    o_q = 0
    o_kv = q_lora
    o_kpe = q_lora + kv_lora
    o_ksw = o_kpe + 128
    o_g0 = o_ksw + 128
    o_g1 = o_g0 + d
    gates0 = jax.nn.sigmoid(rest[:, o_g0:o_g0 + d] + bgate_ref[:, 0:d])
    gates1 = jax.nn.sigmoid(rest[:, o_g1:o_g1 + d] + bgate_ref[:, d:2 * d])
    ga_ref[...] = (gates0 * a).astype(BF16)
    g1_ref[...] = gates1.astype(BF16)

    ang = pos_ref[...] * invf_ref[...]
    cos2 = jnp.cos(ang)
    sin2 = jnp.sin(ang)
    sm_scale = float(QK_NOPE + QK_ROPE) ** -0.5

    qn = _rms(rest[:, o_q:o_q + q_lora], gq_ref[...]).astype(BF16)
    qall = jnp.dot(qn, wq_ref[...], preferred_element_type=F32)
    kvn = _rms(rest[:, o_kv:o_kv + kv_lora], gkv_ref[...]).astype(BF16)
    kn = jnp.dot(kvn, wkn_ref[...], preferred_element_type=F32)
    vt = lax.dot_general(wvt_ref[...], kvn, (((1,), (1,)), ((), ())),
                         preferred_element_type=F32)
    vt_ref[0] = vt.astype(BF16)
    kpe = (rest[:, o_kpe:o_kpe + 128] * cos2 + rest[:, o_ksw:o_ksw + 128] * sin2).astype(BF16)
    hn = N_HEADS * 128
    for hd in range(N_HEADS):
        c0 = hd * 128
        qpe = qall[:, hn + c0:hn + c0 + 128] * cos2 + qall[:, 2 * hn + c0:2 * hn + c0 + 128] * sin2
        q_ref[:, hd * HEAD_PAD:hd * HEAD_PAD + 128] = (qall[:, c0:c0 + 128] * sm_scale).astype(BF16)
        q_ref[:, hd * HEAD_PAD + 128:(hd + 1) * HEAD_PAD] = (qpe * sm_scale).astype(BF16)
        k_ref[:, hd * HEAD_PAD:hd * HEAD_PAD + 128] = kn[:, c0:c0 + 128].astype(BF16)
        k_ref[:, hd * HEAD_PAD + 128:(hd + 1) * HEAD_PAD] = kpe


def _mixer_in_call(x2d, pos_col, mod, g_mix, w_in_p, b_gate, w_grp, pool_scale, w_po, g_q_a, w_q_p,
                   g_kv_a, w_kn, w_vt, invf2, *, bsz, seq):
    t, d = x2d.shape
    tm = TM_IN
    tps = seq // tm
    q_lora = g_q_a.shape[-1]
    kv_lora = g_kv_a.shape[-1]
    hp = N_HEADS * HEAD_PAD
    halo_blocks = tm // POOL_HALO
    kern = functools.partial(_mixer_in_kernel, tm=tm, tiles_per_seq=tps, d=d, q_lora=q_lora,
                             kv_lora=kv_lora)
    row = lambda i: (i, 0)
    return pl.pallas_call(
        kern,
        grid=(t // tm,),
        in_specs=[
            pl.BlockSpec((tm, d), row),
            pl.BlockSpec((POOL_HALO, d), lambda i: (jnp.maximum(i * halo_blocks - 1, 0), 0)),
            pl.BlockSpec((tm, 1), row),
            pl.BlockSpec((1, 6, d), lambda i: (i // tps, 0, 0)),
            _const_spec((1, d)),
            _const_spec(w_in_p.shape),
            _const_spec((1, 2 * d)),
            _const_spec(w_grp.shape),
            _const_spec((1, d)),
            _const_spec(w_po.shape),
            _const_spec((1, q_lora)),
            _const_spec(w_q_p.shape),
            _const_spec((1, kv_lora)),
            _const_spec(w_kn.shape),
            _const_spec(w_vt.shape),
            _const_spec((1, 128)),
        ],
        out_specs=[
            pl.BlockSpec((tm, d), row),
            pl.BlockSpec((tm, d), row),
            pl.BlockSpec((tm, hp), row),
            pl.BlockSpec((tm, hp), row),
            pl.BlockSpec((1, N_HEADS * V_HEAD, tm), lambda i: (i // tps, 0, i % tps)),
        ],
        out_shape=[
            jax.ShapeDtypeStruct((t, d), BF16),
            jax.ShapeDtypeStruct((t, d), BF16),
            jax.ShapeDtypeStruct((t, hp), BF16),
            jax.ShapeDtypeStruct((t, hp), BF16),
            jax.ShapeDtypeStruct((bsz, N_HEADS * V_HEAD, seq), BF16),
        ],
        scratch_shapes=[pltpu.VMEM((tm + POOL_HALO, d), F32)],
        compiler_params=pltpu.CompilerParams(dimension_semantics=("arbitrary",),
                                             vmem_limit_bytes=VMEM_LIMIT),
        name="mixer_in",
    )(x2d, x2d, pos_col, mod, g_mix, w_in_p, b_gate, w_grp, pool_scale, w_po, g_q_a, w_q_p, g_kv_a,
      w_kn, w_vt, invf2)


def _attn_kernel(q_ref, k_ref, vt_ref, o_ref, *, tq, nq):
    for qi in range(nq):
        q0 = qi * tq
        ln = q0 + tq
        q = q_ref[0, q0:q0 + tq, :]
        s = lax.dot_general(k_ref[0, 0:ln, :], q, (((1,), (1,)), ((), ())),
                            preferred_element_type=F32)
        kc = (q0 + lax.broadcasted_iota(jnp.int32, (tq, tq), 0)) // CHUNK
        qc = (q0 + lax.broadcasted_iota(jnp.int32, (tq, tq), 1)) // CHUNK
        s_diag = jnp.where(qc >= kc, s[q0:ln, :], NEG_INF)
        m = jnp.max(s_diag, axis=0, keepdims=True)
        if qi > 0:
            s_main = s[0:q0, :]
            m = jnp.maximum(m, jnp.max(s_main, axis=0, keepdims=True))
        p_diag = jnp.exp(s_diag - m)
        l = jnp.sum(p_diag, axis=0, keepdims=True)
        acc = jnp.dot(vt_ref[0, :, q0:ln], p_diag.astype(BF16), preferred_element_type=F32)
        if qi > 0:
            p_main = jnp.exp(s_main - m)
            l = l + jnp.sum(p_main, axis=0, keepdims=True)
            acc = acc + jnp.dot(vt_ref[0, :, 0:q0], p_main.astype(BF16), preferred_element_type=F32)
        o_ref[0, q0:q0 + tq, :] = (acc / l).T.astype(BF16)


def _attn_call(q3, k3, vt3):
    bsz, seq, _ = q3.shape
    kern = functools.partial(_attn_kernel, tq=TQ, nq=seq // TQ)
    return pl.pallas_call(
        kern,
        grid=(bsz, N_HEADS),
        in_specs=[pl.BlockSpec((1, seq, HEAD_PAD), lambda b, h: (b, 0, h)),
                  pl.BlockSpec((1, seq, HEAD_PAD), lambda b, h: (b, 0, h)),
                  pl.BlockSpec((1, V_HEAD, seq), lambda b, h: (b, h, 0))],
        out_specs=pl.BlockSpec((1, seq, V_HEAD), lambda b, h: (b, 0, h)),
        out_shape=jax.ShapeDtypeStruct((bsz, seq, N_HEADS * V_HEAD), BF16),
        compiler_params=pltpu.CompilerParams(dimension_semantics=("arbitrary", "arbitrary"),
                                             vmem_limit_bytes=VMEM_LIMIT),
        name="mla_attn",
    )(q3, k3, vt3)


def _post_kernel(o_ref, ga_ref, g1_ref, x_ref, mod_ref, wmo_ref, wout_ref, gffn_ref, wrt_ref, br_ref,
                 x1_ref, h2_ref, idx_ref, wgt_ref, *, d):
    m = jnp.dot(o_ref[...], wmo_ref[...], preferred_element_type=F32)
    merged = ga_ref[...].astype(F32) + g1_ref[...].astype(F32) * m
    gate1 = mod_ref[0, 2:3, :]
    x1 = x_ref[...] + gate1 * jnp.dot(merged.astype(BF16), wout_ref[...],
                                      preferred_element_type=F32)
    x1_ref[...] = x1
    shift2 = mod_ref[0, 3:4, :]
    scale2 = mod_ref[0, 4:5, :]
    h2 = _rms(x1, gffn_ref[...]) * (1.0 + scale2) + shift2
    _store_rows(h2_ref, _pack_bf16_pair(h2[:, 0:d // 2], h2[:, d // 2:d]))

    ne = br_ref.shape[0]
    h_hi = h2.astype(BF16)
    h_lo = (h2 - h_hi.astype(F32)).astype(BF16)
    hcat = jnp.concatenate([h_hi, h_lo], axis=1)
    half = hcat.shape[0] // 2
    lg2 = jnp.concatenate(
        [jnp.dot(hcat[0:half], wrt_ref[...], preferred_element_type=F32),
         jnp.dot(hcat[half:], wrt_ref[...], preferred_element_type=F32)], axis=0)
    lg = lg2[:, 0:128] + lg2[:, 128:256]
    logits = lg.T[0:ne, :] + br_ref[...]
    tm = logits.shape[1]
    eid = lax.broadcasted_iota(jnp.int32, (ne, tm), 0)
    vals, idxs = [], []
    cur = logits
    for _ in range(TOP_K):
        mx = jnp.max(cur, axis=0, keepdims=True)
        ix = jnp.min(jnp.where(cur == mx, eid, ne), axis=0, keepdims=True)
        vals.append(mx)
        idxs.append(ix)
        cur = jnp.where(eid == ix, -jnp.inf, cur)
    es = [jnp.exp(v - vals[0]) for v in vals]
    den = es[0] + es[1] + es[2] + es[3]
    idx_ref[...] = jnp.concatenate(idxs, axis=0)
    wgt_ref[...] = jnp.concatenate([e / den for e in es], axis=0)


def _post_call(o2d, ga, g1, x2d, mod, w_mo, w_out, g_ffn, w_rt, b_r, *, seq):
    t, d = x2d.shape
    tm = TM_POST
    tps = seq // tm
    row = lambda i: (i, 0)
    kern = functools.partial(_post_kernel, d=d)
    return pl.pallas_call(
        kern,
        grid=(t // tm,),
        in_specs=[
            pl.BlockSpec((tm, d), row),
            pl.BlockSpec((tm, d), row),
            pl.BlockSpec((tm, d), row),
            pl.BlockSpec((tm, d), row),
            pl.BlockSpec((1, 6, d), lambda i: (i // tps, 0, 0)),
            _const_spec(w_mo.shape),
            _const_spec(w_out.shape),
            _const_spec((1, d)),
            _const_spec(w_rt.shape),
            _const_spec(b_r.shape),
        ],
        out_specs=[
            pl.BlockSpec((tm, d), row),
            pl.BlockSpec((tm * ROW_SUB, 128), row),
            pl.BlockSpec((TOP_K, tm), lambda i: (0, i)),
            pl.BlockSpec((TOP_K, tm), lambda i: (0, i)),
        ],
        out_shape=[
            jax.ShapeDtypeStruct((t, d), F32),
            jax.ShapeDtypeStruct((t * ROW_SUB, 128), jnp.uint32),
            jax.ShapeDtypeStruct((TOP_K, t), jnp.int32),
            jax.ShapeDtypeStruct((TOP_K, t), F32),
        ],
        compiler_params=pltpu.CompilerParams(dimension_semantics=("arbitrary",),
                                             vmem_limit_bytes=VMEM_LIMIT),
        name="post_attn_router",
    )(o2d, ga, g1, x2d, mod, w_mo, w_out, g_ffn, w_rt, b_r)


def _moe_kernel(blk_e_ref, src_cur_ref, src_nx1_ref, src_nx2_ref, h2_hbm, perm_ref, wgu_ref, bgu_ref,
                wd_ref, bd_ref, o_ref, xs_scr, xb_scr, wgu_scr, wd_scr, gsem, *, blk, f):
    s = pl.program_id(0)
    last = pl.num_programs(0) - 1
    xslot = s % 3
    rows = blk * ROW_SUB

    def wait_gathers(slot_):
        pltpu.make_async_copy(h2_hbm.at[pl.ds(0, rows)], xs_scr.at[slot_], gsem.at[slot_]).wait()

    def gather_one(idx_ref, j, slot_, priority):
        src_row = pl.multiple_of(idx_ref[0, 0, j], ROW_SUB)
        pltpu.make_async_copy(h2_hbm.at[pl.ds(src_row, ROW_SUB)],
                              xs_scr.at[slot_, pl.ds(j * ROW_SUB, ROW_SUB)],
                              gsem.at[slot_]).start(priority=priority)

    def gather_looped(idx_ref, slot_):
        def body(jj, c):
            for r in range(2):
                gather_one(idx_ref, 2 * jj + r, slot_, r)
            return c
        lax.fori_loop(0, blk // 2, body, 0, unroll=4)

    @pl.when(s == 0)
    def _():
        gather_looped(src_cur_ref, 0)
        gather_looped(src_nx1_ref, 1)

    wait_gathers(xslot)

    e_cur = blk_e_ref[s]
    e_prev = blk_e_ref[jnp.maximum(s - 1, 0)]

    @pl.when(jnp.logical_or(s == 0, e_cur != e_prev))
    def _():
        perm = perm_ref[...]
        for c in range(2 * f // 256):
            r = jnp.dot(wgu_ref[0, :, c * 256:(c + 1) * 256].astype(BF16), perm,
                        preferred_element_type=F32)
            wgu_scr[:, c * 128:(c + 1) * 128] = r[:, 0:128].astype(BF16)
            wgu_scr[:, f + c * 128:f + (c + 1) * 128] = r[:, 128:256].astype(BF16)
        wd_scr[...] = wd_ref[0].astype(BF16)

    xb_scr[...] = _load_rows(xs_scr.at[xslot], blk).astype(BF16)
    gu = jnp.dot(xb_scr[...], wgu_scr[...], preferred_element_type=F32) + bgu_ref[0]
    gate = jnp.minimum(gu[:, 0:f], SWIGLU_LIMIT)
    up = jnp.clip(gu[:, f:2 * f], -SWIGLU_LIMIT, SWIGLU_LIMIT)
    act = (up + 1.0) * (gate * jax.nn.sigmoid(SWIGLU_ALPHA * gate))
    y = jnp.dot(act.astype(BF16), wd_scr[...], preferred_element_type=F32) + bd_ref[0]
    dh = y.shape[1] // 2
    _store_rows(o_ref, _pack_bf16_pair(y[:, 0:dh], y[:, dh:]))
    for j in range(blk):
        gather_one(src_nx2_ref, j, (s + 2) % 3, j % 2)

    @pl.when(s == last)
    def _():
        wait_gathers((s + 1) % 3)
        wait_gathers((s + 2) % 3)


def _moe_call(blk_e, src3, h2p, perm, w_gu, b_gu_p, w_down, b_down):
    n_steps = src3.shape[0]
    assert n_steps >= 3
    blk = src3.shape[2]
    _, d, f2 = w_gu.shape
    f = f2 // 2
    rows = blk * ROW_SUB
    kern = functools.partial(_moe_kernel, blk=blk, f=f)
    smem_blk = lambda ahead: pl.BlockSpec(
        (1, 1, blk), lambda i, be: (jnp.minimum(i + ahead, n_steps - 1), 0, 0),
        memory_space=pltpu.SMEM)
    grid_spec = pltpu.PrefetchScalarGridSpec(
        num_scalar_prefetch=1,
        grid=(n_steps,),
        in_specs=[
            smem_blk(0),
            smem_blk(1),
            smem_blk(2),
            pl.BlockSpec(memory_space=pl.ANY),
            pl.BlockSpec((256, 256), lambda i, be: (0, 0)),
            pl.BlockSpec((1, d, f2), lambda i, be: (be[i], 0, 0)),
            pl.BlockSpec((1, 1, f2), lambda i, be: (be[i], 0, 0)),
            pl.BlockSpec((1, f, d), lambda i, be: (be[i], 0, 0)),
            pl.BlockSpec((1, 1, d), lambda i, be: (be[i], 0, 0)),
        ],
        out_specs=pl.BlockSpec((rows, 128), lambda i, be: (i, 0)),
        scratch_shapes=[
            pltpu.VMEM((3, rows, 128), jnp.uint32),
            pltpu.VMEM((blk, d), BF16),
            pltpu.VMEM((d, f2), BF16),
            pltpu.VMEM((f, d), BF16),
            pltpu.SemaphoreType.DMA((3,)),
        ],
    )
    return pl.pallas_call(
        kern,
        grid_spec=grid_spec,
        out_shape=jax.ShapeDtypeStruct((n_steps * rows, 128), jnp.uint32),
        compiler_params=pltpu.CompilerParams(dimension_semantics=("arbitrary",),
                                             vmem_limit_bytes=VMEM_LIMIT),
        name="moe_experts",
    )(blk_e, src3, src3, src3, h2p, perm, w_gu, b_gu_p, w_down, b_down)


RUN_BITS = 9


def _final_kernel(tab_cur_ref, tab_nxt_ref, x1_ref, tok_ref, w_ref, mod_ref, fmod_ref, gfin_ref, ys_hbm,
                  o_ref, stage, sem, *, tm, n_exp):
    i = pl.program_id(0)
    n = pl.num_programs(0)
    slot = i % 2
    n_rows = TOP_K * tm

    def issue_tile(tab_ref, slot_):
        def body(e, c):
            cnt = tab_ref[0, 0, e]
            src = tab_ref[0, 1, e]
            dst = tab_ref[0, 2, e]
            off = 0
            for b in reversed(range(RUN_BITS)):
                size = (1 << b) * ROW_SUB
                take = (cnt >> b) & 1

                @pl.when(take == 1)
                def _(off=off, size=size):
                    pltpu.make_async_copy(
                        ys_hbm.at[pl.ds(pl.multiple_of(src + off, ROW_SUB), size)],
                        stage.at[slot_, pl.ds(pl.multiple_of(dst + off, ROW_SUB), size)],
                        sem.at[slot_]).start()
                off = off + take * size
            return c
        lax.fori_loop(0, n_exp, body, 0)

    @pl.when(i == 0)
    def _():
        issue_tile(tab_cur_ref, 0)

    @pl.when(i + 1 < n)
    def _():
        issue_tile(tab_nxt_ref, 1 - slot)

    pltpu.make_async_copy(ys_hbm.at[pl.ds(0, n_rows * ROW_SUB)], stage.at[slot], sem.at[slot]).wait()

    y = _load_rows(stage.at[slot], n_rows).astype(BF16)
    tok = tok_ref[0]
    a = jnp.where(lax.broadcasted_iota(jnp.int32, (tm, n_rows), 0) == tok, w_ref[0], 0.0)
    a_hi = a.astype(BF16)
    a_lo = (a - a_hi.astype(F32)).astype(BF16)
    moe = (jnp.dot(a_hi, y, preferred_element_type=F32)
           + jnp.dot(a_lo, y, preferred_element_type=F32))
    gate2 = mod_ref[0, 5:6, :]
    x2 = x1_ref[...] + gate2 * moe
    fshift = fmod_ref[0, 0:1, :]
    fscale = fmod_ref[0, 1:2, :]
    o_ref[...] = _rms(x2, gfin_ref[...]) * (1.0 + fscale) + fshift


def _final_call(tabs, x1, stage_tok, stage_w, mod, fmod, g_final, ys, *, seq):
    t, d = x1.shape
    tm = TM_FIN
    assert tm < (1 << RUN_BITS)
    tps = seq // tm
    nt = t // tm
    n_exp = tabs.shape[2]
    n_rows = TOP_K * tm
    kern = functools.partial(_final_kernel, tm=tm, n_exp=n_exp)
    tab_spec = lambda ahead: pl.BlockSpec((1, 3, n_exp), lambda i: (jnp.minimum(i + ahead, nt - 1), 0, 0),
                                          memory_space=pltpu.SMEM)
    return pl.pallas_call(
        kern,
        grid=(nt,),
        in_specs=[
            tab_spec(0),
            tab_spec(1),
            pl.BlockSpec((tm, d), lambda i: (i, 0)),
            pl.BlockSpec((1, 1, n_rows), lambda i: (i, 0, 0)),
            pl.BlockSpec((1, 1, n_rows), lambda i: (i, 0, 0)),
            pl.BlockSpec((1, 6, d), lambda i: (i // tps, 0, 0)),
            pl.BlockSpec((1, 2, d), lambda i: (i // tps, 0, 0)),
            _const_spec((1, d)),
            pl.BlockSpec(memory_space=pl.ANY),
        ],
        out_specs=pl.BlockSpec((tm, d), lambda i: (i, 0)),
        out_shape=jax.ShapeDtypeStruct((t, d), F32),
        scratch_shapes=[pltpu.VMEM((2, n_rows * ROW_SUB, 128), jnp.uint32),
                        pltpu.SemaphoreType.DMA((2,))],
        compiler_params=pltpu.CompilerParams(dimension_semantics=("arbitrary",),
                                             vmem_limit_bytes=VMEM_LIMIT),
        name="combine_final",
    )(tabs, tabs, x1, stage_tok, stage_w, mod, fmod, g_final, ys)


def _prep_w_in(w_in, d, q_lora, kv_lora):
    o_kpe = d + q_lora + kv_lora
    o_g = o_kpe + QK_ROPE
    half = QK_ROPE // 2
    kpe = w_in[:, o_kpe:o_kpe + QK_ROPE]
    zpad = jnp.zeros((d, 128 - QK_ROPE), w_in.dtype)
    ksw = jnp.concatenate([-kpe[:, half:], kpe[:, :half]], axis=1)
    return jnp.concatenate([w_in[:, :o_kpe], kpe, zpad, ksw, zpad, w_in[:, o_g:]], axis=1).astype(BF16)


def _prep_w_q(w_q_b):
    ql = w_q_b.shape[0]
    hd = QK_NOPE + QK_ROPE
    half = QK_ROPE // 2
    w = w_q_b.reshape(ql, N_HEADS, hd)
    nope = w[:, :, :QK_NOPE]
    pe = w[:, :, QK_NOPE:]
    zpad = jnp.zeros((ql, N_HEADS, 128 - QK_ROPE), w.dtype)
    pe_p = jnp.concatenate([pe, zpad], axis=2)
    sw_p = jnp.concatenate([-pe[:, :, half:], pe[:, :, :half], zpad], axis=2)
    return jnp.concatenate([nope.reshape(ql, -1), pe_p.reshape(ql, -1), sw_p.reshape(ql, -1)],
                           axis=1).astype(BF16)


def _prep_w_kv(w_kv_b):
    kvl = w_kv_b.shape[0]
    w = w_kv_b.reshape(kvl, N_HEADS, QK_NOPE + V_HEAD)
    w_kn = w[:, :, :QK_NOPE].reshape(kvl, -1).astype(BF16)
    w_vt = w[:, :, QK_NOPE:].reshape(kvl, -1).T.astype(BF16)
    return w_kn, w_vt


def kernel(x, c, positions, w_mod, b_mod, g_mix, w_in, b_gate, w_pool_grp, pool_scale, w_pool_out,
           g_q_a, w_q_b, g_kv_a, w_kv_b, w_mla_out, w_out, g_ffn, w_router, b_router, w_gu, b_gu,
           w_down, b_down, g_final, w_fmod, b_fmod):
    bsz, seq, d = x.shape
    t = bsz * seq
    depth = w_mod.shape[0]
    assert depth == 1
    assert seq % TQ == 0 and seq % TM_IN == 0 and seq % TM_POST == 0 and seq % TM_FIN == 0
    q_lora = g_q_a.shape[-1]
    kv_lora = g_kv_a.shape[-1]
    n_exp = w_gu.shape[1]
    f = w_gu.shape[-1] // 2
    blk = MOE_BLK

    x2d = x.reshape(t, d)
    pos_col = positions.astype(F32).reshape(t, 1)
    inv_freq = 1.0 / (ROPE_THETA ** (jnp.arange(0, QK_ROPE, 2, dtype=F32) / QK_ROPE))
    invf2 = jnp.concatenate([inv_freq, inv_freq, jnp.zeros((128 - QK_ROPE,), F32)]).reshape(1, 128)

    mod = _mod_call(c, w_mod[0], b_mod[0]).reshape(bsz, 6, d)
    fmod = _mod_call(c, w_fmod, b_fmod).reshape(bsz, 2, d)

    w_in_p = _prep_w_in(w_in[0], d, q_lora, kv_lora)
    w_q_p = _prep_w_q(w_q_b[0])
    w_kn, w_vt = _prep_w_kv(w_kv_b[0])
    ga, g1, q2, k2, vt3 = _mixer_in_call(
        x2d, pos_col, mod, g_mix[0].reshape(1, d), w_in_p, b_gate[0].reshape(1, 2 * d),
        w_pool_grp[0].astype(BF16), pool_scale[0].reshape(1, d), w_pool_out[0].astype(BF16),
        g_q_a[0].reshape(1, q_lora), w_q_p, g_kv_a[0].reshape(1, kv_lora), w_kn, w_vt, invf2,
        bsz=bsz, seq=seq)

    hp = N_HEADS * HEAD_PAD
    o = _attn_call(q2.reshape(bsz, seq, hp), k2.reshape(bsz, seq, hp), vt3)

    w_r = w_router[0]
    w_r_hi = w_r.astype(BF16)
    w_r_lo = (w_r - w_r_hi.astype(F32)).astype(BF16)
    lane_pad = lambda a: jnp.pad(a, ((0, 0), (0, 128 - n_exp)))
    w_rt = jnp.concatenate(
        [jnp.concatenate([lane_pad(w_r_hi), lane_pad(w_r_lo)], axis=1),
         jnp.concatenate([lane_pad(w_r_hi), jnp.zeros((d, 128), BF16)], axis=1)], axis=0)
    x1, h2p, idx_t, wgt_t = _post_call(
        o.reshape(t, d), ga, g1, x2d, mod, w_mla_out[0].astype(BF16), w_out[0].astype(BF16),
        g_ffn[0].reshape(1, d), w_rt, b_router[0].reshape(n_exp, 1), seq=seq)

    n_slots = t * TOP_K
    n_rows = n_slots + n_exp * blk
    n_blocks = n_rows // blk
    flat_e = idx_t.T.reshape(-1)
    w_flat = wgt_t.T.reshape(-1)
    e_ids = jnp.arange(n_exp, dtype=jnp.int32)
    order = jnp.argsort(flat_e, stable=True).astype(jnp.int32)
    tt = TM_FIN
    n_tiles = t // tt
    cnt = jnp.sum(flat_e.reshape(n_tiles, tt * TOP_K, 1) == e_ids[None, None, :], axis=1,
                  dtype=jnp.int32)
    counts = jnp.sum(cnt, axis=0)
    padded = (counts + blk - 1) // blk * blk
    pad_end = jnp.cumsum(padded)
    pad_start = pad_end - padded
    grp_start = jnp.cumsum(counts) - counts
    blk_id = jnp.arange(n_blocks, dtype=jnp.int32)
    blk_start = blk_id * blk
    blk_e = jnp.minimum(jnp.sum(pad_end[None, :] <= blk_start[:, None], axis=1, dtype=jnp.int32),
                        n_exp - 1)
    off_in_e = blk_start - pad_start[blk_e]
    nvalid = jnp.clip(counts[blk_e] - off_in_e, 0, blk)
    j = jnp.arange(blk, dtype=jnp.int32)[None, :]
    src = (grp_start[blk_e] + off_in_e)[:, None] + j
    valid = j < nvalid[:, None]
    row_slot = order[jnp.clip(src, 0, n_slots - 1)]
    src3 = (jnp.where(valid, row_slot // TOP_K, 0) * ROW_SUB).reshape(n_blocks, 1, blk)

    before = jnp.cumsum(cnt, axis=0) - cnt
    run_end = jnp.cumsum(cnt, axis=1)
    run_start = run_end - cnt
    tabs = jnp.stack([cnt, (pad_start[None, :] + before) * ROW_SUB, run_start * ROW_SUB], axis=1)
    q = jnp.arange(tt * TOP_K, dtype=jnp.int32)[None, :]
    e_q = jnp.minimum(jnp.sum(run_end[:, None, :] <= q[:, :, None], axis=2, dtype=jnp.int32), n_exp - 1)
    take = lambda tab: jnp.take_along_axis(tab, e_q, axis=1)
    sorted_idx = grp_start[e_q] + take(before) + (q - take(run_start))
    stage_slot = order[sorted_idx]
    stage_tok = stage_slot // TOP_K - jnp.arange(n_tiles, dtype=jnp.int32)[:, None] * tt
    stage_w = w_flat[stage_slot]

    col = jnp.arange(256, dtype=jnp.int32)[None, :]
    row = jnp.arange(256, dtype=jnp.int32)[:, None]
    perm = (row == jnp.where(col < 128, 2 * col, 2 * (col - 128) + 1)).astype(BF16)
    b_gu_p = jnp.concatenate([b_gu[0][:, 0::2], b_gu[0][:, 1::2]], axis=-1).reshape(n_exp, 1, 2 * f)
    ys = _moe_call(blk_e, src3, h2p, perm, w_gu[0], b_gu_p, w_down[0], b_down[0].reshape(n_exp, 1, d))

    out = _final_call(tabs, x1, stage_tok.reshape(n_tiles, 1, tt * TOP_K),
                      stage_w.reshape(n_tiles, 1, tt * TOP_K), mod, fmod, g_final.reshape(1, d), ys,
                      seq=seq)
    return out.reshape(bsz, seq, d)
```

```python
import functools

import jax
import jax.numpy as jnp
from jax import lax
from jax.experimental import pallas as pl
from jax.experimental.pallas import tpu as pltpu

F32 = jnp.float32
BF16 = jnp.bfloat16

CHUNK = 64
POOL_WINDOWS = (2, 4, 8, 16)
POOL_HALO = 16
N_HEADS = 8
QK_NOPE = 128
QK_ROPE = 64
V_HEAD = 128
HEAD_PAD = 256
ROPE_THETA = 10000.0
TOP_K = 4
SWIGLU_LIMIT = 7.0
SWIGLU_ALPHA = 1.702
NORM_EPS = 1e-6
NEG_INF = -1e30

VMEM_LIMIT = 56 * 1024 * 1024

TM_IN = 256
TQ = 512
TM_POST = 512
MOE_BLK = 256
TM_FIN = 256
ROW_SUB = 4


def _const_spec(shape):
    nd = len(shape)
    return pl.BlockSpec(shape, lambda *_: (0,) * nd, pipeline_mode=pl.Buffered(1))


def _rms(xf, g):
    return xf * lax.rsqrt(jnp.mean(xf * xf, axis=-1, keepdims=True) + NORM_EPS) * g


def _pack_bf16_pair(lo, hi):
    lo_b = lax.bitcast_convert_type(lo.astype(BF16).astype(F32), jnp.uint32)
    hi_b = lax.bitcast_convert_type(hi.astype(BF16).astype(F32), jnp.uint32)
    return (hi_b & jnp.uint32(0xFFFF0000)) | (lo_b >> 16)


def _unpack_bf16_pair(p):
    lo = lax.bitcast_convert_type(p << 16, F32)
    hi = lax.bitcast_convert_type(p & jnp.uint32(0xFFFF0000), F32)
    return lo, hi


def _store_rows(ref, packed):
    n = packed.shape[0]
    for q in range(ROW_SUB):
        ref[pl.ds(q, n, stride=ROW_SUB), :] = packed[:, q * 128:(q + 1) * 128]


def _load_rows(ref, n):
    los, his = [], []
    for q in range(ROW_SUB):
        lo, hi = _unpack_bf16_pair(ref[pl.ds(q, n, stride=ROW_SUB), :])
        los.append(lo)
        his.append(hi)
    return jnp.concatenate(los + his, axis=1)


def _mod_kernel(c_ref, w_ref, b_ref, o_ref):
    c = c_ref[...]
    c_act = c * jax.nn.sigmoid(c)
    o_ref[...] = jnp.dot(c_act, w_ref[...], preferred_element_type=F32,
                         precision=lax.Precision.HIGHEST) + b_ref[...]


def _mod_call(c, w, b, tn=1024):
    bsz, d = c.shape
    n = w.shape[1]
    return pl.pallas_call(
        _mod_kernel,
        grid=(n // tn,),
        in_specs=[pl.BlockSpec((bsz, d), lambda j: (0, 0)),
                  pl.BlockSpec((d, tn), lambda j: (0, j)),
                  pl.BlockSpec((1, tn), lambda j: (0, j))],
        out_specs=pl.BlockSpec((bsz, tn), lambda j: (0, j)),
        out_shape=jax.ShapeDtypeStruct((bsz, n), F32),
        compiler_params=pltpu.CompilerParams(dimension_semantics=("arbitrary",),
                                             vmem_limit_bytes=VMEM_LIMIT),
        name="adaln_mod",
    )(c, w, b.reshape(1, n))


def _mixer_in_kernel(x_ref, xh_ref, pos_ref, mod_ref, gmix_ref, win_ref, bgate_ref, wgrp_ref,
                     pscale_ref, wpo_ref, gq_ref, wq_ref, gkv_ref, wkn_ref, wvt_ref, invf_ref,
                     ga_ref, g1_ref, q_ref, k_ref, vt_ref, u_scr, *, tm, tiles_per_seq, d, q_lora,
                     kv_lora):
    i = pl.program_id(0)
    t_in_seq = i % tiles_per_seq
    is_start = t_in_seq == 0
    shift1 = mod_ref[0, 0:1, :]
    scale1 = mod_ref[0, 1:2, :]
    gmix = gmix_ref[...]

    def prenorm(xf):
        return (_rms(xf, gmix) * (1.0 + scale1) + shift1).astype(BF16)

    h = prenorm(x_ref[...])
    hh = prenorm(xh_ref[...])
    u = jnp.dot(h, win_ref[:, 0:d], preferred_element_type=F32)
    uh = jnp.dot(hh, win_ref[:, 0:d], preferred_element_type=F32)
    u_scr[0:POOL_HALO, :] = jnp.where(is_start, 0.0, uh)
    u_scr[POOL_HALO:POOL_HALO + tm, :] = u
    rest = jnp.dot(h, win_ref[:, d:], preferred_element_type=F32)

    gw = d // len(POOL_WINDOWS)
    tseq = t_in_seq * tm + lax.broadcasted_iota(jnp.int32, (tm, 1), 0)
    ys = []
    for g, w in enumerate(POOL_WINDOWS):
        c0 = g * gw
        ug = u_scr[POOL_HALO:POOL_HALO + tm, c0:c0 + gw]
        acc = ug
        for j in range(1, w):
            acc = acc + u_scr[POOL_HALO - j:POOL_HALO - j + tm, c0:c0 + gw]
        cnt = jnp.minimum(tseq + 1, w).astype(F32)
        mixed = (acc / cnt - ug).astype(BF16)
        ys.append(jnp.dot(mixed, wgrp_ref[g], preferred_element_type=F32))
    y = (jnp.concatenate(ys, axis=1) * pscale_ref[...]).astype(BF16)
    a = jnp.dot(y, wpo_ref[...], preferred_element_type=F32)

    o_q = 0
    o_kv = q_lora
    o_kpe = q_lora + kv_lora
    o_ksw = o_kpe + 128
    o_g0 = o_ksw + 128
    o_g1 = o_g0 + d
    gates0 = jax.nn.sigmoid(rest[:, o_g0:o_g0 + d] + bgate_ref[:, 0:d])
    gates1 = jax.nn.sigmoid(rest[:, o_g1:o_g1 + d] + bgate_ref[:, d:2 * d])
    ga_ref[...] = (gates0 * a).astype(BF16)
    g1_ref[...] = gates1.astype(BF16)

    ang = pos_ref[...] * invf_ref[...]
    cos2 = jnp.cos(ang)
    sin2 = jnp.sin(ang)
    sm_scale = float(QK_NOPE + QK_ROPE) ** -0.5

    qn = _rms(rest[:, o_q:o_q + q_lora], gq_ref[...]).astype(BF16)
    qall = jnp.dot(qn, wq_ref[...], preferred_element_type=F32)
    kvn = _rms(rest[:, o_kv:o_kv + kv_lora], gkv_ref[...]).astype(BF16)
    kn = jnp.dot(kvn, wkn_ref[...], preferred_element_type=F32)
    vt = lax.dot_general(wvt_ref[...], kvn, (((1,), (1,)), ((), ())),
                         preferred_element_type=F32)
    vt_ref[0] = vt.astype(BF16)
    kpe = (rest[:, o_kpe:o_kpe + 128] * cos2 + rest[:, o_ksw:o_ksw + 128] * sin2).astype(BF16)
    hn = N_HEADS * 128
    for hd in range(N_HEADS):
        c0 = hd * 128
        qpe = qall[:, hn + c0:hn + c0 + 128] * cos2 + qall[:, 2 * hn + c0:2 * hn + c0 + 128] * sin2
        q_ref[:, hd * HEAD_PAD:hd * HEAD_PAD + 128] = (qall[:, c0:c0 + 128] * sm_scale).astype(BF16)
        q_ref[:, hd * HEAD_PAD + 128:(hd + 1) * HEAD_PAD] = (qpe * sm_scale).astype(BF16)
        k_ref[:, hd * HEAD_PAD:hd * HEAD_PAD + 128] = kn[:, c0:c0 + 128].astype(BF16)
        k_ref[:, hd * HEAD_PAD + 128:(hd + 1) * HEAD_PAD] = kpe


def _mixer_in_call(x2d, pos_col, mod, g_mix, w_in_p, b_gate, w_grp, pool_scale, w_po, g_q_a, w_q_p,
                   g_kv_a, w_kn, w_vt, invf2, *, bsz, seq):
    t, d = x2d.shape
    tm = TM_IN
    tps = seq // tm
    q_lora = g_q_a.shape[-1]
    kv_lora = g_kv_a.shape[-1]
    hp = N_HEADS * HEAD_PAD
    halo_blocks = tm // POOL_HALO
    kern = functools.partial(_mixer_in_kernel, tm=tm, tiles_per_seq=tps, d=d, q_lora=q_lora,
                             kv_lora=kv_lora)
    row = lambda i: (i, 0)
    return pl.pallas_call(
        kern,
        grid=(t // tm,),
        in_specs=[
            pl.BlockSpec((tm, d), row),
            pl.BlockSpec((POOL_HALO, d), lambda i: (jnp.maximum(i * halo_blocks - 1, 0), 0)),
            pl.BlockSpec((tm, 1), row),
            pl.BlockSpec((1, 6, d), lambda i: (i // tps, 0, 0)),
            _const_spec((1, d)),
            _const_spec(w_in_p.shape),
            _const_spec((1, 2 * d)),
            _const_spec(w_grp.shape),
            _const_spec((1, d)),
            _const_spec(w_po.shape),
            _const_spec((1, q_lora)),
            _const_spec(w_q_p.shape),
            _const_spec((1, kv_lora)),
            _const_spec(w_kn.shape),
            _const_spec(w_vt.shape),
            _const_spec((1, 128)),
        ],
        out_specs=[
            pl.BlockSpec((tm, d), row),
            pl.BlockSpec((tm, d), row),
            pl.BlockSpec((tm, hp), row),
            pl.BlockSpec((tm, hp), row),
            pl.BlockSpec((1, N_HEADS * V_HEAD, tm), lambda i: (i // tps, 0, i % tps)),
        ],
        out_shape=[
            jax.ShapeDtypeStruct((t, d), BF16),
            jax.ShapeDtypeStruct((t, d), BF16),
            jax.ShapeDtypeStruct((t, hp), BF16),
            jax.ShapeDtypeStruct((t, hp), BF16),
            jax.ShapeDtypeStruct((bsz, N_HEADS * V_HEAD, seq), BF16),
        ],
        scratch_shapes=[pltpu.VMEM((tm + POOL_HALO, d), F32)],
        compiler_params=pltpu.CompilerParams(dimension_semantics=("arbitrary",),
                                             vmem_limit_bytes=VMEM_LIMIT),
        name="mixer_in",
    )(x2d, x2d, pos_col, mod, g_mix, w_in_p, b_gate, w_grp, pool_scale, w_po, g_q_a, w_q_p, g_kv_a,
      w_kn, w_vt, invf2)


def _attn_kernel(q_ref, k_ref, vt_ref, o_ref, *, tq, nq):
    for qi in range(nq):
        q0 = qi * tq
        ln = q0 + tq
        q = q_ref[0, q0:q0 + tq, :]
        s = lax.dot_general(k_ref[0, 0:ln, :], q, (((1,), (1,)), ((), ())),
                            preferred_element_type=F32)
        kc = (q0 + lax.broadcasted_iota(jnp.int32, (tq, tq), 0)) // CHUNK
        qc = (q0 + lax.broadcasted_iota(jnp.int32, (tq, tq), 1)) // CHUNK
        s_diag = jnp.where(qc >= kc, s[q0:ln, :], NEG_INF)
        m = jnp.max(s_diag, axis=0, keepdims=True)
        if qi > 0:
            s_main = s[0:q0, :]
            m = jnp.maximum(m, jnp.max(s_main, axis=0, keepdims=True))
        p_diag = jnp.exp(s_diag - m)
        l = jnp.sum(p_diag, axis=0, keepdims=True)
        acc = jnp.dot(vt_ref[0, :, q0:ln], p_diag.astype(BF16), preferred_element_type=F32)
        if qi > 0:
            p_main = jnp.exp(s_main - m)
            l = l + jnp.sum(p_main, axis=0, keepdims=True)
            acc = acc + jnp.dot(vt_ref[0, :, 0:q0], p_main.astype(BF16), preferred_element_type=F32)
        o_ref[0, q0:q0 + tq, :] = (acc / l).T.astype(BF16)


def _attn_call(q3, k3, vt3):
    bsz, seq, _ = q3.shape
    kern = functools.partial(_attn_kernel, tq=TQ, nq=seq // TQ)
    return pl.pallas_call(
        kern,
        grid=(bsz, N_HEADS),
        in_specs=[pl.BlockSpec((1, seq, HEAD_PAD), lambda b, h: (b, 0, h)),
                  pl.BlockSpec((1, seq, HEAD_PAD), lambda b, h: (b, 0, h)),
                  pl.BlockSpec((1, V_HEAD, seq), lambda b, h: (b, h, 0))],
        out_specs=pl.BlockSpec((1, seq, V_HEAD), lambda b, h: (b, 0, h)),
        out_shape=jax.ShapeDtypeStruct((bsz, seq, N_HEADS * V_HEAD), BF16),
        compiler_params=pltpu.CompilerParams(dimension_semantics=("arbitrary", "arbitrary"),
                                             vmem_limit_bytes=VMEM_LIMIT),
        name="mla_attn",
    )(q3, k3, vt3)


def _post_kernel(o_ref, ga_ref, g1_ref, x_ref, mod_ref, wmo_ref, wout_ref, gffn_ref, wrt_ref, br_ref,
                 x1_ref, h2_ref, idx_ref, wgt_ref, *, d):
    m = jnp.dot(o_ref[...], wmo_ref[...], preferred_element_type=F32)
    merged = ga_ref[...].astype(F32) + g1_ref[...].astype(F32) * m
    gate1 = mod_ref[0, 2:3, :]
    x1 = x_ref[...] + gate1 * jnp.dot(merged.astype(BF16), wout_ref[...],
                                      preferred_element_type=F32)
    x1_ref[...] = x1
    shift2 = mod_ref[0, 3:4, :]
    scale2 = mod_ref[0, 4:5, :]
    h2 = _rms(x1, gffn_ref[...]) * (1.0 + scale2) + shift2
    _store_rows(h2_ref, _pack_bf16_pair(h2[:, 0:d // 2], h2[:, d // 2:d]))

    ne = br_ref.shape[0]
    h_hi = h2.astype(BF16)
    h_lo = (h2 - h_hi.astype(F32)).astype(BF16)
    hcat = jnp.concatenate([h_hi, h_lo], axis=1)
    half = hcat.shape[0] // 2
    lg2 = jnp.concatenate(
        [jnp.dot(hcat[0:half], wrt_ref[...], preferred_element_type=F32),
         jnp.dot(hcat[half:], wrt_ref[...], preferred_element_type=F32)], axis=0)
    lg = lg2[:, 0:128] + lg2[:, 128:256]
    logits = lg.T[0:ne, :] + br_ref[...]
    tm = logits.shape[1]
    eid = lax.broadcasted_iota(jnp.int32, (ne, tm), 0)
    vals, idxs = [], []
    cur = logits
    for _ in range(TOP_K):
        mx = jnp.max(cur, axis=0, keepdims=True)
        ix = jnp.min(jnp.where(cur == mx, eid, ne), axis=0, keepdims=True)
        vals.append(mx)
        idxs.append(ix)
        cur = jnp.where(eid == ix, -jnp.inf, cur)
    es = [jnp.exp(v - vals[0]) for v in vals]
    den = es[0] + es[1] + es[2] + es[3]
    idx_ref[...] = jnp.concatenate(idxs, axis=0)
    wgt_ref[...] = jnp.concatenate([e / den for e in es], axis=0)


def _post_call(o2d, ga, g1, x2d, mod, w_mo, w_out, g_ffn, w_rt, b_r, *, seq):
    t, d = x2d.shape
    tm = TM_POST
    tps = seq // tm
    row = lambda i: (i, 0)
    kern = functools.partial(_post_kernel, d=d)
    return pl.pallas_call(
        kern,
        grid=(t // tm,),
        in_specs=[
            pl.BlockSpec((tm, d), row),
            pl.BlockSpec((tm, d), row),
            pl.BlockSpec((tm, d), row),
            pl.BlockSpec((tm, d), row),
            pl.BlockSpec((1, 6, d), lambda i: (i // tps, 0, 0)),
            _const_spec(w_mo.shape),
            _const_spec(w_out.shape),
            _const_spec((1, d)),
            _const_spec(w_rt.shape),
            _const_spec(b_r.shape),
        ],
        out_specs=[
            pl.BlockSpec((tm, d), row),
            pl.BlockSpec((tm * ROW_SUB, 128), row),
            pl.BlockSpec((TOP_K, tm), lambda i: (0, i)),
            pl.BlockSpec((TOP_K, tm), lambda i: (0, i)),
        ],
        out_shape=[
            jax.ShapeDtypeStruct((t, d), F32),
            jax.ShapeDtypeStruct((t * ROW_SUB, 128), jnp.uint32),
            jax.ShapeDtypeStruct((TOP_K, t), jnp.int32),
            jax.ShapeDtypeStruct((TOP_K, t), F32),
        ],
        compiler_params=pltpu.CompilerParams(dimension_semantics=("arbitrary",),
                                             vmem_limit_bytes=VMEM_LIMIT),
        name="post_attn_router",
    )(o2d, ga, g1, x2d, mod, w_mo, w_out, g_ffn, w_rt, b_r)


def _moe_kernel(blk_e_ref, src_cur_ref, src_nx1_ref, src_nx2_ref, h2_hbm, perm_ref, wgu_ref, bgu_ref,
                wd_ref, bd_ref, o_ref, xs_scr, xb_scr, wgu_scr, wd_scr, gsem, *, blk, f):
    s = pl.program_id(0)
    last = pl.num_programs(0) - 1
    xslot = s % 3
    rows = blk * ROW_SUB

    def wait_gathers(slot_):
        pltpu.make_async_copy(h2_hbm.at[pl.ds(0, rows)], xs_scr.at[slot_], gsem.at[slot_]).wait()

    def gather_one(idx_ref, j, slot_, priority):
        src_row = pl.multiple_of(idx_ref[0, 0, j], ROW_SUB)
        pltpu.make_async_copy(h2_hbm.at[pl.ds(src_row, ROW_SUB)],
                              xs_scr.at[slot_, pl.ds(j * ROW_SUB, ROW_SUB)],
                              gsem.at[slot_]).start(priority=priority)

    def gather_looped(idx_ref, slot_):
        def body(jj, c):
            for r in range(2):
                gather_one(idx_ref, 2 * jj + r, slot_, r)
            return c
        lax.fori_loop(0, blk // 2, body, 0, unroll=4)

    @pl.when(s == 0)
    def _():
        gather_looped(src_cur_ref, 0)
        gather_looped(src_nx1_ref, 1)

    wait_gathers(xslot)

    e_cur = blk_e_ref[s]
    e_prev = blk_e_ref[jnp.maximum(s - 1, 0)]

    @pl.when(jnp.logical_or(s == 0, e_cur != e_prev))
    def _():
        perm = perm_ref[...]
        for c in range(2 * f // 256):
            r = jnp.dot(wgu_ref[0, :, c * 256:(c + 1) * 256].astype(BF16), perm,
                        preferred_element_type=F32)
            wgu_scr[:, c * 128:(c + 1) * 128] = r[:, 0:128].astype(BF16)
            wgu_scr[:, f + c * 128:f + (c + 1) * 128] = r[:, 128:256].astype(BF16)
        wd_scr[...] = wd_ref[0].astype(BF16)

    xb_scr[...] = _load_rows(xs_scr.at[xslot], blk).astype(BF16)
    gu = jnp.dot(xb_scr[...], wgu_scr[...], preferred_element_type=F32) + bgu_ref[0]
    gate = jnp.minimum(gu[:, 0:f], SWIGLU_LIMIT)
    up = jnp.clip(gu[:, f:2 * f], -SWIGLU_LIMIT, SWIGLU_LIMIT)
    act = (up + 1.0) * (gate * jax.nn.sigmoid(SWIGLU_ALPHA * gate))
    y = jnp.dot(act.astype(BF16), wd_scr[...], preferred_element_type=F32) + bd_ref[0]
    dh = y.shape[1] // 2
    _store_rows(o_ref, _pack_bf16_pair(y[:, 0:dh], y[:, dh:]))
    for j in range(blk):
        gather_one(src_nx2_ref, j, (s + 2) % 3, j % 2)

    @pl.when(s == last)
    def _():
        wait_gathers((s + 1) % 3)
        wait_gathers((s + 2) % 3)


def _moe_call(blk_e, src3, h2p, perm, w_gu, b_gu_p, w_down, b_down):
    n_steps = src3.shape[0]
    assert n_steps >= 3
    blk = src3.shape[2]
    _, d, f2 = w_gu.shape
    f = f2 // 2
    rows = blk * ROW_SUB
    kern = functools.partial(_moe_kernel, blk=blk, f=f)
    smem_blk = lambda ahead: pl.BlockSpec(
        (1, 1, blk), lambda i, be: (jnp.minimum(i + ahead, n_steps - 1), 0, 0),
        memory_space=pltpu.SMEM)
    grid_spec = pltpu.PrefetchScalarGridSpec(
        num_scalar_prefetch=1,
        grid=(n_steps,),
        in_specs=[
            smem_blk(0),
            smem_blk(1),
            smem_blk(2),
            pl.BlockSpec(memory_space=pl.ANY),
            pl.BlockSpec((256, 256), lambda i, be: (0, 0)),
            pl.BlockSpec((1, d, f2), lambda i, be: (be[i], 0, 0)),
            pl.BlockSpec((1, 1, f2), lambda i, be: (be[i], 0, 0)),
            pl.BlockSpec((1, f, d), lambda i, be: (be[i], 0, 0)),
            pl.BlockSpec((1, 1, d), lambda i, be: (be[i], 0, 0)),
        ],
        out_specs=pl.BlockSpec((rows, 128), lambda i, be: (i, 0)),
        scratch_shapes=[
            pltpu.VMEM((3, rows, 128), jnp.uint32),
            pltpu.VMEM((blk, d), BF16),
            pltpu.VMEM((d, f2), BF16),
            pltpu.VMEM((f, d), BF16),
            pltpu.SemaphoreType.DMA((3,)),
        ],
    )
    return pl.pallas_call(
        kern,
        grid_spec=grid_spec,
        out_shape=jax.ShapeDtypeStruct((n_steps * rows, 128), jnp.uint32),
        compiler_params=pltpu.CompilerParams(dimension_semantics=("arbitrary",),
                                             vmem_limit_bytes=VMEM_LIMIT),
        name="moe_experts",
    )(blk_e, src3, src3, src3, h2p, perm, w_gu, b_gu_p, w_down, b_down)


RUN_BITS = 9


def _final_kernel(tab_cur_ref, tab_nxt_ref, x1_ref, tok_ref, w_ref, mod_ref, fmod_ref, gfin_ref, ys_hbm,
                  o_ref, stage, sem, *, tm, n_exp):
    i = pl.program_id(0)
    n = pl.num_programs(0)
    slot = i % 2
    n_rows = TOP_K * tm

    def issue_tile(tab_ref, slot_):
        def body(e, c):
            cnt = tab_ref[0, 0, e]
            src = tab_ref[0, 1, e]
            dst = tab_ref[0, 2, e]
            off = 0
            for b in reversed(range(RUN_BITS)):
                size = (1 << b) * ROW_SUB
                take = (cnt >> b) & 1

                @pl.when(take == 1)
                def _(off=off, size=size):
                    pltpu.make_async_copy(
                        ys_hbm.at[pl.ds(pl.multiple_of(src + off, ROW_SUB), size)],
                        stage.at[slot_, pl.ds(pl.multiple_of(dst + off, ROW_SUB), size)],
                        sem.at[slot_]).start()
                off = off + take * size
            return c
        lax.fori_loop(0, n_exp, body, 0)

    @pl.when(i == 0)
    def _():
        issue_tile(tab_cur_ref, 0)

    @pl.when(i + 1 < n)
    def _():
        issue_tile(tab_nxt_ref, 1 - slot)

    pltpu.make_async_copy(ys_hbm.at[pl.ds(0, n_rows * ROW_SUB)], stage.at[slot], sem.at[slot]).wait()

    y = _load_rows(stage.at[slot], n_rows).astype(BF16)
    tok = tok_ref[0]
    a = jnp.where(lax.broadcasted_iota(jnp.int32, (tm, n_rows), 0) == tok, w_ref[0], 0.0)
    a_hi = a.astype(BF16)
    a_lo = (a - a_hi.astype(F32)).astype(BF16)
    moe = (jnp.dot(a_hi, y, preferred_element_type=F32)
           + jnp.dot(a_lo, y, preferred_element_type=F32))
    gate2 = mod_ref[0, 5:6, :]
    x2 = x1_ref[...] + gate2 * moe
    fshift = fmod_ref[0, 0:1, :]
    fscale = fmod_ref[0, 1:2, :]
    o_ref[...] = _rms(x2, gfin_ref[...]) * (1.0 + fscale) + fshift


def _final_call(tabs, x1, stage_tok, stage_w, mod, fmod, g_final, ys, *, seq):
    t, d = x1.shape
    tm = TM_FIN
    assert tm < (1 << RUN_BITS)
    tps = seq // tm
    nt = t // tm
    n_exp = tabs.shape[2]
    n_rows = TOP_K * tm
    kern = functools.partial(_final_kernel, tm=tm, n_exp=n_exp)
    tab_spec = lambda ahead: pl.BlockSpec((1, 3, n_exp), lambda i: (jnp.minimum(i + ahead, nt - 1), 0, 0),
                                          memory_space=pltpu.SMEM)
    return pl.pallas_call(
        kern,
        grid=(nt,),
        in_specs=[
            tab_spec(0),
            tab_spec(1),
            pl.BlockSpec((tm, d), lambda i: (i, 0)),
            pl.BlockSpec((1, 1, n_rows), lambda i: (i, 0, 0)),
            pl.BlockSpec((1, 1, n_rows), lambda i: (i, 0, 0)),
            pl.BlockSpec((1, 6, d), lambda i: (i // tps, 0, 0)),
            pl.BlockSpec((1, 2, d), lambda i: (i // tps, 0, 0)),
            _const_spec((1, d)),
            pl.BlockSpec(memory_space=pl.ANY),
        ],
        out_specs=pl.BlockSpec((tm, d), lambda i: (i, 0)),
        out_shape=jax.ShapeDtypeStruct((t, d), F32),
        scratch_shapes=[pltpu.VMEM((2, n_rows * ROW_SUB, 128), jnp.uint32),
                        pltpu.SemaphoreType.DMA((2,))],
        compiler_params=pltpu.CompilerParams(dimension_semantics=("arbitrary",),
                                             vmem_limit_bytes=VMEM_LIMIT),
        name="combine_final",
    )(tabs, tabs, x1, stage_tok, stage_w, mod, fmod, g_final, ys)


def _prep_w_in(w_in, d, q_lora, kv_lora):
    o_kpe = d + q_lora + kv_lora
    o_g = o_kpe + QK_ROPE
    half = QK_ROPE // 2
    kpe = w_in[:, o_kpe:o_kpe + QK_ROPE]
    zpad = jnp.zeros((d, 128 - QK_ROPE), w_in.dtype)
    ksw = jnp.concatenate([-kpe[:, half:], kpe[:, :half]], axis=1)
    return jnp.concatenate([w_in[:, :o_kpe], kpe, zpad, ksw, zpad, w_in[:, o_g:]], axis=1).astype(BF16)


def _prep_w_q(w_q_b):
    ql = w_q_b.shape[0]
    hd = QK_NOPE + QK_ROPE
    half = QK_ROPE // 2
    w = w_q_b.reshape(ql, N_HEADS, hd)
    nope = w[:, :, :QK_NOPE]
    pe = w[:, :, QK_NOPE:]
    zpad = jnp.zeros((ql, N_HEADS, 128 - QK_ROPE), w.dtype)
    pe_p = jnp.concatenate([pe, zpad], axis=2)
    sw_p = jnp.concatenate([-pe[:, :, half:], pe[:, :, :half], zpad], axis=2)
    return jnp.concatenate([nope.reshape(ql, -1), pe_p.reshape(ql, -1), sw_p.reshape(ql, -1)],
                           axis=1).astype(BF16)


def _prep_w_kv(w_kv_b):
    kvl = w_kv_b.shape[0]
    w = w_kv_b.reshape(kvl, N_HEADS, QK_NOPE + V_HEAD)
    w_kn = w[:, :, :QK_NOPE].reshape(kvl, -1).astype(BF16)
    w_vt = w[:, :, QK_NOPE:].reshape(kvl, -1).T.astype(BF16)
    return w_kn, w_vt


def kernel(x, c, positions, w_mod, b_mod, g_mix, w_in, b_gate, w_pool_grp, pool_scale, w_pool_out,
           g_q_a, w_q_b, g_kv_a, w_kv_b, w_mla_out, w_out, g_ffn, w_router, b_router, w_gu, b_gu,
           w_down, b_down, g_final, w_fmod, b_fmod):
    bsz, seq, d = x.shape
    t = bsz * seq
    depth = w_mod.shape[0]
    assert depth == 1
    assert seq % TQ == 0 and seq % TM_IN == 0 and seq % TM_POST == 0 and seq % TM_FIN == 0
    q_lora = g_q_a.shape[-1]
    kv_lora = g_kv_a.shape[-1]
    n_exp = w_gu.shape[1]
    f = w_gu.shape[-1] // 2
    blk = MOE_BLK

    x2d = x.reshape(t, d)
    pos_col = positions.astype(F32).reshape(t, 1)
    inv_freq = 1.0 / (ROPE_THETA ** (jnp.arange(0, QK_ROPE, 2, dtype=F32) / QK_ROPE))
    invf2 = jnp.concatenate([inv_freq, inv_freq, jnp.zeros((128 - QK_ROPE,), F32)]).reshape(1, 128)

    mod = _mod_call(c, w_mod[0], b_mod[0]).reshape(bsz, 6, d)
    fmod = _mod_call(c, w_fmod, b_fmod).reshape(bsz, 2, d)

    w_in_p = _prep_w_in(w_in[0], d, q_lora, kv_lora)
    w_q_p = _prep_w_q(w_q_b[0])
    w_kn, w_vt = _prep_w_kv(w_kv_b[0])
    ga, g1, q2, k2, vt3 = _mixer_in_call(
        x2d, pos_col, mod, g_mix[0].reshape(1, d), w_in_p, b_gate[0].reshape(1, 2 * d),
        w_pool_grp[0].astype(BF16), pool_scale[0].reshape(1, d), w_pool_out[0].astype(BF16),
        g_q_a[0].reshape(1, q_lora), w_q_p, g_kv_a[0].reshape(1, kv_lora), w_kn, w_vt, invf2,
        bsz=bsz, seq=seq)

    hp = N_HEADS * HEAD_PAD
    o = _attn_call(q2.reshape(bsz, seq, hp), k2.reshape(bsz, seq, hp), vt3)

    w_r = w_router[0]
    w_r_hi = w_r.astype(BF16)
    w_r_lo = (w_r - w_r_hi.astype(F32)).astype(BF16)
    lane_pad = lambda a: jnp.pad(a, ((0, 0), (0, 128 - n_exp)))
    w_rt = jnp.concatenate(
        [jnp.concatenate([lane_pad(w_r_hi), lane_pad(w_r_lo)], axis=1),
         jnp.concatenate([lane_pad(w_r_hi), jnp.zeros((d, 128), BF16)], axis=1)], axis=0)
    x1, h2p, idx_t, wgt_t = _post_call(
        o.reshape(t, d), ga, g1, x2d, mod, w_mla_out[0].astype(BF16), w_out[0].astype(BF16),
        g_ffn[0].reshape(1, d), w_rt, b_router[0].reshape(n_exp, 1), seq=seq)

    n_slots = t * TOP_K
    n_rows = n_slots + n_exp * blk
    n_blocks = n_rows // blk
    flat_e = idx_t.T.reshape(-1)
    w_flat = wgt_t.T.reshape(-1)
    e_ids = jnp.arange(n_exp, dtype=jnp.int32)
    order = jnp.argsort(flat_e, stable=True).astype(jnp.int32)
    tt = TM_FIN
    n_tiles = t // tt
    cnt = jnp.sum(flat_e.reshape(n_tiles, tt * TOP_K, 1) == e_ids[None, None, :], axis=1,
                  dtype=jnp.int32)
    counts = jnp.sum(cnt, axis=0)
    padded = (counts + blk - 1) // blk * blk
    pad_end = jnp.cumsum(padded)
    pad_start = pad_end - padded
    grp_start = jnp.cumsum(counts) - counts
    blk_id = jnp.arange(n_blocks, dtype=jnp.int32)
    blk_start = blk_id * blk
    blk_e = jnp.minimum(jnp.sum(pad_end[None, :] <= blk_start[:, None], axis=1, dtype=jnp.int32),
                        n_exp - 1)
    off_in_e = blk_start - pad_start[blk_e]
    nvalid = jnp.clip(counts[blk_e] - off_in_e, 0, blk)
    j = jnp.arange(blk, dtype=jnp.int32)[None, :]
    src = (grp_start[blk_e] + off_in_e)[:, None] + j
    valid = j < nvalid[:, None]
    row_slot = order[jnp.clip(src, 0, n_slots - 1)]
    src3 = (jnp.where(valid, row_slot // TOP_K, 0) * ROW_SUB).reshape(n_blocks, 1, blk)

    before = jnp.cumsum(cnt, axis=0) - cnt
    run_end = jnp.cumsum(cnt, axis=1)
    run_start = run_end - cnt
    tabs = jnp.stack([cnt, (pad_start[None, :] + before) * ROW_SUB, run_start * ROW_SUB], axis=1)
    slot_ids = jnp.arange(n_slots, dtype=jnp.int32)
    tile_key = (slot_ids // (tt * TOP_K)) * n_exp + flat_e
    _, stage_slot, stage_w = lax.sort((tile_key, slot_ids, w_flat), num_keys=1, is_stable=True)
    stage_tok = (stage_slot // TOP_K) % tt

    col = jnp.arange(256, dtype=jnp.int32)[None, :]
    row = jnp.arange(256, dtype=jnp.int32)[:, None]
    perm = (row == jnp.where(col < 128, 2 * col, 2 * (col - 128) + 1)).astype(BF16)
    b_gu_p = jnp.concatenate([b_gu[0][:, 0::2], b_gu[0][:, 1::2]], axis=-1).reshape(n_exp, 1, 2 * f)
    ys = _moe_call(blk_e, src3, h2p, perm, w_gu[0], b_gu_p, w_down[0], b_down[0].reshape(n_exp, 1, d))

    out = _final_call(tabs, x1, stage_tok.reshape(n_tiles, 1, tt * TOP_K),
                      stage_w.reshape(n_tiles, 1, tt * TOP_K), mod, fmod, g_final.reshape(1, d), ys,
                      seq=seq)
    return out.reshape(bsz, seq, d)
```

```python
import functools

import jax
import jax.numpy as jnp
from jax import lax
from jax.experimental import pallas as pl
from jax.experimental.pallas import tpu as pltpu

F32 = jnp.float32
BF16 = jnp.bfloat16

CHUNK = 64
POOL_WINDOWS = (2, 4, 8, 16)
POOL_HALO = 16
N_HEADS = 8
QK_NOPE = 128
QK_ROPE = 64
V_HEAD = 128
HEAD_PAD = 256
ROPE_THETA = 10000.0
TOP_K = 4
SWIGLU_LIMIT = 7.0
SWIGLU_ALPHA = 1.702
NORM_EPS = 1e-6
NEG_INF = -1e30
LOG2_E = 1.4426950408889634

VMEM_LIMIT = 56 * 1024 * 1024

TM_IN = 256
TQ = 512
TM_POST = 512
MOE_BLK = 256
TM_FIN = 256
ROW_SUB = 4


def _const_spec(shape):
    nd = len(shape)
    return pl.BlockSpec(shape, lambda *_: (0,) * nd, pipeline_mode=pl.Buffered(1))


def _rms(xf, g):
    return xf * lax.rsqrt(jnp.mean(xf * xf, axis=-1, keepdims=True) + NORM_EPS) * g


def _pack_bf16_pair(lo, hi):
    lo_b = lax.bitcast_convert_type(lo.astype(BF16).astype(F32), jnp.uint32)
    hi_b = lax.bitcast_convert_type(hi.astype(BF16).astype(F32), jnp.uint32)
    return (hi_b & jnp.uint32(0xFFFF0000)) | (lo_b >> 16)


def _unpack_bf16_pair(p):
    lo = lax.bitcast_convert_type(p << 16, F32)
    hi = lax.bitcast_convert_type(p & jnp.uint32(0xFFFF0000), F32)
    return lo, hi


def _store_rows(ref, packed):
    n = packed.shape[0]
    for q in range(ROW_SUB):
        ref[pl.ds(q, n, stride=ROW_SUB), :] = packed[:, q * 128:(q + 1) * 128]


def _load_rows(ref, n):
    los, his = [], []
    for q in range(ROW_SUB):
        lo, hi = _unpack_bf16_pair(ref[pl.ds(q, n, stride=ROW_SUB), :])
        los.append(lo)
        his.append(hi)
    return jnp.concatenate(los + his, axis=1)


def _mod_kernel(c_ref, w_ref, b_ref, o_ref):
    c = c_ref[...]
    c_act = c * jax.nn.sigmoid(c)
    o_ref[...] = jnp.dot(c_act, w_ref[...], preferred_element_type=F32,
                         precision=lax.Precision.HIGHEST) + b_ref[...]


def _mod_call(c, w, b, tn=1024):
    bsz, d = c.shape
    n = w.shape[1]
    return pl.pallas_call(
        _mod_kernel,
        grid=(n // tn,),
        in_specs=[pl.BlockSpec((bsz, d), lambda j: (0, 0)),
                  pl.BlockSpec((d, tn), lambda j: (0, j)),
                  pl.BlockSpec((1, tn), lambda j: (0, j))],
        out_specs=pl.BlockSpec((bsz, tn), lambda j: (0, j)),
        out_shape=jax.ShapeDtypeStruct((bsz, n), F32),
        compiler_params=pltpu.CompilerParams(dimension_semantics=("arbitrary",),
                                             vmem_limit_bytes=VMEM_LIMIT),
        name="adaln_mod",
    )(c, w, b.reshape(1, n))


def _mixer_in_kernel(x_ref, xh_ref, pos_ref, mod_ref, gmix_ref, win_ref, bgate_ref, wgrp_ref,
                     pscale_ref, wpo_ref, gq_ref, wq_ref, gkv_ref, wkn_ref, wvt_ref, invf_ref,
                     ga_ref, g1_ref, q_ref, k_ref, vt_ref, u_scr, *, tm, tiles_per_seq, d, q_lora,
                     kv_lora):
    i = pl.program_id(0)
    t_in_seq = i % tiles_per_seq
    is_start = t_in_seq == 0
    shift1 = mod_ref[0, 0:1, :]
    scale1 = mod_ref[0, 1:2, :]
    gmix = gmix_ref[...]

    def prenorm(xf):
        return (_rms(xf, gmix) * (1.0 + scale1) + shift1).astype(BF16)

    h = prenorm(x_ref[...])
    hh = prenorm(xh_ref[...])
    u = jnp.dot(h, win_ref[:, 0:d], preferred_element_type=F32)
    uh = jnp.dot(hh, win_ref[:, 0:d], preferred_element_type=F32)
    u_scr[0:POOL_HALO, :] = jnp.where(is_start, 0.0, uh)
    u_scr[POOL_HALO:POOL_HALO + tm, :] = u
    rest = jnp.dot(h, win_ref[:, d:], preferred_element_type=F32)

    gw = d // len(POOL_WINDOWS)
    tseq = t_in_seq * tm + lax.broadcasted_iota(jnp.int32, (tm, 1), 0)
    ys = []
    for g, w in enumerate(POOL_WINDOWS):
        c0 = g * gw
        ug = u_scr[POOL_HALO:POOL_HALO + tm, c0:c0 + gw]
        acc = ug
        for j in range(1, w):
            acc = acc + u_scr[POOL_HALO - j:POOL_HALO - j + tm, c0:c0 + gw]
        cnt = jnp.minimum(tseq + 1, w).astype(F32)
        mixed = (acc / cnt - ug).astype(BF16)
        ys.append(jnp.dot(mixed, wgrp_ref[g], preferred_element_type=F32))
    y = (jnp.concatenate(ys, axis=1) * pscale_ref[...]).astype(BF16)
    a = jnp.dot(y, wpo_ref[...], preferred_element_type=F32)

    o_q = 0
    o_kv = q_lora
    o_kpe = q_lora + kv_lora
    o_ksw = o_kpe + 128
    o_g0 = o_ksw + 128
    o_g1 = o_g0 + d
    gates0 = jax.nn.sigmoid(rest[:, o_g0:o_g0 + d] + bgate_ref[:, 0:d])
    gates1 = jax.nn.sigmoid(rest[:, o_g1:o_g1 + d] + bgate_ref[:, d:2 * d])
    ga_ref[...] = (gates0 * a).astype(BF16)
    g1_ref[...] = gates1.astype(BF16)

    ang = pos_ref[...] * invf_ref[...]
    cos2 = jnp.cos(ang)
    sin2 = jnp.sin(ang)
    q_scale = float(QK_NOPE + QK_ROPE) ** -0.5 * LOG2_E

    qn = _rms(rest[:, o_q:o_q + q_lora], gq_ref[...]).astype(BF16)
    qall = jnp.dot(qn, wq_ref[...], preferred_element_type=F32)
    kvn = _rms(rest[:, o_kv:o_kv + kv_lora], gkv_ref[...]).astype(BF16)
    kn = jnp.dot(kvn, wkn_ref[...], preferred_element_type=F32)
    vt = lax.dot_general(wvt_ref[...], kvn, (((1,), (1,)), ((), ())),
                         preferred_element_type=F32)
    vt_ref[0] = vt.astype(BF16)
    kpe = (rest[:, o_kpe:o_kpe + 128] * cos2 + rest[:, o_ksw:o_ksw + 128] * sin2).astype(BF16)
    o_pe = N_HEADS * QK_NOPE
    o_sw = o_pe + N_HEADS * QK_ROPE
    low_half = lax.broadcasted_iota(jnp.int32, (tm, 128), 1) < QK_ROPE
    for pair in range(N_HEADS // 2):
        c0 = pair * 128
        qpe2 = (qall[:, o_pe + c0:o_pe + c0 + 128] * cos2
                + qall[:, o_sw + c0:o_sw + c0 + 128] * sin2) * q_scale
        for hd, part in ((2 * pair, qpe2), (2 * pair + 1, pltpu.roll(qpe2, QK_ROPE, axis=1))):
            n0 = hd * QK_NOPE
            q_ref[:, hd * HEAD_PAD:hd * HEAD_PAD + 128] = (qall[:, n0:n0 + 128] * q_scale).astype(BF16)
            q_ref[:, hd * HEAD_PAD + 128:(hd + 1) * HEAD_PAD] = jnp.where(low_half, part, 0.0).astype(BF16)
            k_ref[:, hd * HEAD_PAD:hd * HEAD_PAD + 128] = kn[:, n0:n0 + 128].astype(BF16)
            k_ref[:, hd * HEAD_PAD + 128:(hd + 1) * HEAD_PAD] = kpe


def _mixer_in_call(x2d, pos_col, mod, g_mix, w_in_p, b_gate, w_grp, pool_scale, w_po, g_q_a, w_q_p,
                   g_kv_a, w_kn, w_vt, invf2, *, bsz, seq):
    t, d = x2d.shape
    tm = TM_IN
    tps = seq // tm
    q_lora = g_q_a.shape[-1]
    kv_lora = g_kv_a.shape[-1]
    hp = N_HEADS * HEAD_PAD
    halo_blocks = tm // POOL_HALO
    kern = functools.partial(_mixer_in_kernel, tm=tm, tiles_per_seq=tps, d=d, q_lora=q_lora,
                             kv_lora=kv_lora)
    row = lambda i: (i, 0)
    return pl.pallas_call(
        kern,
        grid=(t // tm,),
        in_specs=[
            pl.BlockSpec((tm, d), row),
            pl.BlockSpec((POOL_HALO, d), lambda i: (jnp.maximum(i * halo_blocks - 1, 0), 0)),
            pl.BlockSpec((tm, 1), row),
            pl.BlockSpec((1, 6, d), lambda i: (i // tps, 0, 0)),
            _const_spec((1, d)),
            _const_spec(w_in_p.shape),
            _const_spec((1, 2 * d)),
            _const_spec(w_grp.shape),
            _const_spec((1, d)),
            _const_spec(w_po.shape),
            _const_spec((1, q_lora)),
            _const_spec(w_q_p.shape),
            _const_spec((1, kv_lora)),
            _const_spec(w_kn.shape),
            _const_spec(w_vt.shape),
            _const_spec((1, 128)),
        ],
        out_specs=[
            pl.BlockSpec((tm, d), row),
            pl.BlockSpec((tm, d), row),
            pl.BlockSpec((tm, hp), row),
            pl.BlockSpec((tm, hp), row),
            pl.BlockSpec((1, N_HEADS * V_HEAD, tm), lambda i: (i // tps, 0, i % tps)),
        ],
        out_shape=[
            jax.ShapeDtypeStruct((t, d), BF16),
            jax.ShapeDtypeStruct((t, d), BF16),
            jax.ShapeDtypeStruct((t, hp), BF16),
            jax.ShapeDtypeStruct((t, hp), BF16),
            jax.ShapeDtypeStruct((bsz, N_HEADS * V_HEAD, seq), BF16),
        ],
        scratch_shapes=[pltpu.VMEM((tm + POOL_HALO, d), F32)],
        compiler_params=pltpu.CompilerParams(dimension_semantics=("arbitrary",),
                                             vmem_limit_bytes=VMEM_LIMIT),
        name="mixer_in",
    )(x2d, x2d, pos_col, mod, g_mix, w_in_p, b_gate, w_grp, pool_scale, w_po, g_q_a, w_q_p, g_kv_a,
      w_kn, w_vt, invf2)


def _attn_kernel(q_ref, k_ref, vt_ref, o_ref, *, tq, nq):
    for qi in range(nq):
        q0 = qi * tq
        ln = q0 + tq
        q = q_ref[0, q0:q0 + tq, :]
        s = lax.dot_general(k_ref[0, 0:ln, :], q, (((1,), (1,)), ((), ())),
                            preferred_element_type=F32)
        kc = (q0 + lax.broadcasted_iota(jnp.int32, (tq, tq), 0)) // CHUNK
        qc = (q0 + lax.broadcasted_iota(jnp.int32, (tq, tq), 1)) // CHUNK
        s_diag = jnp.where(qc >= kc, s[q0:ln, :], NEG_INF)
        m = jnp.max(s_diag, axis=0, keepdims=True)
        if qi > 0:
            s_main = s[0:q0, :]
            m = jnp.maximum(m, jnp.max(s_main, axis=0, keepdims=True))
        p_diag = jnp.exp2(s_diag - m)
        l = jnp.sum(p_diag, axis=0, keepdims=True)
        acc = jnp.dot(vt_ref[0, :, q0:ln], p_diag.astype(BF16), preferred_element_type=F32)
        if qi > 0:
            p_main = jnp.exp2(s_main - m)
            l = l + jnp.sum(p_main, axis=0, keepdims=True)
            acc = acc + jnp.dot(vt_ref[0, :, 0:q0], p_main.astype(BF16), preferred_element_type=F32)
        o_ref[0, q0:q0 + tq, :] = (acc / l).T.astype(BF16)


def _attn_call(q3, k3, vt3):
    bsz, seq, _ = q3.shape
    kern = functools.partial(_attn_kernel, tq=TQ, nq=seq // TQ)
    return pl.pallas_call(
        kern,
        grid=(bsz, N_HEADS),
        in_specs=[pl.BlockSpec((1, seq, HEAD_PAD), lambda b, h: (b, 0, h)),
                  pl.BlockSpec((1, seq, HEAD_PAD), lambda b, h: (b, 0, h)),
                  pl.BlockSpec((1, V_HEAD, seq), lambda b, h: (b, h, 0))],
        out_specs=pl.BlockSpec((1, seq, V_HEAD), lambda b, h: (b, 0, h)),
        out_shape=jax.ShapeDtypeStruct((bsz, seq, N_HEADS * V_HEAD), BF16),
        compiler_params=pltpu.CompilerParams(dimension_semantics=("arbitrary", "arbitrary"),
                                             vmem_limit_bytes=VMEM_LIMIT),
        name="mla_attn",
    )(q3, k3, vt3)


def _post_kernel(o_ref, ga_ref, g1_ref, x_ref, mod_ref, wmo_ref, wout_ref, gffn_ref, wrt_ref, br_ref,
                 x1_ref, h2_ref, idx_ref, wgt_ref, *, d):
    m = jnp.dot(o_ref[...], wmo_ref[...], preferred_element_type=F32)
    merged = ga_ref[...].astype(F32) + g1_ref[...].astype(F32) * m
    gate1 = mod_ref[0, 2:3, :]
    x1 = x_ref[...] + gate1 * jnp.dot(merged.astype(BF16), wout_ref[...],
                                      preferred_element_type=F32)
    x1_ref[...] = x1
    shift2 = mod_ref[0, 3:4, :]
    scale2 = mod_ref[0, 4:5, :]
    h2 = _rms(x1, gffn_ref[...]) * (1.0 + scale2) + shift2
    _store_rows(h2_ref, _pack_bf16_pair(h2[:, 0:d // 2], h2[:, d // 2:d]))

    ne = br_ref.shape[0]
    h_hi = h2.astype(BF16)
    h_lo = (h2 - h_hi.astype(F32)).astype(BF16)
    hcat = jnp.concatenate([h_hi, h_lo], axis=1)
    half = hcat.shape[0] // 2
    lg2 = jnp.concatenate(
        [jnp.dot(hcat[0:half], wrt_ref[...], preferred_element_type=F32),
         jnp.dot(hcat[half:], wrt_ref[...], preferred_element_type=F32)], axis=0)
    lg = lg2[:, 0:128] + lg2[:, 128:256]
    logits = lg.T[0:ne, :] + br_ref[...]
    tm = logits.shape[1]
    eid = lax.broadcasted_iota(jnp.int32, (ne, tm), 0)
    vals, idxs = [], []
    cur = logits
    for _ in range(TOP_K):
        mx = jnp.max(cur, axis=0, keepdims=True)
        ix = jnp.min(jnp.where(cur == mx, eid, ne), axis=0, keepdims=True)
        vals.append(mx)
        idxs.append(ix)
        cur = jnp.where(eid == ix, -jnp.inf, cur)
    es = [jnp.exp(v - vals[0]) for v in vals]
    den = es[0] + es[1] + es[2] + es[3]
    idx_ref[...] = jnp.concatenate(idxs, axis=0)
    wgt_ref[...] = jnp.concatenate([e / den for e in es], axis=0)


def _post_call(o2d, ga, g1, x2d, mod, w_mo, w_out, g_ffn, w_rt, b_r, *, seq):
    t, d = x2d.shape
    tm = TM_POST
    tps = seq // tm
    row = lambda i: (i, 0)
    kern = functools.partial(_post_kernel, d=d)
    return pl.pallas_call(
        kern,
        grid=(t // tm,),
        in_specs=[
            pl.BlockSpec((tm, d), row),
            pl.BlockSpec((tm, d), row),
            pl.BlockSpec((tm, d), row),
            pl.BlockSpec((tm, d), row),
            pl.BlockSpec((1, 6, d), lambda i: (i // tps, 0, 0)),
            _const_spec(w_mo.shape),
            _const_spec(w_out.shape),
            _const_spec((1, d)),
            _const_spec(w_rt.shape),
            _const_spec(b_r.shape),
        ],
        out_specs=[
            pl.BlockSpec((tm, d), row),
            pl.BlockSpec((tm * ROW_SUB, 128), row),
            pl.BlockSpec((TOP_K, tm), lambda i: (0, i)),
            pl.BlockSpec((TOP_K, tm), lambda i: (0, i)),
        ],
        out_shape=[
            jax.ShapeDtypeStruct((t, d), F32),
            jax.ShapeDtypeStruct((t * ROW_SUB, 128), jnp.uint32),
            jax.ShapeDtypeStruct((TOP_K, t), jnp.int32),
            jax.ShapeDtypeStruct((TOP_K, t), F32),
        ],
        compiler_params=pltpu.CompilerParams(dimension_semantics=("arbitrary",),
                                             vmem_limit_bytes=VMEM_LIMIT),
        name="post_attn_router",
    )(o2d, ga, g1, x2d, mod, w_mo, w_out, g_ffn, w_rt, b_r)


def _moe_kernel(blk_e_ref, src_cur_ref, src_nx1_ref, src_nx2_ref, h2_hbm, perm_ref, wgu_ref, bgu_ref,
                wd_ref, bd_ref, o_ref, xs_scr, xb_scr, wgu_scr, wd_scr, gsem, *, blk, f):
    s = pl.program_id(0)
    last = pl.num_programs(0) - 1
    xslot = s % 3
    rows = blk * ROW_SUB

    def wait_gathers(slot_):
        pltpu.make_async_copy(h2_hbm.at[pl.ds(0, rows)], xs_scr.at[slot_], gsem.at[slot_]).wait()

    def gather_one(idx_ref, j, slot_, priority):
        src_row = pl.multiple_of(idx_ref[0, 0, j], ROW_SUB)
        pltpu.make_async_copy(h2_hbm.at[pl.ds(src_row, ROW_SUB)],
                              xs_scr.at[slot_, pl.ds(j * ROW_SUB, ROW_SUB)],
                              gsem.at[slot_]).start(priority=priority)

    def gather_looped(idx_ref, slot_):
        def body(jj, c):
            for r in range(2):
                gather_one(idx_ref, 2 * jj + r, slot_, r)
            return c
        lax.fori_loop(0, blk // 2, body, 0, unroll=4)

    @pl.when(s == 0)
    def _():
        gather_looped(src_cur_ref, 0)
        gather_looped(src_nx1_ref, 1)

    wait_gathers(xslot)
    for j in range(blk // 2):
        gather_one(src_nx2_ref, j, (s + 2) % 3, j % 2)

    e_cur = blk_e_ref[s]
    e_prev = blk_e_ref[jnp.maximum(s - 1, 0)]

    @pl.when(jnp.logical_or(s == 0, e_cur != e_prev))
    def _():
        perm = perm_ref[...]
        for c in range(2 * f // 256):
            r = jnp.dot(wgu_ref[0, :, c * 256:(c + 1) * 256].astype(BF16), perm,
                        preferred_element_type=F32)
            wgu_scr[:, c * 128:(c + 1) * 128] = r[:, 0:128].astype(BF16)
            wgu_scr[:, f + c * 128:f + (c + 1) * 128] = r[:, 128:256].astype(BF16)
        wd_scr[...] = wd_ref[0].astype(BF16)

    xb_scr[...] = _load_rows(xs_scr.at[xslot], blk).astype(BF16)
    gu = jnp.dot(xb_scr[...], wgu_scr[...], preferred_element_type=F32) + bgu_ref[0]
    gate = jnp.minimum(gu[:, 0:f], SWIGLU_LIMIT)
    up = jnp.clip(gu[:, f:2 * f], -SWIGLU_LIMIT, SWIGLU_LIMIT)
    act = (up + 1.0) * (gate * jax.nn.sigmoid(SWIGLU_ALPHA * gate))
    y = jnp.dot(act.astype(BF16), wd_scr[...], preferred_element_type=F32) + bd_ref[0]
    dh = y.shape[1] // 2
    _store_rows(o_ref, _pack_bf16_pair(y[:, 0:dh], y[:, dh:]))
    for j in range(blk // 2, blk):
        gather_one(src_nx2_ref, j, (s + 2) % 3, j % 2)

    @pl.when(s == last)
    def _():
        wait_gathers((s + 1) % 3)
        wait_gathers((s + 2) % 3)


def _moe_call(blk_e, src3, h2p, perm, w_gu, b_gu_p, w_down, b_down):
    n_steps = src3.shape[0]
    assert n_steps >= 3
    blk = src3.shape[2]
    _, d, f2 = w_gu.shape
    f = f2 // 2
    rows = blk * ROW_SUB
    kern = functools.partial(_moe_kernel, blk=blk, f=f)
    smem_blk = lambda ahead: pl.BlockSpec(
        (1, 1, blk), lambda i, be: (jnp.minimum(i + ahead, n_steps - 1), 0, 0),
        memory_space=pltpu.SMEM)
    grid_spec = pltpu.PrefetchScalarGridSpec(
        num_scalar_prefetch=1,
        grid=(n_steps,),
        in_specs=[
            smem_blk(0),
            smem_blk(1),
            smem_blk(2),
            pl.BlockSpec(memory_space=pl.ANY),
            pl.BlockSpec((256, 256), lambda i, be: (0, 0)),
            pl.BlockSpec((1, d, f2), lambda i, be: (be[i], 0, 0)),
            pl.BlockSpec((1, 1, f2), lambda i, be: (be[i], 0, 0)),
            pl.BlockSpec((1, f, d), lambda i, be: (be[i], 0, 0)),
            pl.BlockSpec((1, 1, d), lambda i, be: (be[i], 0, 0)),
        ],
        out_specs=pl.BlockSpec((rows, 128), lambda i, be: (i, 0)),
        scratch_shapes=[
            pltpu.VMEM((3, rows, 128), jnp.uint32),
            pltpu.VMEM((blk, d), BF16),
            pltpu.VMEM((d, f2), BF16),
            pltpu.VMEM((f, d), BF16),
            pltpu.SemaphoreType.DMA((3,)),
        ],
    )
    return pl.pallas_call(
        kern,
        grid_spec=grid_spec,
        out_shape=jax.ShapeDtypeStruct((n_steps * rows, 128), jnp.uint32),
        compiler_params=pltpu.CompilerParams(dimension_semantics=("arbitrary",),
                                             vmem_limit_bytes=VMEM_LIMIT),
        name="moe_experts",
    )(blk_e, src3, src3, src3, h2p, perm, w_gu, b_gu_p, w_down, b_down)


RUN_BITS = 9


def _final_kernel(tab_cur_ref, tab_nxt_ref, x1_ref, tok_ref, w_ref, mod_ref, fmod_ref, gfin_ref, ys_hbm,
                  o_ref, stage, sem, *, tm, n_exp):
    i = pl.program_id(0)
    n = pl.num_programs(0)
    slot = i % 2
    n_rows = TOP_K * tm

    def issue_tile(tab_ref, slot_):
        def body(e, c):
            cnt = tab_ref[0, 0, e]
            src = tab_ref[0, 1, e]
            dst = tab_ref[0, 2, e]
            off = 0
            for b in reversed(range(RUN_BITS)):
                size = (1 << b) * ROW_SUB
                take = (cnt >> b) & 1

                @pl.when(take == 1)
                def _(off=off, size=size):
                    pltpu.make_async_copy(
                        ys_hbm.at[pl.ds(pl.multiple_of(src + off, ROW_SUB), size)],
                        stage.at[slot_, pl.ds(pl.multiple_of(dst + off, ROW_SUB), size)],
                        sem.at[slot_]).start()
                off = off + take * size
            return c
        lax.fori_loop(0, n_exp, body, 0)

    @pl.when(i == 0)
    def _():
        issue_tile(tab_cur_ref, 0)

    @pl.when(i + 1 < n)
    def _():
        issue_tile(tab_nxt_ref, 1 - slot)

    pltpu.make_async_copy(ys_hbm.at[pl.ds(0, n_rows * ROW_SUB)], stage.at[slot], sem.at[slot]).wait()

    y = _load_rows(stage.at[slot], n_rows).astype(BF16)
    tok = tok_ref[0]
    a = jnp.where(lax.broadcasted_iota(jnp.int32, (tm, n_rows), 0) == tok, w_ref[0], 0.0)
    a_hi = a.astype(BF16)
    a_lo = (a - a_hi.astype(F32)).astype(BF16)
    moe = (jnp.dot(a_hi, y, preferred_element_type=F32)
           + jnp.dot(a_lo, y, preferred_element_type=F32))
    gate2 = mod_ref[0, 5:6, :]
    x2 = x1_ref[...] + gate2 * moe
    fshift = fmod_ref[0, 0:1, :]
    fscale = fmod_ref[0, 1:2, :]
    o_ref[...] = _rms(x2, gfin_ref[...]) * (1.0 + fscale) + fshift


def _final_call(tabs, x1, stage_tok, stage_w, mod, fmod, g_final, ys, *, seq):
    t, d = x1.shape
    tm = TM_FIN
    assert tm < (1 << RUN_BITS)
    tps = seq // tm
    nt = t // tm
    n_exp = tabs.shape[2]
    n_rows = TOP_K * tm
    kern = functools.partial(_final_kernel, tm=tm, n_exp=n_exp)
    tab_spec = lambda ahead: pl.BlockSpec((1, 3, n_exp), lambda i: (jnp.minimum(i + ahead, nt - 1), 0, 0),
                                          memory_space=pltpu.SMEM)
    return pl.pallas_call(
        kern,
        grid=(nt,),
        in_specs=[
            tab_spec(0),
            tab_spec(1),
            pl.BlockSpec((tm, d), lambda i: (i, 0)),
            pl.BlockSpec((1, 1, n_rows), lambda i: (i, 0, 0)),
            pl.BlockSpec((1, 1, n_rows), lambda i: (i, 0, 0)),
            pl.BlockSpec((1, 6, d), lambda i: (i // tps, 0, 0)),
            pl.BlockSpec((1, 2, d), lambda i: (i // tps, 0, 0)),
            _const_spec((1, d)),
            pl.BlockSpec(memory_space=pl.ANY),
        ],
        out_specs=pl.BlockSpec((tm, d), lambda i: (i, 0)),
        out_shape=jax.ShapeDtypeStruct((t, d), F32),
        scratch_shapes=[pltpu.VMEM((2, n_rows * ROW_SUB, 128), jnp.uint32),
                        pltpu.SemaphoreType.DMA((2,))],
        compiler_params=pltpu.CompilerParams(dimension_semantics=("arbitrary",),
                                             vmem_limit_bytes=VMEM_LIMIT),
        name="combine_final",
    )(tabs, tabs, x1, stage_tok, stage_w, mod, fmod, g_final, ys)


def _prep_w_in(w_in, d, q_lora, kv_lora):
    o_kpe = d + q_lora + kv_lora
    o_g = o_kpe + QK_ROPE
    half = QK_ROPE // 2
    kpe = w_in[:, o_kpe:o_kpe + QK_ROPE]
    zpad = jnp.zeros((d, 128 - QK_ROPE), w_in.dtype)
    ksw = jnp.concatenate([-kpe[:, half:], kpe[:, :half]], axis=1)
    return jnp.concatenate([w_in[:, :o_kpe], kpe, zpad, ksw, zpad, w_in[:, o_g:]], axis=1).astype(BF16)


def _prep_w_q(w_q_b):
    ql = w_q_b.shape[0]
    hd = QK_NOPE + QK_ROPE
    half = QK_ROPE // 2
    w = w_q_b.reshape(ql, N_HEADS, hd)
    nope = w[:, :, :QK_NOPE]
    pe = w[:, :, QK_NOPE:]
    sw = jnp.concatenate([-pe[:, :, half:], pe[:, :, :half]], axis=2)
    return jnp.concatenate([nope.reshape(ql, -1), pe.reshape(ql, -1), sw.reshape(ql, -1)],
                           axis=1).astype(BF16)


def _prep_w_kv(w_kv_b):
    kvl = w_kv_b.shape[0]
    w = w_kv_b.reshape(kvl, N_HEADS, QK_NOPE + V_HEAD)
    w_kn = w[:, :, :QK_NOPE].reshape(kvl, -1).astype(BF16)
    w_vt = w[:, :, QK_NOPE:].reshape(kvl, -1).T.astype(BF16)
    return w_kn, w_vt


def kernel(x, c, positions, w_mod, b_mod, g_mix, w_in, b_gate, w_pool_grp, pool_scale, w_pool_out,
           g_q_a, w_q_b, g_kv_a, w_kv_b, w_mla_out, w_out, g_ffn, w_router, b_router, w_gu, b_gu,
           w_down, b_down, g_final, w_fmod, b_fmod):
    bsz, seq, d = x.shape
    t = bsz * seq
    depth = w_mod.shape[0]
    assert depth == 1
    assert seq % TQ == 0 and seq % TM_IN == 0 and seq % TM_POST == 0 and seq % TM_FIN == 0
    q_lora = g_q_a.shape[-1]
    kv_lora = g_kv_a.shape[-1]
    n_exp = w_gu.shape[1]
    f = w_gu.shape[-1] // 2
    blk = MOE_BLK

    x2d = x.reshape(t, d)
    pos_col = positions.astype(F32).reshape(t, 1)
    inv_freq = 1.0 / (ROPE_THETA ** (jnp.arange(0, QK_ROPE, 2, dtype=F32) / QK_ROPE))
    invf2 = jnp.tile(inv_freq, 128 // (QK_ROPE // 2)).reshape(1, 128)

    mod = _mod_call(c, w_mod[0], b_mod[0]).reshape(bsz, 6, d)
    fmod = _mod_call(c, w_fmod, b_fmod).reshape(bsz, 2, d)

    w_in_p = _prep_w_in(w_in[0], d, q_lora, kv_lora)
    w_q_p = _prep_w_q(w_q_b[0])
    w_kn, w_vt = _prep_w_kv(w_kv_b[0])
    ga, g1, q2, k2, vt3 = _mixer_in_call(
        x2d, pos_col, mod, g_mix[0].reshape(1, d), w_in_p, b_gate[0].reshape(1, 2 * d),
        w_pool_grp[0].astype(BF16), pool_scale[0].reshape(1, d), w_pool_out[0].astype(BF16),
        g_q_a[0].reshape(1, q_lora), w_q_p, g_kv_a[0].reshape(1, kv_lora), w_kn, w_vt, invf2,
        bsz=bsz, seq=seq)

    hp = N_HEADS * HEAD_PAD
    o = _attn_call(q2.reshape(bsz, seq, hp), k2.reshape(bsz, seq, hp), vt3)

    w_r = w_router[0]
    w_r_hi = w_r.astype(BF16)
    w_r_lo = (w_r - w_r_hi.astype(F32)).astype(BF16)
    lane_pad = lambda a: jnp.pad(a, ((0, 0), (0, 128 - n_exp)))
    w_rt = jnp.concatenate(
        [jnp.concatenate([lane_pad(w_r_hi), lane_pad(w_r_lo)], axis=1),
         jnp.concatenate([lane_pad(w_r_hi), jnp.zeros((d, 128), BF16)], axis=1)], axis=0)
    x1, h2p, idx_t, wgt_t = _post_call(
        o.reshape(t, d), ga, g1, x2d, mod, w_mla_out[0].astype(BF16), w_out[0].astype(BF16),
        g_ffn[0].reshape(1, d), w_rt, b_router[0].reshape(n_exp, 1), seq=seq)

    n_slots = t * TOP_K
    n_rows = n_slots + n_exp * blk
    n_blocks = n_rows // blk
    flat_e = idx_t.T.reshape(-1)
    w_flat = wgt_t.T.reshape(-1)
    e_ids = jnp.arange(n_exp, dtype=jnp.int32)
    order = jnp.argsort(flat_e, stable=True).astype(jnp.int32)
    tt = TM_FIN
    n_tiles = t // tt
    cnt = jnp.sum(flat_e.reshape(n_tiles, tt * TOP_K, 1) == e_ids[None, None, :], axis=1,
                  dtype=jnp.int32)
    counts = jnp.sum(cnt, axis=0)
    padded = (counts + blk - 1) // blk * blk
    pad_end = jnp.cumsum(padded)
    pad_start = pad_end - padded
    grp_start = jnp.cumsum(counts) - counts
    blk_id = jnp.arange(n_blocks, dtype=jnp.int32)
    blk_start = blk_id * blk
    blk_e = jnp.minimum(jnp.sum(pad_end[None, :] <= blk_start[:, None], axis=1, dtype=jnp.int32),
                        n_exp - 1)
    off_in_e = blk_start - pad_start[blk_e]
    nvalid = jnp.clip(counts[blk_e] - off_in_e, 0, blk)
    j = jnp.arange(blk, dtype=jnp.int32)[None, :]
    src = (grp_start[blk_e] + off_in_e)[:, None] + j
    valid = j < nvalid[:, None]
    row_slot = order[jnp.clip(src, 0, n_slots - 1)]
    src3 = (jnp.where(valid, row_slot // TOP_K, 0) * ROW_SUB).reshape(n_blocks, 1, blk)

    before = jnp.cumsum(cnt, axis=0) - cnt
    run_end = jnp.cumsum(cnt, axis=1)
    run_start = run_end - cnt
    tabs = jnp.stack([cnt, (pad_start[None, :] + before) * ROW_SUB, run_start * ROW_SUB], axis=1)
    slot_ids = jnp.arange(n_slots, dtype=jnp.int32)
    tile_key = (slot_ids // (tt * TOP_K)) * n_exp + flat_e
    _, stage_slot, stage_w = lax.sort((tile_key, slot_ids, w_flat), num_keys=1, is_stable=True)
    stage_tok = (stage_slot // TOP_K) % tt

    col = jnp.arange(256, dtype=jnp.int32)[None, :]
    row = jnp.arange(256, dtype=jnp.int32)[:, None]
    perm = (row == jnp.where(col < 128, 2 * col, 2 * (col - 128) + 1)).astype(BF16)
    b_gu_p = jnp.concatenate([b_gu[0][:, 0::2], b_gu[0][:, 1::2]], axis=-1).reshape(n_exp, 1, 2 * f)
    ys = _moe_call(blk_e, src3, h2p, perm, w_gu[0], b_gu_p, w_down[0], b_down[0].reshape(n_exp, 1, d))

    out = _final_call(tabs, x1, stage_tok.reshape(n_tiles, 1, tt * TOP_K),
                      stage_w.reshape(n_tiles, 1, tt * TOP_K), mod, fmod, g_final.reshape(1, d), ys,
                      seq=seq)
    return out.reshape(bsz, seq, d)
```

```python
import functools

import jax
import jax.numpy as jnp
from jax import lax
from jax.experimental import pallas as pl
from jax.experimental.pallas import tpu as pltpu

F32 = jnp.float32
BF16 = jnp.bfloat16

CHUNK = 64
POOL_WINDOWS = (2, 4, 8, 16)
POOL_HALO = 16
N_HEADS = 8
QK_NOPE = 128
QK_ROPE = 64
V_HEAD = 128
HEAD_PAD = 256
ROPE_THETA = 10000.0
TOP_K = 4
SWIGLU_LIMIT = 7.0
SWIGLU_ALPHA = 1.702
NORM_EPS = 1e-6
NEG_INF = -1e30
LOG2_E = 1.4426950408889634

VMEM_LIMIT = 56 * 1024 * 1024

TM_IN = 256
TQ = 512
TM_POST = 512
MOE_BLK = 256
TM_FIN = 256
ROW_SUB = 4


def _const_spec(shape):
    nd = len(shape)
    return pl.BlockSpec(shape, lambda *_: (0,) * nd, pipeline_mode=pl.Buffered(1))


def _rms(xf, g):
    return xf * lax.rsqrt(jnp.mean(xf * xf, axis=-1, keepdims=True) + NORM_EPS) * g


def _pack_bf16_pair(lo, hi):
    lo_b = lax.bitcast_convert_type(lo.astype(BF16).astype(F32), jnp.uint32)
    hi_b = lax.bitcast_convert_type(hi.astype(BF16).astype(F32), jnp.uint32)
    return (hi_b & jnp.uint32(0xFFFF0000)) | (lo_b >> 16)


def _unpack_bf16_pair(p):
    lo = lax.bitcast_convert_type(p << 16, F32)
    hi = lax.bitcast_convert_type(p & jnp.uint32(0xFFFF0000), F32)
    return lo, hi


def _store_rows(ref, packed):
    n = packed.shape[0]
    for q in range(ROW_SUB):
        ref[pl.ds(q, n, stride=ROW_SUB), :] = packed[:, q * 128:(q + 1) * 128]


def _load_rows(ref, n):
    los, his = [], []
    for q in range(ROW_SUB):
        lo, hi = _unpack_bf16_pair(ref[pl.ds(q, n, stride=ROW_SUB), :])
        los.append(lo)
        his.append(hi)
    return jnp.concatenate(los + his, axis=1)


def _mod_kernel(c_ref, w_ref, b_ref, o_ref):
    c = c_ref[...]
    c_act = c * jax.nn.sigmoid(c)
    o_ref[...] = jnp.dot(c_act, w_ref[...], preferred_element_type=F32,
                         precision=lax.Precision.HIGHEST) + b_ref[...]


def _mod_call(c, w, b, tn=1024):
    bsz, d = c.shape
    n = w.shape[1]
    return pl.pallas_call(
        _mod_kernel,
        grid=(n // tn,),
        in_specs=[pl.BlockSpec((bsz, d), lambda j: (0, 0)),
                  pl.BlockSpec((d, tn), lambda j: (0, j)),
                  pl.BlockSpec((1, tn), lambda j: (0, j))],
        out_specs=pl.BlockSpec((bsz, tn), lambda j: (0, j)),
        out_shape=jax.ShapeDtypeStruct((bsz, n), F32),
        compiler_params=pltpu.CompilerParams(dimension_semantics=("arbitrary",),
                                             vmem_limit_bytes=VMEM_LIMIT),
        name="adaln_mod",
    )(c, w, b.reshape(1, n))


def _mixer_in_kernel(x_ref, xh_ref, pos_ref, mod_ref, gmix_ref, win_ref, bgate_ref, wgrp_ref,
                     pscale_ref, wpo_ref, gq_ref, wq_ref, gkv_ref, wkn_ref, wvt_ref, invf_ref,
                     ga_ref, g1_ref, q_ref, k_ref, vt_ref, u_scr, *, tm, tiles_per_seq, d, q_lora,
                     kv_lora):
    i = pl.program_id(0)
    t_in_seq = i % tiles_per_seq
    is_start = t_in_seq == 0
    shift1 = mod_ref[0, 0:1, :]
    scale1 = mod_ref[0, 1:2, :]
    gmix = gmix_ref[...]

    def prenorm(xf):
        return (_rms(xf, gmix) * (1.0 + scale1) + shift1).astype(BF16)

    h = prenorm(x_ref[...])
    hh = prenorm(xh_ref[...])
    u = jnp.dot(h, win_ref[:, 0:d], preferred_element_type=F32)
    uh = jnp.dot(hh, win_ref[:, 0:d], preferred_element_type=F32)
    u_scr[0:POOL_HALO, :] = jnp.where(is_start, 0.0, uh)
    u_scr[POOL_HALO:POOL_HALO + tm, :] = u
    rest = jnp.dot(h, win_ref[:, d:], preferred_element_type=F32)

    gw = d // len(POOL_WINDOWS)
    tseq = t_in_seq * tm + lax.broadcasted_iota(jnp.int32, (tm, 1), 0)
    ys = []
    for g, w in enumerate(POOL_WINDOWS):
        c0 = g * gw
        ug = u_scr[POOL_HALO:POOL_HALO + tm, c0:c0 + gw]
        acc = ug
        for j in range(1, w):
            acc = acc + u_scr[POOL_HALO - j:POOL_HALO - j + tm, c0:c0 + gw]
        cnt = jnp.minimum(tseq + 1, w).astype(F32)
        mixed = (acc / cnt - ug).astype(BF16)
        ys.append(jnp.dot(mixed, wgrp_ref[g], preferred_element_type=F32))
    y = (jnp.concatenate(ys, axis=1) * pscale_ref[...]).astype(BF16)
    a = jnp.dot(y, wpo_ref[...], preferred_element_type=F32)

    o_q = 0
    o_kv = q_lora
    o_kpe = q_lora + kv_lora
    o_ksw = o_kpe + 128
    o_g0 = o_ksw + 128
    o_g1 = o_g0 + d
    gates0 = jax.nn.sigmoid(rest[:, o_g0:o_g0 + d] + bgate_ref[:, 0:d])
    gates1 = jax.nn.sigmoid(rest[:, o_g1:o_g1 + d] + bgate_ref[:, d:2 * d])
    ga_ref[...] = (gates0 * a).astype(BF16)
    g1_ref[...] = gates1.astype(BF16)

    ang = pos_ref[...] * invf_ref[...]
    cos2 = jnp.cos(ang)
    sin2 = jnp.sin(ang)
    q_scale = float(QK_NOPE + QK_ROPE) ** -0.5 * LOG2_E

    qn = _rms(rest[:, o_q:o_q + q_lora], gq_ref[...]).astype(BF16)
    qall = jnp.dot(qn, wq_ref[...], preferred_element_type=F32)
    kvn = _rms(rest[:, o_kv:o_kv + kv_lora], gkv_ref[...]).astype(BF16)
    kn = jnp.dot(kvn, wkn_ref[...], preferred_element_type=F32)
    vt = lax.dot_general(wvt_ref[...], kvn, (((1,), (1,)), ((), ())),
                         preferred_element_type=F32)
    vt_ref[0] = vt.astype(BF16)
    kpe = (rest[:, o_kpe:o_kpe + 128] * cos2 + rest[:, o_ksw:o_ksw + 128] * sin2).astype(BF16)
    o_pe = N_HEADS * QK_NOPE
    o_sw = o_pe + N_HEADS * QK_ROPE
    low_half = lax.broadcasted_iota(jnp.int32, (tm, 128), 1) < QK_ROPE
    for pair in range(N_HEADS // 2):
        c0 = pair * 128
        qpe2 = (qall[:, o_pe + c0:o_pe + c0 + 128] * cos2
                + qall[:, o_sw + c0:o_sw + c0 + 128] * sin2) * q_scale
        for hd, part in ((2 * pair, qpe2), (2 * pair + 1, pltpu.roll(qpe2, QK_ROPE, axis=1))):
            n0 = hd * QK_NOPE
            q_ref[:, hd * HEAD_PAD:hd * HEAD_PAD + 128] = (qall[:, n0:n0 + 128] * q_scale).astype(BF16)
            q_ref[:, hd * HEAD_PAD + 128:(hd + 1) * HEAD_PAD] = jnp.where(low_half, part, 0.0).astype(BF16)
            k_ref[:, hd * HEAD_PAD:hd * HEAD_PAD + 128] = kn[:, n0:n0 + 128].astype(BF16)
            k_ref[:, hd * HEAD_PAD + 128:(hd + 1) * HEAD_PAD] = kpe


def _mixer_in_call(x2d, pos_col, mod, g_mix, w_in_p, b_gate, w_grp, pool_scale, w_po, g_q_a, w_q_p,
                   g_kv_a, w_kn, w_vt, invf2, *, bsz, seq):
    t, d = x2d.shape
    tm = TM_IN
    tps = seq // tm
    q_lora = g_q_a.shape[-1]
    kv_lora = g_kv_a.shape[-1]
    hp = N_HEADS * HEAD_PAD
    halo_blocks = tm // POOL_HALO
    kern = functools.partial(_mixer_in_kernel, tm=tm, tiles_per_seq=tps, d=d, q_lora=q_lora,
                             kv_lora=kv_lora)
    row = lambda i: (i, 0)
    return pl.pallas_call(
        kern,
        grid=(t // tm,),
        in_specs=[
            pl.BlockSpec((tm, d), row),
            pl.BlockSpec((POOL_HALO, d), lambda i: (jnp.maximum(i * halo_blocks - 1, 0), 0)),
            pl.BlockSpec((tm, 1), row),
            pl.BlockSpec((1, 6, d), lambda i: (i // tps, 0, 0)),
            _const_spec((1, d)),
            _const_spec(w_in_p.shape),
            _const_spec((1, 2 * d)),
            _const_spec(w_grp.shape),
            _const_spec((1, d)),
            _const_spec(w_po.shape),
            _const_spec((1, q_lora)),
            _const_spec(w_q_p.shape),
            _const_spec((1, kv_lora)),
            _const_spec(w_kn.shape),
            _const_spec(w_vt.shape),
            _const_spec((1, 128)),
        ],
        out_specs=[
            pl.BlockSpec((tm, d), row),
            pl.BlockSpec((tm, d), row),
            pl.BlockSpec((tm, hp), row),
            pl.BlockSpec((tm, hp), row),
            pl.BlockSpec((1, N_HEADS * V_HEAD, tm), lambda i: (i // tps, 0, i % tps)),
        ],
        out_shape=[
            jax.ShapeDtypeStruct((t, d), BF16),
            jax.ShapeDtypeStruct((t, d), BF16),
            jax.ShapeDtypeStruct((t, hp), BF16),
            jax.ShapeDtypeStruct((t, hp), BF16),
            jax.ShapeDtypeStruct((bsz, N_HEADS * V_HEAD, seq), BF16),
        ],
        scratch_shapes=[pltpu.VMEM((tm + POOL_HALO, d), F32)],
        compiler_params=pltpu.CompilerParams(dimension_semantics=("arbitrary",),
                                             vmem_limit_bytes=VMEM_LIMIT),
        name="mixer_in",
    )(x2d, x2d, pos_col, mod, g_mix, w_in_p, b_gate, w_grp, pool_scale, w_po, g_q_a, w_q_p, g_kv_a,
      w_kn, w_vt, invf2)


def _attn_kernel(q_ref, k_ref, vt_ref, o_ref, *, tq, nq):
    for qi in range(nq):
        q0 = qi * tq
        ln = q0 + tq
        q = q_ref[0, q0:q0 + tq, :]
        s = lax.dot_general(k_ref[0, 0:ln, :], q, (((1,), (1,)), ((), ())),
                            preferred_element_type=F32)
        kc = (q0 + lax.broadcasted_iota(jnp.int32, (tq, tq), 0)) // CHUNK
        qc = (q0 + lax.broadcasted_iota(jnp.int32, (tq, tq), 1)) // CHUNK
        s_diag = jnp.where(qc >= kc, s[q0:ln, :], NEG_INF)
        m = jnp.max(s_diag, axis=0, keepdims=True)
        if qi > 0:
            s_main = s[0:q0, :]
            m = jnp.maximum(m, jnp.max(s_main, axis=0, keepdims=True))
        p_diag = jnp.exp2(s_diag - m)
        l = jnp.sum(p_diag, axis=0, keepdims=True)
        acc = jnp.dot(vt_ref[0, :, q0:ln], p_diag.astype(BF16), preferred_element_type=F32)
        if qi > 0:
            p_main = jnp.exp2(s_main - m)
            l = l + jnp.sum(p_main, axis=0, keepdims=True)
            acc = acc + jnp.dot(vt_ref[0, :, 0:q0], p_main.astype(BF16), preferred_element_type=F32)
        o_ref[0, q0:q0 + tq, :] = (acc / l).T.astype(BF16)


def _attn_call(q3, k3, vt3):
    bsz, seq, _ = q3.shape
    kern = functools.partial(_attn_kernel, tq=TQ, nq=seq // TQ)
    return pl.pallas_call(
        kern,
        grid=(bsz, N_HEADS),
        in_specs=[pl.BlockSpec((1, seq, HEAD_PAD), lambda b, h: (b, 0, h)),
                  pl.BlockSpec((1, seq, HEAD_PAD), lambda b, h: (b, 0, h)),
                  pl.BlockSpec((1, V_HEAD, seq), lambda b, h: (b, h, 0))],
        out_specs=pl.BlockSpec((1, seq, V_HEAD), lambda b, h: (b, 0, h)),
        out_shape=jax.ShapeDtypeStruct((bsz, seq, N_HEADS * V_HEAD), BF16),
        compiler_params=pltpu.CompilerParams(dimension_semantics=("arbitrary", "arbitrary"),
                                             vmem_limit_bytes=VMEM_LIMIT),
        name="mla_attn",
    )(q3, k3, vt3)


def _post_kernel(o_ref, ga_ref, g1_ref, x_ref, mod_ref, wmo_ref, wout_ref, gffn_ref, wrt_ref, br_ref,
                 x1_ref, h2_ref, idx_ref, wgt_ref, *, d):
    m = jnp.dot(o_ref[...], wmo_ref[...], preferred_element_type=F32)
    merged = ga_ref[...].astype(F32) + g1_ref[...].astype(F32) * m
    gate1 = mod_ref[0, 2:3, :]
    x1 = x_ref[...] + gate1 * jnp.dot(merged.astype(BF16), wout_ref[...],
                                      preferred_element_type=F32)
    x1_ref[...] = x1
    shift2 = mod_ref[0, 3:4, :]
    scale2 = mod_ref[0, 4:5, :]
    h2 = _rms(x1, gffn_ref[...]) * (1.0 + scale2) + shift2
    _store_rows(h2_ref, _pack_bf16_pair(h2[:, 0:d // 2], h2[:, d // 2:d]))

    ne = br_ref.shape[0]
    h_hi = h2.astype(BF16)
    h_lo = (h2 - h_hi.astype(F32)).astype(BF16)
    hcat = jnp.concatenate([h_hi, h_lo], axis=1)
    half = hcat.shape[0] // 2
    lg2 = jnp.concatenate(
        [jnp.dot(hcat[0:half], wrt_ref[...], preferred_element_type=F32),
         jnp.dot(hcat[half:], wrt_ref[...], preferred_element_type=F32)], axis=0)
    lg = lg2[:, 0:128] + lg2[:, 128:256]
    logits = lg.T[0:ne, :] + br_ref[...]
    tm = logits.shape[1]
    eid = lax.broadcasted_iota(jnp.int32, (ne, tm), 0)
    vals, idxs = [], []
    cur = logits
    for _ in range(TOP_K):
        mx = jnp.max(cur, axis=0, keepdims=True)
        ix = jnp.min(jnp.where(cur == mx, eid, ne), axis=0, keepdims=True)
        vals.append(mx)
        idxs.append(ix)
        cur = jnp.where(eid == ix, -jnp.inf, cur)
    es = [jnp.exp(v - vals[0]) for v in vals]
    den = es[0] + es[1] + es[2] + es[3]
    idx_ref[...] = jnp.concatenate(idxs, axis=0)
    wgt_ref[...] = jnp.concatenate([e / den for e in es], axis=0)


def _post_call(o2d, ga, g1, x2d, mod, w_mo, w_out, g_ffn, w_rt, b_r, *, seq):
    t, d = x2d.shape
    tm = TM_POST
    tps = seq // tm
    row = lambda i: (i, 0)
    kern = functools.partial(_post_kernel, d=d)
    return pl.pallas_call(
        kern,
        grid=(t // tm,),
        in_specs=[
            pl.BlockSpec((tm, d), row),
            pl.BlockSpec((tm, d), row),
            pl.BlockSpec((tm, d), row),
            pl.BlockSpec((tm, d), row),
            pl.BlockSpec((1, 6, d), lambda i: (i // tps, 0, 0)),
            _const_spec(w_mo.shape),
            _const_spec(w_out.shape),
            _const_spec((1, d)),
            _const_spec(w_rt.shape),
            _const_spec(b_r.shape),
        ],
        out_specs=[
            pl.BlockSpec((tm, d), row),
            pl.BlockSpec((tm * ROW_SUB, 128), row),
            pl.BlockSpec((TOP_K, tm), lambda i: (0, i)),
            pl.BlockSpec((TOP_K, tm), lambda i: (0, i)),
        ],
        out_shape=[
            jax.ShapeDtypeStruct((t, d), F32),
            jax.ShapeDtypeStruct((t * ROW_SUB, 128), jnp.uint32),
            jax.ShapeDtypeStruct((TOP_K, t), jnp.int32),
            jax.ShapeDtypeStruct((TOP_K, t), F32),
        ],
        compiler_params=pltpu.CompilerParams(dimension_semantics=("arbitrary",),
                                             vmem_limit_bytes=VMEM_LIMIT),
        name="post_attn_router",
    )(o2d, ga, g1, x2d, mod, w_mo, w_out, g_ffn, w_rt, b_r)


RUN_BITS = 9
TAIL_BITS = 8


def _copy_pieces(count, src_ref, src_off, dst_ref, dst_off, sem, bits, wait=False):
    off = 0
    for b in reversed(range(bits)):
        size = (1 << b) * ROW_SUB
        take = (count >> b) & 1

        @pl.when(take == 1)
        def _(off=off, size=size):
            cp = pltpu.make_async_copy(
                src_ref.at[pl.ds(pl.multiple_of(src_off + off, ROW_SUB), size)],
                dst_ref.at[pl.ds(pl.multiple_of(dst_off + off, ROW_SUB), size)], sem)
            if wait:
                cp.wait()
            else:
                cp.start()
        off = off + take * size


def _dispatch_kernel(tab_ref, tail_ref, nused_ref, h_ref, tokc_ref, xs_hbm, stage, zbuf, sem, zsem, *,
                     tm, n_exp, n_blocks):
    i = pl.program_id(0)
    n = pl.num_programs(0)
    slot = i % 2
    n_rows = TOP_K * tm

    def wait_stage(slot_):
        pltpu.make_async_copy(stage.at[slot_], xs_hbm.at[pl.ds(0, n_rows * ROW_SUB)],
                              sem.at[slot_]).wait()

    @pl.when(i == 0)
    def _():
        zbuf[...] = jnp.zeros(zbuf.shape, zbuf.dtype)
        for wait in (False, True):
            def body(e, c, wait=wait):
                _copy_pieces(tail_ref[1, e], zbuf, 0, xs_hbm, tail_ref[0, e], zsem, TAIL_BITS, wait)
                return c
            lax.fori_loop(0, n_exp, body, 0)
        blk_rows = zbuf.shape[0]

        def zero_block(b, c):
            cp = pltpu.make_async_copy(
                zbuf, xs_hbm.at[pl.ds(pl.multiple_of(b * blk_rows, blk_rows), blk_rows)], zsem)
            cp.start()
            cp.wait()
            return c
        lax.fori_loop(nused_ref[0], n_blocks, zero_block, 0)

    @pl.when(i >= 2)
    def _():
        wait_stage(slot)

    h = _load_rows(h_ref, tm).astype(BF16)
    onehot = (tokc_ref[...] == lax.broadcasted_iota(jnp.int32, (n_rows, tm), 1)).astype(BF16)
    rows = jnp.dot(onehot, h, preferred_element_type=F32)
    dh = rows.shape[1] // 2
    _store_rows(stage.at[slot], _pack_bf16_pair(rows[:, 0:dh], rows[:, dh:]))

    def body(e, c):
        _copy_pieces(tab_ref[0, 0, e], stage.at[slot], tab_ref[0, 2, e], xs_hbm, tab_ref[0, 1, e],
                     sem.at[slot], RUN_BITS)
        return c
    lax.fori_loop(0, n_exp, body, 0)

    @pl.when(i == n - 1)
    def _():
        wait_stage(slot)
        wait_stage(1 - slot)


def _dispatch_call(tabs, tail, n_used, h2p, tok_col, *, n_xs_rows):
    nt, _, n_exp = tabs.shape
    tm = TM_FIN
    assert nt >= 2 and tm < (1 << RUN_BITS) and MOE_BLK <= (1 << TAIL_BITS)
    n_rows = TOP_K * tm
    kern = functools.partial(_dispatch_kernel, tm=tm, n_exp=n_exp, n_blocks=n_xs_rows // MOE_BLK)
    return pl.pallas_call(
        kern,
        grid=(nt,),
        in_specs=[
            pl.BlockSpec((1, 3, n_exp), lambda i: (i, 0, 0), memory_space=pltpu.SMEM),
            pl.BlockSpec(memory_space=pltpu.SMEM),
            pl.BlockSpec(memory_space=pltpu.SMEM),
            pl.BlockSpec((tm * ROW_SUB, 128), lambda i: (i, 0)),
            pl.BlockSpec((n_rows, 1), lambda i: (i, 0)),
        ],
        out_specs=pl.BlockSpec(memory_space=pl.ANY),
        out_shape=jax.ShapeDtypeStruct((n_xs_rows * ROW_SUB, 128), jnp.uint32),
        scratch_shapes=[pltpu.VMEM((2, n_rows * ROW_SUB, 128), jnp.uint32),
                        pltpu.VMEM((MOE_BLK * ROW_SUB, 128), jnp.uint32),
                        pltpu.SemaphoreType.DMA((2,)),
                        pltpu.SemaphoreType.DMA(())],
        compiler_params=pltpu.CompilerParams(dimension_semantics=("arbitrary",),
                                             vmem_limit_bytes=VMEM_LIMIT),
        name="moe_dispatch",
    )(tabs, tail, n_used, h2p, tok_col)


def _moe_kernel(blk_e_ref, nused_ref, x_ref, perm_ref, wgu_ref, bgu_ref, wd_ref, bd_ref, o_ref,
                wgu_scr, wd_scr, *, blk, f):
    s = pl.program_id(0)
    e_cur = blk_e_ref[s]
    e_prev = blk_e_ref[jnp.maximum(s - 1, 0)]

    @pl.when(jnp.logical_or(s == 0, e_cur != e_prev))
    def _():
        perm = perm_ref[...]
        for c in range(2 * f // 256):
            r = jnp.dot(wgu_ref[0, :, c * 256:(c + 1) * 256].astype(BF16), perm,
                        preferred_element_type=F32)
            wgu_scr[:, c * 128:(c + 1) * 128] = r[:, 0:128].astype(BF16)
            wgu_scr[:, f + c * 128:f + (c + 1) * 128] = r[:, 128:256].astype(BF16)
        wd_scr[...] = wd_ref[0].astype(BF16)

    @pl.when(s >= nused_ref[0])
    def _():
        o_ref[...] = jnp.zeros(o_ref.shape, o_ref.dtype)

    @pl.when(s < nused_ref[0])
    def _():
        x = _load_rows(x_ref, blk).astype(BF16)
        gu = jnp.dot(x, wgu_scr[...], preferred_element_type=F32) + bgu_ref[0]
        gate = jnp.minimum(gu[:, 0:f], SWIGLU_LIMIT)
        up = jnp.clip(gu[:, f:2 * f], -SWIGLU_LIMIT, SWIGLU_LIMIT)
        act = (up + 1.0) * (gate * jax.nn.sigmoid(SWIGLU_ALPHA * gate))
        y = jnp.dot(act.astype(BF16), wd_scr[...], preferred_element_type=F32) + bd_ref[0]
        dh = y.shape[1] // 2
        _store_rows(o_ref, _pack_bf16_pair(y[:, 0:dh], y[:, dh:]))


def _moe_call(blk_e, n_used, xs, perm, w_gu, b_gu_p, w_down, b_down):
    n_steps = blk_e.shape[0]
    blk = MOE_BLK
    _, d, f2 = w_gu.shape
    f = f2 // 2
    rows = blk * ROW_SUB
    kern = functools.partial(_moe_kernel, blk=blk, f=f)
    row_blk = lambda i, be, nu: (i, 0)
    grid_spec = pltpu.PrefetchScalarGridSpec(
        num_scalar_prefetch=2,
        grid=(n_steps,),
        in_specs=[
            pl.BlockSpec((rows, 128), row_blk),
            pl.BlockSpec((256, 256), lambda i, be, nu: (0, 0)),
            pl.BlockSpec((1, d, f2), lambda i, be, nu: (be[i], 0, 0)),
            pl.BlockSpec((1, 1, f2), lambda i, be, nu: (be[i], 0, 0)),
            pl.BlockSpec((1, f, d), lambda i, be, nu: (be[i], 0, 0)),
            pl.BlockSpec((1, 1, d), lambda i, be, nu: (be[i], 0, 0)),
        ],
        out_specs=pl.BlockSpec((rows, 128), row_blk),
        scratch_shapes=[
            pltpu.VMEM((d, f2), BF16),
            pltpu.VMEM((f, d), BF16),
        ],
    )
    return pl.pallas_call(
        kern,
        grid_spec=grid_spec,
        out_shape=jax.ShapeDtypeStruct((n_steps * rows, 128), jnp.uint32),
        compiler_params=pltpu.CompilerParams(dimension_semantics=("arbitrary",),
                                             vmem_limit_bytes=VMEM_LIMIT),
        name="moe_experts",
    )(blk_e, n_used, xs, perm, w_gu, b_gu_p, w_down, b_down)


def _final_kernel(tab_cur_ref, tab_nxt_ref, x1_ref, tok_ref, w_ref, mod_ref, fmod_ref, gfin_ref, ys_hbm,
                  o_ref, stage, sem, *, tm, n_exp):
    i = pl.program_id(0)
    n = pl.num_programs(0)
    slot = i % 2
    n_rows = TOP_K * tm

    def issue_tile(tab_ref, slot_):
        def body(e, c):
            _copy_pieces(tab_ref[0, 0, e], ys_hbm, tab_ref[0, 1, e], stage.at[slot_], tab_ref[0, 2, e],
                         sem.at[slot_], RUN_BITS)
            return c
        lax.fori_loop(0, n_exp, body, 0)

    @pl.when(i == 0)
    def _():
        issue_tile(tab_cur_ref, 0)

    @pl.when(i + 1 < n)
    def _():
        issue_tile(tab_nxt_ref, 1 - slot)

    pltpu.make_async_copy(ys_hbm.at[pl.ds(0, n_rows * ROW_SUB)], stage.at[slot], sem.at[slot]).wait()

    y = _load_rows(stage.at[slot], n_rows).astype(BF16)
    tok = tok_ref[0]
    a = jnp.where(lax.broadcasted_iota(jnp.int32, (tm, n_rows), 0) == tok, w_ref[0], 0.0)
    a_hi = a.astype(BF16)
    a_lo = (a - a_hi.astype(F32)).astype(BF16)
    moe = (jnp.dot(a_hi, y, preferred_element_type=F32)
           + jnp.dot(a_lo, y, preferred_element_type=F32))
    gate2 = mod_ref[0, 5:6, :]
    x2 = x1_ref[...] + gate2 * moe
    fshift = fmod_ref[0, 0:1, :]
    fscale = fmod_ref[0, 1:2, :]
    o_ref[...] = _rms(x2, gfin_ref[...]) * (1.0 + fscale) + fshift


def _final_call(tabs, x1, stage_tok, stage_w, mod, fmod, g_final, ys, *, seq):
    t, d = x1.shape
    tm = TM_FIN
    assert tm < (1 << RUN_BITS)
    tps = seq // tm
    nt = t // tm
    n_exp = tabs.shape[2]
    n_rows = TOP_K * tm
    kern = functools.partial(_final_kernel, tm=tm, n_exp=n_exp)
    tab_spec = lambda ahead: pl.BlockSpec((1, 3, n_exp), lambda i: (jnp.minimum(i + ahead, nt - 1), 0, 0),
                                          memory_space=pltpu.SMEM)
    return pl.pallas_call(
        kern,
        grid=(nt,),
        in_specs=[
            tab_spec(0),
            tab_spec(1),
            pl.BlockSpec((tm, d), lambda i: (i, 0)),
            pl.BlockSpec((1, 1, n_rows), lambda i: (i, 0, 0)),
            pl.BlockSpec((1, 1, n_rows), lambda i: (i, 0, 0)),
            pl.BlockSpec((1, 6, d), lambda i: (i // tps, 0, 0)),
            pl.BlockSpec((1, 2, d), lambda i: (i // tps, 0, 0)),
            _const_spec((1, d)),
            pl.BlockSpec(memory_space=pl.ANY),
        ],
        out_specs=pl.BlockSpec((tm, d), lambda i: (i, 0)),
        out_shape=jax.ShapeDtypeStruct((t, d), F32),
        scratch_shapes=[pltpu.VMEM((2, n_rows * ROW_SUB, 128), jnp.uint32),
                        pltpu.SemaphoreType.DMA((2,))],
        compiler_params=pltpu.CompilerParams(dimension_semantics=("arbitrary",),
                                             vmem_limit_bytes=VMEM_LIMIT),
        name="combine_final",
    )(tabs, tabs, x1, stage_tok, stage_w, mod, fmod, g_final, ys)


def _prep_w_in(w_in, d, q_lora, kv_lora):
    o_kpe = d + q_lora + kv_lora
    o_g = o_kpe + QK_ROPE
    half = QK_ROPE // 2
    kpe = w_in[:, o_kpe:o_kpe + QK_ROPE]
    zpad = jnp.zeros((d, 128 - QK_ROPE), w_in.dtype)
    ksw = jnp.concatenate([-kpe[:, half:], kpe[:, :half]], axis=1)
    return jnp.concatenate([w_in[:, :o_kpe], kpe, zpad, ksw, zpad, w_in[:, o_g:]], axis=1).astype(BF16)


def _prep_w_q(w_q_b):
    ql = w_q_b.shape[0]
    hd = QK_NOPE + QK_ROPE
    half = QK_ROPE // 2
    w = w_q_b.reshape(ql, N_HEADS, hd)
    nope = w[:, :, :QK_NOPE]
    pe = w[:, :, QK_NOPE:]
    sw = jnp.concatenate([-pe[:, :, half:], pe[:, :, :half]], axis=2)
    return jnp.concatenate([nope.reshape(ql, -1), pe.reshape(ql, -1), sw.reshape(ql, -1)],
                           axis=1).astype(BF16)


def _prep_w_kv(w_kv_b):
    kvl = w_kv_b.shape[0]
    w = w_kv_b.reshape(kvl, N_HEADS, QK_NOPE + V_HEAD)
    w_kn = w[:, :, :QK_NOPE].reshape(kvl, -1).astype(BF16)
    w_vt = w[:, :, QK_NOPE:].reshape(kvl, -1).T.astype(BF16)
    return w_kn, w_vt


def kernel(x, c, positions, w_mod, b_mod, g_mix, w_in, b_gate, w_pool_grp, pool_scale, w_pool_out,
           g_q_a, w_q_b, g_kv_a, w_kv_b, w_mla_out, w_out, g_ffn, w_router, b_router, w_gu, b_gu,
           w_down, b_down, g_final, w_fmod, b_fmod):
    bsz, seq, d = x.shape
    t = bsz * seq
    depth = w_mod.shape[0]
    assert depth == 1
    assert seq % TQ == 0 and seq % TM_IN == 0 and seq % TM_POST == 0 and seq % TM_FIN == 0
    q_lora = g_q_a.shape[-1]
    kv_lora = g_kv_a.shape[-1]
    n_exp = w_gu.shape[1]
    f = w_gu.shape[-1] // 2
    blk = MOE_BLK

    x2d = x.reshape(t, d)
    pos_col = positions.astype(F32).reshape(t, 1)
    inv_freq = 1.0 / (ROPE_THETA ** (jnp.arange(0, QK_ROPE, 2, dtype=F32) / QK_ROPE))
    invf2 = jnp.tile(inv_freq, 128 // (QK_ROPE // 2)).reshape(1, 128)

    mod = _mod_call(c, w_mod[0], b_mod[0]).reshape(bsz, 6, d)
    fmod = _mod_call(c, w_fmod, b_fmod).reshape(bsz, 2, d)

    w_in_p = _prep_w_in(w_in[0], d, q_lora, kv_lora)
    w_q_p = _prep_w_q(w_q_b[0])
    w_kn, w_vt = _prep_w_kv(w_kv_b[0])
    ga, g1, q2, k2, vt3 = _mixer_in_call(
        x2d, pos_col, mod, g_mix[0].reshape(1, d), w_in_p, b_gate[0].reshape(1, 2 * d),
        w_pool_grp[0].astype(BF16), pool_scale[0].reshape(1, d), w_pool_out[0].astype(BF16),
        g_q_a[0].reshape(1, q_lora), w_q_p, g_kv_a[0].reshape(1, kv_lora), w_kn, w_vt, invf2,
        bsz=bsz, seq=seq)

    hp = N_HEADS * HEAD_PAD
    o = _attn_call(q2.reshape(bsz, seq, hp), k2.reshape(bsz, seq, hp), vt3)

    w_r = w_router[0]
    w_r_hi = w_r.astype(BF16)
    w_r_lo = (w_r - w_r_hi.astype(F32)).astype(BF16)
    lane_pad = lambda a: jnp.pad(a, ((0, 0), (0, 128 - n_exp)))
    w_rt = jnp.concatenate(
        [jnp.concatenate([lane_pad(w_r_hi), lane_pad(w_r_lo)], axis=1),
         jnp.concatenate([lane_pad(w_r_hi), jnp.zeros((d, 128), BF16)], axis=1)], axis=0)
    x1, h2p, idx_t, wgt_t = _post_call(
        o.reshape(t, d), ga, g1, x2d, mod, w_mla_out[0].astype(BF16), w_out[0].astype(BF16),
        g_ffn[0].reshape(1, d), w_rt, b_router[0].reshape(n_exp, 1), seq=seq)

    n_slots = t * TOP_K
    n_rows = n_slots + n_exp * blk
    n_blocks = n_rows // blk
    flat_e = idx_t.T.reshape(-1)
    w_flat = wgt_t.T.reshape(-1)
    e_ids = jnp.arange(n_exp, dtype=jnp.int32)
    tt = TM_FIN
    n_tiles = t // tt
    cnt = jnp.sum(flat_e.reshape(n_tiles, tt * TOP_K, 1) == e_ids[None, None, :], axis=1,
                  dtype=jnp.int32)
    counts = jnp.sum(cnt, axis=0)
    padded = (counts + blk - 1) // blk * blk
    pad_end = jnp.cumsum(padded)
    pad_start = pad_end - padded
    blk_start = jnp.arange(n_blocks, dtype=jnp.int32) * blk
    blk_e = jnp.minimum(jnp.sum(pad_end[None, :] <= blk_start[:, None], axis=1, dtype=jnp.int32),
                        n_exp - 1)
    n_used = (pad_end[-1:] // blk).astype(jnp.int32)
    tail = jnp.stack([(pad_start + counts) * ROW_SUB, padded - counts], axis=0).astype(jnp.int32)

    before = jnp.cumsum(cnt, axis=0) - cnt
    run_end = jnp.cumsum(cnt, axis=1)
    run_start = run_end - cnt
    tabs = jnp.stack([cnt, (pad_start[None, :] + before) * ROW_SUB, run_start * ROW_SUB], axis=1)
    slot_ids = jnp.arange(n_slots, dtype=jnp.int32)
    tile_key = (slot_ids // (tt * TOP_K)) * n_exp + flat_e
    _, stage_slot, stage_w = lax.sort((tile_key, slot_ids, w_flat), num_keys=1, is_stable=True)
    stage_tok = (stage_slot // TOP_K) % tt

    col = jnp.arange(256, dtype=jnp.int32)[None, :]
    row = jnp.arange(256, dtype=jnp.int32)[:, None]
    perm = (row == jnp.where(col < 128, 2 * col, 2 * (col - 128) + 1)).astype(BF16)
    b_gu_p = jnp.concatenate([b_gu[0][:, 0::2], b_gu[0][:, 1::2]], axis=-1).reshape(n_exp, 1, 2 * f)
    xs = _dispatch_call(tabs, tail, n_used, h2p, stage_tok.reshape(n_slots, 1), n_xs_rows=n_rows)
    ys = _moe_call(blk_e, n_used, xs, perm, w_gu[0], b_gu_p, w_down[0], b_down[0].reshape(n_exp, 1, d))

    out = _final_call(tabs, x1, stage_tok.reshape(n_tiles, 1, tt * TOP_K),
                      stage_w.reshape(n_tiles, 1, tt * TOP_K), mod, fmod, g_final.reshape(1, d), ys,
                      seq=seq)
    return out.reshape(bsz, seq, d)
```

```python
import functools

import jax
import jax.numpy as jnp
from jax import lax
from jax.experimental import pallas as pl
from jax.experimental.pallas import tpu as pltpu

F32 = jnp.float32
BF16 = jnp.bfloat16

CHUNK = 64
POOL_WINDOWS = (2, 4, 8, 16)
POOL_HALO = 16
N_HEADS = 8
QK_NOPE = 128
QK_ROPE = 64
V_HEAD = 128
HEAD_PAD = 256
ROPE_THETA = 10000.0
TOP_K = 4
SWIGLU_LIMIT = 7.0
SWIGLU_ALPHA = 1.702
NORM_EPS = 1e-6
NEG_INF = -1e30
LOG2_E = 1.4426950408889634

VMEM_LIMIT = 56 * 1024 * 1024

TM_IN = 256
TQ = 512
TM_POST = 512
MOE_BLK = 512
TM_FIN = 256
ROW_SUB = 4


def _const_spec(shape):
    nd = len(shape)
    return pl.BlockSpec(shape, lambda *_: (0,) * nd, pipeline_mode=pl.Buffered(1))


def _rms(xf, g):
    return xf * lax.rsqrt(jnp.mean(xf * xf, axis=-1, keepdims=True) + NORM_EPS) * g


def _pack_bf16_pair(lo, hi):
    lo_b = lax.bitcast_convert_type(lo.astype(BF16).astype(F32), jnp.uint32)
    hi_b = lax.bitcast_convert_type(hi.astype(BF16).astype(F32), jnp.uint32)
    return (hi_b & jnp.uint32(0xFFFF0000)) | (lo_b >> 16)


def _unpack_bf16_pair(p):
    lo = lax.bitcast_convert_type(p << 16, F32)
    hi = lax.bitcast_convert_type(p & jnp.uint32(0xFFFF0000), F32)
    return lo, hi


def _store_rows(ref, packed):
    n = packed.shape[0]
    for q in range(ROW_SUB):
        ref[pl.ds(q, n, stride=ROW_SUB), :] = packed[:, q * 128:(q + 1) * 128]


def _load_rows(ref, n):
    los, his = [], []
    for q in range(ROW_SUB):
        lo, hi = _unpack_bf16_pair(ref[pl.ds(q, n, stride=ROW_SUB), :])
        los.append(lo)
        his.append(hi)
    return jnp.concatenate(los + his, axis=1)


def _mod_kernel(c_ref, w_ref, b_ref, o_ref):
    c = c_ref[...]
    c_act = c * jax.nn.sigmoid(c)
    o_ref[...] = jnp.dot(c_act, w_ref[...], preferred_element_type=F32,
                         precision=lax.Precision.HIGHEST) + b_ref[...]


def _mod_call(c, w, b, tn=1024):
    bsz, d = c.shape
    n = w.shape[1]
    return pl.pallas_call(
        _mod_kernel,
        grid=(n // tn,),
        in_specs=[pl.BlockSpec((bsz, d), lambda j: (0, 0)),
                  pl.BlockSpec((d, tn), lambda j: (0, j)),
                  pl.BlockSpec((1, tn), lambda j: (0, j))],
        out_specs=pl.BlockSpec((bsz, tn), lambda j: (0, j)),
        out_shape=jax.ShapeDtypeStruct((bsz, n), F32),
        compiler_params=pltpu.CompilerParams(dimension_semantics=("arbitrary",),
                                             vmem_limit_bytes=VMEM_LIMIT),
        name="adaln_mod",
    )(c, w, b.reshape(1, n))


def _mixer_in_kernel(x_ref, xh_ref, pos_ref, mod_ref, gmix_ref, win_ref, bgate_ref, wgrp_ref,
                     pscale_ref, wpo_ref, gq_ref, wq_ref, gkv_ref, wkn_ref, wvt_ref, invf_ref,
                     ga_ref, g1_ref, q_ref, k_ref, vt_ref, u_scr, *, tm, tiles_per_seq, d, q_lora,
                     kv_lora):
    i = pl.program_id(0)
    t_in_seq = i % tiles_per_seq
    is_start = t_in_seq == 0
    shift1 = mod_ref[0, 0:1, :]
    scale1 = mod_ref[0, 1:2, :]
    gmix = gmix_ref[...]

    def prenorm(xf):
        return (_rms(xf, gmix) * (1.0 + scale1) + shift1).astype(BF16)

    h = prenorm(x_ref[...])
    hh = prenorm(xh_ref[...])
    u = jnp.dot(h, win_ref[:, 0:d], preferred_element_type=F32)
    uh = jnp.dot(hh, win_ref[:, 0:d], preferred_element_type=F32)
    u_scr[0:POOL_HALO, :] = jnp.where(is_start, 0.0, uh)
    u_scr[POOL_HALO:POOL_HALO + tm, :] = u
    rest = jnp.dot(h, win_ref[:, d:], preferred_element_type=F32)

    gw = d // len(POOL_WINDOWS)
    tseq = t_in_seq * tm + lax.broadcasted_iota(jnp.int32, (tm, 1), 0)
    ys = []
    for g, w in enumerate(POOL_WINDOWS):
        c0 = g * gw
        ug = u_scr[POOL_HALO:POOL_HALO + tm, c0:c0 + gw]
        acc = ug
        for j in range(1, w):
            acc = acc + u_scr[POOL_HALO - j:POOL_HALO - j + tm, c0:c0 + gw]
        cnt = jnp.minimum(tseq + 1, w).astype(F32)
        mixed = (acc / cnt - ug).astype(BF16)
        ys.append(jnp.dot(mixed, wgrp_ref[g], preferred_element_type=F32))
    y = (jnp.concatenate(ys, axis=1) * pscale_ref[...]).astype(BF16)
    a = jnp.dot(y, wpo_ref[...], preferred_element_type=F32)

    o_q = 0
    o_kv = q_lora
    o_kpe = q_lora + kv_lora
    o_ksw = o_kpe + 128
    o_g0 = o_ksw + 128
    o_g1 = o_g0 + d
    gates0 = jax.nn.sigmoid(rest[:, o_g0:o_g0 + d] + bgate_ref[:, 0:d])
    gates1 = jax.nn.sigmoid(rest[:, o_g1:o_g1 + d] + bgate_ref[:, d:2 * d])
    ga_ref[...] = (gates0 * a).astype(BF16)
    g1_ref[...] = gates1.astype(BF16)

    ang = pos_ref[...] * invf_ref[...]
    cos2 = jnp.cos(ang)
    sin2 = jnp.sin(ang)
    q_scale = float(QK_NOPE + QK_ROPE) ** -0.5 * LOG2_E

    qn = _rms(rest[:, o_q:o_q + q_lora], gq_ref[...]).astype(BF16)
    qall = jnp.dot(qn, wq_ref[...], preferred_element_type=F32)
    kvn = _rms(rest[:, o_kv:o_kv + kv_lora], gkv_ref[...]).astype(BF16)
    kn = jnp.dot(kvn, wkn_ref[...], preferred_element_type=F32)
    vt = lax.dot_general(wvt_ref[...], kvn, (((1,), (1,)), ((), ())),
                         preferred_element_type=F32)
    vt_ref[0] = vt.astype(BF16)
    kpe = (rest[:, o_kpe:o_kpe + 128] * cos2 + rest[:, o_ksw:o_ksw + 128] * sin2).astype(BF16)
    o_pe = N_HEADS * QK_NOPE
    o_sw = o_pe + N_HEADS * QK_ROPE
    low_half = lax.broadcasted_iota(jnp.int32, (tm, 128), 1) < QK_ROPE
    for pair in range(N_HEADS // 2):
        c0 = pair * 128
        qpe2 = (qall[:, o_pe + c0:o_pe + c0 + 128] * cos2
                + qall[:, o_sw + c0:o_sw + c0 + 128] * sin2) * q_scale
        for hd, part in ((2 * pair, qpe2), (2 * pair + 1, pltpu.roll(qpe2, QK_ROPE, axis=1))):
            n0 = hd * QK_NOPE
            q_ref[:, hd * HEAD_PAD:hd * HEAD_PAD + 128] = (qall[:, n0:n0 + 128] * q_scale).astype(BF16)
            q_ref[:, hd * HEAD_PAD + 128:(hd + 1) * HEAD_PAD] = jnp.where(low_half, part, 0.0).astype(BF16)
            k_ref[:, hd * HEAD_PAD:hd * HEAD_PAD + 128] = kn[:, n0:n0 + 128].astype(BF16)
            k_ref[:, hd * HEAD_PAD + 128:(hd + 1) * HEAD_PAD] = kpe


def _mixer_in_call(x2d, pos_col, mod, g_mix, w_in_p, b_gate, w_grp, pool_scale, w_po, g_q_a, w_q_p,
                   g_kv_a, w_kn, w_vt, invf2, *, bsz, seq):
    t, d = x2d.shape
    tm = TM_IN
    tps = seq // tm
    q_lora = g_q_a.shape[-1]
    kv_lora = g_kv_a.shape[-1]
    hp = N_HEADS * HEAD_PAD
    halo_blocks = tm // POOL_HALO
    kern = functools.partial(_mixer_in_kernel, tm=tm, tiles_per_seq=tps, d=d, q_lora=q_lora,
                             kv_lora=kv_lora)
    row = lambda i: (i, 0)
    return pl.pallas_call(
        kern,
        grid=(t // tm,),
        in_specs=[
            pl.BlockSpec((tm, d), row),
            pl.BlockSpec((POOL_HALO, d), lambda i: (jnp.maximum(i * halo_blocks - 1, 0), 0)),
            pl.BlockSpec((tm, 1), row),
            pl.BlockSpec((1, 6, d), lambda i: (i // tps, 0, 0)),
            _const_spec((1, d)),
            _const_spec(w_in_p.shape),
            _const_spec((1, 2 * d)),
            _const_spec(w_grp.shape),
            _const_spec((1, d)),
            _const_spec(w_po.shape),
            _const_spec((1, q_lora)),
            _const_spec(w_q_p.shape),
            _const_spec((1, kv_lora)),
            _const_spec(w_kn.shape),
            _const_spec(w_vt.shape),
            _const_spec((1, 128)),
        ],
        out_specs=[
            pl.BlockSpec((tm, d), row),
            pl.BlockSpec((tm, d), row),
            pl.BlockSpec((tm, hp), row),
            pl.BlockSpec((tm, hp), row),
            pl.BlockSpec((1, N_HEADS * V_HEAD, tm), lambda i: (i // tps, 0, i % tps)),
        ],
        out_shape=[
            jax.ShapeDtypeStruct((t, d), BF16),
            jax.ShapeDtypeStruct((t, d), BF16),
            jax.ShapeDtypeStruct((t, hp), BF16),
            jax.ShapeDtypeStruct((t, hp), BF16),
            jax.ShapeDtypeStruct((bsz, N_HEADS * V_HEAD, seq), BF16),
        ],
        scratch_shapes=[pltpu.VMEM((tm + POOL_HALO, d), F32)],
        compiler_params=pltpu.CompilerParams(dimension_semantics=("arbitrary",),
                                             vmem_limit_bytes=VMEM_LIMIT),
        name="mixer_in",
    )(x2d, x2d, pos_col, mod, g_mix, w_in_p, b_gate, w_grp, pool_scale, w_po, g_q_a, w_q_p, g_kv_a,
      w_kn, w_vt, invf2)


def _attn_kernel(q_ref, k_ref, vt_ref, o_ref, *, tq, nq):
    seq = vt_ref.shape[2]
    vt_ext = jnp.concatenate([vt_ref[0], jnp.ones((16, seq), BF16)], axis=0)

    def scores(qi):
        q0 = qi * tq
        ln = q0 + tq
        return lax.dot_general(k_ref[0, 0:ln, :], q_ref[0, q0:q0 + tq, :], (((1,), (1,)), ((), ())),
                               preferred_element_type=F32)

    def finish(qi, s):
        q0 = qi * tq
        ln = q0 + tq
        kc = (q0 + lax.broadcasted_iota(jnp.int32, (tq, tq), 0)) // CHUNK
        qc = (q0 + lax.broadcasted_iota(jnp.int32, (tq, tq), 1)) // CHUNK
        s_diag = jnp.where(qc >= kc, s[q0:ln, :], NEG_INF)
        m = jnp.max(s_diag, axis=0, keepdims=True)
        if qi > 0:
            s_main = s[0:q0, :]
            m = jnp.maximum(m, jnp.max(s_main, axis=0, keepdims=True))
        p_diag = jnp.exp2(s_diag - m).astype(BF16)
        acc = jnp.dot(vt_ext[:, q0:ln], p_diag, preferred_element_type=F32)
        if qi > 0:
            p_main = jnp.exp2(s_main - m).astype(BF16)
            acc = acc + jnp.dot(vt_ext[:, 0:q0], p_main, preferred_element_type=F32)
        l = acc[V_HEAD:V_HEAD + 1, :]
        o_ref[0, q0:q0 + tq, :] = (acc[0:V_HEAD, :] / l).T.astype(BF16)

    order = list(range(nq))[::-1]
    s_cur = scores(order[0])
    for idx, qi in enumerate(order):
        s_next = scores(order[idx + 1]) if idx + 1 < nq else None
        finish(qi, s_cur)
        s_cur = s_next


def _attn_call(q3, k3, vt3):
    bsz, seq, _ = q3.shape
    kern = functools.partial(_attn_kernel, tq=TQ, nq=seq // TQ)
    return pl.pallas_call(
        kern,
        grid=(bsz, N_HEADS),
        in_specs=[pl.BlockSpec((1, seq, HEAD_PAD), lambda b, h: (b, 0, h)),
                  pl.BlockSpec((1, seq, HEAD_PAD), lambda b, h: (b, 0, h)),
                  pl.BlockSpec((1, V_HEAD, seq), lambda b, h: (b, h, 0))],
        out_specs=pl.BlockSpec((1, seq, V_HEAD), lambda b, h: (b, 0, h)),
        out_shape=jax.ShapeDtypeStruct((bsz, seq, N_HEADS * V_HEAD), BF16),
        compiler_params=pltpu.CompilerParams(dimension_semantics=("arbitrary", "arbitrary"),
                                             vmem_limit_bytes=VMEM_LIMIT),
        name="mla_attn",
    )(q3, k3, vt3)


def _post_kernel(o_ref, ga_ref, g1_ref, x_ref, mod_ref, wmo_ref, wout_ref, gffn_ref, wrt_ref, br_ref,
                 x1_ref, h2_ref, idx_ref, wgt_ref, *, d):
    m = jnp.dot(o_ref[...], wmo_ref[...], preferred_element_type=F32)
    merged = ga_ref[...].astype(F32) + g1_ref[...].astype(F32) * m
    gate1 = mod_ref[0, 2:3, :]
    x1 = x_ref[...] + gate1 * jnp.dot(merged.astype(BF16), wout_ref[...],
                                      preferred_element_type=F32)
    x1_ref[...] = x1
    shift2 = mod_ref[0, 3:4, :]
    scale2 = mod_ref[0, 4:5, :]
    h2 = _rms(x1, gffn_ref[...]) * (1.0 + scale2) + shift2
    _store_rows(h2_ref, _pack_bf16_pair(h2[:, 0:d // 2], h2[:, d // 2:d]))

    ne = br_ref.shape[0]
    h_hi = h2.astype(BF16)
    h_lo = (h2 - h_hi.astype(F32)).astype(BF16)
    hcat = jnp.concatenate([h_hi, h_lo], axis=1)
    half = hcat.shape[0] // 2
    lg2 = jnp.concatenate(
        [jnp.dot(hcat[0:half], wrt_ref[...], preferred_element_type=F32),
         jnp.dot(hcat[half:], wrt_ref[...], preferred_element_type=F32)], axis=0)
    lg = lg2[:, 0:128] + lg2[:, 128:256]
    logits = lg.T[0:ne, :] + br_ref[...]
    tm = logits.shape[1]
    eid = lax.broadcasted_iota(jnp.int32, (ne, tm), 0)
    vals, idxs = [], []
    cur = logits
    for _ in range(TOP_K):
        mx = jnp.max(cur, axis=0, keepdims=True)
        ix = jnp.min(jnp.where(cur == mx, eid, ne), axis=0, keepdims=True)
        vals.append(mx)
        idxs.append(ix)
        cur = jnp.where(eid == ix, -jnp.inf, cur)
    es = [jnp.exp(v - vals[0]) for v in vals]
    den = es[0] + es[1] + es[2] + es[3]
    idx_ref[...] = jnp.concatenate(idxs, axis=0)
    wgt_ref[...] = jnp.concatenate([e / den for e in es], axis=0)


def _post_call(o2d, ga, g1, x2d, mod, w_mo, w_out, g_ffn, w_rt, b_r, *, seq):
    t, d = x2d.shape
    tm = TM_POST
    tps = seq // tm
    row = lambda i: (i, 0)
    kern = functools.partial(_post_kernel, d=d)
    return pl.pallas_call(
        kern,
        grid=(t // tm,),
        in_specs=[
            pl.BlockSpec((tm, d), row),
            pl.BlockSpec((tm, d), row),
            pl.BlockSpec((tm, d), row),
            pl.BlockSpec((tm, d), row),
            pl.BlockSpec((1, 6, d), lambda i: (i // tps, 0, 0)),
            _const_spec(w_mo.shape),
            _const_spec(w_out.shape),
            _const_spec((1, d)),
            _const_spec(w_rt.shape),
            _const_spec(b_r.shape),
        ],
        out_specs=[
            pl.BlockSpec((tm, d), row),
            pl.BlockSpec((tm * ROW_SUB, 128), row),
            pl.BlockSpec((TOP_K, tm), lambda i: (0, i)),
            pl.BlockSpec((TOP_K, tm), lambda i: (0, i)),
        ],
        out_shape=[
            jax.ShapeDtypeStruct((t, d), F32),
            jax.ShapeDtypeStruct((t * ROW_SUB, 128), jnp.uint32),
            jax.ShapeDtypeStruct((TOP_K, t), jnp.int32),
            jax.ShapeDtypeStruct((TOP_K, t), F32),
        ],
        compiler_params=pltpu.CompilerParams(dimension_semantics=("arbitrary",),
                                             vmem_limit_bytes=VMEM_LIMIT),
        name="post_attn_router",
    )(o2d, ga, g1, x2d, mod, w_mo, w_out, g_ffn, w_rt, b_r)


RUN_BITS = 9
TAIL_BITS = 9


def _copy_pieces(count, src_ref, src_off, dst_ref, dst_off, sem, bits, wait=False):
    off = 0
    for b in reversed(range(bits)):
        size = (1 << b) * ROW_SUB
        take = (count >> b) & 1

        @pl.when(take == 1)
        def _(off=off, size=size):
            cp = pltpu.make_async_copy(
                src_ref.at[pl.ds(pl.multiple_of(src_off + off, ROW_SUB), size)],
                dst_ref.at[pl.ds(pl.multiple_of(dst_off + off, ROW_SUB), size)], sem)
            if wait:
                cp.wait()
            else:
                cp.start()
        off = off + take * size


def _dispatch_kernel(tab_ref, tail_ref, nused_ref, h_ref, tokc_ref, xs_hbm, stage, zbuf, sem, zsem, *,
                     tm, n_exp, n_blocks):
    i = pl.program_id(0)
    n = pl.num_programs(0)
    slot = i % 2
    n_rows = TOP_K * tm

    def wait_stage(slot_):
        pltpu.make_async_copy(stage.at[slot_], xs_hbm.at[pl.ds(0, n_rows * ROW_SUB)],
                              sem.at[slot_]).wait()

    @pl.when(i == 0)
    def _():
        zbuf[...] = jnp.zeros(zbuf.shape, zbuf.dtype)
        for wait in (False, True):
            def body(e, c, wait=wait):
                _copy_pieces(tail_ref[1, e], zbuf, 0, xs_hbm, tail_ref[0, e], zsem, TAIL_BITS, wait)
                return c
            lax.fori_loop(0, n_exp, body, 0)
        blk_rows = zbuf.shape[0]

        def zero_block(b, c):
            cp = pltpu.make_async_copy(
                zbuf, xs_hbm.at[pl.ds(pl.multiple_of(b * blk_rows, blk_rows), blk_rows)], zsem)
            cp.start()
            cp.wait()
            return c
        lax.fori_loop(nused_ref[0], n_blocks, zero_block, 0)

    @pl.when(i >= 2)
    def _():
        wait_stage(slot)

    h = _load_rows(h_ref, tm).astype(BF16)
    onehot = (tokc_ref[...] == lax.broadcasted_iota(jnp.int32, (n_rows, tm), 1)).astype(BF16)
    rows = jnp.dot(onehot, h, preferred_element_type=F32)
    dh = rows.shape[1] // 2
    _store_rows(stage.at[slot], _pack_bf16_pair(rows[:, 0:dh], rows[:, dh:]))

    def body(e, c):
        _copy_pieces(tab_ref[0, 0, e], stage.at[slot], tab_ref[0, 2, e], xs_hbm, tab_ref[0, 1, e],
                     sem.at[slot], RUN_BITS)
        return c
    lax.fori_loop(0, n_exp, body, 0)

    @pl.when(i == n - 1)
    def _():
        wait_stage(slot)
        wait_stage(1 - slot)


def _dispatch_call(tabs, tail, n_used, h2p, tok_col, *, n_xs_rows):
    nt, _, n_exp = tabs.shape
    tm = TM_FIN
    assert nt >= 2 and tm < (1 << RUN_BITS) and MOE_BLK <= (1 << TAIL_BITS)
    n_rows = TOP_K * tm
    kern = functools.partial(_dispatch_kernel, tm=tm, n_exp=n_exp, n_blocks=n_xs_rows // MOE_BLK)
    return pl.pallas_call(
        kern,
        grid=(nt,),
        in_specs=[
            pl.BlockSpec((1, 3, n_exp), lambda i: (i, 0, 0), memory_space=pltpu.SMEM),
            pl.BlockSpec(memory_space=pltpu.SMEM),
            pl.BlockSpec(memory_space=pltpu.SMEM),
            pl.BlockSpec((tm * ROW_SUB, 128), lambda i: (i, 0)),
            pl.BlockSpec((n_rows, 1), lambda i: (i, 0)),
        ],
        out_specs=pl.BlockSpec(memory_space=pl.ANY),
        out_shape=jax.ShapeDtypeStruct((n_xs_rows * ROW_SUB, 128), jnp.uint32),
        scratch_shapes=[pltpu.VMEM((2, n_rows * ROW_SUB, 128), jnp.uint32),
                        pltpu.VMEM((MOE_BLK * ROW_SUB, 128), jnp.uint32),
                        pltpu.SemaphoreType.DMA((2,)),
                        pltpu.SemaphoreType.DMA(())],
        compiler_params=pltpu.CompilerParams(dimension_semantics=("arbitrary",),
                                             vmem_limit_bytes=VMEM_LIMIT),
        name="moe_dispatch",
    )(tabs, tail, n_used, h2p, tok_col)


def _moe_kernel(blk_e_ref, nused_ref, x_ref, perm_ref, wgu_ref, bgu_ref, wd_ref, bd_ref, o_ref,
                wgu_scr, wd_scr, *, blk, f):
    s = pl.program_id(0)
    e_cur = blk_e_ref[s]
    e_prev = blk_e_ref[jnp.maximum(s - 1, 0)]

    @pl.when(jnp.logical_or(s == 0, e_cur != e_prev))
    def _():
        perm = perm_ref[...]
        for c in range(2 * f // 256):
            r = jnp.dot(wgu_ref[0, :, c * 256:(c + 1) * 256].astype(BF16), perm,
                        preferred_element_type=F32)
            wgu_scr[:, c * 128:(c + 1) * 128] = r[:, 0:128].astype(BF16)
            wgu_scr[:, f + c * 128:f + (c + 1) * 128] = r[:, 128:256].astype(BF16)
        wd_scr[...] = wd_ref[0].astype(BF16)

    @pl.when(s >= nused_ref[0])
    def _():
        o_ref[...] = jnp.zeros(o_ref.shape, o_ref.dtype)

    @pl.when(s < nused_ref[0])
    def _():
        x = _load_rows(x_ref, blk).astype(BF16)
        gu = jnp.dot(x, wgu_scr[...], preferred_element_type=F32) + bgu_ref[0]
        gate = jnp.minimum(gu[:, 0:f], SWIGLU_LIMIT)
        up = jnp.clip(gu[:, f:2 * f], -SWIGLU_LIMIT, SWIGLU_LIMIT)
        act = (up + 1.0) * (gate * jax.nn.sigmoid(SWIGLU_ALPHA * gate))
        y = jnp.dot(act.astype(BF16), wd_scr[...], preferred_element_type=F32) + bd_ref[0]
        dh = y.shape[1] // 2
        _store_rows(o_ref, _pack_bf16_pair(y[:, 0:dh], y[:, dh:]))


def _moe_call(blk_e, n_used, xs, perm, w_gu, b_gu_p, w_down, b_down):
    n_steps = blk_e.shape[0]
    blk = MOE_BLK
    _, d, f2 = w_gu.shape
    f = f2 // 2
    rows = blk * ROW_SUB
    kern = functools.partial(_moe_kernel, blk=blk, f=f)
    row_blk = lambda i, be, nu: (i, 0)
    grid_spec = pltpu.PrefetchScalarGridSpec(
        num_scalar_prefetch=2,
        grid=(n_steps,),
        in_specs=[
            pl.BlockSpec((rows, 128), row_blk),
            pl.BlockSpec((256, 256), lambda i, be, nu: (0, 0)),
            pl.BlockSpec((1, d, f2), lambda i, be, nu: (be[i], 0, 0)),
            pl.BlockSpec((1, 1, f2), lambda i, be, nu: (be[i], 0, 0)),
            pl.BlockSpec((1, f, d), lambda i, be, nu: (be[i], 0, 0)),
            pl.BlockSpec((1, 1, d), lambda i, be, nu: (be[i], 0, 0)),
        ],
        out_specs=pl.BlockSpec((rows, 128), row_blk),
        scratch_shapes=[
            pltpu.VMEM((d, f2), BF16),
            pltpu.VMEM((f, d), BF16),
        ],
    )
    return pl.pallas_call(
        kern,
        grid_spec=grid_spec,
        out_shape=jax.ShapeDtypeStruct((n_steps * rows, 128), jnp.uint32),
        compiler_params=pltpu.CompilerParams(dimension_semantics=("arbitrary",),
                                             vmem_limit_bytes=VMEM_LIMIT),
        name="moe_experts",
    )(blk_e, n_used, xs, perm, w_gu, b_gu_p, w_down, b_down)


def _final_kernel(tab_cur_ref, tab_nxt_ref, x1_ref, tok_ref, w_ref, mod_ref, fmod_ref, gfin_ref, ys_hbm,
                  o_ref, stage, sem, *, tm, n_exp):
    i = pl.program_id(0)
    n = pl.num_programs(0)
    slot = i % 2
    n_rows = TOP_K * tm

    def issue_tile(tab_ref, slot_):
        def body(e, c):
            _copy_pieces(tab_ref[0, 0, e], ys_hbm, tab_ref[0, 1, e], stage.at[slot_], tab_ref[0, 2, e],
                         sem.at[slot_], RUN_BITS)
            return c
        lax.fori_loop(0, n_exp, body, 0)

    @pl.when(i == 0)
    def _():
        issue_tile(tab_cur_ref, 0)

    @pl.when(i + 1 < n)
    def _():
        issue_tile(tab_nxt_ref, 1 - slot)

    pltpu.make_async_copy(ys_hbm.at[pl.ds(0, n_rows * ROW_SUB)], stage.at[slot], sem.at[slot]).wait()

    y = _load_rows(stage.at[slot], n_rows).astype(BF16)
    tok = tok_ref[0]
    a = jnp.where(lax.broadcasted_iota(jnp.int32, (tm, n_rows), 0) == tok, w_ref[0], 0.0)
    a_hi = a.astype(BF16)
    a_lo = (a - a_hi.astype(F32)).astype(BF16)
    moe = (jnp.dot(a_hi, y, preferred_element_type=F32)
           + jnp.dot(a_lo, y, preferred_element_type=F32))
    gate2 = mod_ref[0, 5:6, :]
    x2 = x1_ref[...] + gate2 * moe
    fshift = fmod_ref[0, 0:1, :]
    fscale = fmod_ref[0, 1:2, :]
    o_ref[...] = _rms(x2, gfin_ref[...]) * (1.0 + fscale) + fshift


def _final_call(tabs, x1, stage_tok, stage_w, mod, fmod, g_final, ys, *, seq):
    t, d = x1.shape
    tm = TM_FIN
    assert tm < (1 << RUN_BITS)
    tps = seq // tm
    nt = t // tm
    n_exp = tabs.shape[2]
    n_rows = TOP_K * tm
    kern = functools.partial(_final_kernel, tm=tm, n_exp=n_exp)
    tab_spec = lambda ahead: pl.BlockSpec((1, 3, n_exp), lambda i: (jnp.minimum(i + ahead, nt - 1), 0, 0),
                                          memory_space=pltpu.SMEM)
    return pl.pallas_call(
        kern,
        grid=(nt,),
        in_specs=[
            tab_spec(0),
            tab_spec(1),
            pl.BlockSpec((tm, d), lambda i: (i, 0)),
            pl.BlockSpec((1, 1, n_rows), lambda i: (i, 0, 0)),
            pl.BlockSpec((1, 1, n_rows), lambda i: (i, 0, 0)),
            pl.BlockSpec((1, 6, d), lambda i: (i // tps, 0, 0)),
            pl.BlockSpec((1, 2, d), lambda i: (i // tps, 0, 0)),
            _const_spec((1, d)),
            pl.BlockSpec(memory_space=pl.ANY),
        ],
        out_specs=pl.BlockSpec((tm, d), lambda i: (i, 0)),
        out_shape=jax.ShapeDtypeStruct((t, d), F32),
        scratch_shapes=[pltpu.VMEM((2, n_rows * ROW_SUB, 128), jnp.uint32),
                        pltpu.SemaphoreType.DMA((2,))],
        compiler_params=pltpu.CompilerParams(dimension_semantics=("arbitrary",),
                                             vmem_limit_bytes=VMEM_LIMIT),
        name="combine_final",
    )(tabs, tabs, x1, stage_tok, stage_w, mod, fmod, g_final, ys)


def _prep_w_in(w_in, d, q_lora, kv_lora):
    o_kpe = d + q_lora + kv_lora
    o_g = o_kpe + QK_ROPE
    half = QK_ROPE // 2
    kpe = w_in[:, o_kpe:o_kpe + QK_ROPE]
    zpad = jnp.zeros((d, 128 - QK_ROPE), w_in.dtype)
    ksw = jnp.concatenate([-kpe[:, half:], kpe[:, :half]], axis=1)
    return jnp.concatenate([w_in[:, :o_kpe], kpe, zpad, ksw, zpad, w_in[:, o_g:]], axis=1).astype(BF16)


def _prep_w_q(w_q_b):
    ql = w_q_b.shape[0]
    hd = QK_NOPE + QK_ROPE
    half = QK_ROPE // 2
    w = w_q_b.reshape(ql, N_HEADS, hd)
    nope = w[:, :, :QK_NOPE]
    pe = w[:, :, QK_NOPE:]
    sw = jnp.concatenate([-pe[:, :, half:], pe[:, :, :half]], axis=2)
    return jnp.concatenate([nope.reshape(ql, -1), pe.reshape(ql, -1), sw.reshape(ql, -1)],
                           axis=1).astype(BF16)


def _prep_w_kv(w_kv_b):
    kvl = w_kv_b.shape[0]
    w = w_kv_b.reshape(kvl, N_HEADS, QK_NOPE + V_HEAD)
    w_kn = w[:, :, :QK_NOPE].reshape(kvl, -1).astype(BF16)
    w_vt = w[:, :, QK_NOPE:].reshape(kvl, -1).T.astype(BF16)
    return w_kn, w_vt


def kernel(x, c, positions, w_mod, b_mod, g_mix, w_in, b_gate, w_pool_grp, pool_scale, w_pool_out,
           g_q_a, w_q_b, g_kv_a, w_kv_b, w_mla_out, w_out, g_ffn, w_router, b_router, w_gu, b_gu,
           w_down, b_down, g_final, w_fmod, b_fmod):
    bsz, seq, d = x.shape
    t = bsz * seq
    depth = w_mod.shape[0]
    assert depth == 1
    assert seq % TQ == 0 and seq % TM_IN == 0 and seq % TM_POST == 0 and seq % TM_FIN == 0
    q_lora = g_q_a.shape[-1]
    kv_lora = g_kv_a.shape[-1]
    n_exp = w_gu.shape[1]
    f = w_gu.shape[-1] // 2
    blk = MOE_BLK

    x2d = x.reshape(t, d)
    pos_col = positions.astype(F32).reshape(t, 1)
    inv_freq = 1.0 / (ROPE_THETA ** (jnp.arange(0, QK_ROPE, 2, dtype=F32) / QK_ROPE))
    invf2 = jnp.tile(inv_freq, 128 // (QK_ROPE // 2)).reshape(1, 128)

    mod = _mod_call(c, w_mod[0], b_mod[0]).reshape(bsz, 6, d)
    fmod = _mod_call(c, w_fmod, b_fmod).reshape(bsz, 2, d)

    w_in_p = _prep_w_in(w_in[0], d, q_lora, kv_lora)
    w_q_p = _prep_w_q(w_q_b[0])
    w_kn, w_vt = _prep_w_kv(w_kv_b[0])
    ga, g1, q2, k2, vt3 = _mixer_in_call(
        x2d, pos_col, mod, g_mix[0].reshape(1, d), w_in_p, b_gate[0].reshape(1, 2 * d),
        w_pool_grp[0].astype(BF16), pool_scale[0].reshape(1, d), w_pool_out[0].astype(BF16),
        g_q_a[0].reshape(1, q_lora), w_q_p, g_kv_a[0].reshape(1, kv_lora), w_kn, w_vt, invf2,
        bsz=bsz, seq=seq)

    hp = N_HEADS * HEAD_PAD
    o = _attn_call(q2.reshape(bsz, seq, hp), k2.reshape(bsz, seq, hp), vt3)

    w_r = w_router[0]
    w_r_hi = w_r.astype(BF16)
    w_r_lo = (w_r - w_r_hi.astype(F32)).astype(BF16)
    lane_pad = lambda a: jnp.pad(a, ((0, 0), (0, 128 - n_exp)))
    w_rt = jnp.concatenate(
        [jnp.concatenate([lane_pad(w_r_hi), lane_pad(w_r_lo)], axis=1),
         jnp.concatenate([lane_pad(w_r_hi), jnp.zeros((d, 128), BF16)], axis=1)], axis=0)
    x1, h2p, idx_t, wgt_t = _post_call(
        o.reshape(t, d), ga, g1, x2d, mod, w_mla_out[0].astype(BF16), w_out[0].astype(BF16),
        g_ffn[0].reshape(1, d), w_rt, b_router[0].reshape(n_exp, 1), seq=seq)

    n_slots = t * TOP_K
    n_rows = n_slots + n_exp * blk
    n_blocks = n_rows // blk
    flat_e = idx_t.T.reshape(-1)
    w_flat = wgt_t.T.reshape(-1)
    e_ids = jnp.arange(n_exp, dtype=jnp.int32)
    tt = TM_FIN
    n_tiles = t // tt
    cnt = jnp.sum(flat_e.reshape(n_tiles, tt * TOP_K, 1) == e_ids[None, None, :], axis=1,
                  dtype=jnp.int32)
    counts = jnp.sum(cnt, axis=0)
    padded = (counts + blk - 1) // blk * blk
    pad_end = jnp.cumsum(padded)
    pad_start = pad_end - padded
    blk_start = jnp.arange(n_blocks, dtype=jnp.int32) * blk
    blk_e = jnp.minimum(jnp.sum(pad_end[None, :] <= blk_start[:, None], axis=1, dtype=jnp.int32),
                        n_exp - 1)
    n_used = (pad_end[-1:] // blk).astype(jnp.int32)
    tail = jnp.stack([(pad_start + counts) * ROW_SUB, padded - counts], axis=0).astype(jnp.int32)

    before = jnp.cumsum(cnt, axis=0) - cnt
    run_end = jnp.cumsum(cnt, axis=1)
    run_start = run_end - cnt
    tabs = jnp.stack([cnt, (pad_start[None, :] + before) * ROW_SUB, run_start * ROW_SUB], axis=1)
    slot_ids = jnp.arange(n_slots, dtype=jnp.int32)
    tile_key = (slot_ids // (tt * TOP_K)) * n_exp + flat_e
    _, stage_slot, stage_w = lax.sort((tile_key, slot_ids, w_flat), num_keys=1, is_stable=True)
    stage_tok = (stage_slot // TOP_K) % tt

    col = jnp.arange(256, dtype=jnp.int32)[None, :]
    row = jnp.arange(256, dtype=jnp.int32)[:, None]
    perm = (row == jnp.where(col < 128, 2 * col, 2 * (col - 128) + 1)).astype(BF16)
    b_gu_p = jnp.concatenate([b_gu[0][:, 0::2], b_gu[0][:, 1::2]], axis=-1).reshape(n_exp, 1, 2 * f)
    xs = _dispatch_call(tabs, tail, n_used, h2p, stage_tok.reshape(n_slots, 1), n_xs_rows=n_rows)
    ys = _moe_call(blk_e, n_used, xs, perm, w_gu[0], b_gu_p, w_down[0], b_down[0].reshape(n_exp, 1, d))

    out = _final_call(tabs, x1, stage_tok.reshape(n_tiles, 1, tt * TOP_K),
                      stage_w.reshape(n_tiles, 1, tt * TOP_K), mod, fmod, g_final.reshape(1, d), ys,
                      seq=seq)
    return out.reshape(bsz, seq, d)
```

```python
import functools

import jax
import jax.numpy as jnp
from jax import lax
from jax.experimental import pallas as pl
from jax.experimental.pallas import tpu as pltpu

F32 = jnp.float32
BF16 = jnp.bfloat16

CHUNK = 64
POOL_WINDOWS = (2, 4, 8, 16)
POOL_HALO = 16
N_HEADS = 8
QK_NOPE = 128
QK_ROPE = 64
V_HEAD = 128
HEAD_PAD = 256
ROPE_THETA = 10000.0
TOP_K = 4
SWIGLU_LIMIT = 7.0
SWIGLU_ALPHA = 1.702
NORM_EPS = 1e-6
NEG_INF = -1e30
LOG2_E = 1.4426950408889634

VMEM_LIMIT = 56 * 1024 * 1024

TM_IN = 256
TQ = 512
TM_POST = 512
MOE_BLK = 512
TM_FIN = 256
ROW_SUB = 4


def _const_spec(shape):
    nd = len(shape)
    return pl.BlockSpec(shape, lambda *_: (0,) * nd, pipeline_mode=pl.Buffered(1))


def _rms(xf, g):
    return xf * lax.rsqrt(jnp.mean(xf * xf, axis=-1, keepdims=True) + NORM_EPS) * g


def _pack_bf16_pair(lo, hi):
    lo_b = lax.bitcast_convert_type(lo.astype(BF16).astype(F32), jnp.uint32)
    hi_b = lax.bitcast_convert_type(hi.astype(BF16).astype(F32), jnp.uint32)
    return (hi_b & jnp.uint32(0xFFFF0000)) | (lo_b >> 16)


def _unpack_bf16_pair(p):
    lo = lax.bitcast_convert_type(p << 16, F32)
    hi = lax.bitcast_convert_type(p & jnp.uint32(0xFFFF0000), F32)
    return lo, hi


def _store_rows(ref, packed):
    n = packed.shape[0]
    for q in range(ROW_SUB):
        ref[pl.ds(q, n, stride=ROW_SUB), :] = packed[:, q * 128:(q + 1) * 128]


def _load_rows(ref, n):
    los, his = [], []
    for q in range(ROW_SUB):
        lo, hi = _unpack_bf16_pair(ref[pl.ds(q, n, stride=ROW_SUB), :])
        los.append(lo)
        his.append(hi)
    return jnp.concatenate(los + his, axis=1)


def _mod_kernel(c_ref, w_ref, b_ref, o_ref):
    c = c_ref[...]
    c_act = c * jax.nn.sigmoid(c)
    o_ref[...] = jnp.dot(c_act, w_ref[...], preferred_element_type=F32,
                         precision=lax.Precision.HIGHEST) + b_ref[...]


def _mod_call(c, w, b, tn=1024):
    bsz, d = c.shape
    n = w.shape[1]
    return pl.pallas_call(
        _mod_kernel,
        grid=(n // tn,),
        in_specs=[pl.BlockSpec((bsz, d), lambda j: (0, 0)),
                  pl.BlockSpec((d, tn), lambda j: (0, j)),
                  pl.BlockSpec((1, tn), lambda j: (0, j))],
        out_specs=pl.BlockSpec((bsz, tn), lambda j: (0, j)),
        out_shape=jax.ShapeDtypeStruct((bsz, n), F32),
        compiler_params=pltpu.CompilerParams(dimension_semantics=("arbitrary",),
                                             vmem_limit_bytes=VMEM_LIMIT),
        name="adaln_mod",
    )(c, w, b.reshape(1, n))


def _mixer_in_kernel(x_ref, xh_ref, pos_ref, mod_ref, gmix_ref, win_ref, bgate_ref, wgrp_ref,
                     pscale_ref, wpo_ref, gq_ref, wq_ref, gkv_ref, wkn_ref, wvt_ref, invf_ref,
                     ga_ref, g1_ref, q_ref, k_ref, vt_ref, u_scr, *, tm, tiles_per_seq, d, q_lora,
                     kv_lora):
    i = pl.program_id(0)
    t_in_seq = i % tiles_per_seq
    is_start = t_in_seq == 0
    shift1 = mod_ref[0, 0:1, :]
    scale1 = mod_ref[0, 1:2, :]
    gmix = gmix_ref[...]

    def prenorm(xf):
        return (_rms(xf, gmix) * (1.0 + scale1) + shift1).astype(BF16)

    h = prenorm(x_ref[...])
    hh = prenorm(xh_ref[...])
    u = jnp.dot(h, win_ref[:, 0:d], preferred_element_type=F32)
    uh = jnp.dot(hh, win_ref[:, 0:d], preferred_element_type=F32)
    u_scr[0:POOL_HALO, :] = jnp.where(is_start, 0.0, uh)
    u_scr[POOL_HALO:POOL_HALO + tm, :] = u
    rest = jnp.dot(h, win_ref[:, d:], preferred_element_type=F32)

    gw = d // len(POOL_WINDOWS)
    tseq = t_in_seq * tm + lax.broadcasted_iota(jnp.int32, (tm, 1), 0)
    ys = []
    for g, w in enumerate(POOL_WINDOWS):
        c0 = g * gw
        ug = u_scr[POOL_HALO:POOL_HALO + tm, c0:c0 + gw]
        acc = ug
        for j in range(1, w):
            acc = acc + u_scr[POOL_HALO - j:POOL_HALO - j + tm, c0:c0 + gw]
        cnt = jnp.minimum(tseq + 1, w).astype(F32)
        mixed = (acc / cnt - ug).astype(BF16)
        ys.append(jnp.dot(mixed, wgrp_ref[g], preferred_element_type=F32))
    y = (jnp.concatenate(ys, axis=1) * pscale_ref[...]).astype(BF16)
    a = jnp.dot(y, wpo_ref[...], preferred_element_type=F32)

    o_q = 0
    o_kv = q_lora
    o_kpe = q_lora + kv_lora
    o_ksw = o_kpe + 128
    o_g0 = o_ksw + 128
    o_g1 = o_g0 + d
    gates0 = jax.nn.sigmoid(rest[:, o_g0:o_g0 + d] + bgate_ref[:, 0:d])
    gates1 = jax.nn.sigmoid(rest[:, o_g1:o_g1 + d] + bgate_ref[:, d:2 * d])
    ga_ref[...] = (gates0 * a).astype(BF16)
    g1_ref[...] = gates1.astype(BF16)

    ang = pos_ref[...] * invf_ref[...]
    cos2 = jnp.cos(ang)
    sin2 = jnp.sin(ang)
    q_scale = float(QK_NOPE + QK_ROPE) ** -0.5 * LOG2_E

    qn = _rms(rest[:, o_q:o_q + q_lora], gq_ref[...]).astype(BF16)
    qall = jnp.dot(qn, wq_ref[...], preferred_element_type=F32)
    kvn = _rms(rest[:, o_kv:o_kv + kv_lora], gkv_ref[...]).astype(BF16)
    kn = jnp.dot(kvn, wkn_ref[...], preferred_element_type=F32)
    vt = lax.dot_general(wvt_ref[...], kvn, (((1,), (1,)), ((), ())),
                         preferred_element_type=F32)
    vt_ref[0] = vt.astype(BF16)
    kpe = (rest[:, o_kpe:o_kpe + 128] * cos2 + rest[:, o_ksw:o_ksw + 128] * sin2).astype(BF16)
    o_pe = N_HEADS * QK_NOPE
    o_sw = o_pe + N_HEADS * QK_ROPE
    low_half = lax.broadcasted_iota(jnp.int32, (tm, 128), 1) < QK_ROPE
    for pair in range(N_HEADS // 2):
        c0 = pair * 128
        qpe2 = (qall[:, o_pe + c0:o_pe + c0 + 128] * cos2
                + qall[:, o_sw + c0:o_sw + c0 + 128] * sin2) * q_scale
        for hd, part in ((2 * pair, qpe2), (2 * pair + 1, pltpu.roll(qpe2, QK_ROPE, axis=1))):
            n0 = hd * QK_NOPE
            q_ref[:, hd * HEAD_PAD:hd * HEAD_PAD + 128] = (qall[:, n0:n0 + 128] * q_scale).astype(BF16)
            q_ref[:, hd * HEAD_PAD + 128:(hd + 1) * HEAD_PAD] = jnp.where(low_half, part, 0.0).astype(BF16)
            k_ref[:, hd * HEAD_PAD:hd * HEAD_PAD + 128] = kn[:, n0:n0 + 128].astype(BF16)
            k_ref[:, hd * HEAD_PAD + 128:(hd + 1) * HEAD_PAD] = kpe


def _mixer_in_call(x2d, pos_col, mod, g_mix, w_in_p, b_gate, w_grp, pool_scale, w_po, g_q_a, w_q_p,
                   g_kv_a, w_kn, w_vt, invf2, *, bsz, seq):
    t, d = x2d.shape
    tm = TM_IN
    tps = seq // tm
    q_lora = g_q_a.shape[-1]
    kv_lora = g_kv_a.shape[-1]
    hp = N_HEADS * HEAD_PAD
    halo_blocks = tm // POOL_HALO
    kern = functools.partial(_mixer_in_kernel, tm=tm, tiles_per_seq=tps, d=d, q_lora=q_lora,
                             kv_lora=kv_lora)
    row = lambda i: (i, 0)
    return pl.pallas_call(
        kern,
        grid=(t // tm,),
        in_specs=[
            pl.BlockSpec((tm, d), row),
            pl.BlockSpec((POOL_HALO, d), lambda i: (jnp.maximum(i * halo_blocks - 1, 0), 0)),
            pl.BlockSpec((tm, 1), row),
            pl.BlockSpec((1, 6, d), lambda i: (i // tps, 0, 0)),
            _const_spec((1, d)),
            _const_spec(w_in_p.shape),
            _const_spec((1, 2 * d)),
            _const_spec(w_grp.shape),
            _const_spec((1, d)),
            _const_spec(w_po.shape),
            _const_spec((1, q_lora)),
            _const_spec(w_q_p.shape),
            _const_spec((1, kv_lora)),
            _const_spec(w_kn.shape),
            _const_spec(w_vt.shape),
            _const_spec((1, 128)),
        ],
        out_specs=[
            pl.BlockSpec((tm, d), row),
            pl.BlockSpec((tm, d), row),
            pl.BlockSpec((tm, hp), row),
            pl.BlockSpec((tm, hp), row),
            pl.BlockSpec((1, N_HEADS * V_HEAD, tm), lambda i: (i // tps, 0, i % tps)),
        ],
        out_shape=[
            jax.ShapeDtypeStruct((t, d), BF16),
            jax.ShapeDtypeStruct((t, d), BF16),
            jax.ShapeDtypeStruct((t, hp), BF16),
            jax.ShapeDtypeStruct((t, hp), BF16),
            jax.ShapeDtypeStruct((bsz, N_HEADS * V_HEAD, seq), BF16),
        ],
        scratch_shapes=[pltpu.VMEM((tm + POOL_HALO, d), F32)],
        compiler_params=pltpu.CompilerParams(dimension_semantics=("arbitrary",),
                                             vmem_limit_bytes=VMEM_LIMIT),
        name="mixer_in",
    )(x2d, x2d, pos_col, mod, g_mix, w_in_p, b_gate, w_grp, pool_scale, w_po, g_q_a, w_q_p, g_kv_a,
      w_kn, w_vt, invf2)


def _attn_kernel(q_ref, k_ref, vt_ref, o_ref, *, tq, nq):
    seq = vt_ref.shape[2]
    vt_ext = jnp.concatenate([vt_ref[0], jnp.ones((16, seq), BF16)], axis=0)

    def scores(qi):
        q0 = qi * tq
        ln = q0 + tq
        return lax.dot_general(k_ref[0, 0:ln, :], q_ref[0, q0:q0 + tq, :], (((1,), (1,)), ((), ())),
                               preferred_element_type=F32)

    def finish(qi, s):
        q0 = qi * tq
        ln = q0 + tq
        kc = (q0 + lax.broadcasted_iota(jnp.int32, (tq, tq), 0)) // CHUNK
        qc = (q0 + lax.broadcasted_iota(jnp.int32, (tq, tq), 1)) // CHUNK
        s_diag = jnp.where(qc >= kc, s[q0:ln, :], NEG_INF)
        m = jnp.max(s_diag, axis=0, keepdims=True)
        if qi > 0:
            s_main = s[0:q0, :]
            m = jnp.maximum(m, jnp.max(s_main, axis=0, keepdims=True))
        p_diag = jnp.exp2(s_diag - m).astype(BF16)
        acc = jnp.dot(vt_ext[:, q0:ln], p_diag, preferred_element_type=F32)
        if qi > 0:
            p_main = jnp.exp2(s_main - m).astype(BF16)
            acc = acc + jnp.dot(vt_ext[:, 0:q0], p_main, preferred_element_type=F32)
        l = acc[V_HEAD:V_HEAD + 1, :]
        o_ref[0, q0:q0 + tq, :] = (acc[0:V_HEAD, :] / l).T.astype(BF16)

    order = list(range(nq))[::-1]
    s_cur = scores(order[0])
    for idx, qi in enumerate(order):
        s_next = scores(order[idx + 1]) if idx + 1 < nq else None
        finish(qi, s_cur)
        s_cur = s_next


def _attn_call(q3, k3, vt3):
    bsz, seq, _ = q3.shape
    kern = functools.partial(_attn_kernel, tq=TQ, nq=seq // TQ)
    return pl.pallas_call(
        kern,
        grid=(bsz, N_HEADS),
        in_specs=[pl.BlockSpec((1, seq, HEAD_PAD), lambda b, h: (b, 0, h)),
                  pl.BlockSpec((1, seq, HEAD_PAD), lambda b, h: (b, 0, h)),
                  pl.BlockSpec((1, V_HEAD, seq), lambda b, h: (b, h, 0))],
        out_specs=pl.BlockSpec((1, seq, V_HEAD), lambda b, h: (b, 0, h)),
        out_shape=jax.ShapeDtypeStruct((bsz, seq, N_HEADS * V_HEAD), BF16),
        compiler_params=pltpu.CompilerParams(dimension_semantics=("arbitrary", "arbitrary"),
                                             vmem_limit_bytes=VMEM_LIMIT),
        name="mla_attn",
    )(q3, k3, vt3)


def _post_kernel(o_ref, ga_ref, g1_ref, x_ref, mod_ref, wmo_ref, wout_ref, gffn_ref, wrt_ref, br_ref,
                 x1_ref, h2_ref, pos_ref, wgt_ref, cnt_ref, *, d):
    m = jnp.dot(o_ref[...], wmo_ref[...], preferred_element_type=F32)
    merged = ga_ref[...].astype(F32) + g1_ref[...].astype(F32) * m
    gate1 = mod_ref[0, 2:3, :]
    x1 = x_ref[...] + gate1 * jnp.dot(merged.astype(BF16), wout_ref[...],
                                      preferred_element_type=F32)
    x1_ref[...] = x1
    shift2 = mod_ref[0, 3:4, :]
    scale2 = mod_ref[0, 4:5, :]
    h2 = _rms(x1, gffn_ref[...]) * (1.0 + scale2) + shift2
    _store_rows(h2_ref, _pack_bf16_pair(h2[:, 0:d // 2], h2[:, d // 2:d]))

    ne = br_ref.shape[0]
    h_hi = h2.astype(BF16)
    h_lo = (h2 - h_hi.astype(F32)).astype(BF16)
    hcat = jnp.concatenate([h_hi, h_lo], axis=1)
    half = hcat.shape[0] // 2
    lg2 = jnp.concatenate(
        [jnp.dot(hcat[0:half], wrt_ref[...], preferred_element_type=F32),
         jnp.dot(hcat[half:], wrt_ref[...], preferred_element_type=F32)], axis=0)
    lg = lg2[:, 0:128] + lg2[:, 128:256]
    logits = lg.T[0:ne, :] + br_ref[...]
    tm = logits.shape[1]
    eid = lax.broadcasted_iota(jnp.int32, (ne, tm), 0)
    vals, idxs = [], []
    cur = logits
    for _ in range(TOP_K):
        mx = jnp.max(cur, axis=0, keepdims=True)
        ix = jnp.min(jnp.where(cur == mx, eid, ne), axis=0, keepdims=True)
        vals.append(mx)
        idxs.append(ix)
        cur = jnp.where(eid == ix, -jnp.inf, cur)
    es = [jnp.exp(v - vals[0]) for v in vals]
    den = es[0] + es[1] + es[2] + es[3]
    wgt_ref[...] = jnp.concatenate([e / den for e in es], axis=0)

    tt = TM_FIN
    t_row = lax.broadcasted_iota(jnp.int32, (tm, tm), 0)
    t_col = lax.broadcasted_iota(jnp.int32, (tm, tm), 1)
    same_tile = (t_row // tt) == (t_col // tt)
    in_tile = jnp.where(same_tile, 1.0, 0.0).astype(BF16)
    earlier = jnp.where(same_tile & (t_row < t_col), 1.0, 0.0).astype(BF16)
    lower = jnp.where(lax.broadcasted_iota(jnp.int32, (ne, ne), 1)
                      < lax.broadcasted_iota(jnp.int32, (ne, ne), 0), 1.0, 0.0).astype(BF16)
    picks = [eid == ix for ix in idxs]
    routed = jnp.where(picks[0] | picks[1] | picks[2] | picks[3], 1.0, 0.0).astype(BF16)
    before = jnp.dot(routed, earlier, preferred_element_type=F32)
    cnt_b = jnp.dot(routed, in_tile, preferred_element_type=F32)
    run_start = jnp.dot(lower, cnt_b.astype(BF16), preferred_element_type=F32)
    base = run_start + before
    pos_ref[...] = jnp.concatenate(
        [jnp.sum(jnp.where(pk, base, 0.0), axis=0, keepdims=True) for pk in picks],
        axis=0).astype(jnp.int32)
    tile_sel = jnp.where(lax.broadcasted_iota(jnp.int32, (8, tm), 0)
                         == lax.broadcasted_iota(jnp.int32, (8, tm), 1) // tt, 1.0, 0.0).astype(BF16)
    cnt8 = lax.dot_general(tile_sel, routed, (((1,), (1,)), ((), ())),
                           preferred_element_type=F32)
    cnt_ref[0] = jnp.concatenate([cnt8, jnp.zeros((8, 128 - ne), F32)], axis=1).astype(jnp.int32)


def _post_call(o2d, ga, g1, x2d, mod, w_mo, w_out, g_ffn, w_rt, b_r, *, seq):
    t, d = x2d.shape
    tm = TM_POST
    tps = seq // tm
    row = lambda i: (i, 0)
    n_exp = b_r.shape[0]
    assert tm % TM_FIN == 0 and tm // TM_FIN <= 8 and n_exp <= 128
    kern = functools.partial(_post_kernel, d=d)
    return pl.pallas_call(
        kern,
        grid=(t // tm,),
        in_specs=[
            pl.BlockSpec((tm, d), row),
            pl.BlockSpec((tm, d), row),
            pl.BlockSpec((tm, d), row),
            pl.BlockSpec((tm, d), row),
            pl.BlockSpec((1, 6, d), lambda i: (i // tps, 0, 0)),
            _const_spec(w_mo.shape),
            _const_spec(w_out.shape),
            _const_spec((1, d)),
            _const_spec(w_rt.shape),
            _const_spec(b_r.shape),
        ],
        out_specs=[
            pl.BlockSpec((tm, d), row),
            pl.BlockSpec((tm * ROW_SUB, 128), row),
            pl.BlockSpec((TOP_K, tm), lambda i: (0, i)),
            pl.BlockSpec((TOP_K, tm), lambda i: (0, i)),
            pl.BlockSpec((1, 8, 128), lambda i: (i, 0, 0)),
        ],
        out_shape=[
            jax.ShapeDtypeStruct((t, d), F32),
            jax.ShapeDtypeStruct((t * ROW_SUB, 128), jnp.uint32),
            jax.ShapeDtypeStruct((TOP_K, t), jnp.int32),
            jax.ShapeDtypeStruct((TOP_K, t), F32),
            jax.ShapeDtypeStruct((t // tm, 8, 128), jnp.int32),
        ],
        compiler_params=pltpu.CompilerParams(dimension_semantics=("arbitrary",),
                                             vmem_limit_bytes=VMEM_LIMIT),
        name="post_attn_router",
    )(o2d, ga, g1, x2d, mod, w_mo, w_out, g_ffn, w_rt, b_r)


RUN_BITS = 9
TAIL_BITS = 9


def _copy_pieces(count, src_ref, src_off, dst_ref, dst_off, sem, bits, wait=False):
    off = 0
    for b in reversed(range(bits)):
        size = (1 << b) * ROW_SUB
        take = (count >> b) & 1

        @pl.when(take == 1)
        def _(off=off, size=size):
            cp = pltpu.make_async_copy(
                src_ref.at[pl.ds(pl.multiple_of(src_off + off, ROW_SUB), size)],
                dst_ref.at[pl.ds(pl.multiple_of(dst_off + off, ROW_SUB), size)], sem)
            if wait:
                cp.wait()
            else:
                cp.start()
        off = off + take * size


def _dispatch_kernel(tab_ref, tail_ref, nused_ref, h_ref, pos_ref, xs_hbm, stage, zbuf, sem, zsem, *,
                     tm, n_exp, n_blocks):
    i = pl.program_id(0)
    n = pl.num_programs(0)
    slot = i % 2
    n_rows = TOP_K * tm

    def wait_stage(slot_):
        pltpu.make_async_copy(stage.at[slot_], xs_hbm.at[pl.ds(0, n_rows * ROW_SUB)],
                              sem.at[slot_]).wait()

    @pl.when(i == 0)
    def _():
        zbuf[...] = jnp.zeros(zbuf.shape, zbuf.dtype)
        for wait in (False, True):
            def body(e, c, wait=wait):
                _copy_pieces(tail_ref[1, e], zbuf, 0, xs_hbm, tail_ref[0, e], zsem, TAIL_BITS, wait)
                return c
            lax.fori_loop(0, n_exp, body, 0)
        blk_rows = zbuf.shape[0]

        def zero_block(b, c):
            cp = pltpu.make_async_copy(
                zbuf, xs_hbm.at[pl.ds(pl.multiple_of(b * blk_rows, blk_rows), blk_rows)], zsem)
            cp.start()
            cp.wait()
            return c
        lax.fori_loop(nused_ref[0], n_blocks, zero_block, 0)

    @pl.when(i >= 2)
    def _():
        wait_stage(slot)

    h = _load_rows(h_ref, tm).astype(BF16)
    row_id = lax.broadcasted_iota(jnp.int32, (n_rows, tm), 0)
    hit = row_id == pos_ref[0:1, :]
    for k in range(1, TOP_K):
        hit = hit | (row_id == pos_ref[k:k + 1, :])
    onehot = jnp.where(hit, 1.0, 0.0).astype(BF16)
    rows = jnp.dot(onehot, h, preferred_element_type=F32)
    dh = rows.shape[1] // 2
    _store_rows(stage.at[slot], _pack_bf16_pair(rows[:, 0:dh], rows[:, dh:]))

    def body(e, c):
        _copy_pieces(tab_ref[0, 0, e], stage.at[slot], tab_ref[0, 2, e], xs_hbm, tab_ref[0, 1, e],
                     sem.at[slot], RUN_BITS)
        return c
    lax.fori_loop(0, n_exp, body, 0)

    @pl.when(i == n - 1)
    def _():
        wait_stage(slot)
        wait_stage(1 - slot)


def _dispatch_call(tabs, tail, n_used, h2p, pos_t, *, n_xs_rows):
    nt, _, n_exp = tabs.shape
    tm = TM_FIN
    assert nt >= 2 and tm < (1 << RUN_BITS) and MOE_BLK <= (1 << TAIL_BITS)
    n_rows = TOP_K * tm
    kern = functools.partial(_dispatch_kernel, tm=tm, n_exp=n_exp, n_blocks=n_xs_rows // MOE_BLK)
    return pl.pallas_call(
        kern,
        grid=(nt,),
        in_specs=[
            pl.BlockSpec((1, 3, n_exp), lambda i: (i, 0, 0), memory_space=pltpu.SMEM),
            pl.BlockSpec(memory_space=pltpu.SMEM),
            pl.BlockSpec(memory_space=pltpu.SMEM),
            pl.BlockSpec((tm * ROW_SUB, 128), lambda i: (i, 0)),
            pl.BlockSpec((TOP_K, tm), lambda i: (0, i)),
        ],
        out_specs=pl.BlockSpec(memory_space=pl.ANY),
        out_shape=jax.ShapeDtypeStruct((n_xs_rows * ROW_SUB, 128), jnp.uint32),
        scratch_shapes=[pltpu.VMEM((2, n_rows * ROW_SUB, 128), jnp.uint32),
                        pltpu.VMEM((MOE_BLK * ROW_SUB, 128), jnp.uint32),
                        pltpu.SemaphoreType.DMA((2,)),
                        pltpu.SemaphoreType.DMA(())],
        compiler_params=pltpu.CompilerParams(dimension_semantics=("arbitrary",),
                                             vmem_limit_bytes=VMEM_LIMIT),
        name="moe_dispatch",
    )(tabs, tail, n_used, h2p, pos_t)


def _moe_kernel(blk_e_ref, nused_ref, x_ref, perm_ref, wgu_ref, bgu_ref, wd_ref, bd_ref, o_ref,
                wgu_scr, wd_scr, *, blk, f):
    s = pl.program_id(0)
    e_cur = blk_e_ref[s]
    e_prev = blk_e_ref[jnp.maximum(s - 1, 0)]

    @pl.when(jnp.logical_or(s == 0, e_cur != e_prev))
    def _():
        perm = perm_ref[...]
        for c in range(2 * f // 256):
            r = jnp.dot(wgu_ref[0, :, c * 256:(c + 1) * 256].astype(BF16), perm,
                        preferred_element_type=F32)
            wgu_scr[:, c * 128:(c + 1) * 128] = r[:, 0:128].astype(BF16)
            wgu_scr[:, f + c * 128:f + (c + 1) * 128] = r[:, 128:256].astype(BF16)
        wd_scr[...] = wd_ref[0].astype(BF16)

    @pl.when(s >= nused_ref[0])
    def _():
        o_ref[...] = jnp.zeros(o_ref.shape, o_ref.dtype)

    @pl.when(s < nused_ref[0])
    def _():
        x = _load_rows(x_ref, blk).astype(BF16)
        gu = jnp.dot(x, wgu_scr[...], preferred_element_type=F32) + bgu_ref[0]
        gate = jnp.minimum(gu[:, 0:f], SWIGLU_LIMIT)
        up = jnp.clip(gu[:, f:2 * f], -SWIGLU_LIMIT, SWIGLU_LIMIT)
        act = (up + 1.0) * (gate * jax.nn.sigmoid(SWIGLU_ALPHA * gate))
        y = jnp.dot(act.astype(BF16), wd_scr[...], preferred_element_type=F32) + bd_ref[0]
        dh = y.shape[1] // 2
        _store_rows(o_ref, _pack_bf16_pair(y[:, 0:dh], y[:, dh:]))


def _moe_call(blk_e, n_used, xs, perm, w_gu, b_gu_p, w_down, b_down):
    n_steps = blk_e.shape[0]
    blk = MOE_BLK
    _, d, f2 = w_gu.shape
    f = f2 // 2
    rows = blk * ROW_SUB
    kern = functools.partial(_moe_kernel, blk=blk, f=f)
    row_blk = lambda i, be, nu: (i, 0)
    grid_spec = pltpu.PrefetchScalarGridSpec(
        num_scalar_prefetch=2,
        grid=(n_steps,),
        in_specs=[
            pl.BlockSpec((rows, 128), row_blk),
            pl.BlockSpec((256, 256), lambda i, be, nu: (0, 0)),
            pl.BlockSpec((1, d, f2), lambda i, be, nu: (be[i], 0, 0)),
            pl.BlockSpec((1, 1, f2), lambda i, be, nu: (be[i], 0, 0)),
            pl.BlockSpec((1, f, d), lambda i, be, nu: (be[i], 0, 0)),
            pl.BlockSpec((1, 1, d), lambda i, be, nu: (be[i], 0, 0)),
        ],
        out_specs=pl.BlockSpec((rows, 128), row_blk),
        scratch_shapes=[
            pltpu.VMEM((d, f2), BF16),
            pltpu.VMEM((f, d), BF16),
        ],
    )
    return pl.pallas_call(
        kern,
        grid_spec=grid_spec,
        out_shape=jax.ShapeDtypeStruct((n_steps * rows, 128), jnp.uint32),
        compiler_params=pltpu.CompilerParams(dimension_semantics=("arbitrary",),
                                             vmem_limit_bytes=VMEM_LIMIT),
        name="moe_experts",
    )(blk_e, n_used, xs, perm, w_gu, b_gu_p, w_down, b_down)


def _final_kernel(tab_cur_ref, tab_nxt_ref, x1_ref, pos_ref, w_ref, mod_ref, fmod_ref, gfin_ref, ys_hbm,
                  o_ref, stage, sem, *, tm, n_exp):
    i = pl.program_id(0)
    n = pl.num_programs(0)
    slot = i % 2
    n_rows = TOP_K * tm

    def issue_tile(tab_ref, slot_):
        def body(e, c):
            _copy_pieces(tab_ref[0, 0, e], ys_hbm, tab_ref[0, 1, e], stage.at[slot_], tab_ref[0, 2, e],
                         sem.at[slot_], RUN_BITS)
            return c
        lax.fori_loop(0, n_exp, body, 0)

    @pl.when(i == 0)
    def _():
        issue_tile(tab_cur_ref, 0)

    @pl.when(i + 1 < n)
    def _():
        issue_tile(tab_nxt_ref, 1 - slot)

    pltpu.make_async_copy(ys_hbm.at[pl.ds(0, n_rows * ROW_SUB)], stage.at[slot], sem.at[slot]).wait()

    y = _load_rows(stage.at[slot], n_rows).astype(BF16)
    col_id = lax.broadcasted_iota(jnp.int32, (tm, n_rows), 1)
    pos = pos_ref[...]
    w = w_ref[...]
    a = jnp.where(col_id == pos[:, 0:1], w[:, 0:1], 0.0)
    for k in range(1, TOP_K):
        a = a + jnp.where(col_id == pos[:, k:k + 1], w[:, k:k + 1], 0.0)
    a_hi = a.astype(BF16)
    a_lo = (a - a_hi.astype(F32)).astype(BF16)
    moe = (jnp.dot(a_hi, y, preferred_element_type=F32)
           + jnp.dot(a_lo, y, preferred_element_type=F32))
    gate2 = mod_ref[0, 5:6, :]
    x2 = x1_ref[...] + gate2 * moe
    fshift = fmod_ref[0, 0:1, :]
    fscale = fmod_ref[0, 1:2, :]
    o_ref[...] = _rms(x2, gfin_ref[...]) * (1.0 + fscale) + fshift


def _final_call(tabs, x1, pos_col, w_col, mod, fmod, g_final, ys, *, seq):
    t, d = x1.shape
    tm = TM_FIN
    assert tm < (1 << RUN_BITS)
    tps = seq // tm
    nt = t // tm
    n_exp = tabs.shape[2]
    n_rows = TOP_K * tm
    kern = functools.partial(_final_kernel, tm=tm, n_exp=n_exp)
    tab_spec = lambda ahead: pl.BlockSpec((1, 3, n_exp), lambda i: (jnp.minimum(i + ahead, nt - 1), 0, 0),
                                          memory_space=pltpu.SMEM)
    return pl.pallas_call(
        kern,
        grid=(nt,),
        in_specs=[
            tab_spec(0),
            tab_spec(1),
            pl.BlockSpec((tm, d), lambda i: (i, 0)),
            pl.BlockSpec((tm, TOP_K), lambda i: (i, 0)),
            pl.BlockSpec((tm, TOP_K), lambda i: (i, 0)),
            pl.BlockSpec((1, 6, d), lambda i: (i // tps, 0, 0)),
            pl.BlockSpec((1, 2, d), lambda i: (i // tps, 0, 0)),
            _const_spec((1, d)),
            pl.BlockSpec(memory_space=pl.ANY),
        ],
        out_specs=pl.BlockSpec((tm, d), lambda i: (i, 0)),
        out_shape=jax.ShapeDtypeStruct((t, d), F32),
        scratch_shapes=[pltpu.VMEM((2, n_rows * ROW_SUB, 128), jnp.uint32),
                        pltpu.SemaphoreType.DMA((2,))],
        compiler_params=pltpu.CompilerParams(dimension_semantics=("arbitrary",),
                                             vmem_limit_bytes=VMEM_LIMIT),
        name="combine_final",
    )(tabs, tabs, x1, pos_col, w_col, mod, fmod, g_final, ys)


def _prep_w_in(w_in, d, q_lora, kv_lora):
    o_kpe = d + q_lora + kv_lora
    o_g = o_kpe + QK_ROPE
    half = QK_ROPE // 2
    kpe = w_in[:, o_kpe:o_kpe + QK_ROPE]
    zpad = jnp.zeros((d, 128 - QK_ROPE), w_in.dtype)
    ksw = jnp.concatenate([-kpe[:, half:], kpe[:, :half]], axis=1)
    return jnp.concatenate([w_in[:, :o_kpe], kpe, zpad, ksw, zpad, w_in[:, o_g:]], axis=1).astype(BF16)


def _prep_w_q(w_q_b):
    ql = w_q_b.shape[0]
    hd = QK_NOPE + QK_ROPE
    half = QK_ROPE // 2
    w = w_q_b.reshape(ql, N_HEADS, hd)
    nope = w[:, :, :QK_NOPE]
    pe = w[:, :, QK_NOPE:]
    sw = jnp.concatenate([-pe[:, :, half:], pe[:, :, :half]], axis=2)
    return jnp.concatenate([nope.reshape(ql, -1), pe.reshape(ql, -1), sw.reshape(ql, -1)],
                           axis=1).astype(BF16)


def _prep_w_kv(w_kv_b):
    kvl = w_kv_b.shape[0]
    w = w_kv_b.reshape(kvl, N_HEADS, QK_NOPE + V_HEAD)
    w_kn = w[:, :, :QK_NOPE].reshape(kvl, -1).astype(BF16)
    w_vt = w[:, :, QK_NOPE:].reshape(kvl, -1).T.astype(BF16)
    return w_kn, w_vt


def kernel(x, c, positions, w_mod, b_mod, g_mix, w_in, b_gate, w_pool_grp, pool_scale, w_pool_out,
           g_q_a, w_q_b, g_kv_a, w_kv_b, w_mla_out, w_out, g_ffn, w_router, b_router, w_gu, b_gu,
           w_down, b_down, g_final, w_fmod, b_fmod):
    bsz, seq, d = x.shape
    t = bsz * seq
    depth = w_mod.shape[0]
    assert depth == 1
    assert seq % TQ == 0 and seq % TM_IN == 0 and seq % TM_POST == 0 and seq % TM_FIN == 0
    q_lora = g_q_a.shape[-1]
    kv_lora = g_kv_a.shape[-1]
    n_exp = w_gu.shape[1]
    f = w_gu.shape[-1] // 2
    blk = MOE_BLK

    x2d = x.reshape(t, d)
    pos_col = positions.astype(F32).reshape(t, 1)
    inv_freq = 1.0 / (ROPE_THETA ** (jnp.arange(0, QK_ROPE, 2, dtype=F32) / QK_ROPE))
    invf2 = jnp.tile(inv_freq, 128 // (QK_ROPE // 2)).reshape(1, 128)

    mod = _mod_call(c, w_mod[0], b_mod[0]).reshape(bsz, 6, d)
    fmod = _mod_call(c, w_fmod, b_fmod).reshape(bsz, 2, d)

    w_in_p = _prep_w_in(w_in[0], d, q_lora, kv_lora)
    w_q_p = _prep_w_q(w_q_b[0])
    w_kn, w_vt = _prep_w_kv(w_kv_b[0])
    ga, g1, q2, k2, vt3 = _mixer_in_call(
        x2d, pos_col, mod, g_mix[0].reshape(1, d), w_in_p, b_gate[0].reshape(1, 2 * d),
        w_pool_grp[0].astype(BF16), pool_scale[0].reshape(1, d), w_pool_out[0].astype(BF16),
        g_q_a[0].reshape(1, q_lora), w_q_p, g_kv_a[0].reshape(1, kv_lora), w_kn, w_vt, invf2,
        bsz=bsz, seq=seq)

    hp = N_HEADS * HEAD_PAD
    o = _attn_call(q2.reshape(bsz, seq, hp), k2.reshape(bsz, seq, hp), vt3)

    w_r = w_router[0]
    w_r_hi = w_r.astype(BF16)
    w_r_lo = (w_r - w_r_hi.astype(F32)).astype(BF16)
    lane_pad = lambda a: jnp.pad(a, ((0, 0), (0, 128 - n_exp)))
    w_rt = jnp.concatenate(
        [jnp.concatenate([lane_pad(w_r_hi), lane_pad(w_r_lo)], axis=1),
         jnp.concatenate([lane_pad(w_r_hi), jnp.zeros((d, 128), BF16)], axis=1)], axis=0)
    x1, h2p, pos_t, wgt_t, cnt3 = _post_call(
        o.reshape(t, d), ga, g1, x2d, mod, w_mla_out[0].astype(BF16), w_out[0].astype(BF16),
        g_ffn[0].reshape(1, d), w_rt, b_router[0].reshape(n_exp, 1), seq=seq)

    n_slots = t * TOP_K
    n_rows = n_slots + n_exp * blk
    n_blocks = n_rows // blk
    cnt = cnt3[:, 0:TM_POST // TM_FIN, 0:n_exp].reshape(-1, n_exp)
    counts = jnp.sum(cnt, axis=0)
    padded = (counts + blk - 1) // blk * blk
    pad_end = jnp.cumsum(padded)
    pad_start = pad_end - padded
    blk_start = jnp.arange(n_blocks, dtype=jnp.int32) * blk
    blk_e = jnp.minimum(jnp.sum(pad_end[None, :] <= blk_start[:, None], axis=1, dtype=jnp.int32),
                        n_exp - 1)
    n_used = (pad_end[-1:] // blk).astype(jnp.int32)
    tail = jnp.stack([(pad_start + counts) * ROW_SUB, padded - counts], axis=0).astype(jnp.int32)

    before = jnp.cumsum(cnt, axis=0) - cnt
    run_start = jnp.cumsum(cnt, axis=1) - cnt
    tabs = jnp.stack([cnt, (pad_start[None, :] + before) * ROW_SUB, run_start * ROW_SUB], axis=1)

    col = jnp.arange(256, dtype=jnp.int32)[None, :]
    row = jnp.arange(256, dtype=jnp.int32)[:, None]
    perm = (row == jnp.where(col < 128, 2 * col, 2 * (col - 128) + 1)).astype(BF16)
    b_gu_p = jnp.concatenate([b_gu[0][:, 0::2], b_gu[0][:, 1::2]], axis=-1).reshape(n_exp, 1, 2 * f)
    xs = _dispatch_call(tabs, tail, n_used, h2p, pos_t, n_xs_rows=n_rows)
    ys = _moe_call(blk_e, n_used, xs, perm, w_gu[0], b_gu_p, w_down[0], b_down[0].reshape(n_exp, 1, d))

    out = _final_call(tabs, x1, pos_t.T, wgt_t.T, mod, fmod, g_final.reshape(1, d), ys, seq=seq)
    return out.reshape(bsz, seq, d)
```

```python
import functools

import jax
import jax.numpy as jnp
from jax import lax
from jax.experimental import pallas as pl
from jax.experimental.pallas import tpu as pltpu

F32 = jnp.float32
BF16 = jnp.bfloat16

CHUNK = 64
POOL_WINDOWS = (2, 4, 8, 16)
POOL_HALO = 16
N_HEADS = 8
QK_NOPE = 128
QK_ROPE = 64
V_HEAD = 128
HEAD_PAD = 256
ROPE_THETA = 10000.0
TOP_K = 4
SWIGLU_LIMIT = 7.0
SWIGLU_ALPHA = 1.702
NORM_EPS = 1e-6
NEG_INF = -1e30
LOG2_E = 1.4426950408889634

VMEM_LIMIT = 56 * 1024 * 1024

TM_IN = 256
TQ = 512
TM_POST = 512
MOE_BLK = 512
TM_FIN = 256
ROW_SUB = 4


def _const_spec(shape):
    nd = len(shape)
    return pl.BlockSpec(shape, lambda *_: (0,) * nd, pipeline_mode=pl.Buffered(1))


def _rms(xf, g):
    return xf * lax.rsqrt(jnp.mean(xf * xf, axis=-1, keepdims=True) + NORM_EPS) * g


def _pack_bf16_pair(lo, hi):
    lo_b = lax.bitcast_convert_type(lo.astype(BF16).astype(F32), jnp.uint32)
    hi_b = lax.bitcast_convert_type(hi.astype(BF16).astype(F32), jnp.uint32)
    return (hi_b & jnp.uint32(0xFFFF0000)) | (lo_b >> 16)


def _unpack_bf16_pair(p):
    lo = lax.bitcast_convert_type(p << 16, F32)
    hi = lax.bitcast_convert_type(p & jnp.uint32(0xFFFF0000), F32)
    return lo, hi


def _store_rows(ref, packed):
    n = packed.shape[0]
    for q in range(ROW_SUB):
        ref[pl.ds(q, n, stride=ROW_SUB), :] = packed[:, q * 128:(q + 1) * 128]


def _load_rows(ref, n):
    los, his = [], []
    for q in range(ROW_SUB):
        lo, hi = _unpack_bf16_pair(ref[pl.ds(q, n, stride=ROW_SUB), :])
        los.append(lo)
        his.append(hi)
    return jnp.concatenate(los + his, axis=1)


def _mod_kernel(c_ref, w_ref, b_ref, o_ref):
    c = c_ref[...]
    c_act = c * jax.nn.sigmoid(c)
    o_ref[...] = jnp.dot(c_act, w_ref[...], preferred_element_type=F32,
                         precision=lax.Precision.HIGHEST) + b_ref[...]


def _mod_call(c, w, b, tn=1024):
    bsz, d = c.shape
    n = w.shape[1]
    return pl.pallas_call(
        _mod_kernel,
        grid=(n // tn,),
        in_specs=[pl.BlockSpec((bsz, d), lambda j: (0, 0)),
                  pl.BlockSpec((d, tn), lambda j: (0, j)),
                  pl.BlockSpec((1, tn), lambda j: (0, j))],
        out_specs=pl.BlockSpec((bsz, tn), lambda j: (0, j)),
        out_shape=jax.ShapeDtypeStruct((bsz, n), F32),
        compiler_params=pltpu.CompilerParams(dimension_semantics=("arbitrary",),
                                             vmem_limit_bytes=VMEM_LIMIT),
        name="adaln_mod",
    )(c, w, b.reshape(1, n))


def _mixer_in_kernel(x_ref, xh_ref, pos_ref, mod_ref, gmix_ref, win_ref, bgate_ref, wgrp_ref,
                     pscale_ref, wpo_ref, gq_ref, wq_ref, gkv_ref, wkn_ref, wvt_ref, invf_ref,
                     ga_ref, g1_ref, q_ref, k_ref, vt_ref, u_scr, *, tm, tiles_per_seq, d, q_lora,
                     kv_lora):
    i = pl.program_id(0)
    t_in_seq = i % tiles_per_seq
    is_start = t_in_seq == 0
    shift1 = mod_ref[0, 0:1, :]
    scale1 = mod_ref[0, 1:2, :]
    gmix = gmix_ref[...]

    def prenorm(xf):
        return (_rms(xf, gmix) * (1.0 + scale1) + shift1).astype(BF16)

    h = prenorm(x_ref[...])
    hh = prenorm(xh_ref[...])
    u = jnp.dot(h, win_ref[:, 0:d], preferred_element_type=F32)
    uh = jnp.dot(hh, win_ref[:, 0:d], preferred_element_type=F32)
    u_scr[0:POOL_HALO, :] = jnp.where(is_start, 0.0, uh)
    u_scr[POOL_HALO:POOL_HALO + tm, :] = u
    rest = jnp.dot(h, win_ref[:, d:], preferred_element_type=F32)

    gw = d // len(POOL_WINDOWS)
    tseq = t_in_seq * tm + lax.broadcasted_iota(jnp.int32, (tm, 1), 0)
    ys = []
    for g, w in enumerate(POOL_WINDOWS):
        c0 = g * gw
        ug = u_scr[POOL_HALO:POOL_HALO + tm, c0:c0 + gw]
        acc = ug
        for j in range(1, w):
            acc = acc + u_scr[POOL_HALO - j:POOL_HALO - j + tm, c0:c0 + gw]
        cnt = jnp.minimum(tseq + 1, w).astype(F32)
        mixed = (acc / cnt - ug).astype(BF16)
        ys.append(jnp.dot(mixed, wgrp_ref[g], preferred_element_type=F32))
    y = (jnp.concatenate(ys, axis=1) * pscale_ref[...]).astype(BF16)
    a = jnp.dot(y, wpo_ref[...], preferred_element_type=F32)

    o_q = 0
    o_kv = q_lora
    o_kpe = q_lora + kv_lora
    o_ksw = o_kpe + 128
    o_g0 = o_ksw + 128
    o_g1 = o_g0 + d
    gates0 = jax.nn.sigmoid(rest[:, o_g0:o_g0 + d] + bgate_ref[:, 0:d])
    gates1 = jax.nn.sigmoid(rest[:, o_g1:o_g1 + d] + bgate_ref[:, d:2 * d])
    ga_ref[...] = (gates0 * a).astype(BF16)
    g1_ref[...] = gates1.astype(BF16)

    ang = pos_ref[...] * invf_ref[...]
    cos2 = jnp.cos(ang)
    sin2 = jnp.sin(ang)
    q_scale = float(QK_NOPE + QK_ROPE) ** -0.5 * LOG2_E

    qn = _rms(rest[:, o_q:o_q + q_lora], gq_ref[...]).astype(BF16)
    qall = jnp.dot(qn, wq_ref[...], preferred_element_type=F32)
    kvn = _rms(rest[:, o_kv:o_kv + kv_lora], gkv_ref[...]).astype(BF16)
    kn = jnp.dot(kvn, wkn_ref[...], preferred_element_type=F32)
    vt = lax.dot_general(wvt_ref[...], kvn, (((1,), (1,)), ((), ())),
                         preferred_element_type=F32)
    vt_ref[0] = vt.astype(BF16)
    kpe = (rest[:, o_kpe:o_kpe + 128] * cos2 + rest[:, o_ksw:o_ksw + 128] * sin2).astype(BF16)
    o_pe = N_HEADS * QK_NOPE
    o_sw = o_pe + N_HEADS * QK_ROPE
    low_half = lax.broadcasted_iota(jnp.int32, (tm, 128), 1) < QK_ROPE
    for pair in range(N_HEADS // 2):
        c0 = pair * 128
        qpe2 = (qall[:, o_pe + c0:o_pe + c0 + 128] * cos2
                + qall[:, o_sw + c0:o_sw + c0 + 128] * sin2) * q_scale
        for hd, part in ((2 * pair, qpe2), (2 * pair + 1, pltpu.roll(qpe2, QK_ROPE, axis=1))):
            n0 = hd * QK_NOPE
            q_ref[:, hd * HEAD_PAD:hd * HEAD_PAD + 128] = (qall[:, n0:n0 + 128] * q_scale).astype(BF16)
            q_ref[:, hd * HEAD_PAD + 128:(hd + 1) * HEAD_PAD] = jnp.where(low_half, part, 0.0).astype(BF16)
            k_ref[:, hd * HEAD_PAD:hd * HEAD_PAD + 128] = kn[:, n0:n0 + 128].astype(BF16)
            k_ref[:, hd * HEAD_PAD + 128:(hd + 1) * HEAD_PAD] = kpe


def _mixer_in_call(x2d, pos_col, mod, g_mix, w_in_p, b_gate, w_grp, pool_scale, w_po, g_q_a, w_q_p,
                   g_kv_a, w_kn, w_vt, invf2, *, bsz, seq):
    t, d = x2d.shape
    tm = TM_IN
    tps = seq // tm
    q_lora = g_q_a.shape[-1]
    kv_lora = g_kv_a.shape[-1]
    hp = N_HEADS * HEAD_PAD
    halo_blocks = tm // POOL_HALO
    kern = functools.partial(_mixer_in_kernel, tm=tm, tiles_per_seq=tps, d=d, q_lora=q_lora,
                             kv_lora=kv_lora)
    row = lambda i: (i, 0)
    return pl.pallas_call(
        kern,
        grid=(t // tm,),
        in_specs=[
            pl.BlockSpec((tm, d), row),
            pl.BlockSpec((POOL_HALO, d), lambda i: (jnp.maximum(i * halo_blocks - 1, 0), 0)),
            pl.BlockSpec((tm, 1), row),
            pl.BlockSpec((1, 6, d), lambda i: (i // tps, 0, 0)),
            _const_spec((1, d)),
            _const_spec(w_in_p.shape),
            _const_spec((1, 2 * d)),
            _const_spec(w_grp.shape),
            _const_spec((1, d)),
            _const_spec(w_po.shape),
            _const_spec((1, q_lora)),
            _const_spec(w_q_p.shape),
            _const_spec((1, kv_lora)),
            _const_spec(w_kn.shape),
            _const_spec(w_vt.shape),
            _const_spec((1, 128)),
        ],
        out_specs=[
            pl.BlockSpec((tm, d), row),
            pl.BlockSpec((tm, d), row),
            pl.BlockSpec((tm, hp), row),
            pl.BlockSpec((tm, hp), row),
            pl.BlockSpec((1, N_HEADS * V_HEAD, tm), lambda i: (i // tps, 0, i % tps)),
        ],
        out_shape=[
            jax.ShapeDtypeStruct((t, d), BF16),
            jax.ShapeDtypeStruct((t, d), BF16),
            jax.ShapeDtypeStruct((t, hp), BF16),
            jax.ShapeDtypeStruct((t, hp), BF16),
            jax.ShapeDtypeStruct((bsz, N_HEADS * V_HEAD, seq), BF16),
        ],
        scratch_shapes=[pltpu.VMEM((tm + POOL_HALO, d), F32)],
        compiler_params=pltpu.CompilerParams(dimension_semantics=("arbitrary",),
                                             vmem_limit_bytes=VMEM_LIMIT),
        name="mixer_in",
    )(x2d, x2d, pos_col, mod, g_mix, w_in_p, b_gate, w_grp, pool_scale, w_po, g_q_a, w_q_p, g_kv_a,
      w_kn, w_vt, invf2)


def _attn_kernel(q_ref, k_ref, vt_ref, o_ref, *, tq, nq):
    seq = vt_ref.shape[2]
    vt_ext = jnp.concatenate([vt_ref[0], jnp.ones((16, seq), BF16)], axis=0)

    def scores(qi):
        q0 = qi * tq
        ln = q0 + tq
        return lax.dot_general(k_ref[0, 0:ln, :], q_ref[0, q0:q0 + tq, :], (((1,), (1,)), ((), ())),
                               preferred_element_type=F32)

    def finish(qi, s):
        q0 = qi * tq
        ln = q0 + tq
        cw = tq // 2
        for c in range(2):
            cs = slice(c * cw, (c + 1) * cw)
            kc = (q0 + lax.broadcasted_iota(jnp.int32, (tq, cw), 0)) // CHUNK
            qc = (q0 + c * cw + lax.broadcasted_iota(jnp.int32, (tq, cw), 1)) // CHUNK
            s_diag = jnp.where(qc >= kc, s[q0:ln, cs], NEG_INF)
            m = jnp.max(s_diag, axis=0, keepdims=True)
            if qi > 0:
                s_main = s[0:q0, cs]
                m = jnp.maximum(m, jnp.max(s_main, axis=0, keepdims=True))
            p_diag = jnp.exp2(s_diag - m).astype(BF16)
            acc = jnp.dot(vt_ext[:, q0:ln], p_diag, preferred_element_type=F32)
            if qi > 0:
                p_main = jnp.exp2(s_main - m).astype(BF16)
                acc = acc + jnp.dot(vt_ext[:, 0:q0], p_main, preferred_element_type=F32)
            l = acc[V_HEAD:V_HEAD + 1, :]
            o_ref[0, q0 + c * cw:q0 + (c + 1) * cw, :] = (acc[0:V_HEAD, :] / l).T.astype(BF16)

    order = list(range(nq))[::-1]
    s_cur = scores(order[0])
    for idx, qi in enumerate(order):
        s_next = scores(order[idx + 1]) if idx + 1 < nq else None
        finish(qi, s_cur)
        s_cur = s_next


def _attn_call(q3, k3, vt3):
    bsz, seq, _ = q3.shape
    kern = functools.partial(_attn_kernel, tq=TQ, nq=seq // TQ)
    return pl.pallas_call(
        kern,
        grid=(bsz, N_HEADS),
        in_specs=[pl.BlockSpec((1, seq, HEAD_PAD), lambda b, h: (b, 0, h)),
                  pl.BlockSpec((1, seq, HEAD_PAD), lambda b, h: (b, 0, h)),
                  pl.BlockSpec((1, V_HEAD, seq), lambda b, h: (b, h, 0))],
        out_specs=pl.BlockSpec((1, seq, V_HEAD), lambda b, h: (b, 0, h)),
        out_shape=jax.ShapeDtypeStruct((bsz, seq, N_HEADS * V_HEAD), BF16),
        compiler_params=pltpu.CompilerParams(dimension_semantics=("arbitrary", "arbitrary"),
                                             vmem_limit_bytes=VMEM_LIMIT),
        name="mla_attn",
    )(q3, k3, vt3)


def _post_kernel(o_ref, ga_ref, g1_ref, x_ref, mod_ref, wmo_ref, wout_ref, gffn_ref, wrt_ref, br_ref,
                 x1_ref, h2_ref, pos_ref, wgt_ref, cnt_ref, *, d):
    m = jnp.dot(o_ref[...], wmo_ref[...], preferred_element_type=F32)
    merged = ga_ref[...].astype(F32) + g1_ref[...].astype(F32) * m
    gate1 = mod_ref[0, 2:3, :]
    x1 = x_ref[...] + gate1 * jnp.dot(merged.astype(BF16), wout_ref[...],
                                      preferred_element_type=F32)
    x1_ref[...] = x1
    shift2 = mod_ref[0, 3:4, :]
    scale2 = mod_ref[0, 4:5, :]
    h2 = _rms(x1, gffn_ref[...]) * (1.0 + scale2) + shift2
    _store_rows(h2_ref, _pack_bf16_pair(h2[:, 0:d // 2], h2[:, d // 2:d]))

    ne = br_ref.shape[0]
    h_hi = h2.astype(BF16)
    h_lo = (h2 - h_hi.astype(F32)).astype(BF16)
    hcat = jnp.concatenate([h_hi, h_lo], axis=1)
    half = hcat.shape[0] // 2
    lg2 = jnp.concatenate(
        [jnp.dot(hcat[0:half], wrt_ref[...], preferred_element_type=F32),
         jnp.dot(hcat[half:], wrt_ref[...], preferred_element_type=F32)], axis=0)
    lg = lg2[:, 0:128] + lg2[:, 128:256]
    logits = lg.T[0:ne, :] + br_ref[...]
    tm = logits.shape[1]
    eid = lax.broadcasted_iota(jnp.int32, (ne, tm), 0)
    vals, idxs = [], []
    cur = logits
    for _ in range(TOP_K):
        mx = jnp.max(cur, axis=0, keepdims=True)
        ix = jnp.min(jnp.where(cur == mx, eid, ne), axis=0, keepdims=True)
        vals.append(mx)
        idxs.append(ix)
        cur = jnp.where(eid == ix, -jnp.inf, cur)
    es = [jnp.exp(v - vals[0]) for v in vals]
    den = es[0] + es[1] + es[2] + es[3]
    wgt_ref[...] = jnp.concatenate([e / den for e in es], axis=0)

    tt = TM_FIN
    t_row = lax.broadcasted_iota(jnp.int32, (tm, tm), 0)
    t_col = lax.broadcasted_iota(jnp.int32, (tm, tm), 1)
    same_tile = (t_row // tt) == (t_col // tt)
    in_tile = jnp.where(same_tile, 1.0, 0.0).astype(BF16)
    earlier = jnp.where(same_tile & (t_row < t_col), 1.0, 0.0).astype(BF16)
    lower = jnp.where(lax.broadcasted_iota(jnp.int32, (ne, ne), 1)
                      < lax.broadcasted_iota(jnp.int32, (ne, ne), 0), 1.0, 0.0).astype(BF16)
    picks = [eid == ix for ix in idxs]
    routed = jnp.where(picks[0] | picks[1] | picks[2] | picks[3], 1.0, 0.0).astype(BF16)
    before = jnp.dot(routed, earlier, preferred_element_type=F32)
    cnt_b = jnp.dot(routed, in_tile, preferred_element_type=F32)
    run_start = jnp.dot(lower, cnt_b.astype(BF16), preferred_element_type=F32)
    base = run_start + before
    pos_ref[...] = jnp.concatenate(
        [jnp.sum(jnp.where(pk, base, 0.0), axis=0, keepdims=True) for pk in picks],
        axis=0).astype(jnp.int32)
    tile_sel = jnp.where(lax.broadcasted_iota(jnp.int32, (8, tm), 0)
                         == lax.broadcasted_iota(jnp.int32, (8, tm), 1) // tt, 1.0, 0.0).astype(BF16)
    cnt8 = lax.dot_general(tile_sel, routed, (((1,), (1,)), ((), ())),
                           preferred_element_type=F32)
    cnt_ref[0] = jnp.concatenate([cnt8, jnp.zeros((8, 128 - ne), F32)], axis=1).astype(jnp.int32)


def _post_call(o2d, ga, g1, x2d, mod, w_mo, w_out, g_ffn, w_rt, b_r, *, seq):
    t, d = x2d.shape
    tm = TM_POST
    tps = seq // tm
    row = lambda i: (i, 0)
    n_exp = b_r.shape[0]
    assert tm % TM_FIN == 0 and tm // TM_FIN <= 8 and n_exp <= 128
    kern = functools.partial(_post_kernel, d=d)
    return pl.pallas_call(
        kern,
        grid=(t // tm,),
        in_specs=[
            pl.BlockSpec((tm, d), row),
            pl.BlockSpec((tm, d), row),
            pl.BlockSpec((tm, d), row),
            pl.BlockSpec((tm, d), row),
            pl.BlockSpec((1, 6, d), lambda i: (i // tps, 0, 0)),
            _const_spec(w_mo.shape),
            _const_spec(w_out.shape),
            _const_spec((1, d)),
            _const_spec(w_rt.shape),
            _const_spec(b_r.shape),
        ],
        out_specs=[
            pl.BlockSpec((tm, d), row),
            pl.BlockSpec((tm * ROW_SUB, 128), row),
            pl.BlockSpec((TOP_K, tm), lambda i: (0, i)),
            pl.BlockSpec((TOP_K, tm), lambda i: (0, i)),
            pl.BlockSpec((1, 8, 128), lambda i: (i, 0, 0)),
        ],
        out_shape=[
            jax.ShapeDtypeStruct((t, d), F32),
            jax.ShapeDtypeStruct((t * ROW_SUB, 128), jnp.uint32),
            jax.ShapeDtypeStruct((TOP_K, t), jnp.int32),
            jax.ShapeDtypeStruct((TOP_K, t), F32),
            jax.ShapeDtypeStruct((t // tm, 8, 128), jnp.int32),
        ],
        compiler_params=pltpu.CompilerParams(dimension_semantics=("arbitrary",),
                                             vmem_limit_bytes=VMEM_LIMIT),
        name="post_attn_router",
    )(o2d, ga, g1, x2d, mod, w_mo, w_out, g_ffn, w_rt, b_r)


RUN_BITS = 9
COMMON_RUN_BITS = 6
TAIL_BITS = 9


def _copy_pieces(count, src_ref, src_off, dst_ref, dst_off, sem, bits, wait=False):
    def pieces(bit_range, off):
        for b in reversed(bit_range):
            size = (1 << b) * ROW_SUB
            take = (count >> b) & 1

            @pl.when(take == 1)
            def _(off=off, size=size):
                cp = pltpu.make_async_copy(
                    src_ref.at[pl.ds(pl.multiple_of(src_off + off, ROW_SUB), size)],
                    dst_ref.at[pl.ds(pl.multiple_of(dst_off + off, ROW_SUB), size)], sem)
                if wait:
                    cp.wait()
                else:
                    cp.start()
            off = off + take * size

    lo = min(bits, COMMON_RUN_BITS)
    big = (count >> lo) << lo

    if lo < bits:
        @pl.when(big != 0)
        def _():
            pieces(range(lo, bits), 0)
    pieces(range(lo), big * ROW_SUB)


def _dispatch_kernel(tab_ref, tail_ref, nused_ref, h_ref, pos_ref, xs_hbm, stage, zbuf, sem, zsem, *,
                     tm, n_exp, n_blocks):
    i = pl.program_id(0)
    n = pl.num_programs(0)
    slot = i % 2
    n_rows = TOP_K * tm

    def wait_stage(slot_):
        pltpu.make_async_copy(stage.at[slot_], xs_hbm.at[pl.ds(0, n_rows * ROW_SUB)],
                              sem.at[slot_]).wait()

    @pl.when(i == 0)
    def _():
        zbuf[...] = jnp.zeros(zbuf.shape, zbuf.dtype)
        for wait in (False, True):
            def body(e, c, wait=wait):
                _copy_pieces(tail_ref[1, e], zbuf, 0, xs_hbm, tail_ref[0, e], zsem, TAIL_BITS, wait)
                return c
            lax.fori_loop(0, n_exp, body, 0)
        blk_rows = zbuf.shape[0]

        def zero_block(b, c):
            cp = pltpu.make_async_copy(
                zbuf, xs_hbm.at[pl.ds(pl.multiple_of(b * blk_rows, blk_rows), blk_rows)], zsem)
            cp.start()
            cp.wait()
            return c
        lax.fori_loop(nused_ref[0], n_blocks, zero_block, 0)

    @pl.when(i >= 2)
    def _():
        wait_stage(slot)

    h = _load_rows(h_ref, tm).astype(BF16)
    row_id = lax.broadcasted_iota(jnp.int32, (n_rows, tm), 0)
    hit = row_id == pos_ref[0:1, :]
    for k in range(1, TOP_K):
        hit = hit | (row_id == pos_ref[k:k + 1, :])
    onehot = jnp.where(hit, 1.0, 0.0).astype(BF16)
    rows = jnp.dot(onehot, h, preferred_element_type=F32)
    dh = rows.shape[1] // 2
    _store_rows(stage.at[slot], _pack_bf16_pair(rows[:, 0:dh], rows[:, dh:]))

    def body(e, c):
        _copy_pieces(tab_ref[0, 0, e], stage.at[slot], tab_ref[0, 2, e], xs_hbm, tab_ref[0, 1, e],
                     sem.at[slot], RUN_BITS)
        return c
    lax.fori_loop(0, n_exp, body, 0)

    @pl.when(i == n - 1)
    def _():
        wait_stage(slot)
        wait_stage(1 - slot)


def _dispatch_call(tabs, tail, n_used, h2p, pos_t, *, n_xs_rows):
    nt, _, n_exp = tabs.shape
    tm = TM_FIN
    assert nt >= 2 and tm < (1 << RUN_BITS) and MOE_BLK <= (1 << TAIL_BITS)
    n_rows = TOP_K * tm
    kern = functools.partial(_dispatch_kernel, tm=tm, n_exp=n_exp, n_blocks=n_xs_rows // MOE_BLK)
    return pl.pallas_call(
        kern,
        grid=(nt,),
        in_specs=[
            pl.BlockSpec((1, 3, n_exp), lambda i: (i, 0, 0), memory_space=pltpu.SMEM),
            pl.BlockSpec(memory_space=pltpu.SMEM),
            pl.BlockSpec(memory_space=pltpu.SMEM),
            pl.BlockSpec((tm * ROW_SUB, 128), lambda i: (i, 0)),
            pl.BlockSpec((TOP_K, tm), lambda i: (0, i)),
        ],
        out_specs=pl.BlockSpec(memory_space=pl.ANY),
        out_shape=jax.ShapeDtypeStruct((n_xs_rows * ROW_SUB, 128), jnp.uint32),
        scratch_shapes=[pltpu.VMEM((2, n_rows * ROW_SUB, 128), jnp.uint32),
                        pltpu.VMEM((MOE_BLK * ROW_SUB, 128), jnp.uint32),
                        pltpu.SemaphoreType.DMA((2,)),
                        pltpu.SemaphoreType.DMA(())],
        compiler_params=pltpu.CompilerParams(dimension_semantics=("arbitrary",),
                                             vmem_limit_bytes=VMEM_LIMIT),
        name="moe_dispatch",
    )(tabs, tail, n_used, h2p, pos_t)


def _moe_kernel(blk_e_ref, nused_ref, x_ref, perm_ref, wgu_ref, bgu_ref, wd_ref, bd_ref, o_ref,
                wgu_scr, wd_scr, *, blk, f):
    s = pl.program_id(0)
    e_cur = blk_e_ref[s]
    e_prev = blk_e_ref[jnp.maximum(s - 1, 0)]

    @pl.when(jnp.logical_or(s == 0, e_cur != e_prev))
    def _():
        perm = perm_ref[...]
        for c in range(2 * f // 256):
            r = jnp.dot(wgu_ref[0, :, c * 256:(c + 1) * 256].astype(BF16), perm,
                        preferred_element_type=F32)
            wgu_scr[:, c * 128:(c + 1) * 128] = r[:, 0:128].astype(BF16)
            wgu_scr[:, f + c * 128:f + (c + 1) * 128] = r[:, 128:256].astype(BF16)
        wd_scr[...] = wd_ref[0].astype(BF16)

    @pl.when(s >= nused_ref[0])
    def _():
        o_ref[...] = jnp.zeros(o_ref.shape, o_ref.dtype)

    @pl.when(s < nused_ref[0])
    def _():
        x = _load_rows(x_ref, blk).astype(BF16)
        gu = jnp.dot(x, wgu_scr[...], preferred_element_type=F32) + bgu_ref[0]
        gate = jnp.minimum(gu[:, 0:f], SWIGLU_LIMIT)
        up = jnp.clip(gu[:, f:2 * f], -SWIGLU_LIMIT, SWIGLU_LIMIT)
        act = (up + 1.0) * (gate * jax.nn.sigmoid(SWIGLU_ALPHA * gate))
        y = jnp.dot(act.astype(BF16), wd_scr[...], preferred_element_type=F32) + bd_ref[0]
        dh = y.shape[1] // 2
        _store_rows(o_ref, _pack_bf16_pair(y[:, 0:dh], y[:, dh:]))


def _moe_call(blk_e, n_used, xs, perm, w_gu, b_gu_p, w_down, b_down):
    n_steps = blk_e.shape[0]
    blk = MOE_BLK
    _, d, f2 = w_gu.shape
    f = f2 // 2
    rows = blk * ROW_SUB
    kern = functools.partial(_moe_kernel, blk=blk, f=f)
    row_blk = lambda i, be, nu: (i, 0)
    grid_spec = pltpu.PrefetchScalarGridSpec(
        num_scalar_prefetch=2,
        grid=(n_steps,),
        in_specs=[
            pl.BlockSpec((rows, 128), row_blk),
            pl.BlockSpec((256, 256), lambda i, be, nu: (0, 0)),
            pl.BlockSpec((1, d, f2), lambda i, be, nu: (be[i], 0, 0)),
            pl.BlockSpec((1, 1, f2), lambda i, be, nu: (be[i], 0, 0)),
            pl.BlockSpec((1, f, d), lambda i, be, nu: (be[i], 0, 0)),
            pl.BlockSpec((1, 1, d), lambda i, be, nu: (be[i], 0, 0)),
        ],
        out_specs=pl.BlockSpec((rows, 128), row_blk),
        scratch_shapes=[
            pltpu.VMEM((d, f2), BF16),
            pltpu.VMEM((f, d), BF16),
        ],
    )
    return pl.pallas_call(
        kern,
        grid_spec=grid_spec,
        out_shape=jax.ShapeDtypeStruct((n_steps * rows, 128), jnp.uint32),
        compiler_params=pltpu.CompilerParams(dimension_semantics=("arbitrary",),
                                             vmem_limit_bytes=VMEM_LIMIT),
        name="moe_experts",
    )(blk_e, n_used, xs, perm, w_gu, b_gu_p, w_down, b_down)


def _final_kernel(tab_cur_ref, tab_nxt_ref, x1_ref, pos_ref, w_ref, mod_ref, fmod_ref, gfin_ref, ys_hbm,
                  o_ref, stage, sem, *, tm, n_exp):
    i = pl.program_id(0)
    n = pl.num_programs(0)
    slot = i % 2
    n_rows = TOP_K * tm

    def issue_tile(tab_ref, slot_):
        def body(e, c):
            _copy_pieces(tab_ref[0, 0, e], ys_hbm, tab_ref[0, 1, e], stage.at[slot_], tab_ref[0, 2, e],
                         sem.at[slot_], RUN_BITS)
            return c
        lax.fori_loop(0, n_exp, body, 0)

    @pl.when(i == 0)
    def _():
        issue_tile(tab_cur_ref, 0)

    @pl.when(i + 1 < n)
    def _():
        issue_tile(tab_nxt_ref, 1 - slot)

    pltpu.make_async_copy(ys_hbm.at[pl.ds(0, n_rows * ROW_SUB)], stage.at[slot], sem.at[slot]).wait()

    y = _load_rows(stage.at[slot], n_rows).astype(BF16)
    col_id = lax.broadcasted_iota(jnp.int32, (tm, n_rows), 1)
    pos = pos_ref[...]
    w = w_ref[...]
    a = jnp.where(col_id == pos[:, 0:1], w[:, 0:1], 0.0)
    for k in range(1, TOP_K):
        a = a + jnp.where(col_id == pos[:, k:k + 1], w[:, k:k + 1], 0.0)
    a_hi = a.astype(BF16)
    a_lo = (a - a_hi.astype(F32)).astype(BF16)
    moe = (jnp.dot(a_hi, y, preferred_element_type=F32)
           + jnp.dot(a_lo, y, preferred_element_type=F32))
    gate2 = mod_ref[0, 5:6, :]
    x2 = x1_ref[...] + gate2 * moe
    fshift = fmod_ref[0, 0:1, :]
    fscale = fmod_ref[0, 1:2, :]
    o_ref[...] = _rms(x2, gfin_ref[...]) * (1.0 + fscale) + fshift


def _final_call(tabs, x1, pos_col, w_col, mod, fmod, g_final, ys, *, seq):
    t, d = x1.shape
    tm = TM_FIN
    assert tm < (1 << RUN_BITS)
    tps = seq // tm
    nt = t // tm
    n_exp = tabs.shape[2]
    n_rows = TOP_K * tm
    kern = functools.partial(_final_kernel, tm=tm, n_exp=n_exp)
    tab_spec = lambda ahead: pl.BlockSpec((1, 3, n_exp), lambda i: (jnp.minimum(i + ahead, nt - 1), 0, 0),
                                          memory_space=pltpu.SMEM)
    return pl.pallas_call(
        kern,
        grid=(nt,),
        in_specs=[
            tab_spec(0),
            tab_spec(1),
            pl.BlockSpec((tm, d), lambda i: (i, 0)),
            pl.BlockSpec((tm, TOP_K), lambda i: (i, 0)),
            pl.BlockSpec((tm, TOP_K), lambda i: (i, 0)),
            pl.BlockSpec((1, 6, d), lambda i: (i // tps, 0, 0)),
            pl.BlockSpec((1, 2, d), lambda i: (i // tps, 0, 0)),
            _const_spec((1, d)),
            pl.BlockSpec(memory_space=pl.ANY),
        ],
        out_specs=pl.BlockSpec((tm, d), lambda i: (i, 0)),
        out_shape=jax.ShapeDtypeStruct((t, d), F32),
        scratch_shapes=[pltpu.VMEM((2, n_rows * ROW_SUB, 128), jnp.uint32),
                        pltpu.SemaphoreType.DMA((2,))],
        compiler_params=pltpu.CompilerParams(dimension_semantics=("arbitrary",),
                                             vmem_limit_bytes=VMEM_LIMIT),
        name="combine_final",
    )(tabs, tabs, x1, pos_col, w_col, mod, fmod, g_final, ys)


def _prep_w_in(w_in, d, q_lora, kv_lora):
    o_kpe = d + q_lora + kv_lora
    o_g = o_kpe + QK_ROPE
    half = QK_ROPE // 2
    kpe = w_in[:, o_kpe:o_kpe + QK_ROPE]
    zpad = jnp.zeros((d, 128 - QK_ROPE), w_in.dtype)
    ksw = jnp.concatenate([-kpe[:, half:], kpe[:, :half]], axis=1)
    return jnp.concatenate([w_in[:, :o_kpe], kpe, zpad, ksw, zpad, w_in[:, o_g:]], axis=1).astype(BF16)


def _prep_w_q(w_q_b):
    ql = w_q_b.shape[0]
    hd = QK_NOPE + QK_ROPE
    half = QK_ROPE // 2
    w = w_q_b.reshape(ql, N_HEADS, hd)
    nope = w[:, :, :QK_NOPE]
    pe = w[:, :, QK_NOPE:]
    sw = jnp.concatenate([-pe[:, :, half:], pe[:, :, :half]], axis=2)
    return jnp.concatenate([nope.reshape(ql, -1), pe.reshape(ql, -1), sw.reshape(ql, -1)],
                           axis=1).astype(BF16)


def _prep_w_kv(w_kv_b):
    kvl = w_kv_b.shape[0]
    w = w_kv_b.reshape(kvl, N_HEADS, QK_NOPE + V_HEAD)
    w_kn = w[:, :, :QK_NOPE].reshape(kvl, -1).astype(BF16)
    w_vt = w[:, :, QK_NOPE:].reshape(kvl, -1).T.astype(BF16)
    return w_kn, w_vt


def kernel(x, c, positions, w_mod, b_mod, g_mix, w_in, b_gate, w_pool_grp, pool_scale, w_pool_out,
           g_q_a, w_q_b, g_kv_a, w_kv_b, w_mla_out, w_out, g_ffn, w_router, b_router, w_gu, b_gu,
           w_down, b_down, g_final, w_fmod, b_fmod):
    bsz, seq, d = x.shape
    t = bsz * seq
    depth = w_mod.shape[0]
    assert depth == 1
    assert seq % TQ == 0 and seq % TM_IN == 0 and seq % TM_POST == 0 and seq % TM_FIN == 0
    q_lora = g_q_a.shape[-1]
    kv_lora = g_kv_a.shape[-1]
    n_exp = w_gu.shape[1]
    f = w_gu.shape[-1] // 2
    blk = MOE_BLK

    x2d = x.reshape(t, d)
    pos_col = positions.astype(F32).reshape(t, 1)
    inv_freq = 1.0 / (ROPE_THETA ** (jnp.arange(0, QK_ROPE, 2, dtype=F32) / QK_ROPE))
    invf2 = jnp.tile(inv_freq, 128 // (QK_ROPE // 2)).reshape(1, 128)

    mod = _mod_call(c, w_mod[0], b_mod[0]).reshape(bsz, 6, d)
    fmod = _mod_call(c, w_fmod, b_fmod).reshape(bsz, 2, d)

    w_in_p = _prep_w_in(w_in[0], d, q_lora, kv_lora)
    w_q_p = _prep_w_q(w_q_b[0])
    w_kn, w_vt = _prep_w_kv(w_kv_b[0])
    ga, g1, q2, k2, vt3 = _mixer_in_call(
        x2d, pos_col, mod, g_mix[0].reshape(1, d), w_in_p, b_gate[0].reshape(1, 2 * d),
        w_pool_grp[0].astype(BF16), pool_scale[0].reshape(1, d), w_pool_out[0].astype(BF16),
        g_q_a[0].reshape(1, q_lora), w_q_p, g_kv_a[0].reshape(1, kv_lora), w_kn, w_vt, invf2,
        bsz=bsz, seq=seq)

    hp = N_HEADS * HEAD_PAD
    o = _attn_call(q2.reshape(bsz, seq, hp), k2.reshape(bsz, seq, hp), vt3)

    w_r = w_router[0]
    w_r_hi = w_r.astype(BF16)
    w_r_lo = (w_r - w_r_hi.astype(F32)).astype(BF16)
    lane_pad = lambda a: jnp.pad(a, ((0, 0), (0, 128 - n_exp)))
    w_rt = jnp.concatenate(
        [jnp.concatenate([lane_pad(w_r_hi), lane_pad(w_r_lo)], axis=1),
         jnp.concatenate([lane_pad(w_r_hi), jnp.zeros((d, 128), BF16)], axis=1)], axis=0)
    x1, h2p, pos_t, wgt_t, cnt3 = _post_call(
        o.reshape(t, d), ga, g1, x2d, mod, w_mla_out[0].astype(BF16), w_out[0].astype(BF16),
        g_ffn[0].reshape(1, d), w_rt, b_router[0].reshape(n_exp, 1), seq=seq)

    n_slots = t * TOP_K
    n_rows = n_slots + n_exp * blk
    n_blocks = n_rows // blk
    cnt = cnt3[:, 0:TM_POST // TM_FIN, 0:n_exp].reshape(-1, n_exp)
    counts = jnp.sum(cnt, axis=0)
    padded = (counts + blk - 1) // blk * blk
    pad_end = jnp.cumsum(padded)
    pad_start = pad_end - padded
    blk_start = jnp.arange(n_blocks, dtype=jnp.int32) * blk
    blk_e = jnp.minimum(jnp.sum(pad_end[None, :] <= blk_start[:, None], axis=1, dtype=jnp.int32),
                        n_exp - 1)
    n_used = (pad_end[-1:] // blk).astype(jnp.int32)
    tail = jnp.stack([(pad_start + counts) * ROW_SUB, padded - counts], axis=0).astype(jnp.int32)

    before = jnp.cumsum(cnt, axis=0) - cnt
    run_start = jnp.cumsum(cnt, axis=1) - cnt
    tabs = jnp.stack([cnt, (pad_start[None, :] + before) * ROW_SUB, run_start * ROW_SUB], axis=1)

    col = jnp.arange(256, dtype=jnp.int32)[None, :]
    row = jnp.arange(256, dtype=jnp.int32)[:, None]
    perm = (row == jnp.where(col < 128, 2 * col, 2 * (col - 128) + 1)).astype(BF16)
    b_gu_p = jnp.concatenate([b_gu[0][:, 0::2], b_gu[0][:, 1::2]], axis=-1).reshape(n_exp, 1, 2 * f)
    xs = _dispatch_call(tabs, tail, n_used, h2p, pos_t, n_xs_rows=n_rows)
    ys = _moe_call(blk_e, n_used, xs, perm, w_gu[0], b_gu_p, w_down[0], b_down[0].reshape(n_exp, 1, d))

    out = _final_call(tabs, x1, pos_t.T, wgt_t.T, mod, fmod, g_final.reshape(1, d), ys, seq=seq)
    return out.reshape(bsz, seq, d)
```

```python
import functools

import jax
import jax.numpy as jnp
from jax import lax
from jax.experimental import pallas as pl
from jax.experimental.pallas import tpu as pltpu

F32 = jnp.float32
BF16 = jnp.bfloat16

CHUNK = 64
POOL_WINDOWS = (2, 4, 8, 16)
POOL_HALO = 16
N_HEADS = 8
QK_NOPE = 128
QK_ROPE = 64
V_HEAD = 128
HEAD_PAD = 256
ROPE_THETA = 10000.0
TOP_K = 4
SWIGLU_LIMIT = 7.0
SWIGLU_ALPHA = 1.702
NORM_EPS = 1e-6
NEG_INF = -1e30
LOG2_E = 1.4426950408889634

VMEM_LIMIT = 56 * 1024 * 1024

TM_IN = 256
TQ = 512
TM_POST = 512
MOE_BLK = 512
TM_FIN = 256
ROW_SUB = 4


def _const_spec(shape):
    nd = len(shape)
    return pl.BlockSpec(shape, lambda *_: (0,) * nd, pipeline_mode=pl.Buffered(1))


def _rms(xf, g):
    return xf * lax.rsqrt(jnp.mean(xf * xf, axis=-1, keepdims=True) + NORM_EPS) * g


def _pack_bf16_pair(lo, hi):
    lo_b = lax.bitcast_convert_type(lo.astype(BF16).astype(F32), jnp.uint32)
    hi_b = lax.bitcast_convert_type(hi.astype(BF16).astype(F32), jnp.uint32)
    return (hi_b & jnp.uint32(0xFFFF0000)) | (lo_b >> 16)


def _unpack_bf16_pair(p):
    lo = lax.bitcast_convert_type(p << 16, F32)
    hi = lax.bitcast_convert_type(p & jnp.uint32(0xFFFF0000), F32)
    return lo, hi


def _store_rows(ref, packed):
    n = packed.shape[0]
    for q in range(ROW_SUB):
        ref[pl.ds(q, n, stride=ROW_SUB), :] = packed[:, q * 128:(q + 1) * 128]


def _load_rows(ref, n):
    los, his = [], []
    for q in range(ROW_SUB):
        lo, hi = _unpack_bf16_pair(ref[pl.ds(q, n, stride=ROW_SUB), :])
        los.append(lo)
        his.append(hi)
    return jnp.concatenate(los + his, axis=1)


def _mod_kernel(c_ref, w_ref, b_ref, o_ref):
    c = c_ref[...]
    c_act = c * jax.nn.sigmoid(c)
    o_ref[...] = jnp.dot(c_act, w_ref[...], preferred_element_type=F32,
                         precision=lax.Precision.HIGHEST) + b_ref[...]


def _mod_call(c, w, b, tn=1024):
    bsz, d = c.shape
    n = w.shape[1]
    return pl.pallas_call(
        _mod_kernel,
        grid=(n // tn,),
        in_specs=[pl.BlockSpec((bsz, d), lambda j: (0, 0)),
                  pl.BlockSpec((d, tn), lambda j: (0, j)),
                  pl.BlockSpec((1, tn), lambda j: (0, j))],
        out_specs=pl.BlockSpec((bsz, tn), lambda j: (0, j)),
        out_shape=jax.ShapeDtypeStruct((bsz, n), F32),
        compiler_params=pltpu.CompilerParams(dimension_semantics=("arbitrary",),
                                             vmem_limit_bytes=VMEM_LIMIT),
        name="adaln_mod",
    )(c, w, b.reshape(1, n))


def _mixer_in_kernel(x_ref, xh_ref, pos_ref, mod_ref, gmix_ref, win_ref, bgate_ref, wgrp_ref,
                     pscale_ref, wpo_ref, gq_ref, wq_ref, gkv_ref, wkn_ref, wvt_ref, invf_ref,
                     ga_ref, g1_ref, q_ref, k_ref, vt_ref, u_scr, *, tm, tiles_per_seq, d, q_lora,
                     kv_lora):
    i = pl.program_id(0)
    t_in_seq = i % tiles_per_seq
    is_start = t_in_seq == 0
    shift1 = mod_ref[0, 0:1, :]
    scale1 = mod_ref[0, 1:2, :]
    gmix = gmix_ref[...]

    def prenorm(xf):
        return (_rms(xf, gmix) * (1.0 + scale1) + shift1).astype(BF16)

    h = prenorm(x_ref[...])
    hh = prenorm(xh_ref[...])
    u = jnp.dot(h, win_ref[:, 0:d], preferred_element_type=F32)
    uh = jnp.dot(hh, win_ref[:, 0:d], preferred_element_type=F32)
    u_scr[0:POOL_HALO, :] = jnp.where(is_start, 0.0, uh)
    u_scr[POOL_HALO:POOL_HALO + tm, :] = u
    rest = jnp.dot(h, win_ref[:, d:], preferred_element_type=F32)

    gw = d // len(POOL_WINDOWS)
    tseq = t_in_seq * tm + lax.broadcasted_iota(jnp.int32, (tm, 1), 0)
    ys = []
    for g, w in enumerate(POOL_WINDOWS):
        c0 = g * gw
        ug = u_scr[POOL_HALO:POOL_HALO + tm, c0:c0 + gw]
        acc = ug
        for j in range(1, w):
            acc = acc + u_scr[POOL_HALO - j:POOL_HALO - j + tm, c0:c0 + gw]
        cnt = jnp.minimum(tseq + 1, w).astype(F32)
        mixed = (acc / cnt - ug).astype(BF16)
        ys.append(jnp.dot(mixed, wgrp_ref[g], preferred_element_type=F32))
    y = (jnp.concatenate(ys, axis=1) * pscale_ref[...]).astype(BF16)
    a = jnp.dot(y, wpo_ref[...], preferred_element_type=F32)

    o_q = 0
    o_kv = q_lora
    o_kpe = q_lora + kv_lora
    o_ksw = o_kpe + 128
    o_g0 = o_ksw + 128
    o_g1 = o_g0 + d
    gates0 = jax.nn.sigmoid(rest[:, o_g0:o_g0 + d] + bgate_ref[:, 0:d])
    gates1 = jax.nn.sigmoid(rest[:, o_g1:o_g1 + d] + bgate_ref[:, d:2 * d])
    ga_ref[...] = (gates0 * a).astype(BF16)
    g1_ref[...] = gates1.astype(BF16)

    ang = pos_ref[...] * invf_ref[...]
    cos2 = jnp.cos(ang)
    sin2 = jnp.sin(ang)
    q_scale = float(QK_NOPE + QK_ROPE) ** -0.5 * LOG2_E

    qn = _rms(rest[:, o_q:o_q + q_lora], gq_ref[...]).astype(BF16)
    qall = jnp.dot(qn, wq_ref[...], preferred_element_type=F32)
    kvn = _rms(rest[:, o_kv:o_kv + kv_lora], gkv_ref[...]).astype(BF16)
    kn = jnp.dot(kvn, wkn_ref[...], preferred_element_type=F32)
    vt = lax.dot_general(wvt_ref[...], kvn, (((1,), (1,)), ((), ())),
                         preferred_element_type=F32)
    vt_ref[0] = vt.astype(BF16)
    kpe = (rest[:, o_kpe:o_kpe + 128] * cos2 + rest[:, o_ksw:o_ksw + 128] * sin2).astype(BF16)
    o_pe = N_HEADS * QK_NOPE
    o_sw = o_pe + N_HEADS * QK_ROPE
    low_half = lax.broadcasted_iota(jnp.int32, (tm, 128), 1) < QK_ROPE
    for pair in range(N_HEADS // 2):
        c0 = pair * 128
        qpe2 = (qall[:, o_pe + c0:o_pe + c0 + 128] * cos2
                + qall[:, o_sw + c0:o_sw + c0 + 128] * sin2) * q_scale
        for hd, part in ((2 * pair, qpe2), (2 * pair + 1, pltpu.roll(qpe2, QK_ROPE, axis=1))):
            n0 = hd * QK_NOPE
            q_ref[:, hd * HEAD_PAD:hd * HEAD_PAD + 128] = (qall[:, n0:n0 + 128] * q_scale).astype(BF16)
            q_ref[:, hd * HEAD_PAD + 128:(hd + 1) * HEAD_PAD] = jnp.where(low_half, part, 0.0).astype(BF16)
            k_ref[:, hd * HEAD_PAD:hd * HEAD_PAD + 128] = kn[:, n0:n0 + 128].astype(BF16)
            k_ref[:, hd * HEAD_PAD + 128:(hd + 1) * HEAD_PAD] = kpe


def _mixer_in_call(x2d, pos_col, mod, g_mix, w_in_p, b_gate, w_grp, pool_scale, w_po, g_q_a, w_q_p,
                   g_kv_a, w_kn, w_vt, invf2, *, bsz, seq):
    t, d = x2d.shape
    tm = TM_IN
    tps = seq // tm
    q_lora = g_q_a.shape[-1]
    kv_lora = g_kv_a.shape[-1]
    hp = N_HEADS * HEAD_PAD
    halo_blocks = tm // POOL_HALO
    kern = functools.partial(_mixer_in_kernel, tm=tm, tiles_per_seq=tps, d=d, q_lora=q_lora,
                             kv_lora=kv_lora)
    row = lambda i: (i, 0)
    return pl.pallas_call(
        kern,
        grid=(t // tm,),
        in_specs=[
            pl.BlockSpec((tm, d), row),
            pl.BlockSpec((POOL_HALO, d), lambda i: (jnp.maximum(i * halo_blocks - 1, 0), 0)),
            pl.BlockSpec((tm, 1), row),
            pl.BlockSpec((1, 6, d), lambda i: (i // tps, 0, 0)),
            _const_spec((1, d)),
            _const_spec(w_in_p.shape),
            _const_spec((1, 2 * d)),
            _const_spec(w_grp.shape),
            _const_spec((1, d)),
            _const_spec(w_po.shape),
            _const_spec((1, q_lora)),
            _const_spec(w_q_p.shape),
            _const_spec((1, kv_lora)),
            _const_spec(w_kn.shape),
            _const_spec(w_vt.shape),
            _const_spec((1, 128)),
        ],
        out_specs=[
            pl.BlockSpec((tm, d), row),
            pl.BlockSpec((tm, d), row),
            pl.BlockSpec((tm, hp), row),
            pl.BlockSpec((tm, hp), row),
            pl.BlockSpec((1, N_HEADS * V_HEAD, tm), lambda i: (i // tps, 0, i % tps)),
        ],
        out_shape=[
            jax.ShapeDtypeStruct((t, d), BF16),
            jax.ShapeDtypeStruct((t, d), BF16),
            jax.ShapeDtypeStruct((t, hp), BF16),
            jax.ShapeDtypeStruct((t, hp), BF16),
            jax.ShapeDtypeStruct((bsz, N_HEADS * V_HEAD, seq), BF16),
        ],
        scratch_shapes=[pltpu.VMEM((tm + POOL_HALO, d), F32)],
        compiler_params=pltpu.CompilerParams(dimension_semantics=("arbitrary",),
                                             vmem_limit_bytes=VMEM_LIMIT),
        name="mixer_in",
    )(x2d, x2d, pos_col, mod, g_mix, w_in_p, b_gate, w_grp, pool_scale, w_po, g_q_a, w_q_p, g_kv_a,
      w_kn, w_vt, invf2)


def _attn_kernel(q_ref, k_ref, vt_ref, o_ref, *, tq, nq):
    seq = vt_ref.shape[2]
    vt_ext = jnp.concatenate([vt_ref[0], jnp.ones((16, seq), BF16)], axis=0)

    def scores(qi):
        q0 = qi * tq
        ln = q0 + tq
        return lax.dot_general(k_ref[0, 0:ln, :], q_ref[0, q0:q0 + tq, :], (((1,), (1,)), ((), ())),
                               preferred_element_type=F32)

    def finish(qi, s):
        q0 = qi * tq
        ln = q0 + tq
        cw = tq // 2
        for c in range(2):
            cs = slice(c * cw, (c + 1) * cw)
            kc = (q0 + lax.broadcasted_iota(jnp.int32, (tq, cw), 0)) // CHUNK
            qc = (q0 + c * cw + lax.broadcasted_iota(jnp.int32, (tq, cw), 1)) // CHUNK
            s_diag = jnp.where(qc >= kc, s[q0:ln, cs], NEG_INF)
            m = jnp.max(s_diag, axis=0, keepdims=True)
            if qi > 0:
                s_main = s[0:q0, cs]
                m = jnp.maximum(m, jnp.max(s_main, axis=0, keepdims=True))
            p_diag = jnp.exp2(s_diag - m).astype(BF16)
            acc = jnp.dot(vt_ext[:, q0:ln], p_diag, preferred_element_type=F32)
            if qi > 0:
                p_main = jnp.exp2(s_main - m).astype(BF16)
                acc = acc + jnp.dot(vt_ext[:, 0:q0], p_main, preferred_element_type=F32)
            l = acc[V_HEAD:V_HEAD + 1, :]
            o_ref[0, q0 + c * cw:q0 + (c + 1) * cw, :] = (acc[0:V_HEAD, :] / l).T.astype(BF16)

    order = list(range(nq))[::-1]
    s_cur = scores(order[0])
    for idx, qi in enumerate(order):
        s_next = scores(order[idx + 1]) if idx + 1 < nq else None
        finish(qi, s_cur)
        s_cur = s_next


def _attn_call(q3, k3, vt3):
    bsz, seq, _ = q3.shape
    kern = functools.partial(_attn_kernel, tq=TQ, nq=seq // TQ)
    return pl.pallas_call(
        kern,
        grid=(bsz, N_HEADS),
        in_specs=[pl.BlockSpec((1, seq, HEAD_PAD), lambda b, h: (b, 0, h)),
                  pl.BlockSpec((1, seq, HEAD_PAD), lambda b, h: (b, 0, h)),
                  pl.BlockSpec((1, V_HEAD, seq), lambda b, h: (b, h, 0))],
        out_specs=pl.BlockSpec((1, seq, V_HEAD), lambda b, h: (b, 0, h)),
        out_shape=jax.ShapeDtypeStruct((bsz, seq, N_HEADS * V_HEAD), BF16),
        compiler_params=pltpu.CompilerParams(dimension_semantics=("arbitrary", "arbitrary"),
                                             vmem_limit_bytes=VMEM_LIMIT),
        name="mla_attn",
    )(q3, k3, vt3)


def _post_kernel(o_ref, ga_ref, g1_ref, x_ref, mod_ref, wmo_ref, wout_ref, gffn_ref, wrt_ref, br_ref,
                 x1_ref, h2_ref, pos_ref, wgt_ref, cnt_ref, *, d):
    m = jnp.dot(o_ref[...], wmo_ref[...], preferred_element_type=F32)
    merged = ga_ref[...].astype(F32) + g1_ref[...].astype(F32) * m
    gate1 = mod_ref[0, 2:3, :]
    x1 = x_ref[...] + gate1 * jnp.dot(merged.astype(BF16), wout_ref[...],
                                      preferred_element_type=F32)
    x1_ref[...] = x1
    shift2 = mod_ref[0, 3:4, :]
    scale2 = mod_ref[0, 4:5, :]
    h2 = _rms(x1, gffn_ref[...]) * (1.0 + scale2) + shift2
    _store_rows(h2_ref, _pack_bf16_pair(h2[:, 0:d // 2], h2[:, d // 2:d]))

    ne = br_ref.shape[0]
    h_hi = h2.astype(BF16)
    h_lo = (h2 - h_hi.astype(F32)).astype(BF16)
    hcat = jnp.concatenate([h_hi, h_lo], axis=1)
    half = hcat.shape[0] // 2
    lg2 = jnp.concatenate(
        [jnp.dot(hcat[0:half], wrt_ref[...], preferred_element_type=F32),
         jnp.dot(hcat[half:], wrt_ref[...], preferred_element_type=F32)], axis=0)
    lg = lg2[:, 0:128] + lg2[:, 128:256]
    logits = lg.T[0:ne, :] + br_ref[...]
    tm = logits.shape[1]
    eid = lax.broadcasted_iota(jnp.int32, (ne, tm), 0)
    vals, idxs = [], []
    cur = logits
    for _ in range(TOP_K):
        mx = jnp.max(cur, axis=0, keepdims=True)
        ix = jnp.min(jnp.where(cur == mx, eid, ne), axis=0, keepdims=True)
        vals.append(mx)
        idxs.append(ix)
        cur = jnp.where(eid == ix, -jnp.inf, cur)
    es = [jnp.exp(v - vals[0]) for v in vals]
    den = es[0] + es[1] + es[2] + es[3]
    wgt_ref[...] = jnp.concatenate([e / den for e in es], axis=0)

    tt = TM_FIN
    t_row = lax.broadcasted_iota(jnp.int32, (tm, tm), 0)
    t_col = lax.broadcasted_iota(jnp.int32, (tm, tm), 1)
    same_tile = (t_row // tt) == (t_col // tt)
    in_tile = jnp.where(same_tile, 1.0, 0.0).astype(BF16)
    earlier = jnp.where(same_tile & (t_row < t_col), 1.0, 0.0).astype(BF16)
    lower = jnp.where(lax.broadcasted_iota(jnp.int32, (ne, ne), 1)
                      < lax.broadcasted_iota(jnp.int32, (ne, ne), 0), 1.0, 0.0).astype(BF16)
    picks = [eid == ix for ix in idxs]
    routed = jnp.where(picks[0] | picks[1] | picks[2] | picks[3], 1.0, 0.0).astype(BF16)
    before = jnp.dot(routed, earlier, preferred_element_type=F32)
    cnt_b = jnp.dot(routed, in_tile, preferred_element_type=F32)
    run_start = jnp.dot(lower, cnt_b.astype(BF16), preferred_element_type=F32)
    base = run_start + before
    pos_ref[...] = jnp.concatenate(
        [jnp.sum(jnp.where(pk, base, 0.0), axis=0, keepdims=True) for pk in picks],
        axis=0).astype(jnp.int32)
    tile_sel = jnp.where(lax.broadcasted_iota(jnp.int32, (8, tm), 0)
                         == lax.broadcasted_iota(jnp.int32, (8, tm), 1) // tt, 1.0, 0.0).astype(BF16)
    cnt8 = lax.dot_general(tile_sel, routed, (((1,), (1,)), ((), ())),
                           preferred_element_type=F32)
    cnt_ref[0] = jnp.concatenate([cnt8, jnp.zeros((8, 128 - ne), F32)], axis=1).astype(jnp.int32)


def _post_call(o2d, ga, g1, x2d, mod, w_mo, w_out, g_ffn, w_rt, b_r, *, seq):
    t, d = x2d.shape
    tm = TM_POST
    tps = seq // tm
    row = lambda i: (i, 0)
    n_exp = b_r.shape[0]
    assert tm % TM_FIN == 0 and tm // TM_FIN <= 8 and n_exp <= 128
    kern = functools.partial(_post_kernel, d=d)
    return pl.pallas_call(
        kern,
        grid=(t // tm,),
        in_specs=[
            pl.BlockSpec((tm, d), row),
            pl.BlockSpec((tm, d), row),
            pl.BlockSpec((tm, d), row),
            pl.BlockSpec((tm, d), row),
            pl.BlockSpec((1, 6, d), lambda i: (i // tps, 0, 0)),
            _const_spec(w_mo.shape),
            _const_spec(w_out.shape),
            _const_spec((1, d)),
            _const_spec(w_rt.shape),
            _const_spec(b_r.shape),
        ],
        out_specs=[
            pl.BlockSpec((tm, d), row),
            pl.BlockSpec((tm * ROW_SUB, 128), row),
            pl.BlockSpec((TOP_K, tm), lambda i: (0, i)),
            pl.BlockSpec((TOP_K, tm), lambda i: (0, i)),
            pl.BlockSpec((1, 8, 128), lambda i: (i, 0, 0)),
        ],
        out_shape=[
            jax.ShapeDtypeStruct((t, d), F32),
            jax.ShapeDtypeStruct((t * ROW_SUB, 128), jnp.uint32),
            jax.ShapeDtypeStruct((TOP_K, t), jnp.int32),
            jax.ShapeDtypeStruct((TOP_K, t), F32),
            jax.ShapeDtypeStruct((t // tm, 8, 128), jnp.int32),
        ],
        compiler_params=pltpu.CompilerParams(dimension_semantics=("arbitrary",),
                                             vmem_limit_bytes=VMEM_LIMIT),
        name="post_attn_router",
    )(o2d, ga, g1, x2d, mod, w_mo, w_out, g_ffn, w_rt, b_r)


RUN_BITS = 9
COMMON_RUN_BITS = 6
TAIL_BITS = 9


def _copy_pieces(count, src_ref, src_off, dst_ref, dst_off, sem, bits, wait=False):
    def pieces(bit_range, off):
        for b in reversed(bit_range):
            size = (1 << b) * ROW_SUB
            take = (count >> b) & 1

            @pl.when(take == 1)
            def _(off=off, size=size):
                cp = pltpu.make_async_copy(
                    src_ref.at[pl.ds(pl.multiple_of(src_off + off, ROW_SUB), size)],
                    dst_ref.at[pl.ds(pl.multiple_of(dst_off + off, ROW_SUB), size)], sem)
                if wait:
                    cp.wait()
                else:
                    cp.start()
            off = off + take * size

    lo = min(bits, COMMON_RUN_BITS)
    big = (count >> lo) << lo

    if lo < bits:
        @pl.when(big != 0)
        def _():
            pieces(range(lo, bits), 0)
    pieces(range(lo), big * ROW_SUB)


def _dispatch_kernel(tab_ref, tail_ref, nused_ref, h_ref, pos_ref, xs_hbm, stage, zbuf, sem, zsem, *,
                     tm, n_exp, n_blocks):
    i = pl.program_id(0)
    n = pl.num_programs(0)
    slot = i % 2
    n_rows = TOP_K * tm

    def wait_stage(slot_):
        pltpu.make_async_copy(stage.at[slot_], xs_hbm.at[pl.ds(0, n_rows * ROW_SUB)],
                              sem.at[slot_]).wait()

    @pl.when(i == 0)
    def _():
        zbuf[...] = jnp.zeros(zbuf.shape, zbuf.dtype)
        for wait in (False, True):
            def body(e, c, wait=wait):
                _copy_pieces(tail_ref[1, e], zbuf, 0, xs_hbm, tail_ref[0, e], zsem, TAIL_BITS, wait)
                return c
            lax.fori_loop(0, n_exp, body, 0)
        blk_rows = zbuf.shape[0]

        def zero_block(b, c):
            cp = pltpu.make_async_copy(
                zbuf, xs_hbm.at[pl.ds(pl.multiple_of(b * blk_rows, blk_rows), blk_rows)], zsem)
            cp.start()
            cp.wait()
            return c
        lax.fori_loop(nused_ref[0], n_blocks, zero_block, 0)

    @pl.when(i >= 2)
    def _():
        wait_stage(slot)

    h = _load_rows(h_ref, tm).astype(BF16)
    row_id = lax.broadcasted_iota(jnp.int32, (n_rows, tm), 0)
    hit = row_id == pos_ref[0:1, :]
    for k in range(1, TOP_K):
        hit = hit | (row_id == pos_ref[k:k + 1, :])
    onehot = jnp.where(hit, 1.0, 0.0).astype(BF16)
    rows = jnp.dot(onehot, h, preferred_element_type=F32)
    dh = rows.shape[1] // 2
    _store_rows(stage.at[slot], _pack_bf16_pair(rows[:, 0:dh], rows[:, dh:]))

    def body(e, c):
        _copy_pieces(tab_ref[0, 0, e], stage.at[slot], tab_ref[0, 2, e], xs_hbm, tab_ref[0, 1, e],
                     sem.at[slot], RUN_BITS)
        return c
    lax.fori_loop(0, n_exp, body, 0)

    @pl.when(i == n - 1)
    def _():
        wait_stage(slot)
        wait_stage(1 - slot)


def _dispatch_call(tabs, tail, n_used, h2p, pos_t, *, n_xs_rows):
    nt, _, n_exp = tabs.shape
    tm = TM_FIN
    assert nt >= 2 and tm < (1 << RUN_BITS) and MOE_BLK <= (1 << TAIL_BITS)
    n_rows = TOP_K * tm
    kern = functools.partial(_dispatch_kernel, tm=tm, n_exp=n_exp, n_blocks=n_xs_rows // MOE_BLK)
    return pl.pallas_call(
        kern,
        grid=(nt,),
        in_specs=[
            pl.BlockSpec((1, 3, n_exp), lambda i: (i, 0, 0), memory_space=pltpu.SMEM),
            pl.BlockSpec(memory_space=pltpu.SMEM),
            pl.BlockSpec(memory_space=pltpu.SMEM),
            pl.BlockSpec((tm * ROW_SUB, 128), lambda i: (i, 0)),
            pl.BlockSpec((TOP_K, tm), lambda i: (0, i)),
        ],
        out_specs=pl.BlockSpec(memory_space=pl.ANY),
        out_shape=jax.ShapeDtypeStruct((n_xs_rows * ROW_SUB, 128), jnp.uint32),
        scratch_shapes=[pltpu.VMEM((2, n_rows * ROW_SUB, 128), jnp.uint32),
                        pltpu.VMEM((MOE_BLK * ROW_SUB, 128), jnp.uint32),
                        pltpu.SemaphoreType.DMA((2,)),
                        pltpu.SemaphoreType.DMA(())],
        compiler_params=pltpu.CompilerParams(dimension_semantics=("arbitrary",),
                                             vmem_limit_bytes=VMEM_LIMIT),
        name="moe_dispatch",
    )(tabs, tail, n_used, h2p, pos_t)


def _moe_kernel(blk_e_ref, nused_ref, x_ref, perm_ref, wgu_ref, bgu_ref, wd_ref, bd_ref, o_ref,
                wgu_scr, wd_scr, *, blk, f):
    s = pl.program_id(0)
    e_cur = blk_e_ref[s]
    e_prev = blk_e_ref[jnp.maximum(s - 1, 0)]

    @pl.when(jnp.logical_or(s == 0, e_cur != e_prev))
    def _():
        perm = perm_ref[...]
        for c in range(2 * f // 256):
            r = jnp.dot(wgu_ref[0, :, c * 256:(c + 1) * 256].astype(BF16), perm,
                        preferred_element_type=F32)
            wgu_scr[:, c * 128:(c + 1) * 128] = r[:, 0:128].astype(BF16)
            wgu_scr[:, f + c * 128:f + (c + 1) * 128] = r[:, 128:256].astype(BF16)
        wd_scr[...] = wd_ref[0].astype(BF16)

    @pl.when(s >= nused_ref[0])
    def _():
        o_ref[...] = jnp.zeros(o_ref.shape, o_ref.dtype)

    @pl.when(s < nused_ref[0])
    def _():
        x = _load_rows(x_ref, blk).astype(BF16)
        gu = jnp.dot(x, wgu_scr[...], preferred_element_type=F32) + bgu_ref[0]
        gate = jnp.minimum(gu[:, 0:f], SWIGLU_LIMIT)
        up = jnp.clip(gu[:, f:2 * f], -SWIGLU_LIMIT, SWIGLU_LIMIT)
        act = (up + 1.0) * (gate * jax.nn.sigmoid(SWIGLU_ALPHA * gate))
        y = jnp.dot(act.astype(BF16), wd_scr[...], preferred_element_type=F32) + bd_ref[0]
        dh = y.shape[1] // 2
        _store_rows(o_ref, _pack_bf16_pair(y[:, 0:dh], y[:, dh:]))


def _moe_call(blk_e, n_used, xs, perm, w_gu, b_gu_p, w_down, b_down):
    n_steps = blk_e.shape[0]
    blk = MOE_BLK
    _, d, f2 = w_gu.shape
    f = f2 // 2
    rows = blk * ROW_SUB
    kern = functools.partial(_moe_kernel, blk=blk, f=f)
    row_blk = lambda i, be, nu: (i, 0)
    grid_spec = pltpu.PrefetchScalarGridSpec(
        num_scalar_prefetch=2,
        grid=(n_steps,),
        in_specs=[
            pl.BlockSpec((rows, 128), row_blk),
            pl.BlockSpec((256, 256), lambda i, be, nu: (0, 0)),
            pl.BlockSpec((1, d, f2), lambda i, be, nu: (be[i], 0, 0)),
            pl.BlockSpec((1, 1, f2), lambda i, be, nu: (be[i], 0, 0)),
            pl.BlockSpec((1, f, d), lambda i, be, nu: (be[i], 0, 0)),
            pl.BlockSpec((1, 1, d), lambda i, be, nu: (be[i], 0, 0)),
        ],
        out_specs=pl.BlockSpec((rows, 128), row_blk),
        scratch_shapes=[
            pltpu.VMEM((d, f2), BF16),
            pltpu.VMEM((f, d), BF16),
        ],
    )
    return pl.pallas_call(
        kern,
        grid_spec=grid_spec,
        out_shape=jax.ShapeDtypeStruct((n_steps * rows, 128), jnp.uint32),
        compiler_params=pltpu.CompilerParams(dimension_semantics=("arbitrary",),
                                             vmem_limit_bytes=VMEM_LIMIT),
        name="moe_experts",
    )(blk_e, n_used, xs, perm, w_gu, b_gu_p, w_down, b_down)


def _final_kernel(tab_cur_ref, tab_nxt_ref, x1_ref, pos_ref, w_ref, mod_ref, fmod_ref, gfin_ref, ys_hbm,
                  o_ref, stage, sem, *, tm, n_exp):
    i = pl.program_id(0)
    n = pl.num_programs(0)
    slot = i % 2
    n_rows = TOP_K * tm

    def issue_tile(tab_ref, slot_):
        def body(e, c):
            _copy_pieces(tab_ref[0, 0, e], ys_hbm, tab_ref[0, 1, e], stage.at[slot_], tab_ref[0, 2, e],
                         sem.at[slot_], RUN_BITS)
            return c
        lax.fori_loop(0, n_exp, body, 0)

    @pl.when(i == 0)
    def _():
        issue_tile(tab_cur_ref, 0)

    @pl.when(i + 1 < n)
    def _():
        issue_tile(tab_nxt_ref, 1 - slot)

    pltpu.make_async_copy(ys_hbm.at[pl.ds(0, n_rows * ROW_SUB)], stage.at[slot], sem.at[slot]).wait()

    y = _load_rows(stage.at[slot], n_rows).astype(BF16)
    col_id = lax.broadcasted_iota(jnp.int32, (tm, n_rows), 1)
    pos = pos_ref[...]
    w = w_ref[...]
    a = jnp.where(col_id == pos[:, 0:1], w[:, 0:1], 0.0)
    for k in range(1, TOP_K):
        a = a + jnp.where(col_id == pos[:, k:k + 1], w[:, k:k + 1], 0.0)
    moe = jnp.dot(a.astype(BF16), y, preferred_element_type=F32)
    gate2 = mod_ref[0, 5:6, :]
    x2 = x1_ref[...] + gate2 * moe
    fshift = fmod_ref[0, 0:1, :]
    fscale = fmod_ref[0, 1:2, :]
    o_ref[...] = _rms(x2, gfin_ref[...]) * (1.0 + fscale) + fshift


def _final_call(tabs, x1, pos_col, w_col, mod, fmod, g_final, ys, *, seq):
    t, d = x1.shape
    tm = TM_FIN
    assert tm < (1 << RUN_BITS)
    tps = seq // tm
    nt = t // tm
    n_exp = tabs.shape[2]
    n_rows = TOP_K * tm
    kern = functools.partial(_final_kernel, tm=tm, n_exp=n_exp)
    tab_spec = lambda ahead: pl.BlockSpec((1, 3, n_exp), lambda i: (jnp.minimum(i + ahead, nt - 1), 0, 0),
                                          memory_space=pltpu.SMEM)
    return pl.pallas_call(
        kern,
        grid=(nt,),
        in_specs=[
            tab_spec(0),
            tab_spec(1),
            pl.BlockSpec((tm, d), lambda i: (i, 0)),
            pl.BlockSpec((tm, TOP_K), lambda i: (i, 0)),
            pl.BlockSpec((tm, TOP_K), lambda i: (i, 0)),
            pl.BlockSpec((1, 6, d), lambda i: (i // tps, 0, 0)),
            pl.BlockSpec((1, 2, d), lambda i: (i // tps, 0, 0)),
            _const_spec((1, d)),
            pl.BlockSpec(memory_space=pl.ANY),
        ],
        out_specs=pl.BlockSpec((tm, d), lambda i: (i, 0)),
        out_shape=jax.ShapeDtypeStruct((t, d), F32),
        scratch_shapes=[pltpu.VMEM((2, n_rows * ROW_SUB, 128), jnp.uint32),
                        pltpu.SemaphoreType.DMA((2,))],
        compiler_params=pltpu.CompilerParams(dimension_semantics=("arbitrary",),
                                             vmem_limit_bytes=VMEM_LIMIT),
        name="combine_final",
    )(tabs, tabs, x1, pos_col, w_col, mod, fmod, g_final, ys)


def _prep_w_in(w_in, d, q_lora, kv_lora):
    o_kpe = d + q_lora + kv_lora
    o_g = o_kpe + QK_ROPE
    half = QK_ROPE // 2
    kpe = w_in[:, o_kpe:o_kpe + QK_ROPE]
    zpad = jnp.zeros((d, 128 - QK_ROPE), w_in.dtype)
    ksw = jnp.concatenate([-kpe[:, half:], kpe[:, :half]], axis=1)
    return jnp.concatenate([w_in[:, :o_kpe], kpe, zpad, ksw, zpad, w_in[:, o_g:]], axis=1).astype(BF16)


def _prep_w_q(w_q_b):
    ql = w_q_b.shape[0]
    hd = QK_NOPE + QK_ROPE
    half = QK_ROPE // 2
    w = w_q_b.reshape(ql, N_HEADS, hd)
    nope = w[:, :, :QK_NOPE]
    pe = w[:, :, QK_NOPE:]
    sw = jnp.concatenate([-pe[:, :, half:], pe[:, :, :half]], axis=2)
    return jnp.concatenate([nope.reshape(ql, -1), pe.reshape(ql, -1), sw.reshape(ql, -1)],
                           axis=1).astype(BF16)


def _prep_w_kv(w_kv_b):
    kvl = w_kv_b.shape[0]
    w = w_kv_b.reshape(kvl, N_HEADS, QK_NOPE + V_HEAD)
    w_kn = w[:, :, :QK_NOPE].reshape(kvl, -1).astype(BF16)
    w_vt = w[:, :, QK_NOPE:].reshape(kvl, -1).T.astype(BF16)
    return w_kn, w_vt


def kernel(x, c, positions, w_mod, b_mod, g_mix, w_in, b_gate, w_pool_grp, pool_scale, w_pool_out,
           g_q_a, w_q_b, g_kv_a, w_kv_b, w_mla_out, w_out, g_ffn, w_router, b_router, w_gu, b_gu,
           w_down, b_down, g_final, w_fmod, b_fmod):
    bsz, seq, d = x.shape
    t = bsz * seq
    depth = w_mod.shape[0]
    assert depth == 1
    assert seq % TQ == 0 and seq % TM_IN == 0 and seq % TM_POST == 0 and seq % TM_FIN == 0
    q_lora = g_q_a.shape[-1]
    kv_lora = g_kv_a.shape[-1]
    n_exp = w_gu.shape[1]
    f = w_gu.shape[-1] // 2
    blk = MOE_BLK

    x2d = x.reshape(t, d)
    pos_col = positions.astype(F32).reshape(t, 1)
    inv_freq = 1.0 / (ROPE_THETA ** (jnp.arange(0, QK_ROPE, 2, dtype=F32) / QK_ROPE))
    invf2 = jnp.tile(inv_freq, 128 // (QK_ROPE // 2)).reshape(1, 128)

    mod = _mod_call(c, w_mod[0], b_mod[0]).reshape(bsz, 6, d)
    fmod = _mod_call(c, w_fmod, b_fmod).reshape(bsz, 2, d)

    w_in_p = _prep_w_in(w_in[0], d, q_lora, kv_lora)
    w_q_p = _prep_w_q(w_q_b[0])
    w_kn, w_vt = _prep_w_kv(w_kv_b[0])
    ga, g1, q2, k2, vt3 = _mixer_in_call(
        x2d, pos_col, mod, g_mix[0].reshape(1, d), w_in_p, b_gate[0].reshape(1, 2 * d),
        w_pool_grp[0].astype(BF16), pool_scale[0].reshape(1, d), w_pool_out[0].astype(BF16),
        g_q_a[0].reshape(1, q_lora), w_q_p, g_kv_a[0].reshape(1, kv_lora), w_kn, w_vt, invf2,
        bsz=bsz, seq=seq)

    hp = N_HEADS * HEAD_PAD
    o = _attn_call(q2.reshape(bsz, seq, hp), k2.reshape(bsz, seq, hp), vt3)

    w_r = w_router[0]
    w_r_hi = w_r.astype(BF16)
    w_r_lo = (w_r - w_r_hi.astype(F32)).astype(BF16)
    lane_pad = lambda a: jnp.pad(a, ((0, 0), (0, 128 - n_exp)))
    w_rt = jnp.concatenate(
        [jnp.concatenate([lane_pad(w_r_hi), lane_pad(w_r_lo)], axis=1),
         jnp.concatenate([lane_pad(w_r_hi), jnp.zeros((d, 128), BF16)], axis=1)], axis=0)
    x1, h2p, pos_t, wgt_t, cnt3 = _post_call(
        o.reshape(t, d), ga, g1, x2d, mod, w_mla_out[0].astype(BF16), w_out[0].astype(BF16),
        g_ffn[0].reshape(1, d), w_rt, b_router[0].reshape(n_exp, 1), seq=seq)

    n_slots = t * TOP_K
    n_rows = n_slots + n_exp * blk
    n_blocks = n_rows // blk
    cnt = cnt3[:, 0:TM_POST // TM_FIN, 0:n_exp].reshape(-1, n_exp)
    counts = jnp.sum(cnt, axis=0)
    padded = (counts + blk - 1) // blk * blk
    pad_end = jnp.cumsum(padded)
    pad_start = pad_end - padded
    blk_start = jnp.arange(n_blocks, dtype=jnp.int32) * blk
    blk_e = jnp.minimum(jnp.sum(pad_end[None, :] <= blk_start[:, None], axis=1, dtype=jnp.int32),
                        n_exp - 1)
    n_used = (pad_end[-1:] // blk).astype(jnp.int32)
    tail = jnp.stack([(pad_start + counts) * ROW_SUB, padded - counts], axis=0).astype(jnp.int32)

    before = jnp.cumsum(cnt, axis=0) - cnt
    run_start = jnp.cumsum(cnt, axis=1) - cnt
    tabs = jnp.stack([cnt, (pad_start[None, :] + before) * ROW_SUB, run_start * ROW_SUB], axis=1)

    col = jnp.arange(256, dtype=jnp.int32)[None, :]
    row = jnp.arange(256, dtype=jnp.int32)[:, None]
    perm = (row == jnp.where(col < 128, 2 * col, 2 * (col - 128) + 1)).astype(BF16)
    b_gu_p = jnp.concatenate([b_gu[0][:, 0::2], b_gu[0][:, 1::2]], axis=-1).reshape(n_exp, 1, 2 * f)
    xs = _dispatch_call(tabs, tail, n_used, h2p, pos_t, n_xs_rows=n_rows)
    ys = _moe_call(blk_e, n_used, xs, perm, w_gu[0], b_gu_p, w_down[0], b_down[0].reshape(n_exp, 1, d))

    out = _final_call(tabs, x1, pos_t.T, wgt_t.T, mod, fmod, g_final.reshape(1, d), ys, seq=seq)
    return out.reshape(bsz, seq, d)
```

```python
import functools

import jax
import jax.numpy as jnp
from jax import lax
from jax.experimental import pallas as pl
from jax.experimental.pallas import tpu as pltpu

F32 = jnp.float32
BF16 = jnp.bfloat16

CHUNK = 64
POOL_WINDOWS = (2, 4, 8, 16)
POOL_HALO = 16
N_HEADS = 8
QK_NOPE = 128
QK_ROPE = 64
V_HEAD = 128
HEAD_PAD = 256
ROPE_THETA = 10000.0
TOP_K = 4
SWIGLU_LIMIT = 7.0
SWIGLU_ALPHA = 1.702
NORM_EPS = 1e-6
NEG_INF = -1e30
LOG2_E = 1.4426950408889634

VMEM_LIMIT = 56 * 1024 * 1024

TM_IN = 256
TQ = 512
TM_POST = 512
MOE_BLK = 512
TM_FIN = 256
ROW_SUB = 4


def _const_spec(shape):
    nd = len(shape)
    return pl.BlockSpec(shape, lambda *_: (0,) * nd, pipeline_mode=pl.Buffered(1))


def _rms(xf, g):
    return xf * lax.rsqrt(jnp.mean(xf * xf, axis=-1, keepdims=True) + NORM_EPS) * g


def _pack_bf16_pair(lo, hi):
    lo_b = lax.bitcast_convert_type(lo.astype(BF16).astype(F32), jnp.uint32)
    hi_b = lax.bitcast_convert_type(hi.astype(BF16).astype(F32), jnp.uint32)
    return (hi_b & jnp.uint32(0xFFFF0000)) | (lo_b >> 16)


def _unpack_bf16_pair(p):
    lo = lax.bitcast_convert_type(p << 16, F32)
    hi = lax.bitcast_convert_type(p & jnp.uint32(0xFFFF0000), F32)
    return lo, hi


def _store_rows(ref, packed):
    n = packed.shape[0]
    for q in range(ROW_SUB):
        ref[pl.ds(q, n, stride=ROW_SUB), :] = packed[:, q * 128:(q + 1) * 128]


def _load_rows(ref, n):
    los, his = [], []
    for q in range(ROW_SUB):
        lo, hi = _unpack_bf16_pair(ref[pl.ds(q, n, stride=ROW_SUB), :])
        los.append(lo)
        his.append(hi)
    return jnp.concatenate(los + his, axis=1)


def _mod_kernel(c_ref, w_ref, b_ref, o_ref):
    c = c_ref[...]
    c_act = c * jax.nn.sigmoid(c)
    o_ref[...] = jnp.dot(c_act, w_ref[...], preferred_element_type=F32,
                         precision=lax.Precision.HIGHEST) + b_ref[...]


def _mod_call(c, w, b, tn=1024):
    bsz, d = c.shape
    n = w.shape[1]
    return pl.pallas_call(
        _mod_kernel,
        grid=(n // tn,),
        in_specs=[pl.BlockSpec((bsz, d), lambda j: (0, 0)),
                  pl.BlockSpec((d, tn), lambda j: (0, j)),
                  pl.BlockSpec((1, tn), lambda j: (0, j))],
        out_specs=pl.BlockSpec((bsz, tn), lambda j: (0, j)),
        out_shape=jax.ShapeDtypeStruct((bsz, n), F32),
        compiler_params=pltpu.CompilerParams(dimension_semantics=("arbitrary",),
                                             vmem_limit_bytes=VMEM_LIMIT),
        name="adaln_mod",
    )(c, w, b.reshape(1, n))


def _mixer_in_kernel(x_ref, xh_ref, pos_ref, mod_ref, gmix_ref, win_ref, bgate_ref, wgrp_ref,
                     pscale_ref, wpo_ref, gq_ref, wq_ref, gkv_ref, wkn_ref, wvt_ref, invf_ref,
                     ga_ref, g1_ref, q_ref, k_ref, vt_ref, u_scr, *, tm, tiles_per_seq, d, q_lora,
                     kv_lora):
    i = pl.program_id(0)
    t_in_seq = i % tiles_per_seq
    is_start = t_in_seq == 0
    shift1 = mod_ref[0, 0:1, :]
    scale1 = mod_ref[0, 1:2, :]
    gmix = gmix_ref[...]

    def prenorm(xf):
        return (_rms(xf, gmix) * (1.0 + scale1) + shift1).astype(BF16)

    h = prenorm(x_ref[...])
    hh = prenorm(xh_ref[...])
    u_ext = jnp.dot(jnp.concatenate([hh, h], axis=0), win_ref[:, 0:d],
                    preferred_element_type=F32)
    u_scr[0:POOL_HALO, :] = jnp.where(is_start, 0.0, u_ext[0:POOL_HALO, :])
    u_scr[POOL_HALO:POOL_HALO + tm, :] = u_ext[POOL_HALO:, :]
    rest = jnp.dot(h, win_ref[:, d:], preferred_element_type=F32)

    gw = d // len(POOL_WINDOWS)
    tseq = t_in_seq * tm + lax.broadcasted_iota(jnp.int32, (tm, 1), 0)
    ys = []
    for g, w in enumerate(POOL_WINDOWS):
        c0 = g * gw
        ug = u_scr[POOL_HALO:POOL_HALO + tm, c0:c0 + gw]
        acc = ug
        for j in range(1, w):
            acc = acc + u_scr[POOL_HALO - j:POOL_HALO - j + tm, c0:c0 + gw]
        cnt = jnp.minimum(tseq + 1, w).astype(F32)
        mixed = (acc / cnt - ug).astype(BF16)
        ys.append(jnp.dot(mixed, wgrp_ref[g], preferred_element_type=F32))
    y = (jnp.concatenate(ys, axis=1) * pscale_ref[...]).astype(BF16)
    a = jnp.dot(y, wpo_ref[...], preferred_element_type=F32)

    o_q = 0
    o_kv = q_lora
    o_kpe = q_lora + kv_lora
    o_ksw = o_kpe + 128
    o_g0 = o_ksw + 128
    o_g1 = o_g0 + d
    gates0 = jax.nn.sigmoid(rest[:, o_g0:o_g0 + d] + bgate_ref[:, 0:d])
    gates1 = jax.nn.sigmoid(rest[:, o_g1:o_g1 + d] + bgate_ref[:, d:2 * d])
    ga_ref[...] = (gates0 * a).astype(BF16)
    g1_ref[...] = gates1.astype(BF16)

    ang = pos_ref[...] * invf_ref[...]
    cos2 = jnp.cos(ang)
    sin2 = jnp.sin(ang)
    q_scale = float(QK_NOPE + QK_ROPE) ** -0.5 * LOG2_E

    qn = _rms(rest[:, o_q:o_q + q_lora], gq_ref[...]).astype(BF16)
    qall = jnp.dot(qn, wq_ref[...], preferred_element_type=F32)
    kvn = _rms(rest[:, o_kv:o_kv + kv_lora], gkv_ref[...]).astype(BF16)
    kn = jnp.dot(kvn, wkn_ref[...], preferred_element_type=F32)
    vt = lax.dot_general(wvt_ref[...], kvn, (((1,), (1,)), ((), ())),
                         preferred_element_type=F32)
    vt_ref[0] = vt.astype(BF16)
    kpe = (rest[:, o_kpe:o_kpe + 128] * cos2 + rest[:, o_ksw:o_ksw + 128] * sin2).astype(BF16)
    o_pe = N_HEADS * QK_NOPE
    o_sw = o_pe + N_HEADS * QK_ROPE
    low_half = lax.broadcasted_iota(jnp.int32, (tm, 128), 1) < QK_ROPE
    for pair in range(N_HEADS // 2):
        c0 = pair * 128
        qpe2 = (qall[:, o_pe + c0:o_pe + c0 + 128] * cos2
                + qall[:, o_sw + c0:o_sw + c0 + 128] * sin2) * q_scale
        for hd, part in ((2 * pair, qpe2), (2 * pair + 1, pltpu.roll(qpe2, QK_ROPE, axis=1))):
            n0 = hd * QK_NOPE
            q_ref[:, hd * HEAD_PAD:hd * HEAD_PAD + 128] = (qall[:, n0:n0 + 128] * q_scale).astype(BF16)
            q_ref[:, hd * HEAD_PAD + 128:(hd + 1) * HEAD_PAD] = jnp.where(low_half, part, 0.0).astype(BF16)
            k_ref[:, hd * HEAD_PAD:hd * HEAD_PAD + 128] = kn[:, n0:n0 + 128].astype(BF16)
            k_ref[:, hd * HEAD_PAD + 128:(hd + 1) * HEAD_PAD] = kpe


def _mixer_in_call(x2d, pos_col, mod, g_mix, w_in_p, b_gate, w_grp, pool_scale, w_po, g_q_a, w_q_p,
                   g_kv_a, w_kn, w_vt, invf2, *, bsz, seq):
    t, d = x2d.shape
    tm = TM_IN
    tps = seq // tm
    q_lora = g_q_a.shape[-1]
    kv_lora = g_kv_a.shape[-1]
    hp = N_HEADS * HEAD_PAD
    halo_blocks = tm // POOL_HALO
    kern = functools.partial(_mixer_in_kernel, tm=tm, tiles_per_seq=tps, d=d, q_lora=q_lora,
                             kv_lora=kv_lora)
    row = lambda i: (i, 0)
    return pl.pallas_call(
        kern,
        grid=(t // tm,),
        in_specs=[
            pl.BlockSpec((tm, d), row),
            pl.BlockSpec((POOL_HALO, d), lambda i: (jnp.maximum(i * halo_blocks - 1, 0), 0)),
            pl.BlockSpec((tm, 1), row),
            pl.BlockSpec((1, 6, d), lambda i: (i // tps, 0, 0)),
            _const_spec((1, d)),
            _const_spec(w_in_p.shape),
            _const_spec((1, 2 * d)),
            _const_spec(w_grp.shape),
            _const_spec((1, d)),
            _const_spec(w_po.shape),
            _const_spec((1, q_lora)),
            _const_spec(w_q_p.shape),
            _const_spec((1, kv_lora)),
            _const_spec(w_kn.shape),
            _const_spec(w_vt.shape),
            _const_spec((1, 128)),
        ],
        out_specs=[
            pl.BlockSpec((tm, d), row),
            pl.BlockSpec((tm, d), row),
            pl.BlockSpec((tm, hp), row),
            pl.BlockSpec((tm, hp), row),
            pl.BlockSpec((1, N_HEADS * V_HEAD, tm), lambda i: (i // tps, 0, i % tps)),
        ],
        out_shape=[
            jax.ShapeDtypeStruct((t, d), BF16),
            jax.ShapeDtypeStruct((t, d), BF16),
            jax.ShapeDtypeStruct((t, hp), BF16),
            jax.ShapeDtypeStruct((t, hp), BF16),
            jax.ShapeDtypeStruct((bsz, N_HEADS * V_HEAD, seq), BF16),
        ],
        scratch_shapes=[pltpu.VMEM((tm + POOL_HALO, d), F32)],
        compiler_params=pltpu.CompilerParams(dimension_semantics=("arbitrary",),
                                             vmem_limit_bytes=VMEM_LIMIT),
        name="mixer_in",
    )(x2d, x2d, pos_col, mod, g_mix, w_in_p, b_gate, w_grp, pool_scale, w_po, g_q_a, w_q_p, g_kv_a,
      w_kn, w_vt, invf2)


def _attn_kernel(q_ref, k_ref, vt_ref, o_ref, *, tq, nq):
    seq = vt_ref.shape[2]
    vt_ext = jnp.concatenate([vt_ref[0], jnp.ones((16, seq), BF16)], axis=0)

    def scores(qi):
        q0 = qi * tq
        ln = q0 + tq
        return lax.dot_general(k_ref[0, 0:ln, :], q_ref[0, q0:q0 + tq, :], (((1,), (1,)), ((), ())),
                               preferred_element_type=F32)

    def finish(qi, s):
        q0 = qi * tq
        ln = q0 + tq
        cw = tq // 2
        for c in range(2):
            cs = slice(c * cw, (c + 1) * cw)
            kc = (q0 + lax.broadcasted_iota(jnp.int32, (tq, cw), 0)) // CHUNK
            qc = (q0 + c * cw + lax.broadcasted_iota(jnp.int32, (tq, cw), 1)) // CHUNK
            s_diag = jnp.where(qc >= kc, s[q0:ln, cs], NEG_INF)
            m = jnp.max(s_diag, axis=0, keepdims=True)
            if qi > 0:
                s_main = s[0:q0, cs]
                m = jnp.maximum(m, jnp.max(s_main, axis=0, keepdims=True))
            p_diag = jnp.exp2(s_diag - m).astype(BF16)
            acc = jnp.dot(vt_ext[:, q0:ln], p_diag, preferred_element_type=F32)
            if qi > 0:
                p_main = jnp.exp2(s_main - m).astype(BF16)
                acc = acc + jnp.dot(vt_ext[:, 0:q0], p_main, preferred_element_type=F32)
            l = acc[V_HEAD:V_HEAD + 1, :]
            o_ref[0, q0 + c * cw:q0 + (c + 1) * cw, :] = (acc[0:V_HEAD, :] / l).T.astype(BF16)

    order = list(range(nq))[::-1]
    s_cur = scores(order[0])
    for idx, qi in enumerate(order):
        s_next = scores(order[idx + 1]) if idx + 1 < nq else None
        finish(qi, s_cur)
        s_cur = s_next


def _attn_call(q3, k3, vt3):
    bsz, seq, _ = q3.shape
    kern = functools.partial(_attn_kernel, tq=TQ, nq=seq // TQ)
    return pl.pallas_call(
        kern,
        grid=(bsz, N_HEADS),
        in_specs=[pl.BlockSpec((1, seq, HEAD_PAD), lambda b, h: (b, 0, h)),
                  pl.BlockSpec((1, seq, HEAD_PAD), lambda b, h: (b, 0, h)),
                  pl.BlockSpec((1, V_HEAD, seq), lambda b, h: (b, h, 0))],
        out_specs=pl.BlockSpec((1, seq, V_HEAD), lambda b, h: (b, 0, h)),
        out_shape=jax.ShapeDtypeStruct((bsz, seq, N_HEADS * V_HEAD), BF16),
        compiler_params=pltpu.CompilerParams(dimension_semantics=("arbitrary", "arbitrary"),
                                             vmem_limit_bytes=VMEM_LIMIT),
        name="mla_attn",
    )(q3, k3, vt3)


def _post_kernel(o_ref, ga_ref, g1_ref, x_ref, mod_ref, wmo_ref, wout_ref, gffn_ref, wrt_ref, br_ref,
                 x1_ref, h2_ref, pos_ref, wgt_ref, cnt_ref, *, d):
    m = jnp.dot(o_ref[...], wmo_ref[...], preferred_element_type=F32)
    merged = ga_ref[...].astype(F32) + g1_ref[...].astype(F32) * m
    gate1 = mod_ref[0, 2:3, :]
    x1 = x_ref[...] + gate1 * jnp.dot(merged.astype(BF16), wout_ref[...],
                                      preferred_element_type=F32)
    x1_ref[...] = x1
    shift2 = mod_ref[0, 3:4, :]
    scale2 = mod_ref[0, 4:5, :]
    h2 = _rms(x1, gffn_ref[...]) * (1.0 + scale2) + shift2
    _store_rows(h2_ref, _pack_bf16_pair(h2[:, 0:d // 2], h2[:, d // 2:d]))

    ne = br_ref.shape[0]
    h_hi = h2.astype(BF16)
    h_lo = (h2 - h_hi.astype(F32)).astype(BF16)
    hcat = jnp.concatenate([h_hi, h_lo], axis=1)
    half = hcat.shape[0] // 2
    lg2 = jnp.concatenate(
        [jnp.dot(hcat[0:half], wrt_ref[...], preferred_element_type=F32),
         jnp.dot(hcat[half:], wrt_ref[...], preferred_element_type=F32)], axis=0)
    lg = lg2[:, 0:128] + lg2[:, 128:256]
    logits = lg.T[0:ne, :] + br_ref[...]
    tm = logits.shape[1]
    eid = lax.broadcasted_iota(jnp.int32, (ne, tm), 0)
    vals, idxs = [], []
    cur = logits
    for _ in range(TOP_K):
        mx = jnp.max(cur, axis=0, keepdims=True)
        ix = jnp.min(jnp.where(cur == mx, eid, ne), axis=0, keepdims=True)
        vals.append(mx)
        idxs.append(ix)
        cur = jnp.where(eid == ix, -jnp.inf, cur)
    es = [jnp.exp(v - vals[0]) for v in vals]
    den = es[0] + es[1] + es[2] + es[3]
    wgt_ref[...] = jnp.concatenate([e / den for e in es], axis=0)

    tt = TM_FIN
    t_row = lax.broadcasted_iota(jnp.int32, (tm, tm), 0)
    t_col = lax.broadcasted_iota(jnp.int32, (tm, tm), 1)
    same_tile = (t_row // tt) == (t_col // tt)
    in_tile = jnp.where(same_tile, 1.0, 0.0).astype(BF16)
    earlier = jnp.where(same_tile & (t_row < t_col), 1.0, 0.0).astype(BF16)
    lower = jnp.where(lax.broadcasted_iota(jnp.int32, (ne, ne), 1)
                      < lax.broadcasted_iota(jnp.int32, (ne, ne), 0), 1.0, 0.0).astype(BF16)
    picks = [eid == ix for ix in idxs]
    routed = jnp.where(picks[0] | picks[1] | picks[2] | picks[3], 1.0, 0.0).astype(BF16)
    before = jnp.dot(routed, earlier, preferred_element_type=F32)
    cnt_b = jnp.dot(routed, in_tile, preferred_element_type=F32)
    run_start = jnp.dot(lower, cnt_b.astype(BF16), preferred_element_type=F32)
    base = run_start + before
    pos_ref[...] = jnp.concatenate(
        [jnp.sum(jnp.where(pk, base, 0.0), axis=0, keepdims=True) for pk in picks],
        axis=0).astype(jnp.int32)
    tile_sel = jnp.where(lax.broadcasted_iota(jnp.int32, (8, tm), 0)
                         == lax.broadcasted_iota(jnp.int32, (8, tm), 1) // tt, 1.0, 0.0).astype(BF16)
    cnt8 = lax.dot_general(tile_sel, routed, (((1,), (1,)), ((), ())),
                           preferred_element_type=F32)
    cnt_ref[0] = jnp.concatenate([cnt8, jnp.zeros((8, 128 - ne), F32)], axis=1).astype(jnp.int32)


def _post_call(o2d, ga, g1, x2d, mod, w_mo, w_out, g_ffn, w_rt, b_r, *, seq):
    t, d = x2d.shape
    tm = TM_POST
    tps = seq // tm
    row = lambda i: (i, 0)
    n_exp = b_r.shape[0]
    assert tm % TM_FIN == 0 and tm // TM_FIN <= 8 and n_exp <= 128
    kern = functools.partial(_post_kernel, d=d)
    return pl.pallas_call(
        kern,
        grid=(t // tm,),
        in_specs=[
            pl.BlockSpec((tm, d), row),
            pl.BlockSpec((tm, d), row),
            pl.BlockSpec((tm, d), row),
            pl.BlockSpec((tm, d), row),
            pl.BlockSpec((1, 6, d), lambda i: (i // tps, 0, 0)),
            _const_spec(w_mo.shape),
            _const_spec(w_out.shape),
            _const_spec((1, d)),
            _const_spec(w_rt.shape),
            _const_spec(b_r.shape),
        ],
        out_specs=[
            pl.BlockSpec((tm, d), row),
            pl.BlockSpec((tm * ROW_SUB, 128), row),
            pl.BlockSpec((TOP_K, tm), lambda i: (0, i)),
            pl.BlockSpec((TOP_K, tm), lambda i: (0, i)),
            pl.BlockSpec((1, 8, 128), lambda i: (i, 0, 0)),
        ],
        out_shape=[
            jax.ShapeDtypeStruct((t, d), F32),
            jax.ShapeDtypeStruct((t * ROW_SUB, 128), jnp.uint32),
            jax.ShapeDtypeStruct((TOP_K, t), jnp.int32),
            jax.ShapeDtypeStruct((TOP_K, t), F32),
            jax.ShapeDtypeStruct((t // tm, 8, 128), jnp.int32),
        ],
        compiler_params=pltpu.CompilerParams(dimension_semantics=("arbitrary",),
                                             vmem_limit_bytes=VMEM_LIMIT),
        name="post_attn_router",
    )(o2d, ga, g1, x2d, mod, w_mo, w_out, g_ffn, w_rt, b_r)


RUN_BITS = 9
COMMON_RUN_BITS = 6
TAIL_BITS = 9


def _copy_pieces(count, src_ref, src_off, dst_ref, dst_off, sem, bits, wait=False):
    def pieces(bit_range, off):
        for b in reversed(bit_range):
            size = (1 << b) * ROW_SUB
            take = (count >> b) & 1

            @pl.when(take == 1)
            def _(off=off, size=size):
                cp = pltpu.make_async_copy(
                    src_ref.at[pl.ds(pl.multiple_of(src_off + off, ROW_SUB), size)],
                    dst_ref.at[pl.ds(pl.multiple_of(dst_off + off, ROW_SUB), size)], sem)
                if wait:
                    cp.wait()
                else:
                    cp.start()
            off = off + take * size

    lo = min(bits, COMMON_RUN_BITS)
    big = (count >> lo) << lo

    if lo < bits:
        @pl.when(big != 0)
        def _():
            pieces(range(lo, bits), 0)
    pieces(range(lo), big * ROW_SUB)


def _dispatch_kernel(tab_ref, tail_ref, nused_ref, h_ref, pos_ref, xs_hbm, stage, zbuf, sem, zsem, *,
                     tm, n_exp, n_blocks):
    i = pl.program_id(0)
    n = pl.num_programs(0)
    slot = i % 2
    n_rows = TOP_K * tm

    def wait_stage(slot_):
        pltpu.make_async_copy(stage.at[slot_], xs_hbm.at[pl.ds(0, n_rows * ROW_SUB)],
                              sem.at[slot_]).wait()

    @pl.when(i == 0)
    def _():
        zbuf[...] = jnp.zeros(zbuf.shape, zbuf.dtype)
        for wait in (False, True):
            def body(e, c, wait=wait):
                _copy_pieces(tail_ref[1, e], zbuf, 0, xs_hbm, tail_ref[0, e], zsem, TAIL_BITS, wait)
                return c
            lax.fori_loop(0, n_exp, body, 0)
        blk_rows = zbuf.shape[0]

        def zero_block(b, c):
            cp = pltpu.make_async_copy(
                zbuf, xs_hbm.at[pl.ds(pl.multiple_of(b * blk_rows, blk_rows), blk_rows)], zsem)
            cp.start()
            cp.wait()
            return c
        lax.fori_loop(nused_ref[0], n_blocks, zero_block, 0)

    @pl.when(i >= 2)
    def _():
        wait_stage(slot)

    h = _load_rows(h_ref, tm).astype(BF16)
    row_id = lax.broadcasted_iota(jnp.int32, (n_rows, tm), 0)
    hit = row_id == pos_ref[0:1, :]
    for k in range(1, TOP_K):
        hit = hit | (row_id == pos_ref[k:k + 1, :])
    onehot = jnp.where(hit, 1.0, 0.0).astype(BF16)
    rows = jnp.dot(onehot, h, preferred_element_type=F32)
    dh = rows.shape[1] // 2
    _store_rows(stage.at[slot], _pack_bf16_pair(rows[:, 0:dh], rows[:, dh:]))

    def body(e, c):
        _copy_pieces(tab_ref[0, 0, e], stage.at[slot], tab_ref[0, 2, e], xs_hbm, tab_ref[0, 1, e],
                     sem.at[slot], RUN_BITS)
        return c
    lax.fori_loop(0, n_exp, body, 0)

    @pl.when(i == n - 1)
    def _():
        wait_stage(slot)
        wait_stage(1 - slot)


def _dispatch_call(tabs, tail, n_used, h2p, pos_t, *, n_xs_rows):
    nt, _, n_exp = tabs.shape
    tm = TM_FIN
    assert nt >= 2 and tm < (1 << RUN_BITS) and MOE_BLK <= (1 << TAIL_BITS)
    n_rows = TOP_K * tm
    kern = functools.partial(_dispatch_kernel, tm=tm, n_exp=n_exp, n_blocks=n_xs_rows // MOE_BLK)
    return pl.pallas_call(
        kern,
        grid=(nt,),
        in_specs=[
            pl.BlockSpec((1, 3, n_exp), lambda i: (i, 0, 0), memory_space=pltpu.SMEM),
            pl.BlockSpec(memory_space=pltpu.SMEM),
            pl.BlockSpec(memory_space=pltpu.SMEM),
            pl.BlockSpec((tm * ROW_SUB, 128), lambda i: (i, 0)),
            pl.BlockSpec((TOP_K, tm), lambda i: (0, i)),
        ],
        out_specs=pl.BlockSpec(memory_space=pl.ANY),
        out_shape=jax.ShapeDtypeStruct((n_xs_rows * ROW_SUB, 128), jnp.uint32),
        scratch_shapes=[pltpu.VMEM((2, n_rows * ROW_SUB, 128), jnp.uint32),
                        pltpu.VMEM((MOE_BLK * ROW_SUB, 128), jnp.uint32),
                        pltpu.SemaphoreType.DMA((2,)),
                        pltpu.SemaphoreType.DMA(())],
        compiler_params=pltpu.CompilerParams(dimension_semantics=("arbitrary",),
                                             vmem_limit_bytes=VMEM_LIMIT),
        name="moe_dispatch",
    )(tabs, tail, n_used, h2p, pos_t)


def _moe_kernel(blk_e_ref, nused_ref, x_ref, perm_ref, wgu_ref, bgu_ref, wd_ref, bd_ref, o_ref,
                wgu_scr, wd_scr, *, blk, f):
    s = pl.program_id(0)
    e_cur = blk_e_ref[s]
    e_prev = blk_e_ref[jnp.maximum(s - 1, 0)]

    @pl.when(jnp.logical_or(s == 0, e_cur != e_prev))
    def _():
        perm = perm_ref[...]
        for c in range(2 * f // 256):
            r = jnp.dot(wgu_ref[0, :, c * 256:(c + 1) * 256].astype(BF16), perm,
                        preferred_element_type=F32)
            wgu_scr[:, c * 128:(c + 1) * 128] = r[:, 0:128].astype(BF16)
            wgu_scr[:, f + c * 128:f + (c + 1) * 128] = r[:, 128:256].astype(BF16)
        wd_scr[...] = wd_ref[0].astype(BF16)

    @pl.when(s >= nused_ref[0])
    def _():
        o_ref[...] = jnp.zeros(o_ref.shape, o_ref.dtype)

    @pl.when(s < nused_ref[0])
    def _():
        x = _load_rows(x_ref, blk).astype(BF16)
        gu = jnp.dot(x, wgu_scr[...], preferred_element_type=F32) + bgu_ref[0]
        gate = jnp.minimum(gu[:, 0:f], SWIGLU_LIMIT)
        up = jnp.clip(gu[:, f:2 * f], -SWIGLU_LIMIT, SWIGLU_LIMIT)
        act = (up + 1.0) * (gate * jax.nn.sigmoid(SWIGLU_ALPHA * gate))
        y = jnp.dot(act.astype(BF16), wd_scr[...], preferred_element_type=F32) + bd_ref[0]
        dh = y.shape[1] // 2
        _store_rows(o_ref, _pack_bf16_pair(y[:, 0:dh], y[:, dh:]))


def _moe_call(blk_e, n_used, xs, perm, w_gu, b_gu_p, w_down, b_down):
    n_steps = blk_e.shape[0]
    blk = MOE_BLK
    _, d, f2 = w_gu.shape
    f = f2 // 2
    rows = blk * ROW_SUB
    kern = functools.partial(_moe_kernel, blk=blk, f=f)
    row_blk = lambda i, be, nu: (i, 0)
    grid_spec = pltpu.PrefetchScalarGridSpec(
        num_scalar_prefetch=2,
        grid=(n_steps,),
        in_specs=[
            pl.BlockSpec((rows, 128), row_blk),
            pl.BlockSpec((256, 256), lambda i, be, nu: (0, 0)),
            pl.BlockSpec((1, d, f2), lambda i, be, nu: (be[i], 0, 0)),
            pl.BlockSpec((1, 1, f2), lambda i, be, nu: (be[i], 0, 0)),
            pl.BlockSpec((1, f, d), lambda i, be, nu: (be[i], 0, 0)),
            pl.BlockSpec((1, 1, d), lambda i, be, nu: (be[i], 0, 0)),
        ],
        out_specs=pl.BlockSpec((rows, 128), row_blk),
        scratch_shapes=[
            pltpu.VMEM((d, f2), BF16),
            pltpu.VMEM((f, d), BF16),
        ],
    )
    return pl.pallas_call(
        kern,
        grid_spec=grid_spec,
        out_shape=jax.ShapeDtypeStruct((n_steps * rows, 128), jnp.uint32),
        compiler_params=pltpu.CompilerParams(dimension_semantics=("arbitrary",),
                                             vmem_limit_bytes=VMEM_LIMIT),
        name="moe_experts",
    )(blk_e, n_used, xs, perm, w_gu, b_gu_p, w_down, b_down)


def _final_kernel(tab_cur_ref, tab_nxt_ref, x1_ref, pos_ref, w_ref, mod_ref, fmod_ref, gfin_ref, ys_hbm,
                  o_ref, stage, sem, *, tm, n_exp):
    i = pl.program_id(0)
    n = pl.num_programs(0)
    slot = i % 2
    n_rows = TOP_K * tm

    def issue_tile(tab_ref, slot_):
        def body(e, c):
            _copy_pieces(tab_ref[0, 0, e], ys_hbm, tab_ref[0, 1, e], stage.at[slot_], tab_ref[0, 2, e],
                         sem.at[slot_], RUN_BITS)
            return c
        lax.fori_loop(0, n_exp, body, 0)

    @pl.when(i == 0)
    def _():
        issue_tile(tab_cur_ref, 0)

    @pl.when(i + 1 < n)
    def _():
        issue_tile(tab_nxt_ref, 1 - slot)

    pltpu.make_async_copy(ys_hbm.at[pl.ds(0, n_rows * ROW_SUB)], stage.at[slot], sem.at[slot]).wait()

    y = _load_rows(stage.at[slot], n_rows).astype(BF16)
    col_id = lax.broadcasted_iota(jnp.int32, (tm, n_rows), 1)
    pos = pos_ref[...]
    w = w_ref[...]
    a = jnp.where(col_id == pos[:, 0:1], w[:, 0:1], 0.0)
    for k in range(1, TOP_K):
        a = a + jnp.where(col_id == pos[:, k:k + 1], w[:, k:k + 1], 0.0)
    moe = jnp.dot(a.astype(BF16), y, preferred_element_type=F32)
    gate2 = mod_ref[0, 5:6, :]
    x2 = x1_ref[...] + gate2 * moe
    fshift = fmod_ref[0, 0:1, :]
    fscale = fmod_ref[0, 1:2, :]
    o_ref[...] = _rms(x2, gfin_ref[...]) * (1.0 + fscale) + fshift


def _final_call(tabs, x1, pos_col, w_col, mod, fmod, g_final, ys, *, seq):
    t, d = x1.shape
    tm = TM_FIN
    assert tm < (1 << RUN_BITS)
    tps = seq // tm
    nt = t // tm
    n_exp = tabs.shape[2]
    n_rows = TOP_K * tm
    kern = functools.partial(_final_kernel, tm=tm, n_exp=n_exp)
    tab_spec = lambda ahead: pl.BlockSpec((1, 3, n_exp), lambda i: (jnp.minimum(i + ahead, nt - 1), 0, 0),
                                          memory_space=pltpu.SMEM)
    return pl.pallas_call(
        kern,
        grid=(nt,),
        in_specs=[
            tab_spec(0),
            tab_spec(1),
            pl.BlockSpec((tm, d), lambda i: (i, 0)),
            pl.BlockSpec((tm, TOP_K), lambda i: (i, 0)),
            pl.BlockSpec((tm, TOP_K), lambda i: (i, 0)),
            pl.BlockSpec((1, 6, d), lambda i: (i // tps, 0, 0)),
            pl.BlockSpec((1, 2, d), lambda i: (i // tps, 0, 0)),
            _const_spec((1, d)),
            pl.BlockSpec(memory_space=pl.ANY),
        ],
        out_specs=pl.BlockSpec((tm, d), lambda i: (i, 0)),
        out_shape=jax.ShapeDtypeStruct((t, d), F32),
        scratch_shapes=[pltpu.VMEM((2, n_rows * ROW_SUB, 128), jnp.uint32),
                        pltpu.SemaphoreType.DMA((2,))],
        compiler_params=pltpu.CompilerParams(dimension_semantics=("arbitrary",),
                                             vmem_limit_bytes=VMEM_LIMIT),
        name="combine_final",
    )(tabs, tabs, x1, pos_col, w_col, mod, fmod, g_final, ys)


def _prep_w_in(w_in, d, q_lora, kv_lora):
    o_kpe = d + q_lora + kv_lora
    o_g = o_kpe + QK_ROPE
    half = QK_ROPE // 2
    kpe = w_in[:, o_kpe:o_kpe + QK_ROPE]
    zpad = jnp.zeros((d, 128 - QK_ROPE), w_in.dtype)
    ksw = jnp.concatenate([-kpe[:, half:], kpe[:, :half]], axis=1)
    return jnp.concatenate([w_in[:, :o_kpe], kpe, zpad, ksw, zpad, w_in[:, o_g:]], axis=1).astype(BF16)


def _prep_w_q(w_q_b):
    ql = w_q_b.shape[0]
    hd = QK_NOPE + QK_ROPE
    half = QK_ROPE // 2
    w = w_q_b.reshape(ql, N_HEADS, hd)
    nope = w[:, :, :QK_NOPE]
    pe = w[:, :, QK_NOPE:]
    sw = jnp.concatenate([-pe[:, :, half:], pe[:, :, :half]], axis=2)
    return jnp.concatenate([nope.reshape(ql, -1), pe.reshape(ql, -1), sw.reshape(ql, -1)],
                           axis=1).astype(BF16)


def _prep_w_kv(w_kv_b):
    kvl = w_kv_b.shape[0]
    w = w_kv_b.reshape(kvl, N_HEADS, QK_NOPE + V_HEAD)
    w_kn = w[:, :, :QK_NOPE].reshape(kvl, -1).astype(BF16)
    w_vt = w[:, :, QK_NOPE:].reshape(kvl, -1).T.astype(BF16)
    return w_kn, w_vt


def kernel(x, c, positions, w_mod, b_mod, g_mix, w_in, b_gate, w_pool_grp, pool_scale, w_pool_out,
           g_q_a, w_q_b, g_kv_a, w_kv_b, w_mla_out, w_out, g_ffn, w_router, b_router, w_gu, b_gu,
           w_down, b_down, g_final, w_fmod, b_fmod):
    bsz, seq, d = x.shape
    t = bsz * seq
    depth = w_mod.shape[0]
    assert depth == 1
    assert seq % TQ == 0 and seq % TM_IN == 0 and seq % TM_POST == 0 and seq % TM_FIN == 0
    q_lora = g_q_a.shape[-1]
    kv_lora = g_kv_a.shape[-1]
    n_exp = w_gu.shape[1]
    f = w_gu.shape[-1] // 2
    blk = MOE_BLK

    x2d = x.reshape(t, d)
    pos_col = positions.astype(F32).reshape(t, 1)
    inv_freq = 1.0 / (ROPE_THETA ** (jnp.arange(0, QK_ROPE, 2, dtype=F32) / QK_ROPE))
    invf2 = jnp.tile(inv_freq, 128 // (QK_ROPE // 2)).reshape(1, 128)

    mod = _mod_call(c, w_mod[0], b_mod[0]).reshape(bsz, 6, d)
    fmod = _mod_call(c, w_fmod, b_fmod).reshape(bsz, 2, d)

    w_in_p = _prep_w_in(w_in[0], d, q_lora, kv_lora)
    w_q_p = _prep_w_q(w_q_b[0])
    w_kn, w_vt = _prep_w_kv(w_kv_b[0])
    ga, g1, q2, k2, vt3 = _mixer_in_call(
        x2d, pos_col, mod, g_mix[0].reshape(1, d), w_in_p, b_gate[0].reshape(1, 2 * d),
        w_pool_grp[0].astype(BF16), pool_scale[0].reshape(1, d), w_pool_out[0].astype(BF16),
        g_q_a[0].reshape(1, q_lora), w_q_p, g_kv_a[0].reshape(1, kv_lora), w_kn, w_vt, invf2,
        bsz=bsz, seq=seq)

    hp = N_HEADS * HEAD_PAD
    o = _attn_call(q2.reshape(bsz, seq, hp), k2.reshape(bsz, seq, hp), vt3)

    w_r = w_router[0]
    w_r_hi = w_r.astype(BF16)
    w_r_lo = (w_r - w_r_hi.astype(F32)).astype(BF16)
    lane_pad = lambda a: jnp.pad(a, ((0, 0), (0, 128 - n_exp)))
    w_rt = jnp.concatenate(
        [jnp.concatenate([lane_pad(w_r_hi), lane_pad(w_r_lo)], axis=1),
         jnp.concatenate([lane_pad(w_r_hi), jnp.zeros((d, 128), BF16)], axis=1)], axis=0)
    x1, h2p, pos_t, wgt_t, cnt3 = _post_call(
        o.reshape(t, d), ga, g1, x2d, mod, w_mla_out[0].astype(BF16), w_out[0].astype(BF16),
        g_ffn[0].reshape(1, d), w_rt, b_router[0].reshape(n_exp, 1), seq=seq)

    n_slots = t * TOP_K
    n_rows = n_slots + n_exp * blk
    n_blocks = n_rows // blk
    cnt = cnt3[:, 0:TM_POST // TM_FIN, 0:n_exp].reshape(-1, n_exp)
    counts = jnp.sum(cnt, axis=0)
    padded = (counts + blk - 1) // blk * blk
    pad_end = jnp.cumsum(padded)
    pad_start = pad_end - padded
    blk_start = jnp.arange(n_blocks, dtype=jnp.int32) * blk
    blk_e = jnp.minimum(jnp.sum(pad_end[None, :] <= blk_start[:, None], axis=1, dtype=jnp.int32),
                        n_exp - 1)
    n_used = (pad_end[-1:] // blk).astype(jnp.int32)
    tail = jnp.stack([(pad_start + counts) * ROW_SUB, padded - counts], axis=0).astype(jnp.int32)

    before = jnp.cumsum(cnt, axis=0) - cnt
    run_start = jnp.cumsum(cnt, axis=1) - cnt
    tabs = jnp.stack([cnt, (pad_start[None, :] + before) * ROW_SUB, run_start * ROW_SUB], axis=1)

    col = jnp.arange(256, dtype=jnp.int32)[None, :]
    row = jnp.arange(256, dtype=jnp.int32)[:, None]
    perm = (row == jnp.where(col < 128, 2 * col, 2 * (col - 128) + 1)).astype(BF16)
    b_gu_p = jnp.concatenate([b_gu[0][:, 0::2], b_gu[0][:, 1::2]], axis=-1).reshape(n_exp, 1, 2 * f)
    xs = _dispatch_call(tabs, tail, n_used, h2p, pos_t, n_xs_rows=n_rows)
    ys = _moe_call(blk_e, n_used, xs, perm, w_gu[0], b_gu_p, w_down[0], b_down[0].reshape(n_exp, 1, d))

    out = _final_call(tabs, x1, pos_t.T, wgt_t.T, mod, fmod, g_final.reshape(1, d), ys, seq=seq)
    return out.reshape(bsz, seq, d)
```

```python
import functools

import jax
import jax.numpy as jnp
from jax import lax
from jax.experimental import pallas as pl
from jax.experimental.pallas import tpu as pltpu

F32 = jnp.float32
BF16 = jnp.bfloat16

CHUNK = 64
POOL_WINDOWS = (2, 4, 8, 16)
POOL_HALO = 16
N_HEADS = 8
QK_NOPE = 128
QK_ROPE = 64
V_HEAD = 128
HEAD_PAD = 256
ROPE_THETA = 10000.0
TOP_K = 4
SWIGLU_LIMIT = 7.0
SWIGLU_ALPHA = 1.702
NORM_EPS = 1e-6
NEG_INF = -1e30
LOG2_E = 1.4426950408889634

VMEM_LIMIT = 56 * 1024 * 1024

TM_IN = 256
TQ = 512
TM_POST = 512
MOE_BLK = 512
TM_FIN = 256
TILES_PER_STEP = 2
ROW_SUB = 4


def _const_spec(shape):
    nd = len(shape)
    return pl.BlockSpec(shape, lambda *_: (0,) * nd, pipeline_mode=pl.Buffered(1))


def _rms(xf, g):
    return xf * lax.rsqrt(jnp.mean(xf * xf, axis=-1, keepdims=True) + NORM_EPS) * g


def _pack_bf16_pair(lo, hi):
    lo_b = lax.bitcast_convert_type(lo.astype(BF16).astype(F32), jnp.uint32)
    hi_b = lax.bitcast_convert_type(hi.astype(BF16).astype(F32), jnp.uint32)
    return (hi_b & jnp.uint32(0xFFFF0000)) | (lo_b >> 16)


def _unpack_bf16_pair(p):
    lo = lax.bitcast_convert_type(p << 16, F32)
    hi = lax.bitcast_convert_type(p & jnp.uint32(0xFFFF0000), F32)
    return lo, hi


def _store_rows(ref, packed):
    n = packed.shape[0]
    for q in range(ROW_SUB):
        ref[pl.ds(q, n, stride=ROW_SUB), :] = packed[:, q * 128:(q + 1) * 128]


def _load_rows(ref, n):
    los, his = [], []
    for q in range(ROW_SUB):
        lo, hi = _unpack_bf16_pair(ref[pl.ds(q, n, stride=ROW_SUB), :])
        los.append(lo)
        his.append(hi)
    return jnp.concatenate(los + his, axis=1)


def _mod_kernel(c_ref, w_ref, b_ref, o_ref):
    c = c_ref[...]
    c_act = c * jax.nn.sigmoid(c)
    o_ref[...] = jnp.dot(c_act, w_ref[...], preferred_element_type=F32,
                         precision=lax.Precision.HIGHEST) + b_ref[...]


def _mod_call(c, w, b, tn=1024):
    bsz, d = c.shape
    n = w.shape[1]
    return pl.pallas_call(
        _mod_kernel,
        grid=(n // tn,),
        in_specs=[pl.BlockSpec((bsz, d), lambda j: (0, 0)),
                  pl.BlockSpec((d, tn), lambda j: (0, j)),
                  pl.BlockSpec((1, tn), lambda j: (0, j))],
        out_specs=pl.BlockSpec((bsz, tn), lambda j: (0, j)),
        out_shape=jax.ShapeDtypeStruct((bsz, n), F32),
        compiler_params=pltpu.CompilerParams(dimension_semantics=("arbitrary",),
                                             vmem_limit_bytes=VMEM_LIMIT),
        name="adaln_mod",
    )(c, w, b.reshape(1, n))


def _mixer_in_kernel(x_ref, xh_ref, pos_ref, mod_ref, gmix_ref, win_ref, bgate_ref, wgrp_ref,
                     pscale_ref, wpo_ref, gq_ref, wq_ref, gkv_ref, wkn_ref, wvt_ref, invf_ref,
                     ga_ref, g1_ref, q_ref, k_ref, vt_ref, u_scr, *, tm, tiles_per_seq, d, q_lora,
                     kv_lora):
    i = pl.program_id(0)
    t_in_seq = i % tiles_per_seq
    is_start = t_in_seq == 0
    shift1 = mod_ref[0, 0:1, :]
    scale1 = mod_ref[0, 1:2, :]
    gmix = gmix_ref[...]

    def prenorm(xf):
        return (_rms(xf, gmix) * (1.0 + scale1) + shift1).astype(BF16)

    h = prenorm(x_ref[...])
    hh = prenorm(xh_ref[...])
    u_ext = jnp.dot(jnp.concatenate([hh, h], axis=0), win_ref[:, 0:d],
                    preferred_element_type=F32)
    u_scr[0:POOL_HALO, :] = jnp.where(is_start, 0.0, u_ext[0:POOL_HALO, :])
    u_scr[POOL_HALO:POOL_HALO + tm, :] = u_ext[POOL_HALO:, :]
    rest = jnp.dot(h, win_ref[:, d:], preferred_element_type=F32)

    gw = d // len(POOL_WINDOWS)
    tseq = t_in_seq * tm + lax.broadcasted_iota(jnp.int32, (tm, 1), 0)
    ys = []
    for g, w in enumerate(POOL_WINDOWS):
        c0 = g * gw
        ug = u_scr[POOL_HALO:POOL_HALO + tm, c0:c0 + gw]
        acc = ug
        for j in range(1, w):
            acc = acc + u_scr[POOL_HALO - j:POOL_HALO - j + tm, c0:c0 + gw]
        cnt = jnp.minimum(tseq + 1, w).astype(F32)
        mixed = (acc / cnt - ug).astype(BF16)
        ys.append(jnp.dot(mixed, wgrp_ref[g], preferred_element_type=F32))
    y = (jnp.concatenate(ys, axis=1) * pscale_ref[...]).astype(BF16)
    a = jnp.dot(y, wpo_ref[...], preferred_element_type=F32)

    o_q = 0
    o_kv = q_lora
    o_kpe = q_lora + kv_lora
    o_ksw = o_kpe + 128
    o_g0 = o_ksw + 128
    o_g1 = o_g0 + d
    gates0 = jax.nn.sigmoid(rest[:, o_g0:o_g0 + d] + bgate_ref[:, 0:d])
    gates1 = jax.nn.sigmoid(rest[:, o_g1:o_g1 + d] + bgate_ref[:, d:2 * d])
    ga_ref[...] = (gates0 * a).astype(BF16)
    g1_ref[...] = gates1.astype(BF16)

    ang = pos_ref[...] * invf_ref[...]
    cos2 = jnp.cos(ang)
    sin2 = jnp.sin(ang)
    q_scale = float(QK_NOPE + QK_ROPE) ** -0.5 * LOG2_E

    qn = _rms(rest[:, o_q:o_q + q_lora], gq_ref[...]).astype(BF16)
    qall = jnp.dot(qn, wq_ref[...], preferred_element_type=F32)
    kvn = _rms(rest[:, o_kv:o_kv + kv_lora], gkv_ref[...]).astype(BF16)
    kn = jnp.dot(kvn, wkn_ref[...], preferred_element_type=F32)
    vt = lax.dot_general(wvt_ref[...], kvn, (((1,), (1,)), ((), ())),
                         preferred_element_type=F32)
    vt_ref[0] = vt.astype(BF16)
    kpe = (rest[:, o_kpe:o_kpe + 128] * cos2 + rest[:, o_ksw:o_ksw + 128] * sin2).astype(BF16)
    o_pe = N_HEADS * QK_NOPE
    o_sw = o_pe + N_HEADS * QK_ROPE
    low_half = lax.broadcasted_iota(jnp.int32, (tm, 128), 1) < QK_ROPE
    for pair in range(N_HEADS // 2):
        c0 = pair * 128
        qpe2 = (qall[:, o_pe + c0:o_pe + c0 + 128] * cos2
                + qall[:, o_sw + c0:o_sw + c0 + 128] * sin2) * q_scale
        for hd, part in ((2 * pair, qpe2), (2 * pair + 1, pltpu.roll(qpe2, QK_ROPE, axis=1))):
            n0 = hd * QK_NOPE
            q_ref[:, hd * HEAD_PAD:hd * HEAD_PAD + 128] = (qall[:, n0:n0 + 128] * q_scale).astype(BF16)
            q_ref[:, hd * HEAD_PAD + 128:(hd + 1) * HEAD_PAD] = jnp.where(low_half, part, 0.0).astype(BF16)
            k_ref[:, hd * HEAD_PAD:hd * HEAD_PAD + 128] = kn[:, n0:n0 + 128].astype(BF16)
            k_ref[:, hd * HEAD_PAD + 128:(hd + 1) * HEAD_PAD] = kpe


def _mixer_in_call(x2d, pos_col, mod, g_mix, w_in_p, b_gate, w_grp, pool_scale, w_po, g_q_a, w_q_p,
                   g_kv_a, w_kn, w_vt, invf2, *, bsz, seq):
    t, d = x2d.shape
    tm = TM_IN
    tps = seq // tm
    q_lora = g_q_a.shape[-1]
    kv_lora = g_kv_a.shape[-1]
    hp = N_HEADS * HEAD_PAD
    halo_blocks = tm // POOL_HALO
    kern = functools.partial(_mixer_in_kernel, tm=tm, tiles_per_seq=tps, d=d, q_lora=q_lora,
                             kv_lora=kv_lora)
    row = lambda i: (i, 0)
    return pl.pallas_call(
        kern,
        grid=(t // tm,),
        in_specs=[
            pl.BlockSpec((tm, d), row),
            pl.BlockSpec((POOL_HALO, d), lambda i: (jnp.maximum(i * halo_blocks - 1, 0), 0)),
            pl.BlockSpec((tm, 1), row),
            pl.BlockSpec((1, 6, d), lambda i: (i // tps, 0, 0)),
            _const_spec((1, d)),
            _const_spec(w_in_p.shape),
            _const_spec((1, 2 * d)),
            _const_spec(w_grp.shape),
            _const_spec((1, d)),
            _const_spec(w_po.shape),
            _const_spec((1, q_lora)),
            _const_spec(w_q_p.shape),
            _const_spec((1, kv_lora)),
            _const_spec(w_kn.shape),
            _const_spec(w_vt.shape),
            _const_spec((1, 128)),
        ],
        out_specs=[
            pl.BlockSpec((tm, d), row),
            pl.BlockSpec((tm, d), row),
            pl.BlockSpec((tm, hp), row),
            pl.BlockSpec((tm, hp), row),
            pl.BlockSpec((1, N_HEADS * V_HEAD, tm), lambda i: (i // tps, 0, i % tps)),
        ],
        out_shape=[
            jax.ShapeDtypeStruct((t, d), BF16),
            jax.ShapeDtypeStruct((t, d), BF16),
            jax.ShapeDtypeStruct((t, hp), BF16),
            jax.ShapeDtypeStruct((t, hp), BF16),
            jax.ShapeDtypeStruct((bsz, N_HEADS * V_HEAD, seq), BF16),
        ],
        scratch_shapes=[pltpu.VMEM((tm + POOL_HALO, d), F32)],
        compiler_params=pltpu.CompilerParams(dimension_semantics=("arbitrary",),
                                             vmem_limit_bytes=VMEM_LIMIT),
        name="mixer_in",
    )(x2d, x2d, pos_col, mod, g_mix, w_in_p, b_gate, w_grp, pool_scale, w_po, g_q_a, w_q_p, g_kv_a,
      w_kn, w_vt, invf2)


def _attn_kernel(q_ref, k_ref, vt_ref, o_ref, *, tq, nq):
    seq = vt_ref.shape[2]
    vt_ext = jnp.concatenate([vt_ref[0], jnp.ones((16, seq), BF16)], axis=0)

    def scores(qi):
        q0 = qi * tq
        ln = q0 + tq
        return lax.dot_general(k_ref[0, 0:ln, :], q_ref[0, q0:q0 + tq, :], (((1,), (1,)), ((), ())),
                               preferred_element_type=F32)

    def finish(qi, s):
        q0 = qi * tq
        ln = q0 + tq
        cw = tq // 2
        for c in range(2):
            cs = slice(c * cw, (c + 1) * cw)
            kc = (q0 + lax.broadcasted_iota(jnp.int32, (tq, cw), 0)) // CHUNK
            qc = (q0 + c * cw + lax.broadcasted_iota(jnp.int32, (tq, cw), 1)) // CHUNK
            s_diag = jnp.where(qc >= kc, s[q0:ln, cs], NEG_INF)
            m = jnp.max(s_diag, axis=0, keepdims=True)
            if qi > 0:
                s_main = s[0:q0, cs]
                m = jnp.maximum(m, jnp.max(s_main, axis=0, keepdims=True))
            p_diag = jnp.exp2(s_diag - m).astype(BF16)
            acc = jnp.dot(vt_ext[:, q0:ln], p_diag, preferred_element_type=F32)
            if qi > 0:
                p_main = jnp.exp2(s_main - m).astype(BF16)
                acc = acc + jnp.dot(vt_ext[:, 0:q0], p_main, preferred_element_type=F32)
            l = acc[V_HEAD:V_HEAD + 1, :]
            o_ref[0, q0 + c * cw:q0 + (c + 1) * cw, :] = (acc[0:V_HEAD, :] / l).T.astype(BF16)

    order = list(range(nq))[::-1]
    s_cur = scores(order[0])
    for idx, qi in enumerate(order):
        s_next = scores(order[idx + 1]) if idx + 1 < nq else None
        finish(qi, s_cur)
        s_cur = s_next


def _attn_call(q3, k3, vt3):
    bsz, seq, _ = q3.shape
    kern = functools.partial(_attn_kernel, tq=TQ, nq=seq // TQ)
    return pl.pallas_call(
        kern,
        grid=(bsz, N_HEADS),
        in_specs=[pl.BlockSpec((1, seq, HEAD_PAD), lambda b, h: (b, 0, h)),
                  pl.BlockSpec((1, seq, HEAD_PAD), lambda b, h: (b, 0, h)),
                  pl.BlockSpec((1, V_HEAD, seq), lambda b, h: (b, h, 0))],
        out_specs=pl.BlockSpec((1, seq, V_HEAD), lambda b, h: (b, 0, h)),
        out_shape=jax.ShapeDtypeStruct((bsz, seq, N_HEADS * V_HEAD), BF16),
        compiler_params=pltpu.CompilerParams(dimension_semantics=("arbitrary", "arbitrary"),
                                             vmem_limit_bytes=VMEM_LIMIT),
        name="mla_attn",
    )(q3, k3, vt3)


def _post_kernel(o_ref, ga_ref, g1_ref, x_ref, mod_ref, wmo_ref, wout_ref, gffn_ref, wrt_ref, br_ref,
                 x1_ref, h2_ref, pos_ref, wgt_ref, cnt_ref, *, d):
    m = jnp.dot(o_ref[...], wmo_ref[...], preferred_element_type=F32)
    merged = ga_ref[...].astype(F32) + g1_ref[...].astype(F32) * m
    gate1 = mod_ref[0, 2:3, :]
    x1 = x_ref[...] + gate1 * jnp.dot(merged.astype(BF16), wout_ref[...],
                                      preferred_element_type=F32)
    x1_ref[...] = x1
    shift2 = mod_ref[0, 3:4, :]
    scale2 = mod_ref[0, 4:5, :]
    h2 = _rms(x1, gffn_ref[...]) * (1.0 + scale2) + shift2
    _store_rows(h2_ref, _pack_bf16_pair(h2[:, 0:d // 2], h2[:, d // 2:d]))

    ne = br_ref.shape[0]
    h_hi = h2.astype(BF16)
    h_lo = (h2 - h_hi.astype(F32)).astype(BF16)
    hcat = jnp.concatenate([h_hi, h_lo], axis=1)
    half = hcat.shape[0] // 2
    lg2 = jnp.concatenate(
        [jnp.dot(hcat[0:half], wrt_ref[...], preferred_element_type=F32),
         jnp.dot(hcat[half:], wrt_ref[...], preferred_element_type=F32)], axis=0)
    lg = lg2[:, 0:128] + lg2[:, 128:256]
    logits = lg.T[0:ne, :] + br_ref[...]
    tm = logits.shape[1]
    eid = lax.broadcasted_iota(jnp.int32, (ne, tm), 0)
    vals, idxs = [], []
    cur = logits
    for _ in range(TOP_K):
        mx = jnp.max(cur, axis=0, keepdims=True)
        ix = jnp.min(jnp.where(cur == mx, eid, ne), axis=0, keepdims=True)
        vals.append(mx)
        idxs.append(ix)
        cur = jnp.where(eid == ix, -jnp.inf, cur)
    es = [jnp.exp(v - vals[0]) for v in vals]
    den = es[0] + es[1] + es[2] + es[3]
    wgt_ref[...] = jnp.concatenate([e / den for e in es], axis=0)

    tt = TM_FIN
    t_row = lax.broadcasted_iota(jnp.int32, (tm, tm), 0)
    t_col = lax.broadcasted_iota(jnp.int32, (tm, tm), 1)
    same_tile = (t_row // tt) == (t_col // tt)
    in_tile = jnp.where(same_tile, 1.0, 0.0).astype(BF16)
    earlier = jnp.where(same_tile & (t_row < t_col), 1.0, 0.0).astype(BF16)
    lower = jnp.where(lax.broadcasted_iota(jnp.int32, (ne, ne), 1)
                      < lax.broadcasted_iota(jnp.int32, (ne, ne), 0), 1.0, 0.0).astype(BF16)
    picks = [eid == ix for ix in idxs]
    routed = jnp.where(picks[0] | picks[1] | picks[2] | picks[3], 1.0, 0.0).astype(BF16)
    before = jnp.dot(routed, earlier, preferred_element_type=F32)
    cnt_b = jnp.dot(routed, in_tile, preferred_element_type=F32)
    run_start = jnp.dot(lower, cnt_b.astype(BF16), preferred_element_type=F32)
    base = run_start + before
    pos_ref[...] = jnp.concatenate(
        [jnp.sum(jnp.where(pk, base, 0.0), axis=0, keepdims=True) for pk in picks],
        axis=0).astype(jnp.int32)
    tile_sel = jnp.where(lax.broadcasted_iota(jnp.int32, (8, tm), 0)
                         == lax.broadcasted_iota(jnp.int32, (8, tm), 1) // tt, 1.0, 0.0).astype(BF16)
    cnt8 = lax.dot_general(tile_sel, routed, (((1,), (1,)), ((), ())),
                           preferred_element_type=F32)
    cnt_ref[0] = jnp.concatenate([cnt8, jnp.zeros((8, 128 - ne), F32)], axis=1).astype(jnp.int32)


def _post_call(o2d, ga, g1, x2d, mod, w_mo, w_out, g_ffn, w_rt, b_r, *, seq):
    t, d = x2d.shape
    tm = TM_POST
    tps = seq // tm
    row = lambda i: (i, 0)
    n_exp = b_r.shape[0]
    assert tm % TM_FIN == 0 and tm // TM_FIN <= 8 and n_exp <= 128
    kern = functools.partial(_post_kernel, d=d)
    return pl.pallas_call(
        kern,
        grid=(t // tm,),
        in_specs=[
            pl.BlockSpec((tm, d), row),
            pl.BlockSpec((tm, d), row),
            pl.BlockSpec((tm, d), row),
            pl.BlockSpec((tm, d), row),
            pl.BlockSpec((1, 6, d), lambda i: (i // tps, 0, 0)),
            _const_spec(w_mo.shape),
            _const_spec(w_out.shape),
            _const_spec((1, d)),
            _const_spec(w_rt.shape),
            _const_spec(b_r.shape),
        ],
        out_specs=[
            pl.BlockSpec((tm, d), row),
            pl.BlockSpec((tm * ROW_SUB, 128), row),
            pl.BlockSpec((TOP_K, tm), lambda i: (0, i)),
            pl.BlockSpec((TOP_K, tm), lambda i: (0, i)),
            pl.BlockSpec((1, 8, 128), lambda i: (i, 0, 0)),
        ],
        out_shape=[
            jax.ShapeDtypeStruct((t, d), F32),
            jax.ShapeDtypeStruct((t * ROW_SUB, 128), jnp.uint32),
            jax.ShapeDtypeStruct((TOP_K, t), jnp.int32),
            jax.ShapeDtypeStruct((TOP_K, t), F32),
            jax.ShapeDtypeStruct((t // tm, 8, 128), jnp.int32),
        ],
        compiler_params=pltpu.CompilerParams(dimension_semantics=("arbitrary",),
                                             vmem_limit_bytes=VMEM_LIMIT),
        name="post_attn_router",
    )(o2d, ga, g1, x2d, mod, w_mo, w_out, g_ffn, w_rt, b_r)


RUN_BITS = 9
COMMON_RUN_BITS = 6
TAIL_BITS = 9


def _copy_pieces(count, src_ref, src_off, dst_ref, dst_off, sem, bits, wait=False):
    def pieces(bit_range, off):
        for b in reversed(bit_range):
            size = (1 << b) * ROW_SUB
            take = (count >> b) & 1

            @pl.when(take == 1)
            def _(off=off, size=size):
                cp = pltpu.make_async_copy(
                    src_ref.at[pl.ds(pl.multiple_of(src_off + off, ROW_SUB), size)],
                    dst_ref.at[pl.ds(pl.multiple_of(dst_off + off, ROW_SUB), size)], sem)
                if wait:
                    cp.wait()
                else:
                    cp.start()
            off = off + take * size

    lo = min(bits, COMMON_RUN_BITS)
    big = (count >> lo) << lo

    if lo < bits:
        @pl.when(big != 0)
        def _():
            pieces(range(lo, bits), 0)
    pieces(range(lo), big * ROW_SUB)


def _dispatch_kernel(tab_ref, tail_ref, nused_ref, h_ref, pos_ref, xs_hbm, stage, zbuf, sem, zsem, *,
                     tm, n_exp, n_blocks):
    i = pl.program_id(0)
    n = pl.num_programs(0)
    slot = i % 2
    n_rows = TOP_K * tm

    def wait_stage(slot_):
        for j in range(TILES_PER_STEP):
            pltpu.make_async_copy(stage.at[slot_, j], xs_hbm.at[pl.ds(0, n_rows * ROW_SUB)],
                                  sem.at[slot_]).wait()

    @pl.when(i == 0)
    def _():
        zbuf[...] = jnp.zeros(zbuf.shape, zbuf.dtype)
        for wait in (False, True):
            def body(e, c, wait=wait):
                _copy_pieces(tail_ref[1, e], zbuf, 0, xs_hbm, tail_ref[0, e], zsem, TAIL_BITS, wait)
                return c
            lax.fori_loop(0, n_exp, body, 0)
        blk_rows = zbuf.shape[0]

        def zero_block(b, c):
            cp = pltpu.make_async_copy(
                zbuf, xs_hbm.at[pl.ds(pl.multiple_of(b * blk_rows, blk_rows), blk_rows)], zsem)
            cp.start()
            cp.wait()
            return c
        lax.fori_loop(nused_ref[0], n_blocks, zero_block, 0)

    @pl.when(i >= 2)
    def _():
        wait_stage(slot)

    tiles = range(TILES_PER_STEP)
    hs = [_load_rows(h_ref.at[pl.ds(j * tm * ROW_SUB, tm * ROW_SUB)], tm).astype(BF16)
          for j in tiles]
    row_id = lax.broadcasted_iota(jnp.int32, (n_rows, tm), 0)
    onehots = []
    for j in tiles:
        hit = row_id == pos_ref[0:1, j * tm:(j + 1) * tm]
        for k in range(1, TOP_K):
            hit = hit | (row_id == pos_ref[k:k + 1, j * tm:(j + 1) * tm])
        onehots.append(jnp.where(hit, 1.0, 0.0).astype(BF16))
    rows = [jnp.dot(onehots[j], hs[j], preferred_element_type=F32) for j in tiles]
    for j in tiles:
        dh = rows[j].shape[1] // 2
        _store_rows(stage.at[slot, j], _pack_bf16_pair(rows[j][:, 0:dh], rows[j][:, dh:]))

    for j in tiles:
        def body(e, c, j=j):
            _copy_pieces(tab_ref[j, 0, e], stage.at[slot, j], tab_ref[j, 2, e], xs_hbm,
                         tab_ref[j, 1, e], sem.at[slot], RUN_BITS)
            return c
        lax.fori_loop(0, n_exp, body, 0)

    @pl.when(i == n - 1)
    def _():
        wait_stage(slot)
        wait_stage(1 - slot)


def _dispatch_call(tabs, tail, n_used, h2p, pos_t, *, n_xs_rows):
    n_tiles, _, n_exp = tabs.shape
    tm = TM_FIN
    tps = TILES_PER_STEP
    assert n_tiles % tps == 0
    nt = n_tiles // tps
    assert nt >= 2 and tm < (1 << RUN_BITS) and MOE_BLK <= (1 << TAIL_BITS)
    n_rows = TOP_K * tm
    kern = functools.partial(_dispatch_kernel, tm=tm, n_exp=n_exp, n_blocks=n_xs_rows // MOE_BLK)
    return pl.pallas_call(
        kern,
        grid=(nt,),
        in_specs=[
            pl.BlockSpec((tps, 3, n_exp), lambda i: (i, 0, 0), memory_space=pltpu.SMEM),
            pl.BlockSpec(memory_space=pltpu.SMEM),
            pl.BlockSpec(memory_space=pltpu.SMEM),
            pl.BlockSpec((tps * tm * ROW_SUB, 128), lambda i: (i, 0)),
            pl.BlockSpec((TOP_K, tps * tm), lambda i: (0, i)),
        ],
        out_specs=pl.BlockSpec(memory_space=pl.ANY),
        out_shape=jax.ShapeDtypeStruct((n_xs_rows * ROW_SUB, 128), jnp.uint32),
        scratch_shapes=[pltpu.VMEM((2, tps, n_rows * ROW_SUB, 128), jnp.uint32),
                        pltpu.VMEM((MOE_BLK * ROW_SUB, 128), jnp.uint32),
                        pltpu.SemaphoreType.DMA((2,)),
                        pltpu.SemaphoreType.DMA(())],
        compiler_params=pltpu.CompilerParams(dimension_semantics=("arbitrary",),
                                             vmem_limit_bytes=VMEM_LIMIT),
        name="moe_dispatch",
    )(tabs, tail, n_used, h2p, pos_t)


def _moe_kernel(blk_e_ref, nused_ref, x_ref, perm_ref, wgu_ref, bgu_ref, wd_ref, bd_ref, o_ref,
                wgu_scr, wd_scr, *, blk, f):
    s = pl.program_id(0)
    e_cur = blk_e_ref[s]
    e_prev = blk_e_ref[jnp.maximum(s - 1, 0)]

    @pl.when(jnp.logical_or(s == 0, e_cur != e_prev))
    def _():
        perm = perm_ref[...]
        for c in range(2 * f // 256):
            r = jnp.dot(wgu_ref[0, :, c * 256:(c + 1) * 256].astype(BF16), perm,
                        preferred_element_type=F32)
            wgu_scr[:, c * 128:(c + 1) * 128] = r[:, 0:128].astype(BF16)
            wgu_scr[:, f + c * 128:f + (c + 1) * 128] = r[:, 128:256].astype(BF16)
        wd_scr[...] = wd_ref[0].astype(BF16)

    @pl.when(s >= nused_ref[0])
    def _():
        o_ref[...] = jnp.zeros(o_ref.shape, o_ref.dtype)

    @pl.when(s < nused_ref[0])
    def _():
        x = _load_rows(x_ref, blk).astype(BF16)
        gu = jnp.dot(x, wgu_scr[...], preferred_element_type=F32) + bgu_ref[0]
        gate = jnp.minimum(gu[:, 0:f], SWIGLU_LIMIT)
        up = jnp.clip(gu[:, f:2 * f], -SWIGLU_LIMIT, SWIGLU_LIMIT)
        act = (up + 1.0) * (gate * jax.nn.sigmoid(SWIGLU_ALPHA * gate))
        y = jnp.dot(act.astype(BF16), wd_scr[...], preferred_element_type=F32) + bd_ref[0]
        dh = y.shape[1] // 2
        _store_rows(o_ref, _pack_bf16_pair(y[:, 0:dh], y[:, dh:]))


def _moe_call(blk_e, n_used, xs, perm, w_gu, b_gu_p, w_down, b_down):
    n_steps = blk_e.shape[0]
    blk = MOE_BLK
    _, d, f2 = w_gu.shape
    f = f2 // 2
    rows = blk * ROW_SUB
    kern = functools.partial(_moe_kernel, blk=blk, f=f)
    row_blk = lambda i, be, nu: (i, 0)
    grid_spec = pltpu.PrefetchScalarGridSpec(
        num_scalar_prefetch=2,
        grid=(n_steps,),
        in_specs=[
            pl.BlockSpec((rows, 128), row_blk),
            pl.BlockSpec((256, 256), lambda i, be, nu: (0, 0)),
            pl.BlockSpec((1, d, f2), lambda i, be, nu: (be[i], 0, 0)),
            pl.BlockSpec((1, 1, f2), lambda i, be, nu: (be[i], 0, 0)),
            pl.BlockSpec((1, f, d), lambda i, be, nu: (be[i], 0, 0)),
            pl.BlockSpec((1, 1, d), lambda i, be, nu: (be[i], 0, 0)),
        ],
        out_specs=pl.BlockSpec((rows, 128), row_blk),
        scratch_shapes=[
            pltpu.VMEM((d, f2), BF16),
            pltpu.VMEM((f, d), BF16),
        ],
    )
    return pl.pallas_call(
        kern,
        grid_spec=grid_spec,
        out_shape=jax.ShapeDtypeStruct((n_steps * rows, 128), jnp.uint32),
        compiler_params=pltpu.CompilerParams(dimension_semantics=("arbitrary",),
                                             vmem_limit_bytes=VMEM_LIMIT),
        name="moe_experts",
    )(blk_e, n_used, xs, perm, w_gu, b_gu_p, w_down, b_down)


def _final_kernel(tab_cur_ref, tab_nxt_ref, x1_ref, pos_ref, w_ref, mod_ref, fmod_ref, gfin_ref, ys_hbm,
                  o_ref, stage, sem, *, tm, n_exp):
    i = pl.program_id(0)
    n = pl.num_programs(0)
    slot = i % 2
    n_rows = TOP_K * tm

    tiles = range(TILES_PER_STEP)

    def issue_tiles(tab_ref, slot_):
        for j in tiles:
            def body(e, c, j=j):
                _copy_pieces(tab_ref[j, 0, e], ys_hbm, tab_ref[j, 1, e], stage.at[slot_, j],
                             tab_ref[j, 2, e], sem.at[slot_], RUN_BITS)
                return c
            lax.fori_loop(0, n_exp, body, 0)

    @pl.when(i == 0)
    def _():
        issue_tiles(tab_cur_ref, 0)

    @pl.when(i + 1 < n)
    def _():
        issue_tiles(tab_nxt_ref, 1 - slot)

    for j in tiles:
        pltpu.make_async_copy(ys_hbm.at[pl.ds(0, n_rows * ROW_SUB)], stage.at[slot, j],
                              sem.at[slot]).wait()

    ys = [_load_rows(stage.at[slot, j], n_rows).astype(BF16) for j in tiles]
    col_id = lax.broadcasted_iota(jnp.int32, (tm, n_rows), 1)
    wmats = []
    for j in tiles:
        pos = pos_ref[j * tm:(j + 1) * tm, :]
        w = w_ref[j * tm:(j + 1) * tm, :]
        a = jnp.where(col_id == pos[:, 0:1], w[:, 0:1], 0.0)
        for k in range(1, TOP_K):
            a = a + jnp.where(col_id == pos[:, k:k + 1], w[:, k:k + 1], 0.0)
        wmats.append(a.astype(BF16))
    moes = [jnp.dot(wmats[j], ys[j], preferred_element_type=F32) for j in tiles]
    gate2 = mod_ref[0, 5:6, :]
    fshift = fmod_ref[0, 0:1, :]
    fscale = fmod_ref[0, 1:2, :]
    for j in tiles:
        x2 = x1_ref[j * tm:(j + 1) * tm, :] + gate2 * moes[j]
        o_ref[j * tm:(j + 1) * tm, :] = _rms(x2, gfin_ref[...]) * (1.0 + fscale) + fshift


def _final_call(tabs, x1, pos_col, w_col, mod, fmod, g_final, ys, *, seq):
    t, d = x1.shape
    tm = TM_FIN
    assert tm < (1 << RUN_BITS)
    tls = TILES_PER_STEP
    tms = tls * tm
    assert seq % tms == 0
    tps = seq // tms
    nt = t // tms
    n_exp = tabs.shape[2]
    n_rows = TOP_K * tm
    kern = functools.partial(_final_kernel, tm=tm, n_exp=n_exp)
    tab_spec = lambda ahead: pl.BlockSpec((tls, 3, n_exp), lambda i: (jnp.minimum(i + ahead, nt - 1), 0, 0),
                                          memory_space=pltpu.SMEM)
    return pl.pallas_call(
        kern,
        grid=(nt,),
        in_specs=[
            tab_spec(0),
            tab_spec(1),
            pl.BlockSpec((tms, d), lambda i: (i, 0)),
            pl.BlockSpec((tms, TOP_K), lambda i: (i, 0)),
            pl.BlockSpec((tms, TOP_K), lambda i: (i, 0)),
            pl.BlockSpec((1, 6, d), lambda i: (i // tps, 0, 0)),
            pl.BlockSpec((1, 2, d), lambda i: (i // tps, 0, 0)),
            _const_spec((1, d)),
            pl.BlockSpec(memory_space=pl.ANY),
        ],
        out_specs=pl.BlockSpec((tms, d), lambda i: (i, 0)),
        out_shape=jax.ShapeDtypeStruct((t, d), F32),
        scratch_shapes=[pltpu.VMEM((2, tls, n_rows * ROW_SUB, 128), jnp.uint32),
                        pltpu.SemaphoreType.DMA((2,))],
        compiler_params=pltpu.CompilerParams(dimension_semantics=("arbitrary",),
                                             vmem_limit_bytes=VMEM_LIMIT),
        name="combine_final",
    )(tabs, tabs, x1, pos_col, w_col, mod, fmod, g_final, ys)


def _prep_w_in(w_in, d, q_lora, kv_lora):
    o_kpe = d + q_lora + kv_lora
    o_g = o_kpe + QK_ROPE
    half = QK_ROPE // 2
    kpe = w_in[:, o_kpe:o_kpe + QK_ROPE]
    zpad = jnp.zeros((d, 128 - QK_ROPE), w_in.dtype)
    ksw = jnp.concatenate([-kpe[:, half:], kpe[:, :half]], axis=1)
    return jnp.concatenate([w_in[:, :o_kpe], kpe, zpad, ksw, zpad, w_in[:, o_g:]], axis=1).astype(BF16)


def _prep_w_q(w_q_b):
    ql = w_q_b.shape[0]
    hd = QK_NOPE + QK_ROPE
    half = QK_ROPE // 2
    w = w_q_b.reshape(ql, N_HEADS, hd)
    nope = w[:, :, :QK_NOPE]
    pe = w[:, :, QK_NOPE:]
    sw = jnp.concatenate([-pe[:, :, half:], pe[:, :, :half]], axis=2)
    return jnp.concatenate([nope.reshape(ql, -1), pe.reshape(ql, -1), sw.reshape(ql, -1)],
                           axis=1).astype(BF16)


def _prep_w_kv(w_kv_b):
    kvl = w_kv_b.shape[0]
    w = w_kv_b.reshape(kvl, N_HEADS, QK_NOPE + V_HEAD)
    w_kn = w[:, :, :QK_NOPE].reshape(kvl, -1).astype(BF16)
    w_vt = w[:, :, QK_NOPE:].reshape(kvl, -1).T.astype(BF16)
    return w_kn, w_vt


def kernel(x, c, positions, w_mod, b_mod, g_mix, w_in, b_gate, w_pool_grp, pool_scale, w_pool_out,
           g_q_a, w_q_b, g_kv_a, w_kv_b, w_mla_out, w_out, g_ffn, w_router, b_router, w_gu, b_gu,
           w_down, b_down, g_final, w_fmod, b_fmod):
    bsz, seq, d = x.shape
    t = bsz * seq
    depth = w_mod.shape[0]
    assert depth == 1
    assert seq % TQ == 0 and seq % TM_IN == 0 and seq % TM_POST == 0 and seq % TM_FIN == 0
    q_lora = g_q_a.shape[-1]
    kv_lora = g_kv_a.shape[-1]
    n_exp = w_gu.shape[1]
    f = w_gu.shape[-1] // 2
    blk = MOE_BLK

    x2d = x.reshape(t, d)
    pos_col = positions.astype(F32).reshape(t, 1)
    inv_freq = 1.0 / (ROPE_THETA ** (jnp.arange(0, QK_ROPE, 2, dtype=F32) / QK_ROPE))
    invf2 = jnp.tile(inv_freq, 128 // (QK_ROPE // 2)).reshape(1, 128)

    mod = _mod_call(c, w_mod[0], b_mod[0]).reshape(bsz, 6, d)
    fmod = _mod_call(c, w_fmod, b_fmod).reshape(bsz, 2, d)

    w_in_p = _prep_w_in(w_in[0], d, q_lora, kv_lora)
    w_q_p = _prep_w_q(w_q_b[0])
    w_kn, w_vt = _prep_w_kv(w_kv_b[0])
    ga, g1, q2, k2, vt3 = _mixer_in_call(
        x2d, pos_col, mod, g_mix[0].reshape(1, d), w_in_p, b_gate[0].reshape(1, 2 * d),
        w_pool_grp[0].astype(BF16), pool_scale[0].reshape(1, d), w_pool_out[0].astype(BF16),
        g_q_a[0].reshape(1, q_lora), w_q_p, g_kv_a[0].reshape(1, kv_lora), w_kn, w_vt, invf2,
        bsz=bsz, seq=seq)

    hp = N_HEADS * HEAD_PAD
    o = _attn_call(q2.reshape(bsz, seq, hp), k2.reshape(bsz, seq, hp), vt3)

    w_r = w_router[0]
    w_r_hi = w_r.astype(BF16)
    w_r_lo = (w_r - w_r_hi.astype(F32)).astype(BF16)
    lane_pad = lambda a: jnp.pad(a, ((0, 0), (0, 128 - n_exp)))
    w_rt = jnp.concatenate(
        [jnp.concatenate([lane_pad(w_r_hi), lane_pad(w_r_lo)], axis=1),
         jnp.concatenate([lane_pad(w_r_hi), jnp.zeros((d, 128), BF16)], axis=1)], axis=0)
    x1, h2p, pos_t, wgt_t, cnt3 = _post_call(
        o.reshape(t, d), ga, g1, x2d, mod, w_mla_out[0].astype(BF16), w_out[0].astype(BF16),
        g_ffn[0].reshape(1, d), w_rt, b_router[0].reshape(n_exp, 1), seq=seq)

    n_slots = t * TOP_K
    n_rows = n_slots + n_exp * blk
    n_blocks = n_rows // blk
    cnt = cnt3[:, 0:TM_POST // TM_FIN, 0:n_exp].reshape(-1, n_exp)
    counts = jnp.sum(cnt, axis=0)
    padded = (counts + blk - 1) // blk * blk
    pad_end = jnp.cumsum(padded)
    pad_start = pad_end - padded
    blk_start = jnp.arange(n_blocks, dtype=jnp.int32) * blk
    blk_e = jnp.minimum(jnp.sum(pad_end[None, :] <= blk_start[:, None], axis=1, dtype=jnp.int32),
                        n_exp - 1)
    n_used = (pad_end[-1:] // blk).astype(jnp.int32)
    tail = jnp.stack([(pad_start + counts) * ROW_SUB, padded - counts], axis=0).astype(jnp.int32)

    before = jnp.cumsum(cnt, axis=0) - cnt
    run_start = jnp.cumsum(cnt, axis=1) - cnt
    tabs = jnp.stack([cnt, (pad_start[None, :] + before) * ROW_SUB, run_start * ROW_SUB], axis=1)

    col = jnp.arange(256, dtype=jnp.int32)[None, :]
    row = jnp.arange(256, dtype=jnp.int32)[:, None]
    perm = (row == jnp.where(col < 128, 2 * col, 2 * (col - 128) + 1)).astype(BF16)
    b_gu_p = jnp.concatenate([b_gu[0][:, 0::2], b_gu[0][:, 1::2]], axis=-1).reshape(n_exp, 1, 2 * f)
    xs = _dispatch_call(tabs, tail, n_used, h2p, pos_t, n_xs_rows=n_rows)
    ys = _moe_call(blk_e, n_used, xs, perm, w_gu[0], b_gu_p, w_down[0], b_down[0].reshape(n_exp, 1, d))

    out = _final_call(tabs, x1, pos_t.T, wgt_t.T, mod, fmod, g_final.reshape(1, d), ys, seq=seq)
    return out.reshape(bsz, seq, d)
```

```python
import functools

import jax
import jax.numpy as jnp
from jax import lax
from jax.experimental import pallas as pl
from jax.experimental.pallas import tpu as pltpu

F32 = jnp.float32
BF16 = jnp.bfloat16

CHUNK = 64
POOL_WINDOWS = (2, 4, 8, 16)
POOL_HALO = 16
N_HEADS = 8
QK_NOPE = 128
QK_ROPE = 64
V_HEAD = 128
HEAD_PAD = 256
ROPE_THETA = 10000.0
TOP_K = 4
SWIGLU_LIMIT = 7.0
SWIGLU_ALPHA = 1.702
NORM_EPS = 1e-6
NEG_INF = -1e30
LOG2_E = 1.4426950408889634

VMEM_LIMIT = 56 * 1024 * 1024

TM_IN = 256
TQ = 512
TM_POST = 512
MOE_BLK = 1024
MOE_SUB = 512
TM_FIN = 256
TILES_PER_STEP = 2
ROW_SUB = 4


def _const_spec(shape):
    nd = len(shape)
    return pl.BlockSpec(shape, lambda *_: (0,) * nd, pipeline_mode=pl.Buffered(1))


def _rms(xf, g):
    return xf * lax.rsqrt(jnp.mean(xf * xf, axis=-1, keepdims=True) + NORM_EPS) * g


def _pack_bf16_pair(lo, hi):
    lo_b = lax.bitcast_convert_type(lo.astype(BF16).astype(F32), jnp.uint32)
    hi_b = lax.bitcast_convert_type(hi.astype(BF16).astype(F32), jnp.uint32)
    return (hi_b & jnp.uint32(0xFFFF0000)) | (lo_b >> 16)


def _unpack_bf16_pair(p):
    lo = lax.bitcast_convert_type(p << 16, F32)
    hi = lax.bitcast_convert_type(p & jnp.uint32(0xFFFF0000), F32)
    return lo, hi


def _store_rows(ref, packed):
    n = packed.shape[0]
    for q in range(ROW_SUB):
        ref[pl.ds(q, n, stride=ROW_SUB), :] = packed[:, q * 128:(q + 1) * 128]


def _load_rows(ref, n):
    los, his = [], []
    for q in range(ROW_SUB):
        lo, hi = _unpack_bf16_pair(ref[pl.ds(q, n, stride=ROW_SUB), :])
        los.append(lo)
        his.append(hi)
    return jnp.concatenate(los + his, axis=1)


def _mod_kernel(c_ref, w_ref, b_ref, o_ref):
    c = c_ref[...]
    c_act = c * jax.nn.sigmoid(c)
    o_ref[...] = jnp.dot(c_act, w_ref[...], preferred_element_type=F32,
                         precision=lax.Precision.HIGHEST) + b_ref[...]


def _mod_call(c, w, b, tn=1024):
    bsz, d = c.shape
    n = w.shape[1]
    return pl.pallas_call(
        _mod_kernel,
        grid=(n // tn,),
        in_specs=[pl.BlockSpec((bsz, d), lambda j: (0, 0)),
                  pl.BlockSpec((d, tn), lambda j: (0, j)),
                  pl.BlockSpec((1, tn), lambda j: (0, j))],
        out_specs=pl.BlockSpec((bsz, tn), lambda j: (0, j)),
        out_shape=jax.ShapeDtypeStruct((bsz, n), F32),
        compiler_params=pltpu.CompilerParams(dimension_semantics=("arbitrary",),
                                             vmem_limit_bytes=VMEM_LIMIT),
        name="adaln_mod",
    )(c, w, b.reshape(1, n))


def _mixer_in_kernel(x_ref, xh_ref, pos_ref, mod_ref, gmix_ref, win_ref, bgate_ref, wgrp_ref,
                     pscale_ref, wpo_ref, gq_ref, wq_ref, gkv_ref, wkn_ref, wvt_ref, invf_ref,
                     ga_ref, g1_ref, q_ref, k_ref, vt_ref, u_scr, *, tm, tiles_per_seq, d, q_lora,
                     kv_lora):
    i = pl.program_id(0)
    t_in_seq = i % tiles_per_seq
    is_start = t_in_seq == 0
    shift1 = mod_ref[0, 0:1, :]
    scale1 = mod_ref[0, 1:2, :]
    gmix = gmix_ref[...]

    def prenorm(xf):
        return (_rms(xf, gmix) * (1.0 + scale1) + shift1).astype(BF16)

    h = prenorm(x_ref[...])
    hh = prenorm(xh_ref[...])
    u_ext = jnp.dot(jnp.concatenate([hh, h], axis=0), win_ref[:, 0:d],
                    preferred_element_type=F32)
    u_scr[0:POOL_HALO, :] = jnp.where(is_start, 0.0, u_ext[0:POOL_HALO, :])
    u_scr[POOL_HALO:POOL_HALO + tm, :] = u_ext[POOL_HALO:, :]
    rest = jnp.dot(h, win_ref[:, d:], preferred_element_type=F32)

    gw = d // len(POOL_WINDOWS)
    tseq = t_in_seq * tm + lax.broadcasted_iota(jnp.int32, (tm, 1), 0)
    ys = []
    for g, w in enumerate(POOL_WINDOWS):
        c0 = g * gw
        ug = u_scr[POOL_HALO:POOL_HALO + tm, c0:c0 + gw]
        acc = ug
        for j in range(1, w):
            acc = acc + u_scr[POOL_HALO - j:POOL_HALO - j + tm, c0:c0 + gw]
        cnt = jnp.minimum(tseq + 1, w).astype(F32)
        mixed = (acc / cnt - ug).astype(BF16)
        ys.append(jnp.dot(mixed, wgrp_ref[g], preferred_element_type=F32))
    y = (jnp.concatenate(ys, axis=1) * pscale_ref[...]).astype(BF16)
    a = jnp.dot(y, wpo_ref[...], preferred_element_type=F32)

    o_q = 0
    o_kv = q_lora
    o_kpe = q_lora + kv_lora
    o_ksw = o_kpe + 128
    o_g0 = o_ksw + 128
    o_g1 = o_g0 + d
    gates0 = jax.nn.sigmoid(rest[:, o_g0:o_g0 + d] + bgate_ref[:, 0:d])
    gates1 = jax.nn.sigmoid(rest[:, o_g1:o_g1 + d] + bgate_ref[:, d:2 * d])
    ga_ref[...] = (gates0 * a).astype(BF16)
    g1_ref[...] = gates1.astype(BF16)

    ang = pos_ref[...] * invf_ref[...]
    cos2 = jnp.cos(ang)
    sin2 = jnp.sin(ang)
    q_scale = float(QK_NOPE + QK_ROPE) ** -0.5 * LOG2_E

    qn = _rms(rest[:, o_q:o_q + q_lora], gq_ref[...]).astype(BF16)
    qall = jnp.dot(qn, wq_ref[...], preferred_element_type=F32)
    kvn = _rms(rest[:, o_kv:o_kv + kv_lora], gkv_ref[...]).astype(BF16)
    kn = jnp.dot(kvn, wkn_ref[...], preferred_element_type=F32)
    vt = lax.dot_general(wvt_ref[...], kvn, (((1,), (1,)), ((), ())),
                         preferred_element_type=F32)
    vt_ref[0] = vt.astype(BF16)
    kpe = (rest[:, o_kpe:o_kpe + 128] * cos2 + rest[:, o_ksw:o_ksw + 128] * sin2).astype(BF16)
    o_pe = N_HEADS * QK_NOPE
    o_sw = o_pe + N_HEADS * QK_ROPE
    low_half = lax.broadcasted_iota(jnp.int32, (tm, 128), 1) < QK_ROPE
    for pair in range(N_HEADS // 2):
        c0 = pair * 128
        qpe2 = (qall[:, o_pe + c0:o_pe + c0 + 128] * cos2
                + qall[:, o_sw + c0:o_sw + c0 + 128] * sin2) * q_scale
        for hd, part in ((2 * pair, qpe2), (2 * pair + 1, pltpu.roll(qpe2, QK_ROPE, axis=1))):
            n0 = hd * QK_NOPE
            q_ref[:, hd * HEAD_PAD:hd * HEAD_PAD + 128] = (qall[:, n0:n0 + 128] * q_scale).astype(BF16)
            q_ref[:, hd * HEAD_PAD + 128:(hd + 1) * HEAD_PAD] = jnp.where(low_half, part, 0.0).astype(BF16)
            k_ref[:, hd * HEAD_PAD:hd * HEAD_PAD + 128] = kn[:, n0:n0 + 128].astype(BF16)
            k_ref[:, hd * HEAD_PAD + 128:(hd + 1) * HEAD_PAD] = kpe


def _mixer_in_call(x2d, pos_col, mod, g_mix, w_in_p, b_gate, w_grp, pool_scale, w_po, g_q_a, w_q_p,
                   g_kv_a, w_kn, w_vt, invf2, *, bsz, seq):
    t, d = x2d.shape
    tm = TM_IN
    tps = seq // tm
    q_lora = g_q_a.shape[-1]
    kv_lora = g_kv_a.shape[-1]
    hp = N_HEADS * HEAD_PAD
    halo_blocks = tm // POOL_HALO
    kern = functools.partial(_mixer_in_kernel, tm=tm, tiles_per_seq=tps, d=d, q_lora=q_lora,
                             kv_lora=kv_lora)
    row = lambda i: (i, 0)
    return pl.pallas_call(
        kern,
        grid=(t // tm,),
        in_specs=[
            pl.BlockSpec((tm, d), row),
            pl.BlockSpec((POOL_HALO, d), lambda i: (jnp.maximum(i * halo_blocks - 1, 0), 0)),
            pl.BlockSpec((tm, 1), row),
            pl.BlockSpec((1, 6, d), lambda i: (i // tps, 0, 0)),
            _const_spec((1, d)),
            _const_spec(w_in_p.shape),
            _const_spec((1, 2 * d)),
            _const_spec(w_grp.shape),
            _const_spec((1, d)),
            _const_spec(w_po.shape),
            _const_spec((1, q_lora)),
            _const_spec(w_q_p.shape),
            _const_spec((1, kv_lora)),
            _const_spec(w_kn.shape),
            _const_spec(w_vt.shape),
            _const_spec((1, 128)),
        ],
        out_specs=[
            pl.BlockSpec((tm, d), row),
            pl.BlockSpec((tm, d), row),
            pl.BlockSpec((tm, hp), row),
            pl.BlockSpec((tm, hp), row),
            pl.BlockSpec((1, N_HEADS * V_HEAD, tm), lambda i: (i // tps, 0, i % tps)),
        ],
        out_shape=[
            jax.ShapeDtypeStruct((t, d), BF16),
            jax.ShapeDtypeStruct((t, d), BF16),
            jax.ShapeDtypeStruct((t, hp), BF16),
            jax.ShapeDtypeStruct((t, hp), BF16),
            jax.ShapeDtypeStruct((bsz, N_HEADS * V_HEAD, seq), BF16),
        ],
        scratch_shapes=[pltpu.VMEM((tm + POOL_HALO, d), F32)],
        compiler_params=pltpu.CompilerParams(dimension_semantics=("arbitrary",),
                                             vmem_limit_bytes=VMEM_LIMIT),
        name="mixer_in",
    )(x2d, x2d, pos_col, mod, g_mix, w_in_p, b_gate, w_grp, pool_scale, w_po, g_q_a, w_q_p, g_kv_a,
      w_kn, w_vt, invf2)


def _attn_kernel(q_ref, k_ref, vt_ref, o_ref, *, tq, nq):
    seq = vt_ref.shape[2]
    vt_ext = jnp.concatenate([vt_ref[0], jnp.ones((16, seq), BF16)], axis=0)

    def scores(qi):
        q0 = qi * tq
        ln = q0 + tq
        return lax.dot_general(k_ref[0, 0:ln, :], q_ref[0, q0:q0 + tq, :], (((1,), (1,)), ((), ())),
                               preferred_element_type=F32)

    def finish(qi, s):
        q0 = qi * tq
        ln = q0 + tq
        cw = tq // 2
        for c in range(2):
            cs = slice(c * cw, (c + 1) * cw)
            kc = (q0 + lax.broadcasted_iota(jnp.int32, (tq, cw), 0)) // CHUNK
            qc = (q0 + c * cw + lax.broadcasted_iota(jnp.int32, (tq, cw), 1)) // CHUNK
            s_diag = jnp.where(qc >= kc, s[q0:ln, cs], NEG_INF)
            m = jnp.max(s_diag, axis=0, keepdims=True)
            if qi > 0:
                s_main = s[0:q0, cs]
                m = jnp.maximum(m, jnp.max(s_main, axis=0, keepdims=True))
            p_diag = jnp.exp2(s_diag - m).astype(BF16)
            acc = jnp.dot(vt_ext[:, q0:ln], p_diag, preferred_element_type=F32)
            if qi > 0:
                p_main = jnp.exp2(s_main - m).astype(BF16)
                acc = acc + jnp.dot(vt_ext[:, 0:q0], p_main, preferred_element_type=F32)
            l = acc[V_HEAD:V_HEAD + 1, :]
            o_ref[0, q0 + c * cw:q0 + (c + 1) * cw, :] = (acc[0:V_HEAD, :] / l).T.astype(BF16)

    order = list(range(nq))[::-1]
    s_cur = scores(order[0])
    for idx, qi in enumerate(order):
        s_next = scores(order[idx + 1]) if idx + 1 < nq else None
        finish(qi, s_cur)
        s_cur = s_next


def _attn_call(q3, k3, vt3):
    bsz, seq, _ = q3.shape
    kern = functools.partial(_attn_kernel, tq=TQ, nq=seq // TQ)
    return pl.pallas_call(
        kern,
        grid=(bsz, N_HEADS),
        in_specs=[pl.BlockSpec((1, seq, HEAD_PAD), lambda b, h: (b, 0, h)),
                  pl.BlockSpec((1, seq, HEAD_PAD), lambda b, h: (b, 0, h)),
                  pl.BlockSpec((1, V_HEAD, seq), lambda b, h: (b, h, 0))],
        out_specs=pl.BlockSpec((1, seq, V_HEAD), lambda b, h: (b, 0, h)),
        out_shape=jax.ShapeDtypeStruct((bsz, seq, N_HEADS * V_HEAD), BF16),
        compiler_params=pltpu.CompilerParams(dimension_semantics=("arbitrary", "arbitrary"),
                                             vmem_limit_bytes=VMEM_LIMIT),
        name="mla_attn",
    )(q3, k3, vt3)


def _post_kernel(o_ref, ga_ref, g1_ref, x_ref, mod_ref, wmo_ref, wout_ref, gffn_ref, wrt_ref, br_ref,
                 x1_ref, h2_ref, pos_ref, wgt_ref, cnt_ref, *, d):
    m = jnp.dot(o_ref[...], wmo_ref[...], preferred_element_type=F32)
    merged = ga_ref[...].astype(F32) + g1_ref[...].astype(F32) * m
    gate1 = mod_ref[0, 2:3, :]
    x1 = x_ref[...] + gate1 * jnp.dot(merged.astype(BF16), wout_ref[...],
                                      preferred_element_type=F32)
    x1_ref[...] = x1
    shift2 = mod_ref[0, 3:4, :]
    scale2 = mod_ref[0, 4:5, :]
    h2 = _rms(x1, gffn_ref[...]) * (1.0 + scale2) + shift2
    _store_rows(h2_ref, _pack_bf16_pair(h2[:, 0:d // 2], h2[:, d // 2:d]))

    ne = br_ref.shape[0]
    h_hi = h2.astype(BF16)
    h_lo = (h2 - h_hi.astype(F32)).astype(BF16)
    hcat = jnp.concatenate([h_hi, h_lo], axis=1)
    half = hcat.shape[0] // 2
    lg2 = jnp.concatenate(
        [jnp.dot(hcat[0:half], wrt_ref[...], preferred_element_type=F32),
         jnp.dot(hcat[half:], wrt_ref[...], preferred_element_type=F32)], axis=0)
    lg = lg2[:, 0:128] + lg2[:, 128:256]
    logits = lg.T[0:ne, :] + br_ref[...]
    tm = logits.shape[1]
    eid = lax.broadcasted_iota(jnp.int32, (ne, tm), 0)
    vals, idxs = [], []
    cur = logits
    for _ in range(TOP_K):
        mx = jnp.max(cur, axis=0, keepdims=True)
        ix = jnp.min(jnp.where(cur == mx, eid, ne), axis=0, keepdims=True)
        vals.append(mx)
        idxs.append(ix)
        cur = jnp.where(eid == ix, -jnp.inf, cur)
    es = [jnp.exp(v - vals[0]) for v in vals]
    den = es[0] + es[1] + es[2] + es[3]
    wgt_ref[...] = jnp.concatenate([e / den for e in es], axis=0)

    tt = TM_FIN
    t_row = lax.broadcasted_iota(jnp.int32, (tm, tm), 0)
    t_col = lax.broadcasted_iota(jnp.int32, (tm, tm), 1)
    same_tile = (t_row // tt) == (t_col // tt)
    in_tile = jnp.where(same_tile, 1.0, 0.0).astype(BF16)
    earlier = jnp.where(same_tile & (t_row < t_col), 1.0, 0.0).astype(BF16)
    lower = jnp.where(lax.broadcasted_iota(jnp.int32, (ne, ne), 1)
                      < lax.broadcasted_iota(jnp.int32, (ne, ne), 0), 1.0, 0.0).astype(BF16)
    picks = [eid == ix for ix in idxs]
    routed = jnp.where(picks[0] | picks[1] | picks[2] | picks[3], 1.0, 0.0).astype(BF16)
    before = jnp.dot(routed, earlier, preferred_element_type=F32)
    cnt_b = jnp.dot(routed, in_tile, preferred_element_type=F32)
    run_start = jnp.dot(lower, cnt_b.astype(BF16), preferred_element_type=F32)
    base = run_start + before
    pos_ref[...] = jnp.concatenate(
        [jnp.sum(jnp.where(pk, base, 0.0), axis=0, keepdims=True) for pk in picks],
        axis=0).astype(jnp.int32)
    tile_sel = jnp.where(lax.broadcasted_iota(jnp.int32, (8, tm), 0)
                         == lax.broadcasted_iota(jnp.int32, (8, tm), 1) // tt, 1.0, 0.0).astype(BF16)
    cnt8 = lax.dot_general(tile_sel, routed, (((1,), (1,)), ((), ())),
                           preferred_element_type=F32)
    cnt_ref[0] = jnp.concatenate([cnt8, jnp.zeros((8, 128 - ne), F32)], axis=1).astype(jnp.int32)


def _post_call(o2d, ga, g1, x2d, mod, w_mo, w_out, g_ffn, w_rt, b_r, *, seq):
    t, d = x2d.shape
    tm = TM_POST
    tps = seq // tm
    row = lambda i: (i, 0)
    n_exp = b_r.shape[0]
    assert tm % TM_FIN == 0 and tm // TM_FIN <= 8 and n_exp <= 128
    kern = functools.partial(_post_kernel, d=d)
    return pl.pallas_call(
        kern,
        grid=(t // tm,),
        in_specs=[
            pl.BlockSpec((tm, d), row),
            pl.BlockSpec((tm, d), row),
            pl.BlockSpec((tm, d), row),
            pl.BlockSpec((tm, d), row),
            pl.BlockSpec((1, 6, d), lambda i: (i // tps, 0, 0)),
            _const_spec(w_mo.shape),
            _const_spec(w_out.shape),
            _const_spec((1, d)),
            _const_spec(w_rt.shape),
            _const_spec(b_r.shape),
        ],
        out_specs=[
            pl.BlockSpec((tm, d), row),
            pl.BlockSpec((tm * ROW_SUB, 128), row),
            pl.BlockSpec((TOP_K, tm), lambda i: (0, i)),
            pl.BlockSpec((TOP_K, tm), lambda i: (0, i)),
            pl.BlockSpec((1, 8, 128), lambda i: (i, 0, 0)),
        ],
        out_shape=[
            jax.ShapeDtypeStruct((t, d), F32),
            jax.ShapeDtypeStruct((t * ROW_SUB, 128), jnp.uint32),
            jax.ShapeDtypeStruct((TOP_K, t), jnp.int32),
            jax.ShapeDtypeStruct((TOP_K, t), F32),
            jax.ShapeDtypeStruct((t // tm, 8, 128), jnp.int32),
        ],
        compiler_params=pltpu.CompilerParams(dimension_semantics=("arbitrary",),
                                             vmem_limit_bytes=VMEM_LIMIT),
        name="post_attn_router",
    )(o2d, ga, g1, x2d, mod, w_mo, w_out, g_ffn, w_rt, b_r)


RUN_BITS = 9
COMMON_RUN_BITS = 6
TAIL_BITS = 10


def _copy_pieces(count, src_ref, src_off, dst_ref, dst_off, sem, bits, wait=False):
    def pieces(bit_range, off):
        for b in reversed(bit_range):
            size = (1 << b) * ROW_SUB
            take = (count >> b) & 1

            @pl.when(take == 1)
            def _(off=off, size=size):
                cp = pltpu.make_async_copy(
                    src_ref.at[pl.ds(pl.multiple_of(src_off + off, ROW_SUB), size)],
                    dst_ref.at[pl.ds(pl.multiple_of(dst_off + off, ROW_SUB), size)], sem)
                if wait:
                    cp.wait()
                else:
                    cp.start()
            off = off + take * size

    lo = min(bits, COMMON_RUN_BITS)
    big = (count >> lo) << lo

    if lo < bits:
        @pl.when(big != 0)
        def _():
            pieces(range(lo, bits), 0)
    pieces(range(lo), big * ROW_SUB)


def _dispatch_kernel(tab_ref, tail_ref, nused_ref, h_ref, pos_ref, xs_hbm, stage, zbuf, sem, zsem, *,
                     tm, n_exp, n_blocks):
    i = pl.program_id(0)
    n = pl.num_programs(0)
    slot = i % 2
    n_rows = TOP_K * tm

    def wait_stage(slot_):
        for j in range(TILES_PER_STEP):
            pltpu.make_async_copy(stage.at[slot_, j], xs_hbm.at[pl.ds(0, n_rows * ROW_SUB)],
                                  sem.at[slot_]).wait()

    @pl.when(i == 0)
    def _():
        zbuf[...] = jnp.zeros(zbuf.shape, zbuf.dtype)
        for wait in (False, True):
            def body(e, c, wait=wait):
                _copy_pieces(tail_ref[1, e], zbuf, 0, xs_hbm, tail_ref[0, e], zsem, TAIL_BITS, wait)
                return c
            lax.fori_loop(0, n_exp, body, 0)
        blk_rows = zbuf.shape[0]

        def zero_block(b, c):
            cp = pltpu.make_async_copy(
                zbuf, xs_hbm.at[pl.ds(pl.multiple_of(b * blk_rows, blk_rows), blk_rows)], zsem)
            cp.start()
            cp.wait()
            return c
        lax.fori_loop(nused_ref[0], n_blocks, zero_block, 0)

    @pl.when(i >= 2)
    def _():
        wait_stage(slot)

    tiles = range(TILES_PER_STEP)
    hs = [_load_rows(h_ref.at[pl.ds(j * tm * ROW_SUB, tm * ROW_SUB)], tm).astype(BF16)
          for j in tiles]
    row_id = lax.broadcasted_iota(jnp.int32, (n_rows, tm), 0)
    onehots = []
    for j in tiles:
        hit = row_id == pos_ref[0:1, j * tm:(j + 1) * tm]
        for k in range(1, TOP_K):
            hit = hit | (row_id == pos_ref[k:k + 1, j * tm:(j + 1) * tm])
        onehots.append(jnp.where(hit, 1.0, 0.0).astype(BF16))
    rows = [jnp.dot(onehots[j], hs[j], preferred_element_type=F32) for j in tiles]
    for j in tiles:
        dh = rows[j].shape[1] // 2
        _store_rows(stage.at[slot, j], _pack_bf16_pair(rows[j][:, 0:dh], rows[j][:, dh:]))

    for j in tiles:
        def body(e, c, j=j):
            _copy_pieces(tab_ref[j, 0, e], stage.at[slot, j], tab_ref[j, 2, e], xs_hbm,
                         tab_ref[j, 1, e], sem.at[slot], RUN_BITS)
            return c
        lax.fori_loop(0, n_exp, body, 0)

    @pl.when(i == n - 1)
    def _():
        wait_stage(slot)
        wait_stage(1 - slot)


def _dispatch_call(tabs, tail, n_used, h2p, pos_t, *, n_xs_rows):
    n_tiles, _, n_exp = tabs.shape
    tm = TM_FIN
    tps = TILES_PER_STEP
    assert n_tiles % tps == 0
    nt = n_tiles // tps
    assert nt >= 2 and tm < (1 << RUN_BITS) and MOE_BLK <= (1 << TAIL_BITS)
    n_rows = TOP_K * tm
    kern = functools.partial(_dispatch_kernel, tm=tm, n_exp=n_exp, n_blocks=n_xs_rows // MOE_BLK)
    return pl.pallas_call(
        kern,
        grid=(nt,),
        in_specs=[
            pl.BlockSpec((tps, 3, n_exp), lambda i: (i, 0, 0), memory_space=pltpu.SMEM),
            pl.BlockSpec(memory_space=pltpu.SMEM),
            pl.BlockSpec(memory_space=pltpu.SMEM),
            pl.BlockSpec((tps * tm * ROW_SUB, 128), lambda i: (i, 0)),
            pl.BlockSpec((TOP_K, tps * tm), lambda i: (0, i)),
        ],
        out_specs=pl.BlockSpec(memory_space=pl.ANY),
        out_shape=jax.ShapeDtypeStruct((n_xs_rows * ROW_SUB, 128), jnp.uint32),
        scratch_shapes=[pltpu.VMEM((2, tps, n_rows * ROW_SUB, 128), jnp.uint32),
                        pltpu.VMEM((MOE_BLK * ROW_SUB, 128), jnp.uint32),
                        pltpu.SemaphoreType.DMA((2,)),
                        pltpu.SemaphoreType.DMA(())],
        compiler_params=pltpu.CompilerParams(dimension_semantics=("arbitrary",),
                                             vmem_limit_bytes=VMEM_LIMIT),
        name="moe_dispatch",
    )(tabs, tail, n_used, h2p, pos_t)


def _moe_kernel(blk_e_ref, nvalid_ref, x_ref, perm_ref, wgu_ref, bgu_ref, wd_ref, bd_ref, o_ref,
                wgu_scr, wd_scr, *, blk, sub, f):
    s = pl.program_id(0)
    e_cur = blk_e_ref[s]
    e_prev = blk_e_ref[jnp.maximum(s - 1, 0)]

    @pl.when(jnp.logical_or(s == 0, e_cur != e_prev))
    def _():
        perm = perm_ref[...]
        for c in range(2 * f // 256):
            r = jnp.dot(wgu_ref[0, :, c * 256:(c + 1) * 256].astype(BF16), perm,
                        preferred_element_type=F32)
            wgu_scr[:, c * 128:(c + 1) * 128] = r[:, 0:128].astype(BF16)
            wgu_scr[:, f + c * 128:f + (c + 1) * 128] = r[:, 128:256].astype(BF16)
        wd_scr[...] = wd_ref[0].astype(BF16)

    nv = nvalid_ref[s]
    for g in range(blk // sub):
        rows = pl.ds(g * sub * ROW_SUB, sub * ROW_SUB)

        @pl.when(nv <= g * sub)
        def _(rows=rows):
            o_ref[rows, :] = jnp.zeros((sub * ROW_SUB, 128), o_ref.dtype)

        @pl.when(nv > g * sub)
        def _(rows=rows):
            x = _load_rows(x_ref.at[rows], sub).astype(BF16)
            gu = jnp.dot(x, wgu_scr[...], preferred_element_type=F32) + bgu_ref[0]
            gate = jnp.minimum(gu[:, 0:f], SWIGLU_LIMIT)
            up = jnp.clip(gu[:, f:2 * f], -SWIGLU_LIMIT, SWIGLU_LIMIT)
            act = (up + 1.0) * (gate * jax.nn.sigmoid(SWIGLU_ALPHA * gate))
            y = jnp.dot(act.astype(BF16), wd_scr[...], preferred_element_type=F32) + bd_ref[0]
            dh = y.shape[1] // 2
            _store_rows(o_ref.at[rows], _pack_bf16_pair(y[:, 0:dh], y[:, dh:]))


def _moe_call(blk_e, nvalid, xs, perm, w_gu, b_gu_p, w_down, b_down):
    n_steps = blk_e.shape[0]
    blk = MOE_BLK
    assert blk % MOE_SUB == 0
    _, d, f2 = w_gu.shape
    f = f2 // 2
    rows = blk * ROW_SUB
    kern = functools.partial(_moe_kernel, blk=blk, sub=MOE_SUB, f=f)
    row_blk = lambda i, be, nu: (i, 0)
    grid_spec = pltpu.PrefetchScalarGridSpec(
        num_scalar_prefetch=2,
        grid=(n_steps,),
        in_specs=[
            pl.BlockSpec((rows, 128), row_blk),
            pl.BlockSpec((256, 256), lambda i, be, nu: (0, 0)),
            pl.BlockSpec((1, d, f2), lambda i, be, nu: (be[i], 0, 0)),
            pl.BlockSpec((1, 1, f2), lambda i, be, nu: (be[i], 0, 0)),
            pl.BlockSpec((1, f, d), lambda i, be, nu: (be[i], 0, 0)),
            pl.BlockSpec((1, 1, d), lambda i, be, nu: (be[i], 0, 0)),
        ],
        out_specs=pl.BlockSpec((rows, 128), row_blk),
        scratch_shapes=[
            pltpu.VMEM((d, f2), BF16),
            pltpu.VMEM((f, d), BF16),
        ],
    )
    return pl.pallas_call(
        kern,
        grid_spec=grid_spec,
        out_shape=jax.ShapeDtypeStruct((n_steps * rows, 128), jnp.uint32),
        compiler_params=pltpu.CompilerParams(dimension_semantics=("arbitrary",),
                                             vmem_limit_bytes=VMEM_LIMIT),
        name="moe_experts",
    )(blk_e, nvalid, xs, perm, w_gu, b_gu_p, w_down, b_down)


def _final_kernel(tab_cur_ref, tab_nxt_ref, x1_ref, pos_ref, w_ref, mod_ref, fmod_ref, gfin_ref, ys_hbm,
                  o_ref, stage, sem, *, tm, n_exp):
    i = pl.program_id(0)
    n = pl.num_programs(0)
    slot = i % 2
    n_rows = TOP_K * tm

    tiles = range(TILES_PER_STEP)

    def issue_tiles(tab_ref, slot_):
        for j in tiles:
            def body(e, c, j=j):
                _copy_pieces(tab_ref[j, 0, e], ys_hbm, tab_ref[j, 1, e], stage.at[slot_, j],
                             tab_ref[j, 2, e], sem.at[slot_], RUN_BITS)
                return c
            lax.fori_loop(0, n_exp, body, 0)

    @pl.when(i == 0)
    def _():
        issue_tiles(tab_cur_ref, 0)

    @pl.when(i + 1 < n)
    def _():
        issue_tiles(tab_nxt_ref, 1 - slot)

    for j in tiles:
        pltpu.make_async_copy(ys_hbm.at[pl.ds(0, n_rows * ROW_SUB)], stage.at[slot, j],
                              sem.at[slot]).wait()

    ys = [_load_rows(stage.at[slot, j], n_rows).astype(BF16) for j in tiles]
    col_id = lax.broadcasted_iota(jnp.int32, (tm, n_rows), 1)
    wmats = []
    for j in tiles:
        pos = pos_ref[j * tm:(j + 1) * tm, :]
        w = w_ref[j * tm:(j + 1) * tm, :]
        a = jnp.where(col_id == pos[:, 0:1], w[:, 0:1], 0.0)
        for k in range(1, TOP_K):
            a = a + jnp.where(col_id == pos[:, k:k + 1], w[:, k:k + 1], 0.0)
        wmats.append(a.astype(BF16))
    moes = [jnp.dot(wmats[j], ys[j], preferred_element_type=F32) for j in tiles]
    gate2 = mod_ref[0, 5:6, :]
    fshift = fmod_ref[0, 0:1, :]
    fscale = fmod_ref[0, 1:2, :]
    for j in tiles:
        x2 = x1_ref[j * tm:(j + 1) * tm, :] + gate2 * moes[j]
        o_ref[j * tm:(j + 1) * tm, :] = _rms(x2, gfin_ref[...]) * (1.0 + fscale) + fshift


def _final_call(tabs, x1, pos_col, w_col, mod, fmod, g_final, ys, *, seq):
    t, d = x1.shape
    tm = TM_FIN
    assert tm < (1 << RUN_BITS)
    tls = TILES_PER_STEP
    tms = tls * tm
    assert seq % tms == 0
    tps = seq // tms
    nt = t // tms
    n_exp = tabs.shape[2]
    n_rows = TOP_K * tm
    kern = functools.partial(_final_kernel, tm=tm, n_exp=n_exp)
    tab_spec = lambda ahead: pl.BlockSpec((tls, 3, n_exp), lambda i: (jnp.minimum(i + ahead, nt - 1), 0, 0),
                                          memory_space=pltpu.SMEM)
    return pl.pallas_call(
        kern,
        grid=(nt,),
        in_specs=[
            tab_spec(0),
            tab_spec(1),
            pl.BlockSpec((tms, d), lambda i: (i, 0)),
            pl.BlockSpec((tms, TOP_K), lambda i: (i, 0)),
            pl.BlockSpec((tms, TOP_K), lambda i: (i, 0)),
            pl.BlockSpec((1, 6, d), lambda i: (i // tps, 0, 0)),
            pl.BlockSpec((1, 2, d), lambda i: (i // tps, 0, 0)),
            _const_spec((1, d)),
            pl.BlockSpec(memory_space=pl.ANY),
        ],
        out_specs=pl.BlockSpec((tms, d), lambda i: (i, 0)),
        out_shape=jax.ShapeDtypeStruct((t, d), F32),
        scratch_shapes=[pltpu.VMEM((2, tls, n_rows * ROW_SUB, 128), jnp.uint32),
                        pltpu.SemaphoreType.DMA((2,))],
        compiler_params=pltpu.CompilerParams(dimension_semantics=("arbitrary",),
                                             vmem_limit_bytes=VMEM_LIMIT),
        name="combine_final",
    )(tabs, tabs, x1, pos_col, w_col, mod, fmod, g_final, ys)


def _prep_w_in(w_in, d, q_lora, kv_lora):
    o_kpe = d + q_lora + kv_lora
    o_g = o_kpe + QK_ROPE
    half = QK_ROPE // 2
    kpe = w_in[:, o_kpe:o_kpe + QK_ROPE]
    zpad = jnp.zeros((d, 128 - QK_ROPE), w_in.dtype)
    ksw = jnp.concatenate([-kpe[:, half:], kpe[:, :half]], axis=1)
    return jnp.concatenate([w_in[:, :o_kpe], kpe, zpad, ksw, zpad, w_in[:, o_g:]], axis=1).astype(BF16)


def _prep_w_q(w_q_b):
    ql = w_q_b.shape[0]
    hd = QK_NOPE + QK_ROPE
    half = QK_ROPE // 2
    w = w_q_b.reshape(ql, N_HEADS, hd)
    nope = w[:, :, :QK_NOPE]
    pe = w[:, :, QK_NOPE:]
    sw = jnp.concatenate([-pe[:, :, half:], pe[:, :, :half]], axis=2)
    return jnp.concatenate([nope.reshape(ql, -1), pe.reshape(ql, -1), sw.reshape(ql, -1)],
                           axis=1).astype(BF16)


def _prep_w_kv(w_kv_b):
    kvl = w_kv_b.shape[0]
    w = w_kv_b.reshape(kvl, N_HEADS, QK_NOPE + V_HEAD)
    w_kn = w[:, :, :QK_NOPE].reshape(kvl, -1).astype(BF16)
    w_vt = w[:, :, QK_NOPE:].reshape(kvl, -1).T.astype(BF16)
    return w_kn, w_vt


def kernel(x, c, positions, w_mod, b_mod, g_mix, w_in, b_gate, w_pool_grp, pool_scale, w_pool_out,
           g_q_a, w_q_b, g_kv_a, w_kv_b, w_mla_out, w_out, g_ffn, w_router, b_router, w_gu, b_gu,
           w_down, b_down, g_final, w_fmod, b_fmod):
    bsz, seq, d = x.shape
    t = bsz * seq
    depth = w_mod.shape[0]
    assert depth == 1
    assert seq % TQ == 0 and seq % TM_IN == 0 and seq % TM_POST == 0 and seq % TM_FIN == 0
    q_lora = g_q_a.shape[-1]
    kv_lora = g_kv_a.shape[-1]
    n_exp = w_gu.shape[1]
    f = w_gu.shape[-1] // 2
    blk = MOE_BLK

    x2d = x.reshape(t, d)
    pos_col = positions.astype(F32).reshape(t, 1)
    inv_freq = 1.0 / (ROPE_THETA ** (jnp.arange(0, QK_ROPE, 2, dtype=F32) / QK_ROPE))
    invf2 = jnp.tile(inv_freq, 128 // (QK_ROPE // 2)).reshape(1, 128)

    mod = _mod_call(c, w_mod[0], b_mod[0]).reshape(bsz, 6, d)
    fmod = _mod_call(c, w_fmod, b_fmod).reshape(bsz, 2, d)

    w_in_p = _prep_w_in(w_in[0], d, q_lora, kv_lora)
    w_q_p = _prep_w_q(w_q_b[0])
    w_kn, w_vt = _prep_w_kv(w_kv_b[0])
    ga, g1, q2, k2, vt3 = _mixer_in_call(
        x2d, pos_col, mod, g_mix[0].reshape(1, d), w_in_p, b_gate[0].reshape(1, 2 * d),
        w_pool_grp[0].astype(BF16), pool_scale[0].reshape(1, d), w_pool_out[0].astype(BF16),
        g_q_a[0].reshape(1, q_lora), w_q_p, g_kv_a[0].reshape(1, kv_lora), w_kn, w_vt, invf2,
        bsz=bsz, seq=seq)

    hp = N_HEADS * HEAD_PAD
    o = _attn_call(q2.reshape(bsz, seq, hp), k2.reshape(bsz, seq, hp), vt3)

    w_r = w_router[0]
    w_r_hi = w_r.astype(BF16)
    w_r_lo = (w_r - w_r_hi.astype(F32)).astype(BF16)
    lane_pad = lambda a: jnp.pad(a, ((0, 0), (0, 128 - n_exp)))
    w_rt = jnp.concatenate(
        [jnp.concatenate([lane_pad(w_r_hi), lane_pad(w_r_lo)], axis=1),
         jnp.concatenate([lane_pad(w_r_hi), jnp.zeros((d, 128), BF16)], axis=1)], axis=0)
    x1, h2p, pos_t, wgt_t, cnt3 = _post_call(
        o.reshape(t, d), ga, g1, x2d, mod, w_mla_out[0].astype(BF16), w_out[0].astype(BF16),
        g_ffn[0].reshape(1, d), w_rt, b_router[0].reshape(n_exp, 1), seq=seq)

    n_slots = t * TOP_K
    n_rows = n_slots + n_exp * blk
    n_blocks = n_rows // blk
    cnt = cnt3[:, 0:TM_POST // TM_FIN, 0:n_exp].reshape(-1, n_exp)
    counts = jnp.sum(cnt, axis=0)
    padded = (counts + blk - 1) // blk * blk
    pad_end = jnp.cumsum(padded)
    pad_start = pad_end - padded
    blk_start = jnp.arange(n_blocks, dtype=jnp.int32) * blk
    blk_e = jnp.minimum(jnp.sum(pad_end[None, :] <= blk_start[:, None], axis=1, dtype=jnp.int32),
                        n_exp - 1)
    n_used = (pad_end[-1:] // blk).astype(jnp.int32)
    nvalid = jnp.clip(counts[blk_e] - (blk_start - pad_start[blk_e]), 0, blk)
    nvalid = jnp.where(blk_start < pad_end[-1], nvalid, 0).astype(jnp.int32)
    tail = jnp.stack([(pad_start + counts) * ROW_SUB, padded - counts], axis=0).astype(jnp.int32)

    before = jnp.cumsum(cnt, axis=0) - cnt
    run_start = jnp.cumsum(cnt, axis=1) - cnt
    tabs = jnp.stack([cnt, (pad_start[None, :] + before) * ROW_SUB, run_start * ROW_SUB], axis=1)

    col = jnp.arange(256, dtype=jnp.int32)[None, :]
    row = jnp.arange(256, dtype=jnp.int32)[:, None]
    perm = (row == jnp.where(col < 128, 2 * col, 2 * (col - 128) + 1)).astype(BF16)
    b_gu_p = jnp.concatenate([b_gu[0][:, 0::2], b_gu[0][:, 1::2]], axis=-1).reshape(n_exp, 1, 2 * f)
    xs = _dispatch_call(tabs, tail, n_used, h2p, pos_t, n_xs_rows=n_rows)
    ys = _moe_call(blk_e, nvalid, xs, perm, w_gu[0], b_gu_p, w_down[0], b_down[0].reshape(n_exp, 1, d))

    out = _final_call(tabs, x1, pos_t.T, wgt_t.T, mod, fmod, g_final.reshape(1, d), ys, seq=seq)
    return out.reshape(bsz, seq, d)
```

```python
import functools

import jax
import jax.numpy as jnp
from jax import lax
from jax.experimental import pallas as pl
from jax.experimental.pallas import tpu as pltpu

F32 = jnp.float32
BF16 = jnp.bfloat16

CHUNK = 64
POOL_WINDOWS = (2, 4, 8, 16)
POOL_HALO = 16
N_HEADS = 8
QK_NOPE = 128
QK_ROPE = 64
V_HEAD = 128
HEAD_PAD = 256
ROPE_THETA = 10000.0
TOP_K = 4
SWIGLU_LIMIT = 7.0
SWIGLU_ALPHA = 1.702
NORM_EPS = 1e-6
NEG_INF = -1e30
LOG2_E = 1.4426950408889634

VMEM_LIMIT = 56 * 1024 * 1024

TM_IN = 256
TQ = 512
TM_POST = 512
MOE_BLK = 512
TM_FIN = 256
TILES_PER_STEP = 4
ROW_SUB = 4


def _const_spec(shape):
    nd = len(shape)
    return pl.BlockSpec(shape, lambda *_: (0,) * nd, pipeline_mode=pl.Buffered(1))


def _rms(xf, g):
    return xf * lax.rsqrt(jnp.mean(xf * xf, axis=-1, keepdims=True) + NORM_EPS) * g


def _pack_bf16_pair(lo, hi):
    lo_b = lax.bitcast_convert_type(lo.astype(BF16).astype(F32), jnp.uint32)
    hi_b = lax.bitcast_convert_type(hi.astype(BF16).astype(F32), jnp.uint32)
    return (hi_b & jnp.uint32(0xFFFF0000)) | (lo_b >> 16)


def _unpack_bf16_pair(p):
    lo = lax.bitcast_convert_type(p << 16, F32)
    hi = lax.bitcast_convert_type(p & jnp.uint32(0xFFFF0000), F32)
    return lo, hi


def _store_rows(ref, packed):
    n = packed.shape[0]
    for q in range(ROW_SUB):
        ref[pl.ds(q, n, stride=ROW_SUB), :] = packed[:, q * 128:(q + 1) * 128]


def _load_rows(ref, n):
    los, his = [], []
    for q in range(ROW_SUB):
        lo, hi = _unpack_bf16_pair(ref[pl.ds(q, n, stride=ROW_SUB), :])
        los.append(lo)
        his.append(hi)
    return jnp.concatenate(los + his, axis=1)


def _mod_kernel(c_ref, w_ref, b_ref, o_ref):
    c = c_ref[...]
    c_act = c * jax.nn.sigmoid(c)
    o_ref[...] = jnp.dot(c_act, w_ref[...], preferred_element_type=F32,
                         precision=lax.Precision.HIGHEST) + b_ref[...]


def _mod_call(c, w, b, tn=1024):
    bsz, d = c.shape
    n = w.shape[1]
    return pl.pallas_call(
        _mod_kernel,
        grid=(n // tn,),
        in_specs=[pl.BlockSpec((bsz, d), lambda j: (0, 0)),
                  pl.BlockSpec((d, tn), lambda j: (0, j)),
                  pl.BlockSpec((1, tn), lambda j: (0, j))],
        out_specs=pl.BlockSpec((bsz, tn), lambda j: (0, j)),
        out_shape=jax.ShapeDtypeStruct((bsz, n), F32),
        compiler_params=pltpu.CompilerParams(dimension_semantics=("arbitrary",),
                                             vmem_limit_bytes=VMEM_LIMIT),
        name="adaln_mod",
    )(c, w, b.reshape(1, n))


def _mixer_in_kernel(x_ref, xh_ref, pos_ref, mod_ref, gmix_ref, win_ref, bgate_ref, wgrp_ref,
                     pscale_ref, wpo_ref, gq_ref, wq_ref, gkv_ref, wkn_ref, wvt_ref, invf_ref,
                     ga_ref, g1_ref, q_ref, k_ref, vt_ref, u_scr, *, tm, tiles_per_seq, d, q_lora,
                     kv_lora):
    i = pl.program_id(0)
    t_in_seq = i % tiles_per_seq
    is_start = t_in_seq == 0
    shift1 = mod_ref[0, 0:1, :]
    scale1 = mod_ref[0, 1:2, :]
    gmix = gmix_ref[...]

    def prenorm(xf):
        return (_rms(xf, gmix) * (1.0 + scale1) + shift1).astype(BF16)

    h = prenorm(x_ref[...])
    hh = prenorm(xh_ref[...])
    u_ext = jnp.dot(jnp.concatenate([hh, h], axis=0), win_ref[:, 0:d],
                    preferred_element_type=F32)
    u_scr[0:POOL_HALO, :] = jnp.where(is_start, 0.0, u_ext[0:POOL_HALO, :])
    u_scr[POOL_HALO:POOL_HALO + tm, :] = u_ext[POOL_HALO:, :]
    rest = jnp.dot(h, win_ref[:, d:], preferred_element_type=F32)

    gw = d // len(POOL_WINDOWS)
    tseq = t_in_seq * tm + lax.broadcasted_iota(jnp.int32, (tm, 1), 0)
    ys = []
    for g, w in enumerate(POOL_WINDOWS):
        c0 = g * gw
        ug = u_scr[POOL_HALO:POOL_HALO + tm, c0:c0 + gw]
        acc = ug
        for j in range(1, w):
            acc = acc + u_scr[POOL_HALO - j:POOL_HALO - j + tm, c0:c0 + gw]
        cnt = jnp.minimum(tseq + 1, w).astype(F32)
        mixed = (acc / cnt - ug).astype(BF16)
        ys.append(jnp.dot(mixed, wgrp_ref[g], preferred_element_type=F32))
    y = (jnp.concatenate(ys, axis=1) * pscale_ref[...]).astype(BF16)
    a = jnp.dot(y, wpo_ref[...], preferred_element_type=F32)

    o_q = 0
    o_kv = q_lora
    o_kpe = q_lora + kv_lora
    o_ksw = o_kpe + 128
    o_g0 = o_ksw + 128
    o_g1 = o_g0 + d
    gates0 = jax.nn.sigmoid(rest[:, o_g0:o_g0 + d] + bgate_ref[:, 0:d])
    gates1 = jax.nn.sigmoid(rest[:, o_g1:o_g1 + d] + bgate_ref[:, d:2 * d])
    ga_ref[...] = (gates0 * a).astype(BF16)
    g1_ref[...] = gates1.astype(BF16)

    ang = pos_ref[...] * invf_ref[...]
    cos2 = jnp.cos(ang)
    sin2 = jnp.sin(ang)
    q_scale = float(QK_NOPE + QK_ROPE) ** -0.5 * LOG2_E

    qn = _rms(rest[:, o_q:o_q + q_lora], gq_ref[...]).astype(BF16)
    qall = jnp.dot(qn, wq_ref[...], preferred_element_type=F32)
    kvn = _rms(rest[:, o_kv:o_kv + kv_lora], gkv_ref[...]).astype(BF16)
    kn = jnp.dot(kvn, wkn_ref[...], preferred_element_type=F32)
    vt = lax.dot_general(wvt_ref[...], kvn, (((1,), (1,)), ((), ())),
                         preferred_element_type=F32)
    vt_ref[0] = vt.astype(BF16)
    kpe = (rest[:, o_kpe:o_kpe + 128] * cos2 + rest[:, o_ksw:o_ksw + 128] * sin2).astype(BF16)
    o_pe = N_HEADS * QK_NOPE
    o_sw = o_pe + N_HEADS * QK_ROPE
    low_half = lax.broadcasted_iota(jnp.int32, (tm, 128), 1) < QK_ROPE
    for pair in range(N_HEADS // 2):
        c0 = pair * 128
        qpe2 = (qall[:, o_pe + c0:o_pe + c0 + 128] * cos2
                + qall[:, o_sw + c0:o_sw + c0 + 128] * sin2) * q_scale
        for hd, part in ((2 * pair, qpe2), (2 * pair + 1, pltpu.roll(qpe2, QK_ROPE, axis=1))):
            n0 = hd * QK_NOPE
            q_ref[:, hd * HEAD_PAD:hd * HEAD_PAD + 128] = (qall[:, n0:n0 + 128] * q_scale).astype(BF16)
            q_ref[:, hd * HEAD_PAD + 128:(hd + 1) * HEAD_PAD] = jnp.where(low_half, part, 0.0).astype(BF16)
            k_ref[:, hd * HEAD_PAD:hd * HEAD_PAD + 128] = kn[:, n0:n0 + 128].astype(BF16)
            k_ref[:, hd * HEAD_PAD + 128:(hd + 1) * HEAD_PAD] = kpe


def _mixer_in_call(x2d, pos_col, mod, g_mix, w_in_p, b_gate, w_grp, pool_scale, w_po, g_q_a, w_q_p,
                   g_kv_a, w_kn, w_vt, invf2, *, bsz, seq):
    t, d = x2d.shape
    tm = TM_IN
    tps = seq // tm
    q_lora = g_q_a.shape[-1]
    kv_lora = g_kv_a.shape[-1]
    hp = N_HEADS * HEAD_PAD
    halo_blocks = tm // POOL_HALO
    kern = functools.partial(_mixer_in_kernel, tm=tm, tiles_per_seq=tps, d=d, q_lora=q_lora,
                             kv_lora=kv_lora)
    row = lambda i: (i, 0)
    return pl.pallas_call(
        kern,
        grid=(t // tm,),
        in_specs=[
            pl.BlockSpec((tm, d), row),
            pl.BlockSpec((POOL_HALO, d), lambda i: (jnp.maximum(i * halo_blocks - 1, 0), 0)),
            pl.BlockSpec((tm, 1), row),
            pl.BlockSpec((1, 6, d), lambda i: (i // tps, 0, 0)),
            _const_spec((1, d)),
            _const_spec(w_in_p.shape),
            _const_spec((1, 2 * d)),
            _const_spec(w_grp.shape),
            _const_spec((1, d)),
            _const_spec(w_po.shape),
            _const_spec((1, q_lora)),
            _const_spec(w_q_p.shape),
            _const_spec((1, kv_lora)),
            _const_spec(w_kn.shape),
            _const_spec(w_vt.shape),
            _const_spec((1, 128)),
        ],
        out_specs=[
            pl.BlockSpec((tm, d), row),
            pl.BlockSpec((tm, d), row),
            pl.BlockSpec((tm, hp), row),
            pl.BlockSpec((tm, hp), row),
            pl.BlockSpec((1, N_HEADS * V_HEAD, tm), lambda i: (i // tps, 0, i % tps)),
        ],
        out_shape=[
            jax.ShapeDtypeStruct((t, d), BF16),
            jax.ShapeDtypeStruct((t, d), BF16),
            jax.ShapeDtypeStruct((t, hp), BF16),
            jax.ShapeDtypeStruct((t, hp), BF16),
            jax.ShapeDtypeStruct((bsz, N_HEADS * V_HEAD, seq), BF16),
        ],
        scratch_shapes=[pltpu.VMEM((tm + POOL_HALO, d), F32)],
        compiler_params=pltpu.CompilerParams(dimension_semantics=("arbitrary",),
                                             vmem_limit_bytes=VMEM_LIMIT),
        name="mixer_in",
    )(x2d, x2d, pos_col, mod, g_mix, w_in_p, b_gate, w_grp, pool_scale, w_po, g_q_a, w_q_p, g_kv_a,
      w_kn, w_vt, invf2)


def _attn_kernel(q_ref, k_ref, vt_ref, o_ref, *, tq, nq):
    seq = vt_ref.shape[2]
    vt_ext = jnp.concatenate([vt_ref[0], jnp.ones((16, seq), BF16)], axis=0)

    def scores(qi):
        q0 = qi * tq
        ln = q0 + tq
        return lax.dot_general(k_ref[0, 0:ln, :], q_ref[0, q0:q0 + tq, :], (((1,), (1,)), ((), ())),
                               preferred_element_type=F32)

    def finish(qi, s):
        q0 = qi * tq
        ln = q0 + tq
        cw = tq // 2
        for c in range(2):
            cs = slice(c * cw, (c + 1) * cw)
            kc = (q0 + lax.broadcasted_iota(jnp.int32, (tq, cw), 0)) // CHUNK
            qc = (q0 + c * cw + lax.broadcasted_iota(jnp.int32, (tq, cw), 1)) // CHUNK
            s_diag = jnp.where(qc >= kc, s[q0:ln, cs], NEG_INF)
            m = jnp.max(s_diag, axis=0, keepdims=True)
            if qi > 0:
                s_main = s[0:q0, cs]
                m = jnp.maximum(m, jnp.max(s_main, axis=0, keepdims=True))
            p_diag = jnp.exp2(s_diag - m).astype(BF16)
            acc = jnp.dot(vt_ext[:, q0:ln], p_diag, preferred_element_type=F32)
            if qi > 0:
                p_main = jnp.exp2(s_main - m).astype(BF16)
                acc = acc + jnp.dot(vt_ext[:, 0:q0], p_main, preferred_element_type=F32)
            l = acc[V_HEAD:V_HEAD + 1, :]
            o_ref[0, q0 + c * cw:q0 + (c + 1) * cw, :] = (acc[0:V_HEAD, :] / l).T.astype(BF16)

    order = list(range(nq))[::-1]
    s_cur = scores(order[0])
    for idx, qi in enumerate(order):
        s_next = scores(order[idx + 1]) if idx + 1 < nq else None
        finish(qi, s_cur)
        s_cur = s_next


def _attn_call(q3, k3, vt3):
    bsz, seq, _ = q3.shape
    kern = functools.partial(_attn_kernel, tq=TQ, nq=seq // TQ)
    return pl.pallas_call(
        kern,
        grid=(bsz, N_HEADS),
        in_specs=[pl.BlockSpec((1, seq, HEAD_PAD), lambda b, h: (b, 0, h)),
                  pl.BlockSpec((1, seq, HEAD_PAD), lambda b, h: (b, 0, h)),
                  pl.BlockSpec((1, V_HEAD, seq), lambda b, h: (b, h, 0))],
        out_specs=pl.BlockSpec((1, seq, V_HEAD), lambda b, h: (b, 0, h)),
        out_shape=jax.ShapeDtypeStruct((bsz, seq, N_HEADS * V_HEAD), BF16),
        compiler_params=pltpu.CompilerParams(dimension_semantics=("arbitrary", "arbitrary"),
                                             vmem_limit_bytes=VMEM_LIMIT),
        name="mla_attn",
    )(q3, k3, vt3)


def _post_kernel(o_ref, ga_ref, g1_ref, x_ref, mod_ref, wmo_ref, wout_ref, gffn_ref, wrt_ref, br_ref,
                 x1_ref, h2_ref, pos_ref, wgt_ref, cnt_ref, *, d):
    m = jnp.dot(o_ref[...], wmo_ref[...], preferred_element_type=F32)
    merged = ga_ref[...].astype(F32) + g1_ref[...].astype(F32) * m
    gate1 = mod_ref[0, 2:3, :]
    x1 = x_ref[...] + gate1 * jnp.dot(merged.astype(BF16), wout_ref[...],
                                      preferred_element_type=F32)
    x1_ref[...] = x1
    shift2 = mod_ref[0, 3:4, :]
    scale2 = mod_ref[0, 4:5, :]
    h2 = _rms(x1, gffn_ref[...]) * (1.0 + scale2) + shift2
    _store_rows(h2_ref, _pack_bf16_pair(h2[:, 0:d // 2], h2[:, d // 2:d]))

    ne = br_ref.shape[0]
    h_hi = h2.astype(BF16)
    h_lo = (h2 - h_hi.astype(F32)).astype(BF16)
    hcat = jnp.concatenate([h_hi, h_lo], axis=1)
    half = hcat.shape[0] // 2
    lg2 = jnp.concatenate(
        [jnp.dot(hcat[0:half], wrt_ref[...], preferred_element_type=F32),
         jnp.dot(hcat[half:], wrt_ref[...], preferred_element_type=F32)], axis=0)
    lg = lg2[:, 0:128] + lg2[:, 128:256]
    logits = lg.T[0:ne, :] + br_ref[...]
    tm = logits.shape[1]
    eid = lax.broadcasted_iota(jnp.int32, (ne, tm), 0)
    vals, idxs = [], []
    cur = logits
    for _ in range(TOP_K):
        mx = jnp.max(cur, axis=0, keepdims=True)
        ix = jnp.min(jnp.where(cur == mx, eid, ne), axis=0, keepdims=True)
        vals.append(mx)
        idxs.append(ix)
        cur = jnp.where(eid == ix, -jnp.inf, cur)
    es = [jnp.exp(v - vals[0]) for v in vals]
    den = es[0] + es[1] + es[2] + es[3]
    wgt_ref[...] = jnp.concatenate([e / den for e in es], axis=0)

    tt = TM_FIN
    t_row = lax.broadcasted_iota(jnp.int32, (tm, tm), 0)
    t_col = lax.broadcasted_iota(jnp.int32, (tm, tm), 1)
    same_tile = (t_row // tt) == (t_col // tt)
    in_tile = jnp.where(same_tile, 1.0, 0.0).astype(BF16)
    earlier = jnp.where(same_tile & (t_row < t_col), 1.0, 0.0).astype(BF16)
    lower = jnp.where(lax.broadcasted_iota(jnp.int32, (ne, ne), 1)
                      < lax.broadcasted_iota(jnp.int32, (ne, ne), 0), 1.0, 0.0).astype(BF16)
    picks = [eid == ix for ix in idxs]
    routed = jnp.where(picks[0] | picks[1] | picks[2] | picks[3], 1.0, 0.0).astype(BF16)
    before = jnp.dot(routed, earlier, preferred_element_type=F32)
    cnt_b = jnp.dot(routed, in_tile, preferred_element_type=F32)
    run_start = jnp.dot(lower, cnt_b.astype(BF16), preferred_element_type=F32)
    base = run_start + before
    pos_ref[...] = jnp.concatenate(
        [jnp.sum(jnp.where(pk, base, 0.0), axis=0, keepdims=True) for pk in picks],
        axis=0).astype(jnp.int32)
    tile_sel = jnp.where(lax.broadcasted_iota(jnp.int32, (8, tm), 0)
                         == lax.broadcasted_iota(jnp.int32, (8, tm), 1) // tt, 1.0, 0.0).astype(BF16)
    cnt8 = lax.dot_general(tile_sel, routed, (((1,), (1,)), ((), ())),
                           preferred_element_type=F32)
    cnt_ref[0] = jnp.concatenate([cnt8, jnp.zeros((8, 128 - ne), F32)], axis=1).astype(jnp.int32)


def _post_call(o2d, ga, g1, x2d, mod, w_mo, w_out, g_ffn, w_rt, b_r, *, seq):
    t, d = x2d.shape
    tm = TM_POST
    tps = seq // tm
    row = lambda i: (i, 0)
    n_exp = b_r.shape[0]
    assert tm % TM_FIN == 0 and tm // TM_FIN <= 8 and n_exp <= 128
    kern = functools.partial(_post_kernel, d=d)
    return pl.pallas_call(
        kern,
        grid=(t // tm,),
        in_specs=[
            pl.BlockSpec((tm, d), row),
            pl.BlockSpec((tm, d), row),
            pl.BlockSpec((tm, d), row),
            pl.BlockSpec((tm, d), row),
            pl.BlockSpec((1, 6, d), lambda i: (i // tps, 0, 0)),
            _const_spec(w_mo.shape),
            _const_spec(w_out.shape),
            _const_spec((1, d)),
            _const_spec(w_rt.shape),
            _const_spec(b_r.shape),
        ],
        out_specs=[
            pl.BlockSpec((tm, d), row),
            pl.BlockSpec((tm * ROW_SUB, 128), row),
            pl.BlockSpec((TOP_K, tm), lambda i: (0, i)),
            pl.BlockSpec((TOP_K, tm), lambda i: (0, i)),
            pl.BlockSpec((1, 8, 128), lambda i: (i, 0, 0)),
        ],
        out_shape=[
            jax.ShapeDtypeStruct((t, d), F32),
            jax.ShapeDtypeStruct((t * ROW_SUB, 128), jnp.uint32),
            jax.ShapeDtypeStruct((TOP_K, t), jnp.int32),
            jax.ShapeDtypeStruct((TOP_K, t), F32),
            jax.ShapeDtypeStruct((t // tm, 8, 128), jnp.int32),
        ],
        compiler_params=pltpu.CompilerParams(dimension_semantics=("arbitrary",),
                                             vmem_limit_bytes=VMEM_LIMIT),
        name="post_attn_router",
    )(o2d, ga, g1, x2d, mod, w_mo, w_out, g_ffn, w_rt, b_r)


RUN_BITS = 9
COMMON_RUN_BITS = 6
TAIL_BITS = 9


def _copy_pieces(count, src_ref, src_off, dst_ref, dst_off, sem, bits, wait=False):
    def pieces(bit_range, off):
        for b in reversed(bit_range):
            size = (1 << b) * ROW_SUB
            take = (count >> b) & 1

            @pl.when(take == 1)
            def _(off=off, size=size):
                cp = pltpu.make_async_copy(
                    src_ref.at[pl.ds(pl.multiple_of(src_off + off, ROW_SUB), size)],
                    dst_ref.at[pl.ds(pl.multiple_of(dst_off + off, ROW_SUB), size)], sem)
                if wait:
                    cp.wait()
                else:
                    cp.start()
            off = off + take * size

    lo = min(bits, COMMON_RUN_BITS)
    big = (count >> lo) << lo

    if lo < bits:
        @pl.when(big != 0)
        def _():
            pieces(range(lo, bits), 0)
    pieces(range(lo), big * ROW_SUB)


def _dispatch_kernel(tab_ref, tail_ref, nused_ref, h_ref, pos_ref, xs_hbm, stage, zbuf, sem, zsem, *,
                     tm, n_exp, n_blocks):
    i = pl.program_id(0)
    n = pl.num_programs(0)
    slot = i % 2
    n_rows = TOP_K * tm

    def wait_stage(slot_):
        for j in range(TILES_PER_STEP):
            pltpu.make_async_copy(stage.at[slot_, j], xs_hbm.at[pl.ds(0, n_rows * ROW_SUB)],
                                  sem.at[slot_]).wait()

    @pl.when(i == 0)
    def _():
        zbuf[...] = jnp.zeros(zbuf.shape, zbuf.dtype)
        for wait in (False, True):
            def body(e, c, wait=wait):
                _copy_pieces(tail_ref[1, e], zbuf, 0, xs_hbm, tail_ref[0, e], zsem, TAIL_BITS, wait)
                return c
            lax.fori_loop(0, n_exp, body, 0)
        blk_rows = zbuf.shape[0]

        def zero_block(b, c):
            cp = pltpu.make_async_copy(
                zbuf, xs_hbm.at[pl.ds(pl.multiple_of(b * blk_rows, blk_rows), blk_rows)], zsem)
            cp.start()
            cp.wait()
            return c
        lax.fori_loop(nused_ref[0], n_blocks, zero_block, 0)

    @pl.when(i >= 2)
    def _():
        wait_stage(slot)

    tiles = range(TILES_PER_STEP)
    hs = [_load_rows(h_ref.at[pl.ds(j * tm * ROW_SUB, tm * ROW_SUB)], tm).astype(BF16)
          for j in tiles]
    row_id = lax.broadcasted_iota(jnp.int32, (n_rows, tm), 0)
    onehots = []
    for j in tiles:
        hit = row_id == pos_ref[0:1, j * tm:(j + 1) * tm]
        for k in range(1, TOP_K):
            hit = hit | (row_id == pos_ref[k:k + 1, j * tm:(j + 1) * tm])
        onehots.append(jnp.where(hit, 1.0, 0.0).astype(BF16))
    rows = [jnp.dot(onehots[j], hs[j], preferred_element_type=F32) for j in tiles]
    for j in tiles:
        dh = rows[j].shape[1] // 2
        _store_rows(stage.at[slot, j], _pack_bf16_pair(rows[j][:, 0:dh], rows[j][:, dh:]))

    for j in tiles:
        def body(e, c, j=j):
            _copy_pieces(tab_ref[j, 0, e], stage.at[slot, j], tab_ref[j, 2, e], xs_hbm,
                         tab_ref[j, 1, e], sem.at[slot], RUN_BITS)
            return c
        lax.fori_loop(0, n_exp, body, 0)

    @pl.when(i == n - 1)
    def _():
        wait_stage(slot)
        wait_stage(1 - slot)


def _dispatch_call(tabs, tail, n_used, h2p, pos_t, *, n_xs_rows):
    n_tiles, _, n_exp = tabs.shape
    tm = TM_FIN
    tps = TILES_PER_STEP
    assert n_tiles % tps == 0
    nt = n_tiles // tps
    assert nt >= 2 and tm < (1 << RUN_BITS) and MOE_BLK <= (1 << TAIL_BITS)
    n_rows = TOP_K * tm
    kern = functools.partial(_dispatch_kernel, tm=tm, n_exp=n_exp, n_blocks=n_xs_rows // MOE_BLK)
    return pl.pallas_call(
        kern,
        grid=(nt,),
        in_specs=[
            pl.BlockSpec((tps, 3, n_exp), lambda i: (i, 0, 0), memory_space=pltpu.SMEM),
            pl.BlockSpec(memory_space=pltpu.SMEM),
            pl.BlockSpec(memory_space=pltpu.SMEM),
            pl.BlockSpec((tps * tm * ROW_SUB, 128), lambda i: (i, 0)),
            pl.BlockSpec((TOP_K, tps * tm), lambda i: (0, i)),
        ],
        out_specs=pl.BlockSpec(memory_space=pl.ANY),
        out_shape=jax.ShapeDtypeStruct((n_xs_rows * ROW_SUB, 128), jnp.uint32),
        scratch_shapes=[pltpu.VMEM((2, tps, n_rows * ROW_SUB, 128), jnp.uint32),
                        pltpu.VMEM((MOE_BLK * ROW_SUB, 128), jnp.uint32),
                        pltpu.SemaphoreType.DMA((2,)),
                        pltpu.SemaphoreType.DMA(())],
        compiler_params=pltpu.CompilerParams(dimension_semantics=("arbitrary",),
                                             vmem_limit_bytes=VMEM_LIMIT),
        name="moe_dispatch",
    )(tabs, tail, n_used, h2p, pos_t)


def _moe_kernel(blk_e_ref, nused_ref, x_ref, perm_ref, wgu_ref, bgu_ref, wd_ref, bd_ref, o_ref,
                wgu_scr, wd_scr, *, blk, f):
    s = pl.program_id(0)
    e_cur = blk_e_ref[s]
    e_prev = blk_e_ref[jnp.maximum(s - 1, 0)]

    @pl.when(jnp.logical_or(s == 0, e_cur != e_prev))
    def _():
        perm = perm_ref[...]
        for c in range(2 * f // 256):
            r = jnp.dot(wgu_ref[0, :, c * 256:(c + 1) * 256].astype(BF16), perm,
                        preferred_element_type=F32)
            wgu_scr[:, c * 128:(c + 1) * 128] = r[:, 0:128].astype(BF16)
            wgu_scr[:, f + c * 128:f + (c + 1) * 128] = r[:, 128:256].astype(BF16)
        wd_scr[...] = wd_ref[0].astype(BF16)

    @pl.when(s >= nused_ref[0])
    def _():
        o_ref[...] = jnp.zeros(o_ref.shape, o_ref.dtype)

    @pl.when(s < nused_ref[0])
    def _():
        x = _load_rows(x_ref, blk).astype(BF16)
        gu = jnp.dot(x, wgu_scr[...], preferred_element_type=F32) + bgu_ref[0]
        gate = jnp.minimum(gu[:, 0:f], SWIGLU_LIMIT)
        up = jnp.clip(gu[:, f:2 * f], -SWIGLU_LIMIT, SWIGLU_LIMIT)
        act = (up + 1.0) * (gate * jax.nn.sigmoid(SWIGLU_ALPHA * gate))
        y = jnp.dot(act.astype(BF16), wd_scr[...], preferred_element_type=F32) + bd_ref[0]
        dh = y.shape[1] // 2
        _store_rows(o_ref, _pack_bf16_pair(y[:, 0:dh], y[:, dh:]))


def _moe_call(blk_e, n_used, xs, perm, w_gu, b_gu_p, w_down, b_down):
    n_steps = blk_e.shape[0]
    blk = MOE_BLK
    _, d, f2 = w_gu.shape
    f = f2 // 2
    rows = blk * ROW_SUB
    kern = functools.partial(_moe_kernel, blk=blk, f=f)
    row_blk = lambda i, be, nu: (i, 0)
    grid_spec = pltpu.PrefetchScalarGridSpec(
        num_scalar_prefetch=2,
        grid=(n_steps,),
        in_specs=[
            pl.BlockSpec((rows, 128), row_blk),
            pl.BlockSpec((256, 256), lambda i, be, nu: (0, 0)),
            pl.BlockSpec((1, d, f2), lambda i, be, nu: (be[i], 0, 0)),
            pl.BlockSpec((1, 1, f2), lambda i, be, nu: (be[i], 0, 0)),
            pl.BlockSpec((1, f, d), lambda i, be, nu: (be[i], 0, 0)),
            pl.BlockSpec((1, 1, d), lambda i, be, nu: (be[i], 0, 0)),
        ],
        out_specs=pl.BlockSpec((rows, 128), row_blk),
        scratch_shapes=[
            pltpu.VMEM((d, f2), BF16),
            pltpu.VMEM((f, d), BF16),
        ],
    )
    return pl.pallas_call(
        kern,
        grid_spec=grid_spec,
        out_shape=jax.ShapeDtypeStruct((n_steps * rows, 128), jnp.uint32),
        compiler_params=pltpu.CompilerParams(dimension_semantics=("arbitrary",),
                                             vmem_limit_bytes=VMEM_LIMIT),
        name="moe_experts",
    )(blk_e, n_used, xs, perm, w_gu, b_gu_p, w_down, b_down)


def _final_kernel(tab_cur_ref, tab_nxt_ref, x1_ref, pos_ref, w_ref, mod_ref, fmod_ref, gfin_ref, ys_hbm,
                  o_ref, stage, sem, *, tm, n_exp):
    i = pl.program_id(0)
    n = pl.num_programs(0)
    slot = i % 2
    n_rows = TOP_K * tm

    tiles = range(TILES_PER_STEP)

    def issue_tiles(tab_ref, slot_):
        for j in tiles:
            def body(e, c, j=j):
                _copy_pieces(tab_ref[j, 0, e], ys_hbm, tab_ref[j, 1, e], stage.at[slot_, j],
                             tab_ref[j, 2, e], sem.at[slot_], RUN_BITS)
                return c
            lax.fori_loop(0, n_exp, body, 0)

    @pl.when(i == 0)
    def _():
        issue_tiles(tab_cur_ref, 0)

    @pl.when(i + 1 < n)
    def _():
        issue_tiles(tab_nxt_ref, 1 - slot)

    for j in tiles:
        pltpu.make_async_copy(ys_hbm.at[pl.ds(0, n_rows * ROW_SUB)], stage.at[slot, j],
                              sem.at[slot]).wait()

    ys = [_load_rows(stage.at[slot, j], n_rows).astype(BF16) for j in tiles]
    col_id = lax.broadcasted_iota(jnp.int32, (tm, n_rows), 1)
    wmats = []
    for j in tiles:
        pos = pos_ref[j * tm:(j + 1) * tm, :]
        w = w_ref[j * tm:(j + 1) * tm, :]
        a = jnp.where(col_id == pos[:, 0:1], w[:, 0:1], 0.0)
        for k in range(1, TOP_K):
            a = a + jnp.where(col_id == pos[:, k:k + 1], w[:, k:k + 1], 0.0)
        wmats.append(a.astype(BF16))
    moes = [jnp.dot(wmats[j], ys[j], preferred_element_type=F32) for j in tiles]
    gate2 = mod_ref[0, 5:6, :]
    fshift = fmod_ref[0, 0:1, :]
    fscale = fmod_ref[0, 1:2, :]
    for j in tiles:
        x2 = x1_ref[j * tm:(j + 1) * tm, :] + gate2 * moes[j]
        o_ref[j * tm:(j + 1) * tm, :] = _rms(x2, gfin_ref[...]) * (1.0 + fscale) + fshift


def _final_call(tabs, x1, pos_col, w_col, mod, fmod, g_final, ys, *, seq):
    t, d = x1.shape
    tm = TM_FIN
    assert tm < (1 << RUN_BITS)
    tls = TILES_PER_STEP
    tms = tls * tm
    assert seq % tms == 0
    tps = seq // tms
    nt = t // tms
    n_exp = tabs.shape[2]
    n_rows = TOP_K * tm
    kern = functools.partial(_final_kernel, tm=tm, n_exp=n_exp)
    tab_spec = lambda ahead: pl.BlockSpec((tls, 3, n_exp), lambda i: (jnp.minimum(i + ahead, nt - 1), 0, 0),
                                          memory_space=pltpu.SMEM)
    return pl.pallas_call(
        kern,
        grid=(nt,),
        in_specs=[
            tab_spec(0),
            tab_spec(1),
            pl.BlockSpec((tms, d), lambda i: (i, 0)),
            pl.BlockSpec((tms, TOP_K), lambda i: (i, 0)),
            pl.BlockSpec((tms, TOP_K), lambda i: (i, 0)),
            pl.BlockSpec((1, 6, d), lambda i: (i // tps, 0, 0)),
            pl.BlockSpec((1, 2, d), lambda i: (i // tps, 0, 0)),
            _const_spec((1, d)),
            pl.BlockSpec(memory_space=pl.ANY),
        ],
        out_specs=pl.BlockSpec((tms, d), lambda i: (i, 0)),
        out_shape=jax.ShapeDtypeStruct((t, d), F32),
        scratch_shapes=[pltpu.VMEM((2, tls, n_rows * ROW_SUB, 128), jnp.uint32),
                        pltpu.SemaphoreType.DMA((2,))],
        compiler_params=pltpu.CompilerParams(dimension_semantics=("arbitrary",),
                                             vmem_limit_bytes=VMEM_LIMIT),
        name="combine_final",
    )(tabs, tabs, x1, pos_col, w_col, mod, fmod, g_final, ys)


def _prep_w_in(w_in, d, q_lora, kv_lora):
    o_kpe = d + q_lora + kv_lora
    o_g = o_kpe + QK_ROPE
    half = QK_ROPE // 2
    kpe = w_in[:, o_kpe:o_kpe + QK_ROPE]
    zpad = jnp.zeros((d, 128 - QK_ROPE), w_in.dtype)
    ksw = jnp.concatenate([-kpe[:, half:], kpe[:, :half]], axis=1)
    return jnp.concatenate([w_in[:, :o_kpe], kpe, zpad, ksw, zpad, w_in[:, o_g:]], axis=1).astype(BF16)


def _prep_w_q(w_q_b):
    ql = w_q_b.shape[0]
    hd = QK_NOPE + QK_ROPE
    half = QK_ROPE // 2
    w = w_q_b.reshape(ql, N_HEADS, hd)
    nope = w[:, :, :QK_NOPE]
    pe = w[:, :, QK_NOPE:]
    sw = jnp.concatenate([-pe[:, :, half:], pe[:, :, :half]], axis=2)
    return jnp.concatenate([nope.reshape(ql, -1), pe.reshape(ql, -1), sw.reshape(ql, -1)],
                           axis=1).astype(BF16)


def _prep_w_kv(w_kv_b):
    kvl = w_kv_b.shape[0]
    w = w_kv_b.reshape(kvl, N_HEADS, QK_NOPE + V_HEAD)
    w_kn = w[:, :, :QK_NOPE].reshape(kvl, -1).astype(BF16)
    w_vt = w[:, :, QK_NOPE:].reshape(kvl, -1).T.astype(BF16)
    return w_kn, w_vt


def kernel(x, c, positions, w_mod, b_mod, g_mix, w_in, b_gate, w_pool_grp, pool_scale, w_pool_out,
           g_q_a, w_q_b, g_kv_a, w_kv_b, w_mla_out, w_out, g_ffn, w_router, b_router, w_gu, b_gu,
           w_down, b_down, g_final, w_fmod, b_fmod):
    bsz, seq, d = x.shape
    t = bsz * seq
    depth = w_mod.shape[0]
    assert depth == 1
    assert seq % TQ == 0 and seq % TM_IN == 0 and seq % TM_POST == 0 and seq % TM_FIN == 0
    q_lora = g_q_a.shape[-1]
    kv_lora = g_kv_a.shape[-1]
    n_exp = w_gu.shape[1]
    f = w_gu.shape[-1] // 2
    blk = MOE_BLK

    x2d = x.reshape(t, d)
    pos_col = positions.astype(F32).reshape(t, 1)
    inv_freq = 1.0 / (ROPE_THETA ** (jnp.arange(0, QK_ROPE, 2, dtype=F32) / QK_ROPE))
    invf2 = jnp.tile(inv_freq, 128 // (QK_ROPE // 2)).reshape(1, 128)

    mod = _mod_call(c, w_mod[0], b_mod[0]).reshape(bsz, 6, d)
    fmod = _mod_call(c, w_fmod, b_fmod).reshape(bsz, 2, d)

    w_in_p = _prep_w_in(w_in[0], d, q_lora, kv_lora)
    w_q_p = _prep_w_q(w_q_b[0])
    w_kn, w_vt = _prep_w_kv(w_kv_b[0])
    ga, g1, q2, k2, vt3 = _mixer_in_call(
        x2d, pos_col, mod, g_mix[0].reshape(1, d), w_in_p, b_gate[0].reshape(1, 2 * d),
        w_pool_grp[0].astype(BF16), pool_scale[0].reshape(1, d), w_pool_out[0].astype(BF16),
        g_q_a[0].reshape(1, q_lora), w_q_p, g_kv_a[0].reshape(1, kv_lora), w_kn, w_vt, invf2,
        bsz=bsz, seq=seq)

    hp = N_HEADS * HEAD_PAD
    o = _attn_call(q2.reshape(bsz, seq, hp), k2.reshape(bsz, seq, hp), vt3)

    w_r = w_router[0]
    w_r_hi = w_r.astype(BF16)
    w_r_lo = (w_r - w_r_hi.astype(F32)).astype(BF16)
    lane_pad = lambda a: jnp.pad(a, ((0, 0), (0, 128 - n_exp)))
    w_rt = jnp.concatenate(
        [jnp.concatenate([lane_pad(w_r_hi), lane_pad(w_r_lo)], axis=1),
         jnp.concatenate([lane_pad(w_r_hi), jnp.zeros((d, 128), BF16)], axis=1)], axis=0)
    x1, h2p, pos_t, wgt_t, cnt3 = _post_call(
        o.reshape(t, d), ga, g1, x2d, mod, w_mla_out[0].astype(BF16), w_out[0].astype(BF16),
        g_ffn[0].reshape(1, d), w_rt, b_router[0].reshape(n_exp, 1), seq=seq)

    n_slots = t * TOP_K
    n_rows = n_slots + n_exp * blk
    n_blocks = n_rows // blk
    cnt = cnt3[:, 0:TM_POST // TM_FIN, 0:n_exp].reshape(-1, n_exp)
    counts = jnp.sum(cnt, axis=0)
    padded = (counts + blk - 1) // blk * blk
    pad_end = jnp.cumsum(padded)
    pad_start = pad_end - padded
    blk_start = jnp.arange(n_blocks, dtype=jnp.int32) * blk
    blk_e = jnp.minimum(jnp.sum(pad_end[None, :] <= blk_start[:, None], axis=1, dtype=jnp.int32),
                        n_exp - 1)
    n_used = (pad_end[-1:] // blk).astype(jnp.int32)
    tail = jnp.stack([(pad_start + counts) * ROW_SUB, padded - counts], axis=0).astype(jnp.int32)

    before = jnp.cumsum(cnt, axis=0) - cnt
    run_start = jnp.cumsum(cnt, axis=1) - cnt
    tabs = jnp.stack([cnt, (pad_start[None, :] + before) * ROW_SUB, run_start * ROW_SUB], axis=1)

    col = jnp.arange(256, dtype=jnp.int32)[None, :]
    row = jnp.arange(256, dtype=jnp.int32)[:, None]
    perm = (row == jnp.where(col < 128, 2 * col, 2 * (col - 128) + 1)).astype(BF16)
    b_gu_p = jnp.concatenate([b_gu[0][:, 0::2], b_gu[0][:, 1::2]], axis=-1).reshape(n_exp, 1, 2 * f)
    xs = _dispatch_call(tabs, tail, n_used, h2p, pos_t, n_xs_rows=n_rows)
    ys = _moe_call(blk_e, n_used, xs, perm, w_gu[0], b_gu_p, w_down[0], b_down[0].reshape(n_exp, 1, d))

    out = _final_call(tabs, x1, pos_t.T, wgt_t.T, mod, fmod, g_final.reshape(1, d), ys, seq=seq)
    return out.reshape(bsz, seq, d)
```

```python
import functools

import jax
import jax.numpy as jnp
from jax import lax
from jax.experimental import pallas as pl
from jax.experimental.pallas import tpu as pltpu

F32 = jnp.float32
BF16 = jnp.bfloat16

CHUNK = 64
POOL_WINDOWS = (2, 4, 8, 16)
POOL_HALO = 16
N_HEADS = 8
QK_NOPE = 128
QK_ROPE = 64
V_HEAD = 128
HEAD_PAD = 256
ROPE_THETA = 10000.0
TOP_K = 4
SWIGLU_LIMIT = 7.0
SWIGLU_ALPHA = 1.702
NORM_EPS = 1e-6
NEG_INF = -1e30
LOG2_E = 1.4426950408889634

VMEM_LIMIT = 56 * 1024 * 1024

TM_IN = 256
TQ = 512
TM_POST = 512
MOE_BLK = 512
TM_FIN = 256
TILES_PER_STEP = 2
ROW_SUB = 4


def _const_spec(shape):
    nd = len(shape)
    return pl.BlockSpec(shape, lambda *_: (0,) * nd, pipeline_mode=pl.Buffered(1))


def _rms(xf, g):
    return xf * lax.rsqrt(jnp.mean(xf * xf, axis=-1, keepdims=True) + NORM_EPS) * g


def _pack_bf16_pair(lo, hi):
    lo_b = lax.bitcast_convert_type(lo.astype(BF16).astype(F32), jnp.uint32)
    hi_b = lax.bitcast_convert_type(hi.astype(BF16).astype(F32), jnp.uint32)
    return (hi_b & jnp.uint32(0xFFFF0000)) | (lo_b >> 16)


def _unpack_bf16_pair(p):
    lo = lax.bitcast_convert_type(p << 16, F32)
    hi = lax.bitcast_convert_type(p & jnp.uint32(0xFFFF0000), F32)
    return lo, hi


def _store_rows(ref, packed):
    n = packed.shape[0]
    for q in range(ROW_SUB):
        ref[pl.ds(q, n, stride=ROW_SUB), :] = packed[:, q * 128:(q + 1) * 128]


def _load_rows(ref, n):
    los, his = [], []
    for q in range(ROW_SUB):
        lo, hi = _unpack_bf16_pair(ref[pl.ds(q, n, stride=ROW_SUB), :])
        los.append(lo)
        his.append(hi)
    return jnp.concatenate(los + his, axis=1)


def _mod_kernel(c_ref, w_ref, b_ref, o_ref):
    c = c_ref[...]
    c_act = c * jax.nn.sigmoid(c)
    o_ref[...] = jnp.dot(c_act, w_ref[...], preferred_element_type=F32,
                         precision=lax.Precision.HIGHEST) + b_ref[...]


def _mod_call(c, w, b, tn=1024):
    bsz, d = c.shape
    n = w.shape[1]
    return pl.pallas_call(
        _mod_kernel,
        grid=(n // tn,),
        in_specs=[pl.BlockSpec((bsz, d), lambda j: (0, 0)),
                  pl.BlockSpec((d, tn), lambda j: (0, j)),
                  pl.BlockSpec((1, tn), lambda j: (0, j))],
        out_specs=pl.BlockSpec((bsz, tn), lambda j: (0, j)),
        out_shape=jax.ShapeDtypeStruct((bsz, n), F32),
        compiler_params=pltpu.CompilerParams(dimension_semantics=("arbitrary",),
                                             vmem_limit_bytes=VMEM_LIMIT),
        name="adaln_mod",
    )(c, w, b.reshape(1, n))


def _mixer_in_kernel(x_ref, xh_ref, pos_ref, mod_ref, gmix_ref, win_ref, bgate_ref, wgrp_ref,
                     pscale_ref, wpo_ref, gq_ref, wq_ref, gkv_ref, wkn_ref, wvt_ref, invf_ref,
                     ga_ref, g1_ref, q_ref, k_ref, vt_ref, u_scr, *, tm, tiles_per_seq, d, q_lora,
                     kv_lora):
    i = pl.program_id(0)
    t_in_seq = i % tiles_per_seq
    is_start = t_in_seq == 0
    shift1 = mod_ref[0, 0:1, :]
    scale1 = mod_ref[0, 1:2, :]
    gmix = gmix_ref[...]

    def prenorm(xf):
        return (_rms(xf, gmix) * (1.0 + scale1) + shift1).astype(BF16)

    h = prenorm(x_ref[...])
    hh = prenorm(xh_ref[...])
    u_ext = jnp.dot(jnp.concatenate([hh, h], axis=0), win_ref[:, 0:d],
                    preferred_element_type=F32)
    u_scr[0:POOL_HALO, :] = jnp.where(is_start, 0.0, u_ext[0:POOL_HALO, :])
    u_scr[POOL_HALO:POOL_HALO + tm, :] = u_ext[POOL_HALO:, :]
    rest = jnp.dot(h, win_ref[:, d:], preferred_element_type=F32)

    gw = d // len(POOL_WINDOWS)
    tseq = t_in_seq * tm + lax.broadcasted_iota(jnp.int32, (tm, 1), 0)
    ys = []
    for g, w in enumerate(POOL_WINDOWS):
        c0 = g * gw
        ug = u_scr[POOL_HALO:POOL_HALO + tm, c0:c0 + gw]
        acc = ug
        for j in range(1, w):
            acc = acc + u_scr[POOL_HALO - j:POOL_HALO - j + tm, c0:c0 + gw]
        cnt = jnp.minimum(tseq + 1, w).astype(F32)
        mixed = (acc / cnt - ug).astype(BF16)
        ys.append(jnp.dot(mixed, wgrp_ref[g], preferred_element_type=F32))
    y = (jnp.concatenate(ys, axis=1) * pscale_ref[...]).astype(BF16)
    a = jnp.dot(y, wpo_ref[...], preferred_element_type=F32)

    o_q = 0
    o_kv = q_lora
    o_kpe = q_lora + kv_lora
    o_ksw = o_kpe + 128
    o_g0 = o_ksw + 128
    o_g1 = o_g0 + d
    gates0 = jax.nn.sigmoid(rest[:, o_g0:o_g0 + d] + bgate_ref[:, 0:d])
    gates1 = jax.nn.sigmoid(rest[:, o_g1:o_g1 + d] + bgate_ref[:, d:2 * d])
    ga_ref[...] = (gates0 * a).astype(BF16)
    g1_ref[...] = gates1.astype(BF16)

    ang = pos_ref[...] * invf_ref[...]
    cos2 = jnp.cos(ang)
    sin2 = jnp.sin(ang)
    q_scale = float(QK_NOPE + QK_ROPE) ** -0.5 * LOG2_E

    qn = _rms(rest[:, o_q:o_q + q_lora], gq_ref[...]).astype(BF16)
    qall = jnp.dot(qn, wq_ref[...], preferred_element_type=F32)
    kvn = _rms(rest[:, o_kv:o_kv + kv_lora], gkv_ref[...]).astype(BF16)
    kn = jnp.dot(kvn, wkn_ref[...], preferred_element_type=F32)
    vt = lax.dot_general(wvt_ref[...], kvn, (((1,), (1,)), ((), ())),
                         preferred_element_type=F32)
    vt_ref[0] = vt.astype(BF16)
    kpe = (rest[:, o_kpe:o_kpe + 128] * cos2 + rest[:, o_ksw:o_ksw + 128] * sin2).astype(BF16)
    o_pe = N_HEADS * QK_NOPE
    o_sw = o_pe + N_HEADS * QK_ROPE
    low_half = lax.broadcasted_iota(jnp.int32, (tm, 128), 1) < QK_ROPE
    for pair in range(N_HEADS // 2):
        c0 = pair * 128
        qpe2 = (qall[:, o_pe + c0:o_pe + c0 + 128] * cos2
                + qall[:, o_sw + c0:o_sw + c0 + 128] * sin2) * q_scale
        for hd, part in ((2 * pair, qpe2), (2 * pair + 1, pltpu.roll(qpe2, QK_ROPE, axis=1))):
            n0 = hd * QK_NOPE
            q_ref[:, hd * HEAD_PAD:hd * HEAD_PAD + 128] = (qall[:, n0:n0 + 128] * q_scale).astype(BF16)
            q_ref[:, hd * HEAD_PAD + 128:(hd + 1) * HEAD_PAD] = jnp.where(low_half, part, 0.0).astype(BF16)
            k_ref[:, hd * HEAD_PAD:hd * HEAD_PAD + 128] = kn[:, n0:n0 + 128].astype(BF16)
            k_ref[:, hd * HEAD_PAD + 128:(hd + 1) * HEAD_PAD] = kpe


def _mixer_in_call(x2d, pos_col, mod, g_mix, w_in_p, b_gate, w_grp, pool_scale, w_po, g_q_a, w_q_p,
                   g_kv_a, w_kn, w_vt, invf2, *, bsz, seq):
    t, d = x2d.shape
    tm = TM_IN
    tps = seq // tm
    q_lora = g_q_a.shape[-1]
    kv_lora = g_kv_a.shape[-1]
    hp = N_HEADS * HEAD_PAD
    halo_blocks = tm // POOL_HALO
    kern = functools.partial(_mixer_in_kernel, tm=tm, tiles_per_seq=tps, d=d, q_lora=q_lora,
                             kv_lora=kv_lora)
    row = lambda i: (i, 0)
    return pl.pallas_call(
        kern,
        grid=(t // tm,),
        in_specs=[
            pl.BlockSpec((tm, d), row),
            pl.BlockSpec((POOL_HALO, d), lambda i: (jnp.maximum(i * halo_blocks - 1, 0), 0)),
            pl.BlockSpec((tm, 1), row),
            pl.BlockSpec((1, 6, d), lambda i: (i // tps, 0, 0)),
            _const_spec((1, d)),
            _const_spec(w_in_p.shape),
            _const_spec((1, 2 * d)),
            _const_spec(w_grp.shape),
            _const_spec((1, d)),
            _const_spec(w_po.shape),
            _const_spec((1, q_lora)),
            _const_spec(w_q_p.shape),
            _const_spec((1, kv_lora)),
            _const_spec(w_kn.shape),
            _const_spec(w_vt.shape),
            _const_spec((1, 128)),
        ],
        out_specs=[
            pl.BlockSpec((tm, d), row),
            pl.BlockSpec((tm, d), row),
            pl.BlockSpec((tm, hp), row),
            pl.BlockSpec((tm, hp), row),
            pl.BlockSpec((1, N_HEADS * V_HEAD, tm), lambda i: (i // tps, 0, i % tps)),
        ],
        out_shape=[
            jax.ShapeDtypeStruct((t, d), BF16),
            jax.ShapeDtypeStruct((t, d), BF16),
            jax.ShapeDtypeStruct((t, hp), BF16),
            jax.ShapeDtypeStruct((t, hp), BF16),
            jax.ShapeDtypeStruct((bsz, N_HEADS * V_HEAD, seq), BF16),
        ],
        scratch_shapes=[pltpu.VMEM((tm + POOL_HALO, d), F32)],
        compiler_params=pltpu.CompilerParams(dimension_semantics=("arbitrary",),
                                             vmem_limit_bytes=VMEM_LIMIT),
        name="mixer_in",
    )(x2d, x2d, pos_col, mod, g_mix, w_in_p, b_gate, w_grp, pool_scale, w_po, g_q_a, w_q_p, g_kv_a,
      w_kn, w_vt, invf2)


def _attn_kernel(q_ref, k_ref, vt_ref, o_ref, *, tq, nq):
    seq = vt_ref.shape[2]
    vt_ext = jnp.concatenate([vt_ref[0], jnp.ones((16, seq), BF16)], axis=0)

    def scores(qi):
        q0 = qi * tq
        ln = q0 + tq
        return lax.dot_general(k_ref[0, 0:ln, :], q_ref[0, q0:q0 + tq, :], (((1,), (1,)), ((), ())),
                               preferred_element_type=F32)

    def finish(qi, s):
        q0 = qi * tq
        ln = q0 + tq
        cw = tq // 2
        for c in range(2):
            cs = slice(c * cw, (c + 1) * cw)
            kc = (q0 + lax.broadcasted_iota(jnp.int32, (tq, cw), 0)) // CHUNK
            qc = (q0 + c * cw + lax.broadcasted_iota(jnp.int32, (tq, cw), 1)) // CHUNK
            s_diag = jnp.where(qc >= kc, s[q0:ln, cs], NEG_INF)
            m = jnp.max(s_diag, axis=0, keepdims=True)
            if qi > 0:
                s_main = s[0:q0, cs]
                m = jnp.maximum(m, jnp.max(s_main, axis=0, keepdims=True))
            p_diag = jnp.exp2(s_diag - m).astype(BF16)
            acc = jnp.dot(vt_ext[:, q0:ln], p_diag, preferred_element_type=F32)
            if qi > 0:
                p_main = jnp.exp2(s_main - m).astype(BF16)
                acc = acc + jnp.dot(vt_ext[:, 0:q0], p_main, preferred_element_type=F32)
            l = acc[V_HEAD:V_HEAD + 1, :]
            o_ref[0, q0 + c * cw:q0 + (c + 1) * cw, :] = (acc[0:V_HEAD, :] / l).T.astype(BF16)

    order = list(range(nq))[::-1]
    s_cur = scores(order[0])
    for idx, qi in enumerate(order):
        s_next = scores(order[idx + 1]) if idx + 1 < nq else None
        finish(qi, s_cur)
        s_cur = s_next


def _attn_call(q3, k3, vt3):
    bsz, seq, _ = q3.shape
    kern = functools.partial(_attn_kernel, tq=TQ, nq=seq // TQ)
    return pl.pallas_call(
        kern,
        grid=(bsz, N_HEADS),
        in_specs=[pl.BlockSpec((1, seq, HEAD_PAD), lambda b, h: (b, 0, h)),
                  pl.BlockSpec((1, seq, HEAD_PAD), lambda b, h: (b, 0, h)),
                  pl.BlockSpec((1, V_HEAD, seq), lambda b, h: (b, h, 0))],
        out_specs=pl.BlockSpec((1, seq, V_HEAD), lambda b, h: (b, 0, h)),
        out_shape=jax.ShapeDtypeStruct((bsz, seq, N_HEADS * V_HEAD), BF16),
        compiler_params=pltpu.CompilerParams(dimension_semantics=("arbitrary", "arbitrary"),
                                             vmem_limit_bytes=VMEM_LIMIT),
        name="mla_attn",
    )(q3, k3, vt3)


def _post_kernel(o_ref, ga_ref, g1_ref, x_ref, mod_ref, wmo_ref, wout_ref, gffn_ref, wrt_ref, br_ref,
                 x1_ref, stg_ref, pos_ref, wgt_ref, cnt_ref, *, d):
    m = jnp.dot(o_ref[...], wmo_ref[...], preferred_element_type=F32)
    merged = ga_ref[...].astype(F32) + g1_ref[...].astype(F32) * m
    gate1 = mod_ref[0, 2:3, :]
    x1 = x_ref[...] + gate1 * jnp.dot(merged.astype(BF16), wout_ref[...],
                                      preferred_element_type=F32)
    x1_ref[...] = x1
    shift2 = mod_ref[0, 3:4, :]
    scale2 = mod_ref[0, 4:5, :]
    h2 = _rms(x1, gffn_ref[...]) * (1.0 + scale2) + shift2

    ne = br_ref.shape[0]
    h_hi = h2.astype(BF16)
    h_lo = (h2 - h_hi.astype(F32)).astype(BF16)
    hcat = jnp.concatenate([h_hi, h_lo], axis=1)
    half = hcat.shape[0] // 2
    lg2 = jnp.concatenate(
        [jnp.dot(hcat[0:half], wrt_ref[...], preferred_element_type=F32),
         jnp.dot(hcat[half:], wrt_ref[...], preferred_element_type=F32)], axis=0)
    lg = lg2[:, 0:128] + lg2[:, 128:256]
    logits = lg.T[0:ne, :] + br_ref[...]
    tm = logits.shape[1]
    eid = lax.broadcasted_iota(jnp.int32, (ne, tm), 0)
    vals, idxs = [], []
    cur = logits
    for _ in range(TOP_K):
        mx = jnp.max(cur, axis=0, keepdims=True)
        ix = jnp.min(jnp.where(cur == mx, eid, ne), axis=0, keepdims=True)
        vals.append(mx)
        idxs.append(ix)
        cur = jnp.where(eid == ix, -jnp.inf, cur)
    es = [jnp.exp(v - vals[0]) for v in vals]
    den = es[0] + es[1] + es[2] + es[3]
    wgt_ref[...] = jnp.concatenate([e / den for e in es], axis=0)

    tt = TM_FIN
    t_row = lax.broadcasted_iota(jnp.int32, (tm, tm), 0)
    t_col = lax.broadcasted_iota(jnp.int32, (tm, tm), 1)
    same_tile = (t_row // tt) == (t_col // tt)
    in_tile = jnp.where(same_tile, 1.0, 0.0).astype(BF16)
    earlier = jnp.where(same_tile & (t_row < t_col), 1.0, 0.0).astype(BF16)
    lower = jnp.where(lax.broadcasted_iota(jnp.int32, (ne, ne), 1)
                      < lax.broadcasted_iota(jnp.int32, (ne, ne), 0), 1.0, 0.0).astype(BF16)
    picks = [eid == ix for ix in idxs]
    routed = jnp.where(picks[0] | picks[1] | picks[2] | picks[3], 1.0, 0.0).astype(BF16)
    before = jnp.dot(routed, earlier, preferred_element_type=F32)
    cnt_b = jnp.dot(routed, in_tile, preferred_element_type=F32)
    run_start = jnp.dot(lower, cnt_b.astype(BF16), preferred_element_type=F32)
    base = run_start + before
    pos_ref[...] = jnp.concatenate(
        [jnp.sum(jnp.where(pk, base, 0.0), axis=0, keepdims=True) for pk in picks],
        axis=0).astype(jnp.int32)
    tile_sel = jnp.where(lax.broadcasted_iota(jnp.int32, (8, tm), 0)
                         == lax.broadcasted_iota(jnp.int32, (8, tm), 1) // tt, 1.0, 0.0).astype(BF16)
    cnt8 = lax.dot_general(tile_sel, routed, (((1,), (1,)), ((), ())),
                           preferred_element_type=F32)
    cnt_ref[0] = jnp.concatenate([cnt8, jnp.zeros((8, 128 - ne), F32)], axis=1).astype(jnp.int32)

    hb = h2.astype(BF16)
    n_rows = TOP_K * tt
    row_id = lax.broadcasted_iota(jnp.int32, (n_rows, tt), 0)
    for j in range(tm // tt):
        hit = row_id == pos_ref[0:1, j * tt:(j + 1) * tt]
        for k in range(1, TOP_K):
            hit = hit | (row_id == pos_ref[k:k + 1, j * tt:(j + 1) * tt])
        onehot = jnp.where(hit, 1.0, 0.0).astype(BF16)
        rows = jnp.dot(onehot, hb[j * tt:(j + 1) * tt, :], preferred_element_type=F32)
        _store_rows(stg_ref.at[pl.ds(j * n_rows * ROW_SUB, n_rows * ROW_SUB)],
                    _pack_bf16_pair(rows[:, 0:d // 2], rows[:, d // 2:d]))


def _post_call(o2d, ga, g1, x2d, mod, w_mo, w_out, g_ffn, w_rt, b_r, *, seq):
    t, d = x2d.shape
    tm = TM_POST
    tps = seq // tm
    row = lambda i: (i, 0)
    n_exp = b_r.shape[0]
    assert tm % TM_FIN == 0 and tm // TM_FIN <= 8 and n_exp <= 128
    kern = functools.partial(_post_kernel, d=d)
    return pl.pallas_call(
        kern,
        grid=(t // tm,),
        in_specs=[
            pl.BlockSpec((tm, d), row),
            pl.BlockSpec((tm, d), row),
            pl.BlockSpec((tm, d), row),
            pl.BlockSpec((tm, d), row),
            pl.BlockSpec((1, 6, d), lambda i: (i // tps, 0, 0)),
            _const_spec(w_mo.shape),
            _const_spec(w_out.shape),
            _const_spec((1, d)),
            _const_spec(w_rt.shape),
            _const_spec(b_r.shape),
        ],
        out_specs=[
            pl.BlockSpec((tm, d), row),
            pl.BlockSpec((tm * TOP_K * ROW_SUB, 128), row),
            pl.BlockSpec((TOP_K, tm), lambda i: (0, i)),
            pl.BlockSpec((TOP_K, tm), lambda i: (0, i)),
            pl.BlockSpec((1, 8, 128), lambda i: (i, 0, 0)),
        ],
        out_shape=[
            jax.ShapeDtypeStruct((t, d), F32),
            jax.ShapeDtypeStruct((t * TOP_K * ROW_SUB, 128), jnp.uint32),
            jax.ShapeDtypeStruct((TOP_K, t), jnp.int32),
            jax.ShapeDtypeStruct((TOP_K, t), F32),
            jax.ShapeDtypeStruct((t // tm, 8, 128), jnp.int32),
        ],
        compiler_params=pltpu.CompilerParams(dimension_semantics=("arbitrary",),
                                             vmem_limit_bytes=VMEM_LIMIT),
        name="post_attn_router",
    )(o2d, ga, g1, x2d, mod, w_mo, w_out, g_ffn, w_rt, b_r)


RUN_BITS = 9
COMMON_RUN_BITS = 6
TAIL_BITS = 9


def _copy_pieces(count, src_ref, src_off, dst_ref, dst_off, sem, bits, wait=False):
    def pieces(bit_range, off):
        for b in reversed(bit_range):
            size = (1 << b) * ROW_SUB
            take = (count >> b) & 1

            @pl.when(take == 1)
            def _(off=off, size=size):
                cp = pltpu.make_async_copy(
                    src_ref.at[pl.ds(pl.multiple_of(src_off + off, ROW_SUB), size)],
                    dst_ref.at[pl.ds(pl.multiple_of(dst_off + off, ROW_SUB), size)], sem)
                if wait:
                    cp.wait()
                else:
                    cp.start()
            off = off + take * size

    lo = min(bits, COMMON_RUN_BITS)
    big = (count >> lo) << lo

    if lo < bits:
        @pl.when(big != 0)
        def _():
            pieces(range(lo, bits), 0)
    pieces(range(lo), big * ROW_SUB)


def _dispatch_kernel(tab_ref, tail_ref, nused_ref, stg_hbm, xs_hbm, zbuf, sem, zsem, *,
                     tm, n_exp, n_blocks, tiles_per_step):
    i = pl.program_id(0)
    n = pl.num_programs(0)
    slot = i % 2
    n_rows = TOP_K * tm

    def wait_step(slot_):
        for _ in range(tiles_per_step):
            pltpu.make_async_copy(stg_hbm.at[pl.ds(0, n_rows * ROW_SUB)],
                                  xs_hbm.at[pl.ds(0, n_rows * ROW_SUB)], sem.at[slot_]).wait()

    @pl.when(i == 0)
    def _():
        zbuf[...] = jnp.zeros(zbuf.shape, zbuf.dtype)
        for wait in (False, True):
            def body(e, c, wait=wait):
                _copy_pieces(tail_ref[1, e], zbuf, 0, xs_hbm, tail_ref[0, e], zsem, TAIL_BITS, wait)
                return c
            lax.fori_loop(0, n_exp, body, 0)
        blk_rows = zbuf.shape[0]

        def zero_block(b, c):
            cp = pltpu.make_async_copy(
                zbuf, xs_hbm.at[pl.ds(pl.multiple_of(b * blk_rows, blk_rows), blk_rows)], zsem)
            cp.start()
            cp.wait()
            return c
        lax.fori_loop(nused_ref[0], n_blocks, zero_block, 0)

    for j in range(tiles_per_step):
        tile_base = (i * tiles_per_step + j) * (n_rows * ROW_SUB)

        def body(e, c, j=j, tile_base=tile_base):
            _copy_pieces(tab_ref[j, 0, e], stg_hbm, tile_base + tab_ref[j, 2, e], xs_hbm,
                         tab_ref[j, 1, e], sem.at[slot], RUN_BITS)
            return c
        lax.fori_loop(0, n_exp, body, 0)

    @pl.when(i >= 1)
    def _():
        wait_step(1 - slot)

    @pl.when(i == n - 1)
    def _():
        wait_step(slot)


DISPATCH_TILES_PER_STEP = 8


def _dispatch_call(tabs, tail, n_used, staged, *, n_xs_rows):
    n_tiles, _, n_exp = tabs.shape
    tm = TM_FIN
    tps = DISPATCH_TILES_PER_STEP
    assert n_tiles % tps == 0
    nt = n_tiles // tps
    assert nt >= 2 and tm < (1 << RUN_BITS) and MOE_BLK <= (1 << TAIL_BITS)
    kern = functools.partial(_dispatch_kernel, tm=tm, n_exp=n_exp, n_blocks=n_xs_rows // MOE_BLK,
                             tiles_per_step=tps)
    return pl.pallas_call(
        kern,
        grid=(nt,),
        in_specs=[
            pl.BlockSpec((tps, 3, n_exp), lambda i: (i, 0, 0), memory_space=pltpu.SMEM),
            pl.BlockSpec(memory_space=pltpu.SMEM),
            pl.BlockSpec(memory_space=pltpu.SMEM),
            pl.BlockSpec(memory_space=pl.ANY),
        ],
        out_specs=pl.BlockSpec(memory_space=pl.ANY),
        out_shape=jax.ShapeDtypeStruct((n_xs_rows * ROW_SUB, 128), jnp.uint32),
        scratch_shapes=[pltpu.VMEM((MOE_BLK * ROW_SUB, 128), jnp.uint32),
                        pltpu.SemaphoreType.DMA((2,)),
                        pltpu.SemaphoreType.DMA(())],
        compiler_params=pltpu.CompilerParams(dimension_semantics=("arbitrary",),
                                             vmem_limit_bytes=VMEM_LIMIT),
        name="moe_dispatch",
    )(tabs, tail, n_used, staged)


def _moe_kernel(blk_e_ref, nused_ref, x_ref, perm_ref, wgu_ref, bgu_ref, wd_ref, bd_ref, o_ref,
                wgu_scr, wd_scr, *, blk, f):
    s = pl.program_id(0)
    e_cur = blk_e_ref[s]
    e_prev = blk_e_ref[jnp.maximum(s - 1, 0)]

    @pl.when(jnp.logical_or(s == 0, e_cur != e_prev))
    def _():
        perm = perm_ref[...]
        for c in range(2 * f // 256):
            r = jnp.dot(wgu_ref[0, :, c * 256:(c + 1) * 256].astype(BF16), perm,
                        preferred_element_type=F32)
            wgu_scr[:, c * 128:(c + 1) * 128] = r[:, 0:128].astype(BF16)
            wgu_scr[:, f + c * 128:f + (c + 1) * 128] = r[:, 128:256].astype(BF16)
        wd_scr[...] = wd_ref[0].astype(BF16)

    @pl.when(s >= nused_ref[0])
    def _():
        o_ref[...] = jnp.zeros(o_ref.shape, o_ref.dtype)

    @pl.when(s < nused_ref[0])
    def _():
        x = _load_rows(x_ref, blk).astype(BF16)
        gu = jnp.dot(x, wgu_scr[...], preferred_element_type=F32) + bgu_ref[0]
        gate = jnp.minimum(gu[:, 0:f], SWIGLU_LIMIT)
        up = jnp.clip(gu[:, f:2 * f], -SWIGLU_LIMIT, SWIGLU_LIMIT)
        act = (up + 1.0) * (gate * jax.nn.sigmoid(SWIGLU_ALPHA * gate))
        y = jnp.dot(act.astype(BF16), wd_scr[...], preferred_element_type=F32) + bd_ref[0]
        dh = y.shape[1] // 2
        _store_rows(o_ref, _pack_bf16_pair(y[:, 0:dh], y[:, dh:]))


def _moe_call(blk_e, n_used, xs, perm, w_gu, b_gu_p, w_down, b_down):
    n_steps = blk_e.shape[0]
    blk = MOE_BLK
    _, d, f2 = w_gu.shape
    f = f2 // 2
    rows = blk * ROW_SUB
    kern = functools.partial(_moe_kernel, blk=blk, f=f)
    row_blk = lambda i, be, nu: (i, 0)
    grid_spec = pltpu.PrefetchScalarGridSpec(
        num_scalar_prefetch=2,
        grid=(n_steps,),
        in_specs=[
            pl.BlockSpec((rows, 128), row_blk),
            pl.BlockSpec((256, 256), lambda i, be, nu: (0, 0)),
            pl.BlockSpec((1, d, f2), lambda i, be, nu: (be[i], 0, 0)),
            pl.BlockSpec((1, 1, f2), lambda i, be, nu: (be[i], 0, 0)),
            pl.BlockSpec((1, f, d), lambda i, be, nu: (be[i], 0, 0)),
            pl.BlockSpec((1, 1, d), lambda i, be, nu: (be[i], 0, 0)),
        ],
        out_specs=pl.BlockSpec((rows, 128), row_blk),
        scratch_shapes=[
            pltpu.VMEM((d, f2), BF16),
            pltpu.VMEM((f, d), BF16),
        ],
    )
    return pl.pallas_call(
        kern,
        grid_spec=grid_spec,
        out_shape=jax.ShapeDtypeStruct((n_steps * rows, 128), jnp.uint32),
        compiler_params=pltpu.CompilerParams(dimension_semantics=("arbitrary",),
                                             vmem_limit_bytes=VMEM_LIMIT),
        name="moe_experts",
    )(blk_e, n_used, xs, perm, w_gu, b_gu_p, w_down, b_down)


def _final_kernel(tab_cur_ref, tab_nxt_ref, x1_ref, pos_ref, w_ref, mod_ref, fmod_ref, gfin_ref, ys_hbm,
                  o_ref, stage, sem, *, tm, n_exp):
    i = pl.program_id(0)
    n = pl.num_programs(0)
    slot = i % 2
    n_rows = TOP_K * tm

    tiles = range(TILES_PER_STEP)

    def issue_tiles(tab_ref, slot_):
        for j in tiles:
            def body(e, c, j=j):
                _copy_pieces(tab_ref[j, 0, e], ys_hbm, tab_ref[j, 1, e], stage.at[slot_, j],
                             tab_ref[j, 2, e], sem.at[slot_], RUN_BITS)
                return c
            lax.fori_loop(0, n_exp, body, 0)

    @pl.when(i == 0)
    def _():
        issue_tiles(tab_cur_ref, 0)

    @pl.when(i + 1 < n)
    def _():
        issue_tiles(tab_nxt_ref, 1 - slot)

    for j in tiles:
        pltpu.make_async_copy(ys_hbm.at[pl.ds(0, n_rows * ROW_SUB)], stage.at[slot, j],
                              sem.at[slot]).wait()

    ys = [_load_rows(stage.at[slot, j], n_rows).astype(BF16) for j in tiles]
    col_id = lax.broadcasted_iota(jnp.int32, (tm, n_rows), 1)
    wmats = []
    for j in tiles:
        pos = pos_ref[j * tm:(j + 1) * tm, :]
        w = w_ref[j * tm:(j + 1) * tm, :]
        a = jnp.where(col_id == pos[:, 0:1], w[:, 0:1], 0.0)
        for k in range(1, TOP_K):
            a = a + jnp.where(col_id == pos[:, k:k + 1], w[:, k:k + 1], 0.0)
        wmats.append(a.astype(BF16))
    moes = [jnp.dot(wmats[j], ys[j], preferred_element_type=F32) for j in tiles]
    gate2 = mod_ref[0, 5:6, :]
    fshift = fmod_ref[0, 0:1, :]
    fscale = fmod_ref[0, 1:2, :]
    for j in tiles:
        x2 = x1_ref[j * tm:(j + 1) * tm, :] + gate2 * moes[j]
        o_ref[j * tm:(j + 1) * tm, :] = _rms(x2, gfin_ref[...]) * (1.0 + fscale) + fshift


def _final_call(tabs, x1, pos_col, w_col, mod, fmod, g_final, ys, *, seq):
    t, d = x1.shape
    tm = TM_FIN
    assert tm < (1 << RUN_BITS)
    tls = TILES_PER_STEP
    tms = tls * tm
    assert seq % tms == 0
    tps = seq // tms
    nt = t // tms
    n_exp = tabs.shape[2]
    n_rows = TOP_K * tm
    kern = functools.partial(_final_kernel, tm=tm, n_exp=n_exp)
    tab_spec = lambda ahead: pl.BlockSpec((tls, 3, n_exp), lambda i: (jnp.minimum(i + ahead, nt - 1), 0, 0),
                                          memory_space=pltpu.SMEM)
    return pl.pallas_call(
        kern,
        grid=(nt,),
        in_specs=[
            tab_spec(0),
            tab_spec(1),
            pl.BlockSpec((tms, d), lambda i: (i, 0)),
            pl.BlockSpec((tms, TOP_K), lambda i: (i, 0)),
            pl.BlockSpec((tms, TOP_K), lambda i: (i, 0)),
            pl.BlockSpec((1, 6, d), lambda i: (i // tps, 0, 0)),
            pl.BlockSpec((1, 2, d), lambda i: (i // tps, 0, 0)),
            _const_spec((1, d)),
            pl.BlockSpec(memory_space=pl.ANY),
        ],
        out_specs=pl.BlockSpec((tms, d), lambda i: (i, 0)),
        out_shape=jax.ShapeDtypeStruct((t, d), F32),
        scratch_shapes=[pltpu.VMEM((2, tls, n_rows * ROW_SUB, 128), jnp.uint32),
                        pltpu.SemaphoreType.DMA((2,))],
        compiler_params=pltpu.CompilerParams(dimension_semantics=("arbitrary",),
                                             vmem_limit_bytes=VMEM_LIMIT),
        name="combine_final",
    )(tabs, tabs, x1, pos_col, w_col, mod, fmod, g_final, ys)


def _prep_w_in(w_in, d, q_lora, kv_lora):
    o_kpe = d + q_lora + kv_lora
    o_g = o_kpe + QK_ROPE
    half = QK_ROPE // 2
    kpe = w_in[:, o_kpe:o_kpe + QK_ROPE]
    zpad = jnp.zeros((d, 128 - QK_ROPE), w_in.dtype)
    ksw = jnp.concatenate([-kpe[:, half:], kpe[:, :half]], axis=1)
    return jnp.concatenate([w_in[:, :o_kpe], kpe, zpad, ksw, zpad, w_in[:, o_g:]], axis=1).astype(BF16)


def _prep_w_q(w_q_b):
    ql = w_q_b.shape[0]
    hd = QK_NOPE + QK_ROPE
    half = QK_ROPE // 2
    w = w_q_b.reshape(ql, N_HEADS, hd)
    nope = w[:, :, :QK_NOPE]
    pe = w[:, :, QK_NOPE:]
    sw = jnp.concatenate([-pe[:, :, half:], pe[:, :, :half]], axis=2)
    return jnp.concatenate([nope.reshape(ql, -1), pe.reshape(ql, -1), sw.reshape(ql, -1)],
                           axis=1).astype(BF16)


def _prep_w_kv(w_kv_b):
    kvl = w_kv_b.shape[0]
    w = w_kv_b.reshape(kvl, N_HEADS, QK_NOPE + V_HEAD)
    w_kn = w[:, :, :QK_NOPE].reshape(kvl, -1).astype(BF16)
    w_vt = w[:, :, QK_NOPE:].reshape(kvl, -1).T.astype(BF16)
    return w_kn, w_vt


def kernel(x, c, positions, w_mod, b_mod, g_mix, w_in, b_gate, w_pool_grp, pool_scale, w_pool_out,
           g_q_a, w_q_b, g_kv_a, w_kv_b, w_mla_out, w_out, g_ffn, w_router, b_router, w_gu, b_gu,
           w_down, b_down, g_final, w_fmod, b_fmod):
    bsz, seq, d = x.shape
    t = bsz * seq
    depth = w_mod.shape[0]
    assert depth == 1
    assert seq % TQ == 0 and seq % TM_IN == 0 and seq % TM_POST == 0 and seq % TM_FIN == 0
    q_lora = g_q_a.shape[-1]
    kv_lora = g_kv_a.shape[-1]
    n_exp = w_gu.shape[1]
    f = w_gu.shape[-1] // 2
    blk = MOE_BLK

    x2d = x.reshape(t, d)
    pos_col = positions.astype(F32).reshape(t, 1)
    inv_freq = 1.0 / (ROPE_THETA ** (jnp.arange(0, QK_ROPE, 2, dtype=F32) / QK_ROPE))
    invf2 = jnp.tile(inv_freq, 128 // (QK_ROPE // 2)).reshape(1, 128)

    mod = _mod_call(c, w_mod[0], b_mod[0]).reshape(bsz, 6, d)
    fmod = _mod_call(c, w_fmod, b_fmod).reshape(bsz, 2, d)

    w_in_p = _prep_w_in(w_in[0], d, q_lora, kv_lora)
    w_q_p = _prep_w_q(w_q_b[0])
    w_kn, w_vt = _prep_w_kv(w_kv_b[0])
    ga, g1, q2, k2, vt3 = _mixer_in_call(
        x2d, pos_col, mod, g_mix[0].reshape(1, d), w_in_p, b_gate[0].reshape(1, 2 * d),
        w_pool_grp[0].astype(BF16), pool_scale[0].reshape(1, d), w_pool_out[0].astype(BF16),
        g_q_a[0].reshape(1, q_lora), w_q_p, g_kv_a[0].reshape(1, kv_lora), w_kn, w_vt, invf2,
        bsz=bsz, seq=seq)

    hp = N_HEADS * HEAD_PAD
    o = _attn_call(q2.reshape(bsz, seq, hp), k2.reshape(bsz, seq, hp), vt3)

    w_r = w_router[0]
    w_r_hi = w_r.astype(BF16)
    w_r_lo = (w_r - w_r_hi.astype(F32)).astype(BF16)
    lane_pad = lambda a: jnp.pad(a, ((0, 0), (0, 128 - n_exp)))
    w_rt = jnp.concatenate(
        [jnp.concatenate([lane_pad(w_r_hi), lane_pad(w_r_lo)], axis=1),
         jnp.concatenate([lane_pad(w_r_hi), jnp.zeros((d, 128), BF16)], axis=1)], axis=0)
    x1, staged, pos_t, wgt_t, cnt3 = _post_call(
        o.reshape(t, d), ga, g1, x2d, mod, w_mla_out[0].astype(BF16), w_out[0].astype(BF16),
        g_ffn[0].reshape(1, d), w_rt, b_router[0].reshape(n_exp, 1), seq=seq)

    n_slots = t * TOP_K
    n_rows = n_slots + n_exp * blk
    n_blocks = n_rows // blk
    cnt = cnt3[:, 0:TM_POST // TM_FIN, 0:n_exp].reshape(-1, n_exp)
    counts = jnp.sum(cnt, axis=0)
    padded = (counts + blk - 1) // blk * blk
    pad_end = jnp.cumsum(padded)
    pad_start = pad_end - padded
    blk_start = jnp.arange(n_blocks, dtype=jnp.int32) * blk
    blk_e = jnp.minimum(jnp.sum(pad_end[None, :] <= blk_start[:, None], axis=1, dtype=jnp.int32),
                        n_exp - 1)
    n_used = (pad_end[-1:] // blk).astype(jnp.int32)
    tail = jnp.stack([(pad_start + counts) * ROW_SUB, padded - counts], axis=0).astype(jnp.int32)

    before = jnp.cumsum(cnt, axis=0) - cnt
    run_start = jnp.cumsum(cnt, axis=1) - cnt
    tabs = jnp.stack([cnt, (pad_start[None, :] + before) * ROW_SUB, run_start * ROW_SUB], axis=1)

    col = jnp.arange(256, dtype=jnp.int32)[None, :]
    row = jnp.arange(256, dtype=jnp.int32)[:, None]
    perm = (row == jnp.where(col < 128, 2 * col, 2 * (col - 128) + 1)).astype(BF16)
    b_gu_p = jnp.concatenate([b_gu[0][:, 0::2], b_gu[0][:, 1::2]], axis=-1).reshape(n_exp, 1, 2 * f)
    xs = _dispatch_call(tabs, tail, n_used, staged, n_xs_rows=n_rows)
    ys = _moe_call(blk_e, n_used, xs, perm, w_gu[0], b_gu_p, w_down[0], b_down[0].reshape(n_exp, 1, d))

    out = _final_call(tabs, x1, pos_t.T, wgt_t.T, mod, fmod, g_final.reshape(1, d), ys, seq=seq)
    return out.reshape(bsz, seq, d)
```

```python
import functools

import jax
import jax.numpy as jnp
from jax import lax
from jax.experimental import pallas as pl
from jax.experimental.pallas import tpu as pltpu

F32 = jnp.float32
BF16 = jnp.bfloat16

CHUNK = 64
POOL_WINDOWS = (2, 4, 8, 16)
POOL_HALO = 16
N_HEADS = 8
QK_NOPE = 128
QK_ROPE = 64
V_HEAD = 128
HEAD_PAD = 256
ROPE_THETA = 10000.0
TOP_K = 4
SWIGLU_LIMIT = 7.0
SWIGLU_ALPHA = 1.702
NORM_EPS = 1e-6
NEG_INF = -1e30
LOG2_E = 1.4426950408889634

VMEM_LIMIT = 56 * 1024 * 1024

TM_IN = 256
TQ = 512
TM_POST = 512
MOE_BLK = 512
TM_FIN = 256
TILES_PER_STEP = 2
ROW_SUB = 4


def _const_spec(shape):
    nd = len(shape)
    return pl.BlockSpec(shape, lambda *_: (0,) * nd, pipeline_mode=pl.Buffered(1))


def _rms(xf, g):
    return xf * lax.rsqrt(jnp.mean(xf * xf, axis=-1, keepdims=True) + NORM_EPS) * g


def _pack_bf16_pair(lo, hi):
    lo_b = lax.bitcast_convert_type(lo.astype(BF16).astype(F32), jnp.uint32)
    hi_b = lax.bitcast_convert_type(hi.astype(BF16).astype(F32), jnp.uint32)
    return (hi_b & jnp.uint32(0xFFFF0000)) | (lo_b >> 16)


def _unpack_bf16_pair(p):
    lo = lax.bitcast_convert_type(p << 16, F32)
    hi = lax.bitcast_convert_type(p & jnp.uint32(0xFFFF0000), F32)
    return lo, hi


def _store_rows(ref, packed):
    n = packed.shape[0]
    for q in range(ROW_SUB):
        ref[pl.ds(q, n, stride=ROW_SUB), :] = packed[:, q * 128:(q + 1) * 128]


def _load_rows(ref, n):
    los, his = [], []
    for q in range(ROW_SUB):
        lo, hi = _unpack_bf16_pair(ref[pl.ds(q, n, stride=ROW_SUB), :])
        los.append(lo)
        his.append(hi)
    return jnp.concatenate(los + his, axis=1)


def _mod_kernel(c_ref, w_ref, b_ref, o_ref):
    c = c_ref[...]
    c_act = c * jax.nn.sigmoid(c)
    o_ref[...] = jnp.dot(c_act, w_ref[...], preferred_element_type=F32,
                         precision=lax.Precision.HIGHEST) + b_ref[...]


def _mod_call(c, w, b, tn=1024):
    bsz, d = c.shape
    n = w.shape[1]
    return pl.pallas_call(
        _mod_kernel,
        grid=(n // tn,),
        in_specs=[pl.BlockSpec((bsz, d), lambda j: (0, 0)),
                  pl.BlockSpec((d, tn), lambda j: (0, j)),
                  pl.BlockSpec((1, tn), lambda j: (0, j))],
        out_specs=pl.BlockSpec((bsz, tn), lambda j: (0, j)),
        out_shape=jax.ShapeDtypeStruct((bsz, n), F32),
        compiler_params=pltpu.CompilerParams(dimension_semantics=("arbitrary",),
                                             vmem_limit_bytes=VMEM_LIMIT),
        name="adaln_mod",
    )(c, w, b.reshape(1, n))


def _mixer_in_kernel(x_ref, xh_ref, pos_ref, mod_ref, gmix_ref, win_ref, bgate_ref, wgrp_ref,
                     pscale_ref, wpo_ref, gq_ref, wq_ref, gkv_ref, wkn_ref, wvt_ref, invf_ref,
                     ga_ref, g1_ref, q_ref, k_ref, vt_ref, u_scr, *, tm, tiles_per_seq, d, q_lora,
                     kv_lora):
    i = pl.program_id(0)
    t_in_seq = i % tiles_per_seq
    is_start = t_in_seq == 0
    shift1 = mod_ref[0, 0:1, :]
    scale1 = mod_ref[0, 1:2, :]
    gmix = gmix_ref[...]

    def prenorm(xf):
        return (_rms(xf, gmix) * (1.0 + scale1) + shift1).astype(BF16)

    h = prenorm(x_ref[...])
    hh = prenorm(xh_ref[...])
    u_ext = jnp.dot(jnp.concatenate([hh, h], axis=0), win_ref[:, 0:d],
                    preferred_element_type=F32)
    u_scr[0:POOL_HALO, :] = jnp.where(is_start, 0.0, u_ext[0:POOL_HALO, :])
    u_scr[POOL_HALO:POOL_HALO + tm, :] = u_ext[POOL_HALO:, :]
    rest = jnp.dot(h, win_ref[:, d:], preferred_element_type=F32)

    gw = d // len(POOL_WINDOWS)
    tseq = t_in_seq * tm + lax.broadcasted_iota(jnp.int32, (tm, 1), 0)
    ys = []
    for g, w in enumerate(POOL_WINDOWS):
        c0 = g * gw
        ug = u_scr[POOL_HALO:POOL_HALO + tm, c0:c0 + gw]
        acc = ug
        for j in range(1, w):
            acc = acc + u_scr[POOL_HALO - j:POOL_HALO - j + tm, c0:c0 + gw]
        cnt = jnp.minimum(tseq + 1, w).astype(F32)
        mixed = (acc / cnt - ug).astype(BF16)
        ys.append(jnp.dot(mixed, wgrp_ref[g], preferred_element_type=F32))
    y = (jnp.concatenate(ys, axis=1) * pscale_ref[...]).astype(BF16)
    a = jnp.dot(y, wpo_ref[...], preferred_element_type=F32)

    o_q = 0
    o_kv = q_lora
    o_kpe = q_lora + kv_lora
    o_ksw = o_kpe + 128
    o_g0 = o_ksw + 128
    o_g1 = o_g0 + d
    gates0 = jax.nn.sigmoid(rest[:, o_g0:o_g0 + d] + bgate_ref[:, 0:d])
    gates1 = jax.nn.sigmoid(rest[:, o_g1:o_g1 + d] + bgate_ref[:, d:2 * d])
    ga_ref[...] = (gates0 * a).astype(BF16)
    g1_ref[...] = gates1.astype(BF16)

    ang = pos_ref[...] * invf_ref[...]
    cos2 = jnp.cos(ang)
    sin2 = jnp.sin(ang)
    q_scale = float(QK_NOPE + QK_ROPE) ** -0.5 * LOG2_E

    qn = _rms(rest[:, o_q:o_q + q_lora], gq_ref[...]).astype(BF16)
    qall = jnp.dot(qn, wq_ref[...], preferred_element_type=F32)
    kvn = _rms(rest[:, o_kv:o_kv + kv_lora], gkv_ref[...]).astype(BF16)
    kn = jnp.dot(kvn, wkn_ref[...], preferred_element_type=F32)
    vt = lax.dot_general(wvt_ref[...], kvn, (((1,), (1,)), ((), ())),
                         preferred_element_type=F32)
    vt_ref[0] = vt.astype(BF16)
    kpe = (rest[:, o_kpe:o_kpe + 128] * cos2 + rest[:, o_ksw:o_ksw + 128] * sin2).astype(BF16)
    o_pe = N_HEADS * QK_NOPE
    o_sw = o_pe + N_HEADS * QK_ROPE
    low_half = lax.broadcasted_iota(jnp.int32, (tm, 128), 1) < QK_ROPE
    for pair in range(N_HEADS // 2):
        c0 = pair * 128
        qpe2 = (qall[:, o_pe + c0:o_pe + c0 + 128] * cos2
                + qall[:, o_sw + c0:o_sw + c0 + 128] * sin2) * q_scale
        for hd, part in ((2 * pair, qpe2), (2 * pair + 1, pltpu.roll(qpe2, QK_ROPE, axis=1))):
            n0 = hd * QK_NOPE
            q_ref[:, hd * HEAD_PAD:hd * HEAD_PAD + 128] = (qall[:, n0:n0 + 128] * q_scale).astype(BF16)
            q_ref[:, hd * HEAD_PAD + 128:(hd + 1) * HEAD_PAD] = jnp.where(low_half, part, 0.0).astype(BF16)
            k_ref[:, hd * HEAD_PAD:hd * HEAD_PAD + 128] = kn[:, n0:n0 + 128].astype(BF16)
            k_ref[:, hd * HEAD_PAD + 128:(hd + 1) * HEAD_PAD] = kpe


def _mixer_in_call(x2d, pos_col, mod, g_mix, w_in_p, b_gate, w_grp, pool_scale, w_po, g_q_a, w_q_p,
                   g_kv_a, w_kn, w_vt, invf2, *, bsz, seq):
    t, d = x2d.shape
    tm = TM_IN
    tps = seq // tm
    q_lora = g_q_a.shape[-1]
    kv_lora = g_kv_a.shape[-1]
    hp = N_HEADS * HEAD_PAD
    halo_blocks = tm // POOL_HALO
    kern = functools.partial(_mixer_in_kernel, tm=tm, tiles_per_seq=tps, d=d, q_lora=q_lora,
                             kv_lora=kv_lora)
    row = lambda i: (i, 0)
    return pl.pallas_call(
        kern,
        grid=(t // tm,),
        in_specs=[
            pl.BlockSpec((tm, d), row),
            pl.BlockSpec((POOL_HALO, d), lambda i: (jnp.maximum(i * halo_blocks - 1, 0), 0)),
            pl.BlockSpec((tm, 1), row),
            pl.BlockSpec((1, 6, d), lambda i: (i // tps, 0, 0)),
            _const_spec((1, d)),
            _const_spec(w_in_p.shape),
            _const_spec((1, 2 * d)),
            _const_spec(w_grp.shape),
            _const_spec((1, d)),
            _const_spec(w_po.shape),
            _const_spec((1, q_lora)),
            _const_spec(w_q_p.shape),
            _const_spec((1, kv_lora)),
            _const_spec(w_kn.shape),
            _const_spec(w_vt.shape),
            _const_spec((1, 128)),
        ],
        out_specs=[
            pl.BlockSpec((tm, d), row),
            pl.BlockSpec((tm, d), row),
            pl.BlockSpec((tm, hp), row),
            pl.BlockSpec((tm, hp), row),
            pl.BlockSpec((1, N_HEADS * V_HEAD, tm), lambda i: (i // tps, 0, i % tps)),
        ],
        out_shape=[
            jax.ShapeDtypeStruct((t, d), BF16),
            jax.ShapeDtypeStruct((t, d), BF16),
            jax.ShapeDtypeStruct((t, hp), BF16),
            jax.ShapeDtypeStruct((t, hp), BF16),
            jax.ShapeDtypeStruct((bsz, N_HEADS * V_HEAD, seq), BF16),
        ],
        scratch_shapes=[pltpu.VMEM((tm + POOL_HALO, d), F32)],
        compiler_params=pltpu.CompilerParams(dimension_semantics=("arbitrary",),
                                             vmem_limit_bytes=VMEM_LIMIT),
        name="mixer_in",
    )(x2d, x2d, pos_col, mod, g_mix, w_in_p, b_gate, w_grp, pool_scale, w_po, g_q_a, w_q_p, g_kv_a,
      w_kn, w_vt, invf2)


def _attn_kernel(q_ref, k_ref, vt_ref, o_ref, *, tq, nq):
    seq = vt_ref.shape[2]
    vt_ext = jnp.concatenate([vt_ref[0], jnp.ones((16, seq), BF16)], axis=0)

    def scores(qi):
        q0 = qi * tq
        ln = q0 + tq
        return lax.dot_general(k_ref[0, 0:ln, :], q_ref[0, q0:q0 + tq, :], (((1,), (1,)), ((), ())),
                               preferred_element_type=F32)

    def finish(qi, s):
        q0 = qi * tq
        ln = q0 + tq
        cw = tq // 2
        for c in range(2):
            cs = slice(c * cw, (c + 1) * cw)
            kc = (q0 + lax.broadcasted_iota(jnp.int32, (tq, cw), 0)) // CHUNK
            qc = (q0 + c * cw + lax.broadcasted_iota(jnp.int32, (tq, cw), 1)) // CHUNK
            s_diag = jnp.where(qc >= kc, s[q0:ln, cs], NEG_INF)
            m = jnp.max(s_diag, axis=0, keepdims=True)
            if qi > 0:
                s_main = s[0:q0, cs]
                m = jnp.maximum(m, jnp.max(s_main, axis=0, keepdims=True))
            p_diag = jnp.exp2(s_diag - m).astype(BF16)
            acc = jnp.dot(vt_ext[:, q0:ln], p_diag, preferred_element_type=F32)
            if qi > 0:
                p_main = jnp.exp2(s_main - m).astype(BF16)
                acc = acc + jnp.dot(vt_ext[:, 0:q0], p_main, preferred_element_type=F32)
            l = acc[V_HEAD:V_HEAD + 1, :]
            o_ref[0, q0 + c * cw:q0 + (c + 1) * cw, :] = (acc[0:V_HEAD, :] / l).T.astype(BF16)

    order = list(range(nq))[::-1]
    s_cur = scores(order[0])
    for idx, qi in enumerate(order):
        s_next = scores(order[idx + 1]) if idx + 1 < nq else None
        finish(qi, s_cur)
        s_cur = s_next


def _attn_call(q3, k3, vt3):
    bsz, seq, _ = q3.shape
    kern = functools.partial(_attn_kernel, tq=TQ, nq=seq // TQ)
    return pl.pallas_call(
        kern,
        grid=(bsz, N_HEADS),
        in_specs=[pl.BlockSpec((1, seq, HEAD_PAD), lambda b, h: (b, 0, h)),
                  pl.BlockSpec((1, seq, HEAD_PAD), lambda b, h: (b, 0, h)),
                  pl.BlockSpec((1, V_HEAD, seq), lambda b, h: (b, h, 0))],
        out_specs=pl.BlockSpec((1, seq, V_HEAD), lambda b, h: (b, 0, h)),
        out_shape=jax.ShapeDtypeStruct((bsz, seq, N_HEADS * V_HEAD), BF16),
        compiler_params=pltpu.CompilerParams(dimension_semantics=("arbitrary", "arbitrary"),
                                             vmem_limit_bytes=VMEM_LIMIT),
        name="mla_attn",
    )(q3, k3, vt3)


def _post_kernel(o_ref, ga_ref, g1_ref, x_ref, mod_ref, wmo_ref, wout_ref, gffn_ref, wrt_ref, br_ref,
                 x1_ref, stg_ref, pos_ref, wgt_ref, cnt_ref, *, d):
    m = jnp.dot(o_ref[...], wmo_ref[...], preferred_element_type=F32)
    merged = ga_ref[...].astype(F32) + g1_ref[...].astype(F32) * m
    gate1 = mod_ref[0, 2:3, :]
    x1 = x_ref[...] + gate1 * jnp.dot(merged.astype(BF16), wout_ref[...],
                                      preferred_element_type=F32)
    x1_ref[...] = x1
    shift2 = mod_ref[0, 3:4, :]
    scale2 = mod_ref[0, 4:5, :]
    h2 = _rms(x1, gffn_ref[...]) * (1.0 + scale2) + shift2

    ne = br_ref.shape[0]
    h_hi = h2.astype(BF16)
    h_lo = (h2 - h_hi.astype(F32)).astype(BF16)
    hcat = jnp.concatenate([h_hi, h_lo], axis=1)
    half = hcat.shape[0] // 2
    lg2 = jnp.concatenate(
        [jnp.dot(hcat[0:half], wrt_ref[...], preferred_element_type=F32),
         jnp.dot(hcat[half:], wrt_ref[...], preferred_element_type=F32)], axis=0)
    lg = lg2[:, 0:128] + lg2[:, 128:256]
    logits = lg.T[0:ne, :] + br_ref[...]
    tm = logits.shape[1]
    eid = lax.broadcasted_iota(jnp.int32, (ne, tm), 0)
    vals, idxs = [], []
    cur = logits
    for _ in range(TOP_K):
        mx = jnp.max(cur, axis=0, keepdims=True)
        ix = jnp.min(jnp.where(cur == mx, eid, ne), axis=0, keepdims=True)
        vals.append(mx)
        idxs.append(ix)
        cur = jnp.where(eid == ix, -jnp.inf, cur)
    es = [jnp.exp(v - vals[0]) for v in vals]
    den = es[0] + es[1] + es[2] + es[3]
    wgt_ref[...] = jnp.concatenate([e / den for e in es], axis=0)

    tt = TM_FIN
    t_row = lax.broadcasted_iota(jnp.int32, (tm, tm), 0)
    t_col = lax.broadcasted_iota(jnp.int32, (tm, tm), 1)
    same_tile = (t_row // tt) == (t_col // tt)
    in_tile = jnp.where(same_tile, 1.0, 0.0).astype(BF16)
    earlier = jnp.where(same_tile & (t_row < t_col), 1.0, 0.0).astype(BF16)
    lower = jnp.where(lax.broadcasted_iota(jnp.int32, (ne, ne), 1)
                      < lax.broadcasted_iota(jnp.int32, (ne, ne), 0), 1.0, 0.0).astype(BF16)
    picks = [eid == ix for ix in idxs]
    routed = jnp.where(picks[0] | picks[1] | picks[2] | picks[3], 1.0, 0.0).astype(BF16)
    before = jnp.dot(routed, earlier, preferred_element_type=F32)
    cnt_b = jnp.dot(routed, in_tile, preferred_element_type=F32)
    run_start = jnp.dot(lower, cnt_b.astype(BF16), preferred_element_type=F32)
    base = run_start + before
    pos_ref[...] = jnp.concatenate(
        [jnp.sum(jnp.where(pk, base, 0.0), axis=0, keepdims=True) for pk in picks],
        axis=0).astype(jnp.int32)
    tile_sel = jnp.where(lax.broadcasted_iota(jnp.int32, (8, tm), 0)
                         == lax.broadcasted_iota(jnp.int32, (8, tm), 1) // tt, 1.0, 0.0).astype(BF16)
    cnt8 = lax.dot_general(tile_sel, routed, (((1,), (1,)), ((), ())),
                           preferred_element_type=F32)
    cnt_ref[0] = jnp.concatenate([cnt8, jnp.zeros((8, 128 - ne), F32)], axis=1).astype(jnp.int32)

    hb = h2.astype(BF16)
    n_rows = TOP_K * tt
    row_id = lax.broadcasted_iota(jnp.int32, (n_rows, tt), 0)
    for j in range(tm // tt):
        hit = row_id == pos_ref[0:1, j * tt:(j + 1) * tt]
        for k in range(1, TOP_K):
            hit = hit | (row_id == pos_ref[k:k + 1, j * tt:(j + 1) * tt])
        onehot = jnp.where(hit, 1.0, 0.0).astype(BF16)
        rows = jnp.dot(onehot, hb[j * tt:(j + 1) * tt, :], preferred_element_type=F32)
        _store_rows(stg_ref.at[pl.ds(j * n_rows * ROW_SUB, n_rows * ROW_SUB)],
                    _pack_bf16_pair(rows[:, 0:d // 2], rows[:, d // 2:d]))


def _post_call(o2d, ga, g1, x2d, mod, w_mo, w_out, g_ffn, w_rt, b_r, *, seq):
    t, d = x2d.shape
    tm = TM_POST
    tps = seq // tm
    row = lambda i: (i, 0)
    n_exp = b_r.shape[0]
    assert tm % TM_FIN == 0 and tm // TM_FIN <= 8 and n_exp <= 128
    kern = functools.partial(_post_kernel, d=d)
    return pl.pallas_call(
        kern,
        grid=(t // tm,),
        in_specs=[
            pl.BlockSpec((tm, d), row),
            pl.BlockSpec((tm, d), row),
            pl.BlockSpec((tm, d), row),
            pl.BlockSpec((tm, d), row),
            pl.BlockSpec((1, 6, d), lambda i: (i // tps, 0, 0)),
            _const_spec(w_mo.shape),
            _const_spec(w_out.shape),
            _const_spec((1, d)),
            _const_spec(w_rt.shape),
            _const_spec(b_r.shape),
        ],
        out_specs=[
            pl.BlockSpec((tm, d), row),
            pl.BlockSpec((tm * TOP_K * ROW_SUB, 128), row),
            pl.BlockSpec((TOP_K, tm), lambda i: (0, i)),
            pl.BlockSpec((TOP_K, tm), lambda i: (0, i)),
            pl.BlockSpec((1, 8, 128), lambda i: (i, 0, 0)),
        ],
        out_shape=[
            jax.ShapeDtypeStruct((t, d), F32),
            jax.ShapeDtypeStruct((t * TOP_K * ROW_SUB, 128), jnp.uint32),
            jax.ShapeDtypeStruct((TOP_K, t), jnp.int32),
            jax.ShapeDtypeStruct((TOP_K, t), F32),
            jax.ShapeDtypeStruct((t // tm, 8, 128), jnp.int32),
        ],
        compiler_params=pltpu.CompilerParams(dimension_semantics=("arbitrary",),
                                             vmem_limit_bytes=VMEM_LIMIT),
        name="post_attn_router",
    )(o2d, ga, g1, x2d, mod, w_mo, w_out, g_ffn, w_rt, b_r)


RUN_BITS = 9
COMMON_RUN_BITS = 6
TAIL_BITS = 9


def _copy_pieces(count, src_ref, src_off, dst_ref, dst_off, sem, bits, wait=False):
    def pieces(bit_range, off):
        for b in reversed(bit_range):
            size = (1 << b) * ROW_SUB
            take = (count >> b) & 1

            @pl.when(take == 1)
            def _(off=off, size=size):
                cp = pltpu.make_async_copy(
                    src_ref.at[pl.ds(pl.multiple_of(src_off + off, ROW_SUB), size)],
                    dst_ref.at[pl.ds(pl.multiple_of(dst_off + off, ROW_SUB), size)], sem)
                if wait:
                    cp.wait()
                else:
                    cp.start()
            off = off + take * size

    lo = min(bits, COMMON_RUN_BITS)
    big = (count >> lo) << lo

    if lo < bits:
        @pl.when(big != 0)
        def _():
            pieces(range(lo, bits), 0)
    pieces(range(lo), big * ROW_SUB)


def _dispatch_kernel(tab_ref, tail_ref, nused_ref, stg_ref, xs_hbm, zbuf, sem, zsem, *,
                     tm, n_exp, n_blocks, tiles_per_step):
    i = pl.program_id(0)
    n_rows = TOP_K * tm

    @pl.when(i == 0)
    def _():
        zbuf[...] = jnp.zeros(zbuf.shape, zbuf.dtype)
        for wait in (False, True):
            def body(e, c, wait=wait):
                _copy_pieces(tail_ref[1, e], zbuf, 0, xs_hbm, tail_ref[0, e], zsem, TAIL_BITS, wait)
                return c
            lax.fori_loop(0, n_exp, body, 0)
        blk_rows = zbuf.shape[0]

        def zero_block(b, c):
            cp = pltpu.make_async_copy(
                zbuf, xs_hbm.at[pl.ds(pl.multiple_of(b * blk_rows, blk_rows), blk_rows)], zsem)
            cp.start()
            cp.wait()
            return c
        lax.fori_loop(nused_ref[0], n_blocks, zero_block, 0)

    for j in range(tiles_per_step):
        def body(e, c, j=j):
            _copy_pieces(tab_ref[j, 0, e], stg_ref, j * n_rows * ROW_SUB + tab_ref[j, 2, e], xs_hbm,
                         tab_ref[j, 1, e], sem, RUN_BITS)
            return c
        lax.fori_loop(0, n_exp, body, 0)

    for _ in range(tiles_per_step):
        pltpu.make_async_copy(stg_ref.at[pl.ds(0, n_rows * ROW_SUB)],
                              xs_hbm.at[pl.ds(0, n_rows * ROW_SUB)], sem).wait()


DISPATCH_TILES_PER_STEP = 8


def _dispatch_call(tabs, tail, n_used, staged, *, n_xs_rows):
    n_tiles, _, n_exp = tabs.shape
    tm = TM_FIN
    tps = DISPATCH_TILES_PER_STEP
    assert n_tiles % tps == 0
    nt = n_tiles // tps
    assert nt >= 2 and tm < (1 << RUN_BITS) and MOE_BLK <= (1 << TAIL_BITS)
    kern = functools.partial(_dispatch_kernel, tm=tm, n_exp=n_exp, n_blocks=n_xs_rows // MOE_BLK,
                             tiles_per_step=tps)
    return pl.pallas_call(
        kern,
        grid=(nt,),
        in_specs=[
            pl.BlockSpec((tps, 3, n_exp), lambda i: (i, 0, 0), memory_space=pltpu.SMEM),
            pl.BlockSpec(memory_space=pltpu.SMEM),
            pl.BlockSpec(memory_space=pltpu.SMEM),
            pl.BlockSpec((tps * TOP_K * tm * ROW_SUB, 128), lambda i: (i, 0)),
        ],
        out_specs=pl.BlockSpec(memory_space=pl.ANY),
        out_shape=jax.ShapeDtypeStruct((n_xs_rows * ROW_SUB, 128), jnp.uint32),
        scratch_shapes=[pltpu.VMEM((MOE_BLK * ROW_SUB, 128), jnp.uint32),
                        pltpu.SemaphoreType.DMA(()),
                        pltpu.SemaphoreType.DMA(())],
        compiler_params=pltpu.CompilerParams(dimension_semantics=("arbitrary",),
                                             vmem_limit_bytes=VMEM_LIMIT),
        name="moe_dispatch",
    )(tabs, tail, n_used, staged)


def _moe_kernel(blk_e_ref, nused_ref, x_ref, perm_ref, wgu_ref, bgu_ref, wd_ref, bd_ref, o_ref,
                wgu_scr, wd_scr, *, blk, f):
    s = pl.program_id(0)
    e_cur = blk_e_ref[s]
    e_prev = blk_e_ref[jnp.maximum(s - 1, 0)]

    @pl.when(jnp.logical_or(s == 0, e_cur != e_prev))
    def _():
        perm = perm_ref[...]
        for c in range(2 * f // 256):
            r = jnp.dot(wgu_ref[0, :, c * 256:(c + 1) * 256].astype(BF16), perm,
                        preferred_element_type=F32)
            wgu_scr[:, c * 128:(c + 1) * 128] = r[:, 0:128].astype(BF16)
            wgu_scr[:, f + c * 128:f + (c + 1) * 128] = r[:, 128:256].astype(BF16)
        wd_scr[...] = wd_ref[0].astype(BF16)

    @pl.when(s >= nused_ref[0])
    def _():
        o_ref[...] = jnp.zeros(o_ref.shape, o_ref.dtype)

    @pl.when(s < nused_ref[0])
    def _():
        x = _load_rows(x_ref, blk).astype(BF16)
        gu = jnp.dot(x, wgu_scr[...], preferred_element_type=F32) + bgu_ref[0]
        gate = jnp.minimum(gu[:, 0:f], SWIGLU_LIMIT)
        up = jnp.clip(gu[:, f:2 * f], -SWIGLU_LIMIT, SWIGLU_LIMIT)
        act = (up + 1.0) * (gate * jax.nn.sigmoid(SWIGLU_ALPHA * gate))
        y = jnp.dot(act.astype(BF16), wd_scr[...], preferred_element_type=F32) + bd_ref[0]
        dh = y.shape[1] // 2
        _store_rows(o_ref, _pack_bf16_pair(y[:, 0:dh], y[:, dh:]))


def _moe_call(blk_e, n_used, xs, perm, w_gu, b_gu_p, w_down, b_down):
    n_steps = blk_e.shape[0]
    blk = MOE_BLK
    _, d, f2 = w_gu.shape
    f = f2 // 2
    rows = blk * ROW_SUB
    kern = functools.partial(_moe_kernel, blk=blk, f=f)
    row_blk = lambda i, be, nu: (i, 0)
    grid_spec = pltpu.PrefetchScalarGridSpec(
        num_scalar_prefetch=2,
        grid=(n_steps,),
        in_specs=[
            pl.BlockSpec((rows, 128), row_blk),
            pl.BlockSpec((256, 256), lambda i, be, nu: (0, 0)),
            pl.BlockSpec((1, d, f2), lambda i, be, nu: (be[i], 0, 0)),
            pl.BlockSpec((1, 1, f2), lambda i, be, nu: (be[i], 0, 0)),
            pl.BlockSpec((1, f, d), lambda i, be, nu: (be[i], 0, 0)),
            pl.BlockSpec((1, 1, d), lambda i, be, nu: (be[i], 0, 0)),
        ],
        out_specs=pl.BlockSpec((rows, 128), row_blk),
        scratch_shapes=[
            pltpu.VMEM((d, f2), BF16),
            pltpu.VMEM((f, d), BF16),
        ],
    )
    return pl.pallas_call(
        kern,
        grid_spec=grid_spec,
        out_shape=jax.ShapeDtypeStruct((n_steps * rows, 128), jnp.uint32),
        compiler_params=pltpu.CompilerParams(dimension_semantics=("arbitrary",),
                                             vmem_limit_bytes=VMEM_LIMIT),
        name="moe_experts",
    )(blk_e, n_used, xs, perm, w_gu, b_gu_p, w_down, b_down)


def _final_kernel(tab_cur_ref, tab_nxt_ref, x1_ref, pos_ref, w_ref, mod_ref, fmod_ref, gfin_ref, ys_hbm,
                  o_ref, stage, sem, *, tm, n_exp):
    i = pl.program_id(0)
    n = pl.num_programs(0)
    slot = i % 2
    n_rows = TOP_K * tm

    tiles = range(TILES_PER_STEP)

    def issue_tiles(tab_ref, slot_):
        for j in tiles:
            def body(e, c, j=j):
                _copy_pieces(tab_ref[j, 0, e], ys_hbm, tab_ref[j, 1, e], stage.at[slot_, j],
                             tab_ref[j, 2, e], sem.at[slot_], RUN_BITS)
                return c
            lax.fori_loop(0, n_exp, body, 0)

    @pl.when(i == 0)
    def _():
        issue_tiles(tab_cur_ref, 0)

    @pl.when(i + 1 < n)
    def _():
        issue_tiles(tab_nxt_ref, 1 - slot)

    for j in tiles:
        pltpu.make_async_copy(ys_hbm.at[pl.ds(0, n_rows * ROW_SUB)], stage.at[slot, j],
                              sem.at[slot]).wait()

    ys = [_load_rows(stage.at[slot, j], n_rows).astype(BF16) for j in tiles]
    col_id = lax.broadcasted_iota(jnp.int32, (tm, n_rows), 1)
    wmats = []
    for j in tiles:
        pos = pos_ref[j * tm:(j + 1) * tm, :]
        w = w_ref[j * tm:(j + 1) * tm, :]
        a = jnp.where(col_id == pos[:, 0:1], w[:, 0:1], 0.0)
        for k in range(1, TOP_K):
            a = a + jnp.where(col_id == pos[:, k:k + 1], w[:, k:k + 1], 0.0)
        wmats.append(a.astype(BF16))
    moes = [jnp.dot(wmats[j], ys[j], preferred_element_type=F32) for j in tiles]
    gate2 = mod_ref[0, 5:6, :]
    fshift = fmod_ref[0, 0:1, :]
    fscale = fmod_ref[0, 1:2, :]
    for j in tiles:
        x2 = x1_ref[j * tm:(j + 1) * tm, :] + gate2 * moes[j]
        o_ref[j * tm:(j + 1) * tm, :] = _rms(x2, gfin_ref[...]) * (1.0 + fscale) + fshift


def _final_call(tabs, x1, pos_col, w_col, mod, fmod, g_final, ys, *, seq):
    t, d = x1.shape
    tm = TM_FIN
    assert tm < (1 << RUN_BITS)
    tls = TILES_PER_STEP
    tms = tls * tm
    assert seq % tms == 0
    tps = seq // tms
    nt = t // tms
    n_exp = tabs.shape[2]
    n_rows = TOP_K * tm
    kern = functools.partial(_final_kernel, tm=tm, n_exp=n_exp)
    tab_spec = lambda ahead: pl.BlockSpec((tls, 3, n_exp), lambda i: (jnp.minimum(i + ahead, nt - 1), 0, 0),
                                          memory_space=pltpu.SMEM)
    return pl.pallas_call(
        kern,
        grid=(nt,),
        in_specs=[
            tab_spec(0),
            tab_spec(1),
            pl.BlockSpec((tms, d), lambda i: (i, 0)),
            pl.BlockSpec((tms, TOP_K), lambda i: (i, 0)),
            pl.BlockSpec((tms, TOP_K), lambda i: (i, 0)),
            pl.BlockSpec((1, 6, d), lambda i: (i // tps, 0, 0)),
            pl.BlockSpec((1, 2, d), lambda i: (i // tps, 0, 0)),
            _const_spec((1, d)),
            pl.BlockSpec(memory_space=pl.ANY),
        ],
        out_specs=pl.BlockSpec((tms, d), lambda i: (i, 0)),
        out_shape=jax.ShapeDtypeStruct((t, d), F32),
        scratch_shapes=[pltpu.VMEM((2, tls, n_rows * ROW_SUB, 128), jnp.uint32),
                        pltpu.SemaphoreType.DMA((2,))],
        compiler_params=pltpu.CompilerParams(dimension_semantics=("arbitrary",),
                                             vmem_limit_bytes=VMEM_LIMIT),
        name="combine_final",
    )(tabs, tabs, x1, pos_col, w_col, mod, fmod, g_final, ys)


def _prep_w_in(w_in, d, q_lora, kv_lora):
    o_kpe = d + q_lora + kv_lora
    o_g = o_kpe + QK_ROPE
    half = QK_ROPE // 2
    kpe = w_in[:, o_kpe:o_kpe + QK_ROPE]
    zpad = jnp.zeros((d, 128 - QK_ROPE), w_in.dtype)
    ksw = jnp.concatenate([-kpe[:, half:], kpe[:, :half]], axis=1)
    return jnp.concatenate([w_in[:, :o_kpe], kpe, zpad, ksw, zpad, w_in[:, o_g:]], axis=1).astype(BF16)


def _prep_w_q(w_q_b):
    ql = w_q_b.shape[0]
    hd = QK_NOPE + QK_ROPE
    half = QK_ROPE // 2
    w = w_q_b.reshape(ql, N_HEADS, hd)
    nope = w[:, :, :QK_NOPE]
    pe = w[:, :, QK_NOPE:]
    sw = jnp.concatenate([-pe[:, :, half:], pe[:, :, :half]], axis=2)
    return jnp.concatenate([nope.reshape(ql, -1), pe.reshape(ql, -1), sw.reshape(ql, -1)],
                           axis=1).astype(BF16)


def _prep_w_kv(w_kv_b):
    kvl = w_kv_b.shape[0]
    w = w_kv_b.reshape(kvl, N_HEADS, QK_NOPE + V_HEAD)
    w_kn = w[:, :, :QK_NOPE].reshape(kvl, -1).astype(BF16)
    w_vt = w[:, :, QK_NOPE:].reshape(kvl, -1).T.astype(BF16)
    return w_kn, w_vt


def kernel(x, c, positions, w_mod, b_mod, g_mix, w_in, b_gate, w_pool_grp, pool_scale, w_pool_out,
           g_q_a, w_q_b, g_kv_a, w_kv_b, w_mla_out, w_out, g_ffn, w_router, b_router, w_gu, b_gu,
           w_down, b_down, g_final, w_fmod, b_fmod):
    bsz, seq, d = x.shape
    t = bsz * seq
    depth = w_mod.shape[0]
    assert depth == 1
    assert seq % TQ == 0 and seq % TM_IN == 0 and seq % TM_POST == 0 and seq % TM_FIN == 0
    q_lora = g_q_a.shape[-1]
    kv_lora = g_kv_a.shape[-1]
    n_exp = w_gu.shape[1]
    f = w_gu.shape[-1] // 2
    blk = MOE_BLK

    x2d = x.reshape(t, d)
    pos_col = positions.astype(F32).reshape(t, 1)
    inv_freq = 1.0 / (ROPE_THETA ** (jnp.arange(0, QK_ROPE, 2, dtype=F32) / QK_ROPE))
    invf2 = jnp.tile(inv_freq, 128 // (QK_ROPE // 2)).reshape(1, 128)

    mod = _mod_call(c, w_mod[0], b_mod[0]).reshape(bsz, 6, d)
    fmod = _mod_call(c, w_fmod, b_fmod).reshape(bsz, 2, d)

    w_in_p = _prep_w_in(w_in[0], d, q_lora, kv_lora)
    w_q_p = _prep_w_q(w_q_b[0])
    w_kn, w_vt = _prep_w_kv(w_kv_b[0])
    ga, g1, q2, k2, vt3 = _mixer_in_call(
        x2d, pos_col, mod, g_mix[0].reshape(1, d), w_in_p, b_gate[0].reshape(1, 2 * d),
        w_pool_grp[0].astype(BF16), pool_scale[0].reshape(1, d), w_pool_out[0].astype(BF16),
        g_q_a[0].reshape(1, q_lora), w_q_p, g_kv_a[0].reshape(1, kv_lora), w_kn, w_vt, invf2,
        bsz=bsz, seq=seq)

    hp = N_HEADS * HEAD_PAD
    o = _attn_call(q2.reshape(bsz, seq, hp), k2.reshape(bsz, seq, hp), vt3)

    w_r = w_router[0]
    w_r_hi = w_r.astype(BF16)
    w_r_lo = (w_r - w_r_hi.astype(F32)).astype(BF16)
    lane_pad = lambda a: jnp.pad(a, ((0, 0), (0, 128 - n_exp)))
    w_rt = jnp.concatenate(
        [jnp.concatenate([lane_pad(w_r_hi), lane_pad(w_r_lo)], axis=1),
         jnp.concatenate([lane_pad(w_r_hi), jnp.zeros((d, 128), BF16)], axis=1)], axis=0)
    x1, staged, pos_t, wgt_t, cnt3 = _post_call(
        o.reshape(t, d), ga, g1, x2d, mod, w_mla_out[0].astype(BF16), w_out[0].astype(BF16),
        g_ffn[0].reshape(1, d), w_rt, b_router[0].reshape(n_exp, 1), seq=seq)

    n_slots = t * TOP_K
    n_rows = n_slots + n_exp * blk
    n_blocks = n_rows // blk
    cnt = cnt3[:, 0:TM_POST // TM_FIN, 0:n_exp].reshape(-1, n_exp)
    counts = jnp.sum(cnt, axis=0)
    padded = (counts + blk - 1) // blk * blk
    pad_end = jnp.cumsum(padded)
    pad_start = pad_end - padded
    blk_start = jnp.arange(n_blocks, dtype=jnp.int32) * blk
    blk_e = jnp.minimum(jnp.sum(pad_end[None, :] <= blk_start[:, None], axis=1, dtype=jnp.int32),
                        n_exp - 1)
    n_used = (pad_end[-1:] // blk).astype(jnp.int32)
    tail = jnp.stack([(pad_start + counts) * ROW_SUB, padded - counts], axis=0).astype(jnp.int32)

    before = jnp.cumsum(cnt, axis=0) - cnt
    run_start = jnp.cumsum(cnt, axis=1) - cnt
    tabs = jnp.stack([cnt, (pad_start[None, :] + before) * ROW_SUB, run_start * ROW_SUB], axis=1)

    col = jnp.arange(256, dtype=jnp.int32)[None, :]
    row = jnp.arange(256, dtype=jnp.int32)[:, None]
    perm = (row == jnp.where(col < 128, 2 * col, 2 * (col - 128) + 1)).astype(BF16)
    b_gu_p = jnp.concatenate([b_gu[0][:, 0::2], b_gu[0][:, 1::2]], axis=-1).reshape(n_exp, 1, 2 * f)
    xs = _dispatch_call(tabs, tail, n_used, staged, n_xs_rows=n_rows)
    ys = _moe_call(blk_e, n_used, xs, perm, w_gu[0], b_gu_p, w_down[0], b_down[0].reshape(n_exp, 1, d))

    out = _final_call(tabs, x1, pos_t.T, wgt_t.T, mod, fmod, g_final.reshape(1, d), ys, seq=seq)
    return out.reshape(bsz, seq, d)
```

```python
import functools

import jax
import jax.numpy as jnp
from jax import lax
from jax.experimental import pallas as pl
from jax.experimental.pallas import tpu as pltpu

F32 = jnp.float32
BF16 = jnp.bfloat16

CHUNK = 64
POOL_WINDOWS = (2, 4, 8, 16)
POOL_HALO = 16
N_HEADS = 8
QK_NOPE = 128
QK_ROPE = 64
V_HEAD = 128
HEAD_PAD = 256
ROPE_THETA = 10000.0
TOP_K = 4
SWIGLU_LIMIT = 7.0
SWIGLU_ALPHA = 1.702
NORM_EPS = 1e-6
NEG_INF = -1e30
LOG2_E = 1.4426950408889634

VMEM_LIMIT = 56 * 1024 * 1024

TM_IN = 256
TQ = 512
TM_POST = 512
MOE_BLK = 512
TM_FIN = 256
TILES_PER_STEP = 2
ROW_SUB = 4


def _const_spec(shape):
    nd = len(shape)
    return pl.BlockSpec(shape, lambda *_: (0,) * nd, pipeline_mode=pl.Buffered(1))


def _rms(xf, g):
    return xf * lax.rsqrt(jnp.mean(xf * xf, axis=-1, keepdims=True) + NORM_EPS) * g


def _pack_bf16_pair(lo, hi):
    lo_b = lax.bitcast_convert_type(lo.astype(BF16).astype(F32), jnp.uint32)
    hi_b = lax.bitcast_convert_type(hi.astype(BF16).astype(F32), jnp.uint32)
    return (hi_b & jnp.uint32(0xFFFF0000)) | (lo_b >> 16)


def _unpack_bf16_pair(p):
    lo = lax.bitcast_convert_type(p << 16, F32)
    hi = lax.bitcast_convert_type(p & jnp.uint32(0xFFFF0000), F32)
    return lo, hi


def _store_rows(ref, packed):
    n = packed.shape[0]
    for q in range(ROW_SUB):
        ref[pl.ds(q, n, stride=ROW_SUB), :] = packed[:, q * 128:(q + 1) * 128]


def _load_rows(ref, n):
    los, his = [], []
    for q in range(ROW_SUB):
        lo, hi = _unpack_bf16_pair(ref[pl.ds(q, n, stride=ROW_SUB), :])
        los.append(lo)
        his.append(hi)
    return jnp.concatenate(los + his, axis=1)


def _mod_kernel(c_ref, w_ref, b_ref, o_ref):
    c = c_ref[...]
    c_act = c * jax.nn.sigmoid(c)
    o_ref[...] = jnp.dot(c_act, w_ref[...], preferred_element_type=F32,
                         precision=lax.Precision.HIGHEST) + b_ref[...]


def _mod_call(c, w, b, tn=1024):
    bsz, d = c.shape
    n = w.shape[1]
    return pl.pallas_call(
        _mod_kernel,
        grid=(n // tn,),
        in_specs=[pl.BlockSpec((bsz, d), lambda j: (0, 0)),
                  pl.BlockSpec((d, tn), lambda j: (0, j)),
                  pl.BlockSpec((1, tn), lambda j: (0, j))],
        out_specs=pl.BlockSpec((bsz, tn), lambda j: (0, j)),
        out_shape=jax.ShapeDtypeStruct((bsz, n), F32),
        compiler_params=pltpu.CompilerParams(dimension_semantics=("arbitrary",),
                                             vmem_limit_bytes=VMEM_LIMIT),
        name="adaln_mod",
    )(c, w, b.reshape(1, n))


def _mixer_in_kernel(x_ref, xh_ref, pos_ref, mod_ref, gmix_ref, win_ref, bgate_ref, wgrp_ref,
                     pscale_ref, wpo_ref, gq_ref, wq_ref, gkv_ref, wkn_ref, wvt_ref, invf_ref,
                     ga_ref, g1_ref, q_ref, k_ref, vt_ref, u_scr, *, tm, tiles_per_seq, d, q_lora,
                     kv_lora):
    i = pl.program_id(0)
    t_in_seq = i % tiles_per_seq
    is_start = t_in_seq == 0
    shift1 = mod_ref[0, 0:1, :]
    scale1 = mod_ref[0, 1:2, :]
    gmix = gmix_ref[...]

    def prenorm(xf):
        return (_rms(xf, gmix) * (1.0 + scale1) + shift1).astype(BF16)

    h = prenorm(x_ref[...])
    hh = prenorm(xh_ref[...])
    u_ext = jnp.dot(jnp.concatenate([hh, h], axis=0), win_ref[:, 0:d],
                    preferred_element_type=F32)
    u_scr[0:POOL_HALO, :] = jnp.where(is_start, 0.0, u_ext[0:POOL_HALO, :])
    u_scr[POOL_HALO:POOL_HALO + tm, :] = u_ext[POOL_HALO:, :]
    rest = jnp.dot(h, win_ref[:, d:], preferred_element_type=F32)

    gw = d // len(POOL_WINDOWS)
    tseq = t_in_seq * tm + lax.broadcasted_iota(jnp.int32, (tm, 1), 0)
    ys = []
    for g, w in enumerate(POOL_WINDOWS):
        c0 = g * gw
        ug = u_scr[POOL_HALO:POOL_HALO + tm, c0:c0 + gw]
        acc = ug
        for j in range(1, w):
            acc = acc + u_scr[POOL_HALO - j:POOL_HALO - j + tm, c0:c0 + gw]
        cnt = jnp.minimum(tseq + 1, w).astype(F32)
        mixed = (acc / cnt - ug).astype(BF16)
        ys.append(jnp.dot(mixed, wgrp_ref[g], preferred_element_type=F32))
    y = (jnp.concatenate(ys, axis=1) * pscale_ref[...]).astype(BF16)
    a = jnp.dot(y, wpo_ref[...], preferred_element_type=F32)

    o_q = 0
    o_kv = q_lora
    o_kpe = q_lora + kv_lora
    o_ksw = o_kpe + 128
    o_g0 = o_ksw + 128
    o_g1 = o_g0 + d
    gates0 = jax.nn.sigmoid(rest[:, o_g0:o_g0 + d] + bgate_ref[:, 0:d])
    gates1 = jax.nn.sigmoid(rest[:, o_g1:o_g1 + d] + bgate_ref[:, d:2 * d])
    ga_ref[...] = (gates0 * a).astype(BF16)
    g1_ref[...] = gates1.astype(BF16)

    ang = pos_ref[...] * invf_ref[...]
    cos2 = jnp.cos(ang)
    sin2 = jnp.sin(ang)
    q_scale = float(QK_NOPE + QK_ROPE) ** -0.5 * LOG2_E

    qn = _rms(rest[:, o_q:o_q + q_lora], gq_ref[...]).astype(BF16)
    qall = jnp.dot(qn, wq_ref[...], preferred_element_type=F32)
    kvn = _rms(rest[:, o_kv:o_kv + kv_lora], gkv_ref[...]).astype(BF16)
    kn = jnp.dot(kvn, wkn_ref[...], preferred_element_type=F32)
    vt = lax.dot_general(wvt_ref[...], kvn, (((1,), (1,)), ((), ())),
                         preferred_element_type=F32)
    vt_ref[0] = vt.astype(BF16)
    kpe = (rest[:, o_kpe:o_kpe + 128] * cos2 + rest[:, o_ksw:o_ksw + 128] * sin2).astype(BF16)
    o_pe = N_HEADS * QK_NOPE
    o_sw = o_pe + N_HEADS * QK_ROPE
    low_half = lax.broadcasted_iota(jnp.int32, (tm, 128), 1) < QK_ROPE
    for pair in range(N_HEADS // 2):
        c0 = pair * 128
        qpe2 = (qall[:, o_pe + c0:o_pe + c0 + 128] * cos2
                + qall[:, o_sw + c0:o_sw + c0 + 128] * sin2) * q_scale
        for hd, part in ((2 * pair, qpe2), (2 * pair + 1, pltpu.roll(qpe2, QK_ROPE, axis=1))):
            n0 = hd * QK_NOPE
            q_ref[:, hd * HEAD_PAD:hd * HEAD_PAD + 128] = (qall[:, n0:n0 + 128] * q_scale).astype(BF16)
            q_ref[:, hd * HEAD_PAD + 128:(hd + 1) * HEAD_PAD] = jnp.where(low_half, part, 0.0).astype(BF16)
            k_ref[:, hd * HEAD_PAD:hd * HEAD_PAD + 128] = kn[:, n0:n0 + 128].astype(BF16)
            k_ref[:, hd * HEAD_PAD + 128:(hd + 1) * HEAD_PAD] = kpe


def _mixer_in_call(x2d, pos_col, mod, g_mix, w_in_p, b_gate, w_grp, pool_scale, w_po, g_q_a, w_q_p,
                   g_kv_a, w_kn, w_vt, invf2, *, bsz, seq):
    t, d = x2d.shape
    tm = TM_IN
    tps = seq // tm
    q_lora = g_q_a.shape[-1]
    kv_lora = g_kv_a.shape[-1]
    hp = N_HEADS * HEAD_PAD
    halo_blocks = tm // POOL_HALO
    kern = functools.partial(_mixer_in_kernel, tm=tm, tiles_per_seq=tps, d=d, q_lora=q_lora,
                             kv_lora=kv_lora)
    row = lambda i: (i, 0)
    return pl.pallas_call(
        kern,
        grid=(t // tm,),
        in_specs=[
            pl.BlockSpec((tm, d), row),
            pl.BlockSpec((POOL_HALO, d), lambda i: (jnp.maximum(i * halo_blocks - 1, 0), 0)),
            pl.BlockSpec((tm, 1), row),
            pl.BlockSpec((1, 6, d), lambda i: (i // tps, 0, 0)),
            _const_spec((1, d)),
            _const_spec(w_in_p.shape),
            _const_spec((1, 2 * d)),
            _const_spec(w_grp.shape),
            _const_spec((1, d)),
            _const_spec(w_po.shape),
            _const_spec((1, q_lora)),
            _const_spec(w_q_p.shape),
            _const_spec((1, kv_lora)),
            _const_spec(w_kn.shape),
            _const_spec(w_vt.shape),
            _const_spec((1, 128)),
        ],
        out_specs=[
            pl.BlockSpec((tm, d), row),
            pl.BlockSpec((tm, d), row),
            pl.BlockSpec((tm, hp), row),
            pl.BlockSpec((tm, hp), row),
            pl.BlockSpec((1, N_HEADS * V_HEAD, tm), lambda i: (i // tps, 0, i % tps)),
        ],
        out_shape=[
            jax.ShapeDtypeStruct((t, d), BF16),
            jax.ShapeDtypeStruct((t, d), BF16),
            jax.ShapeDtypeStruct((t, hp), BF16),
            jax.ShapeDtypeStruct((t, hp), BF16),
            jax.ShapeDtypeStruct((bsz, N_HEADS * V_HEAD, seq), BF16),
        ],
        scratch_shapes=[pltpu.VMEM((tm + POOL_HALO, d), F32)],
        compiler_params=pltpu.CompilerParams(
            dimension_semantics=("arbitrary",), vmem_limit_bytes=VMEM_LIMIT,
            allow_input_fusion=[i in (5, 7, 9, 11, 13, 14) for i in range(16)]),
        name="mixer_in",
    )(x2d, x2d, pos_col, mod, g_mix, w_in_p, b_gate, w_grp, pool_scale, w_po, g_q_a, w_q_p, g_kv_a,
      w_kn, w_vt, invf2)


def _attn_kernel(q_ref, k_ref, vt_ref, o_ref, *, tq, nq):
    seq = vt_ref.shape[2]
    vt_ext = jnp.concatenate([vt_ref[0], jnp.ones((16, seq), BF16)], axis=0)

    def scores(qi):
        q0 = qi * tq
        ln = q0 + tq
        return lax.dot_general(k_ref[0, 0:ln, :], q_ref[0, q0:q0 + tq, :], (((1,), (1,)), ((), ())),
                               preferred_element_type=F32)

    def finish(qi, s):
        q0 = qi * tq
        ln = q0 + tq
        cw = tq // 2
        for c in range(2):
            cs = slice(c * cw, (c + 1) * cw)
            kc = (q0 + lax.broadcasted_iota(jnp.int32, (tq, cw), 0)) // CHUNK
            qc = (q0 + c * cw + lax.broadcasted_iota(jnp.int32, (tq, cw), 1)) // CHUNK
            s_diag = jnp.where(qc >= kc, s[q0:ln, cs], NEG_INF)
            m = jnp.max(s_diag, axis=0, keepdims=True)
            if qi > 0:
                s_main = s[0:q0, cs]
                m = jnp.maximum(m, jnp.max(s_main, axis=0, keepdims=True))
            p_diag = jnp.exp2(s_diag - m).astype(BF16)
            acc = jnp.dot(vt_ext[:, q0:ln], p_diag, preferred_element_type=F32)
            if qi > 0:
                p_main = jnp.exp2(s_main - m).astype(BF16)
                acc = acc + jnp.dot(vt_ext[:, 0:q0], p_main, preferred_element_type=F32)
            l = acc[V_HEAD:V_HEAD + 1, :]
            o_ref[0, q0 + c * cw:q0 + (c + 1) * cw, :] = (acc[0:V_HEAD, :] / l).T.astype(BF16)

    order = list(range(nq))[::-1]
    s_cur = scores(order[0])
    for idx, qi in enumerate(order):
        s_next = scores(order[idx + 1]) if idx + 1 < nq else None
        finish(qi, s_cur)
        s_cur = s_next


def _attn_call(q3, k3, vt3):
    bsz, seq, _ = q3.shape
    kern = functools.partial(_attn_kernel, tq=TQ, nq=seq // TQ)
    return pl.pallas_call(
        kern,
        grid=(bsz, N_HEADS),
        in_specs=[pl.BlockSpec((1, seq, HEAD_PAD), lambda b, h: (b, 0, h)),
                  pl.BlockSpec((1, seq, HEAD_PAD), lambda b, h: (b, 0, h)),
                  pl.BlockSpec((1, V_HEAD, seq), lambda b, h: (b, h, 0))],
        out_specs=pl.BlockSpec((1, seq, V_HEAD), lambda b, h: (b, 0, h)),
        out_shape=jax.ShapeDtypeStruct((bsz, seq, N_HEADS * V_HEAD), BF16),
        compiler_params=pltpu.CompilerParams(dimension_semantics=("arbitrary", "arbitrary"),
                                             vmem_limit_bytes=VMEM_LIMIT),
        name="mla_attn",
    )(q3, k3, vt3)


def _post_kernel(o_ref, ga_ref, g1_ref, x_ref, mod_ref, wmo_ref, wout_ref, gffn_ref, wrt_ref, br_ref,
                 x1_ref, h2_ref, pos_ref, wgt_ref, cnt_ref, *, d):
    m = jnp.dot(o_ref[...], wmo_ref[...], preferred_element_type=F32)
    merged = ga_ref[...].astype(F32) + g1_ref[...].astype(F32) * m
    gate1 = mod_ref[0, 2:3, :]
    x1 = x_ref[...] + gate1 * jnp.dot(merged.astype(BF16), wout_ref[...],
                                      preferred_element_type=F32)
    x1_ref[...] = x1
    shift2 = mod_ref[0, 3:4, :]
    scale2 = mod_ref[0, 4:5, :]
    h2 = _rms(x1, gffn_ref[...]) * (1.0 + scale2) + shift2
    _store_rows(h2_ref, _pack_bf16_pair(h2[:, 0:d // 2], h2[:, d // 2:d]))

    ne = br_ref.shape[0]
    h_hi = h2.astype(BF16)
    h_lo = (h2 - h_hi.astype(F32)).astype(BF16)
    hcat = jnp.concatenate([h_hi, h_lo], axis=1)
    half = hcat.shape[0] // 2
    lg2 = jnp.concatenate(
        [jnp.dot(hcat[0:half], wrt_ref[...], preferred_element_type=F32),
         jnp.dot(hcat[half:], wrt_ref[...], preferred_element_type=F32)], axis=0)
    lg = lg2[:, 0:128] + lg2[:, 128:256]
    logits = lg.T[0:ne, :] + br_ref[...]
    tm = logits.shape[1]
    eid = lax.broadcasted_iota(jnp.int32, (ne, tm), 0)
    vals, idxs = [], []
    cur = logits
    for _ in range(TOP_K):
        mx = jnp.max(cur, axis=0, keepdims=True)
        ix = jnp.min(jnp.where(cur == mx, eid, ne), axis=0, keepdims=True)
        vals.append(mx)
        idxs.append(ix)
        cur = jnp.where(eid == ix, -jnp.inf, cur)
    es = [jnp.exp(v - vals[0]) for v in vals]
    den = es[0] + es[1] + es[2] + es[3]
    wgt_ref[...] = jnp.concatenate([e / den for e in es], axis=0)

    tt = TM_FIN
    t_row = lax.broadcasted_iota(jnp.int32, (tm, tm), 0)
    t_col = lax.broadcasted_iota(jnp.int32, (tm, tm), 1)
    same_tile = (t_row // tt) == (t_col // tt)
    in_tile = jnp.where(same_tile, 1.0, 0.0).astype(BF16)
    earlier = jnp.where(same_tile & (t_row < t_col), 1.0, 0.0).astype(BF16)
    lower = jnp.where(lax.broadcasted_iota(jnp.int32, (ne, ne), 1)
                      < lax.broadcasted_iota(jnp.int32, (ne, ne), 0), 1.0, 0.0).astype(BF16)
    picks = [eid == ix for ix in idxs]
    routed = jnp.where(picks[0] | picks[1] | picks[2] | picks[3], 1.0, 0.0).astype(BF16)
    before = jnp.dot(routed, earlier, preferred_element_type=F32)
    cnt_b = jnp.dot(routed, in_tile, preferred_element_type=F32)
    run_start = jnp.dot(lower, cnt_b.astype(BF16), preferred_element_type=F32)
    base = run_start + before
    pos_ref[...] = jnp.concatenate(
        [jnp.sum(jnp.where(pk, base, 0.0), axis=0, keepdims=True) for pk in picks],
        axis=0).astype(jnp.int32)
    tile_sel = jnp.where(lax.broadcasted_iota(jnp.int32, (8, tm), 0)
                         == lax.broadcasted_iota(jnp.int32, (8, tm), 1) // tt, 1.0, 0.0).astype(BF16)
    cnt8 = lax.dot_general(tile_sel, routed, (((1,), (1,)), ((), ())),
                           preferred_element_type=F32)
    cnt_ref[0] = jnp.concatenate([cnt8, jnp.zeros((8, 128 - ne), F32)], axis=1).astype(jnp.int32)


def _post_call(o2d, ga, g1, x2d, mod, w_mo, w_out, g_ffn, w_rt, b_r, *, seq):
    t, d = x2d.shape
    tm = TM_POST
    tps = seq // tm
    row = lambda i: (i, 0)
    n_exp = b_r.shape[0]
    assert tm % TM_FIN == 0 and tm // TM_FIN <= 8 and n_exp <= 128
    kern = functools.partial(_post_kernel, d=d)
    return pl.pallas_call(
        kern,
        grid=(t // tm,),
        in_specs=[
            pl.BlockSpec((tm, d), row),
            pl.BlockSpec((tm, d), row),
            pl.BlockSpec((tm, d), row),
            pl.BlockSpec((tm, d), row),
            pl.BlockSpec((1, 6, d), lambda i: (i // tps, 0, 0)),
            _const_spec(w_mo.shape),
            _const_spec(w_out.shape),
            _const_spec((1, d)),
            _const_spec(w_rt.shape),
            _const_spec(b_r.shape),
        ],
        out_specs=[
            pl.BlockSpec((tm, d), row),
            pl.BlockSpec((tm * ROW_SUB, 128), row),
            pl.BlockSpec((TOP_K, tm), lambda i: (0, i)),
            pl.BlockSpec((TOP_K, tm), lambda i: (0, i)),
            pl.BlockSpec((1, 8, 128), lambda i: (i, 0, 0)),
        ],
        out_shape=[
            jax.ShapeDtypeStruct((t, d), F32),
            jax.ShapeDtypeStruct((t * ROW_SUB, 128), jnp.uint32),
            jax.ShapeDtypeStruct((TOP_K, t), jnp.int32),
            jax.ShapeDtypeStruct((TOP_K, t), F32),
            jax.ShapeDtypeStruct((t // tm, 8, 128), jnp.int32),
        ],
        compiler_params=pltpu.CompilerParams(dimension_semantics=("arbitrary",),
                                             vmem_limit_bytes=VMEM_LIMIT),
        name="post_attn_router",
    )(o2d, ga, g1, x2d, mod, w_mo, w_out, g_ffn, w_rt, b_r)


RUN_BITS = 9
COMMON_RUN_BITS = 6
TAIL_BITS = 9


def _copy_pieces(count, src_ref, src_off, dst_ref, dst_off, sem, bits, wait=False):
    def pieces(bit_range, off):
        for b in reversed(bit_range):
            size = (1 << b) * ROW_SUB
            take = (count >> b) & 1

            @pl.when(take == 1)
            def _(off=off, size=size):
                cp = pltpu.make_async_copy(
                    src_ref.at[pl.ds(pl.multiple_of(src_off + off, ROW_SUB), size)],
                    dst_ref.at[pl.ds(pl.multiple_of(dst_off + off, ROW_SUB), size)], sem)
                if wait:
                    cp.wait()
                else:
                    cp.start()
            off = off + take * size

    lo = min(bits, COMMON_RUN_BITS)
    big = (count >> lo) << lo

    if lo < bits:
        @pl.when(big != 0)
        def _():
            pieces(range(lo, bits), 0)
    pieces(range(lo), big * ROW_SUB)


def _dispatch_kernel(tab_ref, tail_ref, nused_ref, h_ref, pos_ref, xs_hbm, stage, zbuf, sem, zsem, *,
                     tm, n_exp, n_blocks):
    i = pl.program_id(0)
    n = pl.num_programs(0)
    slot = i % 2
    n_rows = TOP_K * tm

    def wait_stage(slot_):
        for j in range(TILES_PER_STEP):
            pltpu.make_async_copy(stage.at[slot_, j], xs_hbm.at[pl.ds(0, n_rows * ROW_SUB)],
                                  sem.at[slot_]).wait()

    @pl.when(i == 0)
    def _():
        zbuf[...] = jnp.zeros(zbuf.shape, zbuf.dtype)
        for wait in (False, True):
            def body(e, c, wait=wait):
                _copy_pieces(tail_ref[1, e], zbuf, 0, xs_hbm, tail_ref[0, e], zsem, TAIL_BITS, wait)
                return c
            lax.fori_loop(0, n_exp, body, 0)
        blk_rows = zbuf.shape[0]

        def zero_block(b, c):
            cp = pltpu.make_async_copy(
                zbuf, xs_hbm.at[pl.ds(pl.multiple_of(b * blk_rows, blk_rows), blk_rows)], zsem)
            cp.start()
            cp.wait()
            return c
        lax.fori_loop(nused_ref[0], n_blocks, zero_block, 0)

    @pl.when(i >= 2)
    def _():
        wait_stage(slot)

    tiles = range(TILES_PER_STEP)
    hs = [_load_rows(h_ref.at[pl.ds(j * tm * ROW_SUB, tm * ROW_SUB)], tm).astype(BF16)
          for j in tiles]
    row_id = lax.broadcasted_iota(jnp.int32, (n_rows, tm), 0)
    onehots = []
    for j in tiles:
        hit = row_id == pos_ref[0:1, j * tm:(j + 1) * tm]
        for k in range(1, TOP_K):
            hit = hit | (row_id == pos_ref[k:k + 1, j * tm:(j + 1) * tm])
        onehots.append(jnp.where(hit, 1.0, 0.0).astype(BF16))
    rows = [jnp.dot(onehots[j], hs[j], preferred_element_type=F32) for j in tiles]
    for j in tiles:
        dh = rows[j].shape[1] // 2
        _store_rows(stage.at[slot, j], _pack_bf16_pair(rows[j][:, 0:dh], rows[j][:, dh:]))

    for j in tiles:
        def body(e, c, j=j):
            _copy_pieces(tab_ref[j, 0, e], stage.at[slot, j], tab_ref[j, 2, e], xs_hbm,
                         tab_ref[j, 1, e], sem.at[slot], RUN_BITS)
            return c
        lax.fori_loop(0, n_exp, body, 0)

    @pl.when(i == n - 1)
    def _():
        wait_stage(slot)
        wait_stage(1 - slot)


def _dispatch_call(tabs, tail, n_used, h2p, pos_t, *, n_xs_rows):
    n_tiles, _, n_exp = tabs.shape
    tm = TM_FIN
    tps = TILES_PER_STEP
    assert n_tiles % tps == 0
    nt = n_tiles // tps
    assert nt >= 2 and tm < (1 << RUN_BITS) and MOE_BLK <= (1 << TAIL_BITS)
    n_rows = TOP_K * tm
    kern = functools.partial(_dispatch_kernel, tm=tm, n_exp=n_exp, n_blocks=n_xs_rows // MOE_BLK)
    return pl.pallas_call(
        kern,
        grid=(nt,),
        in_specs=[
            pl.BlockSpec((tps, 3, n_exp), lambda i: (i, 0, 0), memory_space=pltpu.SMEM),
            pl.BlockSpec(memory_space=pltpu.SMEM),
            pl.BlockSpec(memory_space=pltpu.SMEM),
            pl.BlockSpec((tps * tm * ROW_SUB, 128), lambda i: (i, 0)),
            pl.BlockSpec((TOP_K, tps * tm), lambda i: (0, i)),
        ],
        out_specs=pl.BlockSpec(memory_space=pl.ANY),
        out_shape=jax.ShapeDtypeStruct((n_xs_rows * ROW_SUB, 128), jnp.uint32),
        scratch_shapes=[pltpu.VMEM((2, tps, n_rows * ROW_SUB, 128), jnp.uint32),
                        pltpu.VMEM((MOE_BLK * ROW_SUB, 128), jnp.uint32),
                        pltpu.SemaphoreType.DMA((2,)),
                        pltpu.SemaphoreType.DMA(())],
        compiler_params=pltpu.CompilerParams(dimension_semantics=("arbitrary",),
                                             vmem_limit_bytes=VMEM_LIMIT),
        name="moe_dispatch",
    )(tabs, tail, n_used, h2p, pos_t)


def _moe_kernel(blk_e_ref, nused_ref, x_ref, perm_ref, wgu_ref, bgu_ref, wd_ref, bd_ref, o_ref,
                wgu_scr, wd_scr, *, blk, f):
    s = pl.program_id(0)
    e_cur = blk_e_ref[s]
    e_prev = blk_e_ref[jnp.maximum(s - 1, 0)]

    @pl.when(jnp.logical_or(s == 0, e_cur != e_prev))
    def _():
        perm = perm_ref[...]
        for c in range(2 * f // 256):
            r = jnp.dot(wgu_ref[0, :, c * 256:(c + 1) * 256].astype(BF16), perm,
                        preferred_element_type=F32)
            wgu_scr[:, c * 128:(c + 1) * 128] = r[:, 0:128].astype(BF16)
            wgu_scr[:, f + c * 128:f + (c + 1) * 128] = r[:, 128:256].astype(BF16)
        wd_scr[...] = wd_ref[0].astype(BF16)

    @pl.when(s >= nused_ref[0])
    def _():
        o_ref[...] = jnp.zeros(o_ref.shape, o_ref.dtype)

    @pl.when(s < nused_ref[0])
    def _():
        x = _load_rows(x_ref, blk).astype(BF16)
        gu = jnp.dot(x, wgu_scr[...], preferred_element_type=F32) + bgu_ref[0]
        gate = jnp.minimum(gu[:, 0:f], SWIGLU_LIMIT)
        up = jnp.clip(gu[:, f:2 * f], -SWIGLU_LIMIT, SWIGLU_LIMIT)
        act = (up + 1.0) * (gate * jax.nn.sigmoid(SWIGLU_ALPHA * gate))
        y = jnp.dot(act.astype(BF16), wd_scr[...], preferred_element_type=F32) + bd_ref[0]
        dh = y.shape[1] // 2
        _store_rows(o_ref, _pack_bf16_pair(y[:, 0:dh], y[:, dh:]))


def _moe_call(blk_e, n_used, xs, perm, w_gu, b_gu_p, w_down, b_down):
    n_steps = blk_e.shape[0]
    blk = MOE_BLK
    _, d, f2 = w_gu.shape
    f = f2 // 2
    rows = blk * ROW_SUB
    kern = functools.partial(_moe_kernel, blk=blk, f=f)
    row_blk = lambda i, be, nu: (i, 0)
    grid_spec = pltpu.PrefetchScalarGridSpec(
        num_scalar_prefetch=2,
        grid=(n_steps,),
        in_specs=[
            pl.BlockSpec((rows, 128), row_blk),
            pl.BlockSpec((256, 256), lambda i, be, nu: (0, 0)),
            pl.BlockSpec((1, d, f2), lambda i, be, nu: (be[i], 0, 0)),
            pl.BlockSpec((1, 1, f2), lambda i, be, nu: (be[i], 0, 0)),
            pl.BlockSpec((1, f, d), lambda i, be, nu: (be[i], 0, 0)),
            pl.BlockSpec((1, 1, d), lambda i, be, nu: (be[i], 0, 0)),
        ],
        out_specs=pl.BlockSpec((rows, 128), row_blk),
        scratch_shapes=[
            pltpu.VMEM((d, f2), BF16),
            pltpu.VMEM((f, d), BF16),
        ],
    )
    return pl.pallas_call(
        kern,
        grid_spec=grid_spec,
        out_shape=jax.ShapeDtypeStruct((n_steps * rows, 128), jnp.uint32),
        compiler_params=pltpu.CompilerParams(dimension_semantics=("arbitrary",),
                                             vmem_limit_bytes=VMEM_LIMIT),
        name="moe_experts",
    )(blk_e, n_used, xs, perm, w_gu, b_gu_p, w_down, b_down)


def _final_kernel(tab_cur_ref, tab_nxt_ref, x1_ref, pos_ref, w_ref, mod_ref, fmod_ref, gfin_ref, ys_hbm,
                  o_ref, stage, sem, *, tm, n_exp):
    i = pl.program_id(0)
    n = pl.num_programs(0)
    slot = i % 2
    n_rows = TOP_K * tm

    tiles = range(TILES_PER_STEP)

    def issue_tiles(tab_ref, slot_):
        for j in tiles:
            def body(e, c, j=j):
                _copy_pieces(tab_ref[j, 0, e], ys_hbm, tab_ref[j, 1, e], stage.at[slot_, j],
                             tab_ref[j, 2, e], sem.at[slot_], RUN_BITS)
                return c
            lax.fori_loop(0, n_exp, body, 0)

    @pl.when(i == 0)
    def _():
        issue_tiles(tab_cur_ref, 0)

    @pl.when(i + 1 < n)
    def _():
        issue_tiles(tab_nxt_ref, 1 - slot)

    for j in tiles:
        pltpu.make_async_copy(ys_hbm.at[pl.ds(0, n_rows * ROW_SUB)], stage.at[slot, j],
                              sem.at[slot]).wait()

    ys = [_load_rows(stage.at[slot, j], n_rows).astype(BF16) for j in tiles]
    col_id = lax.broadcasted_iota(jnp.int32, (tm, n_rows), 1)
    wmats = []
    for j in tiles:
        pos = pos_ref[j * tm:(j + 1) * tm, :]
        w = w_ref[j * tm:(j + 1) * tm, :]
        a = jnp.where(col_id == pos[:, 0:1], w[:, 0:1], 0.0)
        for k in range(1, TOP_K):
            a = a + jnp.where(col_id == pos[:, k:k + 1], w[:, k:k + 1], 0.0)
        wmats.append(a.astype(BF16))
    moes = [jnp.dot(wmats[j], ys[j], preferred_element_type=F32) for j in tiles]
    gate2 = mod_ref[0, 5:6, :]
    fshift = fmod_ref[0, 0:1, :]
    fscale = fmod_ref[0, 1:2, :]
    for j in tiles:
        x2 = x1_ref[j * tm:(j + 1) * tm, :] + gate2 * moes[j]
        o_ref[j * tm:(j + 1) * tm, :] = _rms(x2, gfin_ref[...]) * (1.0 + fscale) + fshift


def _final_call(tabs, x1, pos_col, w_col, mod, fmod, g_final, ys, *, seq):
    t, d = x1.shape
    tm = TM_FIN
    assert tm < (1 << RUN_BITS)
    tls = TILES_PER_STEP
    tms = tls * tm
    assert seq % tms == 0
    tps = seq // tms
    nt = t // tms
    n_exp = tabs.shape[2]
    n_rows = TOP_K * tm
    kern = functools.partial(_final_kernel, tm=tm, n_exp=n_exp)
    tab_spec = lambda ahead: pl.BlockSpec((tls, 3, n_exp), lambda i: (jnp.minimum(i + ahead, nt - 1), 0, 0),
                                          memory_space=pltpu.SMEM)
    return pl.pallas_call(
        kern,
        grid=(nt,),
        in_specs=[
            tab_spec(0),
            tab_spec(1),
            pl.BlockSpec((tms, d), lambda i: (i, 0)),
            pl.BlockSpec((tms, TOP_K), lambda i: (i, 0)),
            pl.BlockSpec((tms, TOP_K), lambda i: (i, 0)),
            pl.BlockSpec((1, 6, d), lambda i: (i // tps, 0, 0)),
            pl.BlockSpec((1, 2, d), lambda i: (i // tps, 0, 0)),
            _const_spec((1, d)),
            pl.BlockSpec(memory_space=pl.ANY),
        ],
        out_specs=pl.BlockSpec((tms, d), lambda i: (i, 0)),
        out_shape=jax.ShapeDtypeStruct((t, d), F32),
        scratch_shapes=[pltpu.VMEM((2, tls, n_rows * ROW_SUB, 128), jnp.uint32),
                        pltpu.SemaphoreType.DMA((2,))],
        compiler_params=pltpu.CompilerParams(dimension_semantics=("arbitrary",),
                                             vmem_limit_bytes=VMEM_LIMIT),
        name="combine_final",
    )(tabs, tabs, x1, pos_col, w_col, mod, fmod, g_final, ys)


def _prep_w_in(w_in, d, q_lora, kv_lora):
    o_kpe = d + q_lora + kv_lora
    o_g = o_kpe + QK_ROPE
    half = QK_ROPE // 2
    kpe = w_in[:, o_kpe:o_kpe + QK_ROPE]
    zpad = jnp.zeros((d, 128 - QK_ROPE), w_in.dtype)
    ksw = jnp.concatenate([-kpe[:, half:], kpe[:, :half]], axis=1)
    return jnp.concatenate([w_in[:, :o_kpe], kpe, zpad, ksw, zpad, w_in[:, o_g:]], axis=1).astype(BF16)


def _prep_w_q(w_q_b):
    ql = w_q_b.shape[0]
    hd = QK_NOPE + QK_ROPE
    half = QK_ROPE // 2
    w = w_q_b.reshape(ql, N_HEADS, hd)
    nope = w[:, :, :QK_NOPE]
    pe = w[:, :, QK_NOPE:]
    sw = jnp.concatenate([-pe[:, :, half:], pe[:, :, :half]], axis=2)
    return jnp.concatenate([nope.reshape(ql, -1), pe.reshape(ql, -1), sw.reshape(ql, -1)],
                           axis=1).astype(BF16)


def _prep_w_kv(w_kv_b):
    kvl = w_kv_b.shape[0]
    w = w_kv_b.reshape(kvl, N_HEADS, QK_NOPE + V_HEAD)
    w_kn = w[:, :, :QK_NOPE].reshape(kvl, -1).astype(BF16)
    w_vt = w[:, :, QK_NOPE:].reshape(kvl, -1).T.astype(BF16)
    return w_kn, w_vt


def kernel(x, c, positions, w_mod, b_mod, g_mix, w_in, b_gate, w_pool_grp, pool_scale, w_pool_out,
           g_q_a, w_q_b, g_kv_a, w_kv_b, w_mla_out, w_out, g_ffn, w_router, b_router, w_gu, b_gu,
           w_down, b_down, g_final, w_fmod, b_fmod):
    bsz, seq, d = x.shape
    t = bsz * seq
    depth = w_mod.shape[0]
    assert depth == 1
    assert seq % TQ == 0 and seq % TM_IN == 0 and seq % TM_POST == 0 and seq % TM_FIN == 0
    q_lora = g_q_a.shape[-1]
    kv_lora = g_kv_a.shape[-1]
    n_exp = w_gu.shape[1]
    f = w_gu.shape[-1] // 2
    blk = MOE_BLK

    x2d = x.reshape(t, d)
    pos_col = positions.astype(F32).reshape(t, 1)
    inv_freq = 1.0 / (ROPE_THETA ** (jnp.arange(0, QK_ROPE, 2, dtype=F32) / QK_ROPE))
    invf2 = jnp.tile(inv_freq, 128 // (QK_ROPE // 2)).reshape(1, 128)

    mod = _mod_call(c, w_mod[0], b_mod[0]).reshape(bsz, 6, d)
    fmod = _mod_call(c, w_fmod, b_fmod).reshape(bsz, 2, d)

    w_in_p = _prep_w_in(w_in[0], d, q_lora, kv_lora)
    w_q_p = _prep_w_q(w_q_b[0])
    w_kn, w_vt = _prep_w_kv(w_kv_b[0])
    ga, g1, q2, k2, vt3 = _mixer_in_call(
        x2d, pos_col, mod, g_mix[0].reshape(1, d), w_in_p, b_gate[0].reshape(1, 2 * d),
        w_pool_grp[0].astype(BF16), pool_scale[0].reshape(1, d), w_pool_out[0].astype(BF16),
        g_q_a[0].reshape(1, q_lora), w_q_p, g_kv_a[0].reshape(1, kv_lora), w_kn, w_vt, invf2,
        bsz=bsz, seq=seq)

    hp = N_HEADS * HEAD_PAD
    o = _attn_call(q2.reshape(bsz, seq, hp), k2.reshape(bsz, seq, hp), vt3)

    w_r = w_router[0]
    w_r_hi = w_r.astype(BF16)
    w_r_lo = (w_r - w_r_hi.astype(F32)).astype(BF16)
    lane_pad = lambda a: jnp.pad(a, ((0, 0), (0, 128 - n_exp)))
    w_rt = jnp.concatenate(
        [jnp.concatenate([lane_pad(w_r_hi), lane_pad(w_r_lo)], axis=1),
         jnp.concatenate([lane_pad(w_r_hi), jnp.zeros((d, 128), BF16)], axis=1)], axis=0)
    x1, h2p, pos_t, wgt_t, cnt3 = _post_call(
        o.reshape(t, d), ga, g1, x2d, mod, w_mla_out[0].astype(BF16), w_out[0].astype(BF16),
        g_ffn[0].reshape(1, d), w_rt, b_router[0].reshape(n_exp, 1), seq=seq)

    n_slots = t * TOP_K
    n_rows = n_slots + n_exp * blk
    n_blocks = n_rows // blk
    cnt = cnt3[:, 0:TM_POST // TM_FIN, 0:n_exp].reshape(-1, n_exp)
    counts = jnp.sum(cnt, axis=0)
    padded = (counts + blk - 1) // blk * blk
    pad_end = jnp.cumsum(padded)
    pad_start = pad_end - padded
    blk_start = jnp.arange(n_blocks, dtype=jnp.int32) * blk
    blk_e = jnp.minimum(jnp.sum(pad_end[None, :] <= blk_start[:, None], axis=1, dtype=jnp.int32),
                        n_exp - 1)
    n_used = (pad_end[-1:] // blk).astype(jnp.int32)
    tail = jnp.stack([(pad_start + counts) * ROW_SUB, padded - counts], axis=0).astype(jnp.int32)

    before = jnp.cumsum(cnt, axis=0) - cnt
    run_start = jnp.cumsum(cnt, axis=1) - cnt
    tabs = jnp.stack([cnt, (pad_start[None, :] + before) * ROW_SUB, run_start * ROW_SUB], axis=1)

    col = jnp.arange(256, dtype=jnp.int32)[None, :]
    row = jnp.arange(256, dtype=jnp.int32)[:, None]
    perm = (row == jnp.where(col < 128, 2 * col, 2 * (col - 128) + 1)).astype(BF16)
    b_gu_p = jnp.concatenate([b_gu[0][:, 0::2], b_gu[0][:, 1::2]], axis=-1).reshape(n_exp, 1, 2 * f)
    xs = _dispatch_call(tabs, tail, n_used, h2p, pos_t, n_xs_rows=n_rows)
    ys = _moe_call(blk_e, n_used, xs, perm, w_gu[0], b_gu_p, w_down[0], b_down[0].reshape(n_exp, 1, d))

    out = _final_call(tabs, x1, pos_t.T, wgt_t.T, mod, fmod, g_final.reshape(1, d), ys, seq=seq)
    return out.reshape(bsz, seq, d)
```

```python
import functools

import jax
import jax.numpy as jnp
from jax import lax
from jax.experimental import pallas as pl
from jax.experimental.pallas import tpu as pltpu

F32 = jnp.float32
BF16 = jnp.bfloat16

CHUNK = 64
POOL_WINDOWS = (2, 4, 8, 16)
POOL_HALO = 16
N_HEADS = 8
QK_NOPE = 128
QK_ROPE = 64
V_HEAD = 128
HEAD_PAD = 256
ROPE_THETA = 10000.0
TOP_K = 4
SWIGLU_LIMIT = 7.0
SWIGLU_ALPHA = 1.702
NORM_EPS = 1e-6
NEG_INF = -1e30
LOG2_E = 1.4426950408889634

VMEM_LIMIT = 56 * 1024 * 1024

TM_IN = 256
TQ = 512
TM_POST = 512
MOE_BLK = 512
TM_FIN = 256
TILES_PER_STEP = 2
ROW_SUB = 4


def _const_spec(shape):
    nd = len(shape)
    return pl.BlockSpec(shape, lambda *_: (0,) * nd, pipeline_mode=pl.Buffered(1))


def _rms(xf, g):
    return xf * lax.rsqrt(jnp.mean(xf * xf, axis=-1, keepdims=True) + NORM_EPS) * g


def _pack_bf16_pair(lo, hi):
    lo_b = lax.bitcast_convert_type(lo.astype(BF16).astype(F32), jnp.uint32)
    hi_b = lax.bitcast_convert_type(hi.astype(BF16).astype(F32), jnp.uint32)
    return (hi_b & jnp.uint32(0xFFFF0000)) | (lo_b >> 16)


def _unpack_bf16_pair(p):
    lo = lax.bitcast_convert_type(p << 16, F32)
    hi = lax.bitcast_convert_type(p & jnp.uint32(0xFFFF0000), F32)
    return lo, hi


def _store_rows(ref, packed):
    n = packed.shape[0]
    for q in range(ROW_SUB):
        ref[pl.ds(q, n, stride=ROW_SUB), :] = packed[:, q * 128:(q + 1) * 128]


def _load_rows(ref, n):
    los, his = [], []
    for q in range(ROW_SUB):
        lo, hi = _unpack_bf16_pair(ref[pl.ds(q, n, stride=ROW_SUB), :])
        los.append(lo)
        his.append(hi)
    return jnp.concatenate(los + his, axis=1)


def _mod_kernel(c_ref, w1_ref, b1_ref, w2_ref, b2_ref, o1_ref, o2_ref, *, n1_blocks):
    j = pl.program_id(0)
    c = c_ref[...]
    c_act = c * jax.nn.sigmoid(c)

    @pl.when(j < n1_blocks)
    def _():
        o1_ref[...] = jnp.dot(c_act, w1_ref[...], preferred_element_type=F32,
                              precision=lax.Precision.HIGHEST) + b1_ref[...]

    @pl.when(j >= n1_blocks)
    def _():
        o2_ref[...] = jnp.dot(c_act, w2_ref[...], preferred_element_type=F32,
                              precision=lax.Precision.HIGHEST) + b2_ref[...]


def _mod_call(c, w1, b1, w2, b2, tn=1024):
    bsz, d = c.shape
    n1, n2 = w1.shape[1], w2.shape[1]
    nb1, nb2 = n1 // tn, n2 // tn
    first = lambda j: (0, jnp.minimum(j, nb1 - 1))
    second = lambda j: (0, jnp.maximum(j - nb1, 0))
    return pl.pallas_call(
        functools.partial(_mod_kernel, n1_blocks=nb1),
        grid=(nb1 + nb2,),
        in_specs=[pl.BlockSpec((bsz, d), lambda j: (0, 0)),
                  pl.BlockSpec((d, tn), first),
                  pl.BlockSpec((1, tn), first),
                  pl.BlockSpec((d, tn), second),
                  pl.BlockSpec((1, tn), second)],
        out_specs=[pl.BlockSpec((bsz, tn), first),
                   pl.BlockSpec((bsz, tn), second)],
        out_shape=[jax.ShapeDtypeStruct((bsz, n1), F32),
                   jax.ShapeDtypeStruct((bsz, n2), F32)],
        compiler_params=pltpu.CompilerParams(dimension_semantics=("arbitrary",),
                                             vmem_limit_bytes=VMEM_LIMIT),
        name="adaln_mod",
    )(c, w1, b1.reshape(1, n1), w2, b2.reshape(1, n2))


def _mixer_in_kernel(x_ref, xh_ref, pos_ref, mod_ref, gmix_ref, win_ref, bgate_ref, wgrp_ref,
                     pscale_ref, wpo_ref, gq_ref, wq_ref, gkv_ref, wkn_ref, wvt_ref, invf_ref,
                     ga_ref, g1_ref, q_ref, k_ref, vt_ref, u_scr, *, tm, tiles_per_seq, d, q_lora,
                     kv_lora):
    i = pl.program_id(0)
    t_in_seq = i % tiles_per_seq
    is_start = t_in_seq == 0
    shift1 = mod_ref[0, 0:1, :]
    scale1 = mod_ref[0, 1:2, :]
    gmix = gmix_ref[...]

    def prenorm(xf):
        return (_rms(xf, gmix) * (1.0 + scale1) + shift1).astype(BF16)

    h = prenorm(x_ref[...])
    hh = prenorm(xh_ref[...])
    u_ext = jnp.dot(jnp.concatenate([hh, h], axis=0), win_ref[:, 0:d],
                    preferred_element_type=F32)
    u_scr[0:POOL_HALO, :] = jnp.where(is_start, 0.0, u_ext[0:POOL_HALO, :])
    u_scr[POOL_HALO:POOL_HALO + tm, :] = u_ext[POOL_HALO:, :]
    rest = jnp.dot(h, win_ref[:, d:], preferred_element_type=F32)

    gw = d // len(POOL_WINDOWS)
    tseq = t_in_seq * tm + lax.broadcasted_iota(jnp.int32, (tm, 1), 0)
    ys = []
    for g, w in enumerate(POOL_WINDOWS):
        c0 = g * gw
        ug = u_scr[POOL_HALO:POOL_HALO + tm, c0:c0 + gw]
        acc = ug
        for j in range(1, w):
            acc = acc + u_scr[POOL_HALO - j:POOL_HALO - j + tm, c0:c0 + gw]
        cnt = jnp.minimum(tseq + 1, w).astype(F32)
        mixed = (acc / cnt - ug).astype(BF16)
        ys.append(jnp.dot(mixed, wgrp_ref[g], preferred_element_type=F32))
    y = (jnp.concatenate(ys, axis=1) * pscale_ref[...]).astype(BF16)
    a = jnp.dot(y, wpo_ref[...], preferred_element_type=F32)

    o_q = 0
    o_kv = q_lora
    o_kpe = q_lora + kv_lora
    o_ksw = o_kpe + 128
    o_g0 = o_ksw + 128
    o_g1 = o_g0 + d
    gates0 = jax.nn.sigmoid(rest[:, o_g0:o_g0 + d] + bgate_ref[:, 0:d])
    gates1 = jax.nn.sigmoid(rest[:, o_g1:o_g1 + d] + bgate_ref[:, d:2 * d])
    ga_ref[...] = (gates0 * a).astype(BF16)
    g1_ref[...] = gates1.astype(BF16)

    ang = pos_ref[...] * invf_ref[...]
    cos2 = jnp.cos(ang)
    sin2 = jnp.sin(ang)
    q_scale = float(QK_NOPE + QK_ROPE) ** -0.5 * LOG2_E

    qn = _rms(rest[:, o_q:o_q + q_lora], gq_ref[...]).astype(BF16)
    qall = jnp.dot(qn, wq_ref[...], preferred_element_type=F32)
    kvn = _rms(rest[:, o_kv:o_kv + kv_lora], gkv_ref[...]).astype(BF16)
    kn = jnp.dot(kvn, wkn_ref[...], preferred_element_type=F32)
    vt = lax.dot_general(wvt_ref[...], kvn, (((1,), (1,)), ((), ())),
                         preferred_element_type=F32)
    vt_ref[0] = vt.astype(BF16)
    kpe = (rest[:, o_kpe:o_kpe + 128] * cos2 + rest[:, o_ksw:o_ksw + 128] * sin2).astype(BF16)
    o_pe = N_HEADS * QK_NOPE
    o_sw = o_pe + N_HEADS * QK_ROPE
    low_half = lax.broadcasted_iota(jnp.int32, (tm, 128), 1) < QK_ROPE
    for pair in range(N_HEADS // 2):
        c0 = pair * 128
        qpe2 = (qall[:, o_pe + c0:o_pe + c0 + 128] * cos2
                + qall[:, o_sw + c0:o_sw + c0 + 128] * sin2) * q_scale
        for hd, part in ((2 * pair, qpe2), (2 * pair + 1, pltpu.roll(qpe2, QK_ROPE, axis=1))):
            n0 = hd * QK_NOPE
            q_ref[:, hd * HEAD_PAD:hd * HEAD_PAD + 128] = (qall[:, n0:n0 + 128] * q_scale).astype(BF16)
            q_ref[:, hd * HEAD_PAD + 128:(hd + 1) * HEAD_PAD] = jnp.where(low_half, part, 0.0).astype(BF16)
            k_ref[:, hd * HEAD_PAD:hd * HEAD_PAD + 128] = kn[:, n0:n0 + 128].astype(BF16)
            k_ref[:, hd * HEAD_PAD + 128:(hd + 1) * HEAD_PAD] = kpe


def _mixer_in_call(x2d, pos_col, mod, g_mix, w_in_p, b_gate, w_grp, pool_scale, w_po, g_q_a, w_q_p,
                   g_kv_a, w_kn, w_vt, invf2, *, bsz, seq):
    t, d = x2d.shape
    tm = TM_IN
    tps = seq // tm
    q_lora = g_q_a.shape[-1]
    kv_lora = g_kv_a.shape[-1]
    hp = N_HEADS * HEAD_PAD
    halo_blocks = tm // POOL_HALO
    kern = functools.partial(_mixer_in_kernel, tm=tm, tiles_per_seq=tps, d=d, q_lora=q_lora,
                             kv_lora=kv_lora)
    row = lambda i: (i, 0)
    return pl.pallas_call(
        kern,
        grid=(t // tm,),
        in_specs=[
            pl.BlockSpec((tm, d), row),
            pl.BlockSpec((POOL_HALO, d), lambda i: (jnp.maximum(i * halo_blocks - 1, 0), 0)),
            pl.BlockSpec((tm, 1), row),
            pl.BlockSpec((1, 6, d), lambda i: (i // tps, 0, 0)),
            _const_spec((1, d)),
            _const_spec(w_in_p.shape),
            _const_spec((1, 2 * d)),
            _const_spec(w_grp.shape),
            _const_spec((1, d)),
            _const_spec(w_po.shape),
            _const_spec((1, q_lora)),
            _const_spec(w_q_p.shape),
            _const_spec((1, kv_lora)),
            _const_spec(w_kn.shape),
            _const_spec(w_vt.shape),
            _const_spec((1, 128)),
        ],
        out_specs=[
            pl.BlockSpec((tm, d), row),
            pl.BlockSpec((tm, d), row),
            pl.BlockSpec((tm, hp), row),
            pl.BlockSpec((tm, hp), row),
            pl.BlockSpec((1, N_HEADS * V_HEAD, tm), lambda i: (i // tps, 0, i % tps)),
        ],
        out_shape=[
            jax.ShapeDtypeStruct((t, d), BF16),
            jax.ShapeDtypeStruct((t, d), BF16),
            jax.ShapeDtypeStruct((t, hp), BF16),
            jax.ShapeDtypeStruct((t, hp), BF16),
            jax.ShapeDtypeStruct((bsz, N_HEADS * V_HEAD, seq), BF16),
        ],
        scratch_shapes=[pltpu.VMEM((tm + POOL_HALO, d), F32)],
        compiler_params=pltpu.CompilerParams(dimension_semantics=("arbitrary",),
                                             vmem_limit_bytes=VMEM_LIMIT),
        name="mixer_in",
    )(x2d, x2d, pos_col, mod, g_mix, w_in_p, b_gate, w_grp, pool_scale, w_po, g_q_a, w_q_p, g_kv_a,
      w_kn, w_vt, invf2)


def _attn_kernel(q_ref, k_ref, vt_ref, o_ref, *, tq, nq):
    seq = vt_ref.shape[2]
    vt_ext = jnp.concatenate([vt_ref[0], jnp.ones((16, seq), BF16)], axis=0)

    def scores(qi):
        q0 = qi * tq
        ln = q0 + tq
        return lax.dot_general(k_ref[0, 0:ln, :], q_ref[0, q0:q0 + tq, :], (((1,), (1,)), ((), ())),
                               preferred_element_type=F32)

    def finish(qi, s):
        q0 = qi * tq
        ln = q0 + tq
        cw = tq // 2
        for c in range(2):
            cs = slice(c * cw, (c + 1) * cw)
            kc = (q0 + lax.broadcasted_iota(jnp.int32, (tq, cw), 0)) // CHUNK
            qc = (q0 + c * cw + lax.broadcasted_iota(jnp.int32, (tq, cw), 1)) // CHUNK
            s_diag = jnp.where(qc >= kc, s[q0:ln, cs], NEG_INF)
            m = jnp.max(s_diag, axis=0, keepdims=True)
            if qi > 0:
                s_main = s[0:q0, cs]
                m = jnp.maximum(m, jnp.max(s_main, axis=0, keepdims=True))
            p_diag = jnp.exp2(s_diag - m).astype(BF16)
            acc = jnp.dot(vt_ext[:, q0:ln], p_diag, preferred_element_type=F32)
            if qi > 0:
                p_main = jnp.exp2(s_main - m).astype(BF16)
                acc = acc + jnp.dot(vt_ext[:, 0:q0], p_main, preferred_element_type=F32)
            l = acc[V_HEAD:V_HEAD + 1, :]
            o_ref[0, q0 + c * cw:q0 + (c + 1) * cw, :] = (acc[0:V_HEAD, :] / l).T.astype(BF16)

    order = list(range(nq))[::-1]
    s_cur = scores(order[0])
    for idx, qi in enumerate(order):
        s_next = scores(order[idx + 1]) if idx + 1 < nq else None
        finish(qi, s_cur)
        s_cur = s_next


def _attn_call(q3, k3, vt3):
    bsz, seq, _ = q3.shape
    kern = functools.partial(_attn_kernel, tq=TQ, nq=seq // TQ)
    return pl.pallas_call(
        kern,
        grid=(bsz, N_HEADS),
        in_specs=[pl.BlockSpec((1, seq, HEAD_PAD), lambda b, h: (b, 0, h)),
                  pl.BlockSpec((1, seq, HEAD_PAD), lambda b, h: (b, 0, h)),
                  pl.BlockSpec((1, V_HEAD, seq), lambda b, h: (b, h, 0))],
        out_specs=pl.BlockSpec((1, seq, V_HEAD), lambda b, h: (b, 0, h)),
        out_shape=jax.ShapeDtypeStruct((bsz, seq, N_HEADS * V_HEAD), BF16),
        compiler_params=pltpu.CompilerParams(dimension_semantics=("arbitrary", "arbitrary"),
                                             vmem_limit_bytes=VMEM_LIMIT),
        name="mla_attn",
    )(q3, k3, vt3)


def _post_kernel(o_ref, ga_ref, g1_ref, x_ref, mod_ref, wmo_ref, wout_ref, gffn_ref, wrt_ref, br_ref,
                 x1_ref, h2_ref, pos_ref, wgt_ref, cnt_ref, *, d):
    m = jnp.dot(o_ref[...], wmo_ref[...], preferred_element_type=F32)
    merged = ga_ref[...].astype(F32) + g1_ref[...].astype(F32) * m
    gate1 = mod_ref[0, 2:3, :]
    x1 = x_ref[...] + gate1 * jnp.dot(merged.astype(BF16), wout_ref[...],
                                      preferred_element_type=F32)
    x1_ref[...] = x1
    shift2 = mod_ref[0, 3:4, :]
    scale2 = mod_ref[0, 4:5, :]
    h2 = _rms(x1, gffn_ref[...]) * (1.0 + scale2) + shift2
    _store_rows(h2_ref, _pack_bf16_pair(h2[:, 0:d // 2], h2[:, d // 2:d]))

    ne = br_ref.shape[0]
    h_hi = h2.astype(BF16)
    h_lo = (h2 - h_hi.astype(F32)).astype(BF16)
    hcat = jnp.concatenate([h_hi, h_lo], axis=1)
    half = hcat.shape[0] // 2
    lg2 = jnp.concatenate(
        [jnp.dot(hcat[0:half], wrt_ref[...], preferred_element_type=F32),
         jnp.dot(hcat[half:], wrt_ref[...], preferred_element_type=F32)], axis=0)
    lg = lg2[:, 0:128] + lg2[:, 128:256]
    logits = lg.T[0:ne, :] + br_ref[...]
    tm = logits.shape[1]
    eid = lax.broadcasted_iota(jnp.int32, (ne, tm), 0)
    vals, idxs = [], []
    cur = logits
    for _ in range(TOP_K):
        mx = jnp.max(cur, axis=0, keepdims=True)
        ix = jnp.min(jnp.where(cur == mx, eid, ne), axis=0, keepdims=True)
        vals.append(mx)
        idxs.append(ix)
        cur = jnp.where(eid == ix, -jnp.inf, cur)
    es = [jnp.exp(v - vals[0]) for v in vals]
    den = es[0] + es[1] + es[2] + es[3]
    wgt_ref[...] = jnp.concatenate([e / den for e in es], axis=0)

    tt = TM_FIN
    t_row = lax.broadcasted_iota(jnp.int32, (tm, tm), 0)
    t_col = lax.broadcasted_iota(jnp.int32, (tm, tm), 1)
    same_tile = (t_row // tt) == (t_col // tt)
    in_tile = jnp.where(same_tile, 1.0, 0.0).astype(BF16)
    earlier = jnp.where(same_tile & (t_row < t_col), 1.0, 0.0).astype(BF16)
    lower = jnp.where(lax.broadcasted_iota(jnp.int32, (ne, ne), 1)
                      < lax.broadcasted_iota(jnp.int32, (ne, ne), 0), 1.0, 0.0).astype(BF16)
    picks = [eid == ix for ix in idxs]
    routed = jnp.where(picks[0] | picks[1] | picks[2] | picks[3], 1.0, 0.0).astype(BF16)
    before = jnp.dot(routed, earlier, preferred_element_type=F32)
    cnt_b = jnp.dot(routed, in_tile, preferred_element_type=F32)
    run_start = jnp.dot(lower, cnt_b.astype(BF16), preferred_element_type=F32)
    base = run_start + before
    pos_ref[...] = jnp.concatenate(
        [jnp.sum(jnp.where(pk, base, 0.0), axis=0, keepdims=True) for pk in picks],
        axis=0).astype(jnp.int32)
    tile_sel = jnp.where(lax.broadcasted_iota(jnp.int32, (8, tm), 0)
                         == lax.broadcasted_iota(jnp.int32, (8, tm), 1) // tt, 1.0, 0.0).astype(BF16)
    cnt8 = lax.dot_general(tile_sel, routed, (((1,), (1,)), ((), ())),
                           preferred_element_type=F32)
    cnt_ref[0] = jnp.concatenate([cnt8, jnp.zeros((8, 128 - ne), F32)], axis=1).astype(jnp.int32)


def _post_call(o2d, ga, g1, x2d, mod, w_mo, w_out, g_ffn, w_rt, b_r, *, seq):
    t, d = x2d.shape
    tm = TM_POST
    tps = seq // tm
    row = lambda i: (i, 0)
    n_exp = b_r.shape[0]
    assert tm % TM_FIN == 0 and tm // TM_FIN <= 8 and n_exp <= 128
    kern = functools.partial(_post_kernel, d=d)
    return pl.pallas_call(
        kern,
        grid=(t // tm,),
        in_specs=[
            pl.BlockSpec((tm, d), row),
            pl.BlockSpec((tm, d), row),
            pl.BlockSpec((tm, d), row),
            pl.BlockSpec((tm, d), row),
            pl.BlockSpec((1, 6, d), lambda i: (i // tps, 0, 0)),
            _const_spec(w_mo.shape),
            _const_spec(w_out.shape),
            _const_spec((1, d)),
            _const_spec(w_rt.shape),
            _const_spec(b_r.shape),
        ],
        out_specs=[
            pl.BlockSpec((tm, d), row),
            pl.BlockSpec((tm * ROW_SUB, 128), row),
            pl.BlockSpec((TOP_K, tm), lambda i: (0, i)),
            pl.BlockSpec((TOP_K, tm), lambda i: (0, i)),
            pl.BlockSpec((1, 8, 128), lambda i: (i, 0, 0)),
        ],
        out_shape=[
            jax.ShapeDtypeStruct((t, d), F32),
            jax.ShapeDtypeStruct((t * ROW_SUB, 128), jnp.uint32),
            jax.ShapeDtypeStruct((TOP_K, t), jnp.int32),
            jax.ShapeDtypeStruct((TOP_K, t), F32),
            jax.ShapeDtypeStruct((t // tm, 8, 128), jnp.int32),
        ],
        compiler_params=pltpu.CompilerParams(dimension_semantics=("arbitrary",),
                                             vmem_limit_bytes=VMEM_LIMIT),
        name="post_attn_router",
    )(o2d, ga, g1, x2d, mod, w_mo, w_out, g_ffn, w_rt, b_r)


RUN_BITS = 9
COMMON_RUN_BITS = 6
TAIL_BITS = 9


def _copy_pieces(count, src_ref, src_off, dst_ref, dst_off, sem, bits, wait=False):
    def pieces(bit_range, off):
        for b in reversed(bit_range):
            size = (1 << b) * ROW_SUB
            take = (count >> b) & 1

            @pl.when(take == 1)
            def _(off=off, size=size):
                cp = pltpu.make_async_copy(
                    src_ref.at[pl.ds(pl.multiple_of(src_off + off, ROW_SUB), size)],
                    dst_ref.at[pl.ds(pl.multiple_of(dst_off + off, ROW_SUB), size)], sem)
                if wait:
                    cp.wait()
                else:
                    cp.start()
            off = off + take * size

    lo = min(bits, COMMON_RUN_BITS)
    big = (count >> lo) << lo

    if lo < bits:
        @pl.when(big != 0)
        def _():
            pieces(range(lo, bits), 0)
    pieces(range(lo), big * ROW_SUB)


def _dispatch_kernel(tab_ref, tail_ref, nused_ref, h_ref, pos_ref, xs_hbm, stage, zbuf, sem, zsem, *,
                     tm, n_exp, n_blocks):
    i = pl.program_id(0)
    n = pl.num_programs(0)
    slot = i % 2
    n_rows = TOP_K * tm

    def wait_stage(slot_):
        for j in range(TILES_PER_STEP):
            pltpu.make_async_copy(stage.at[slot_, j], xs_hbm.at[pl.ds(0, n_rows * ROW_SUB)],
                                  sem.at[slot_]).wait()

    @pl.when(i == 0)
    def _():
        zbuf[...] = jnp.zeros(zbuf.shape, zbuf.dtype)
        for wait in (False, True):
            def body(e, c, wait=wait):
                _copy_pieces(tail_ref[1, e], zbuf, 0, xs_hbm, tail_ref[0, e], zsem, TAIL_BITS, wait)
                return c
            lax.fori_loop(0, n_exp, body, 0)
        blk_rows = zbuf.shape[0]

        def zero_block(b, c):
            cp = pltpu.make_async_copy(
                zbuf, xs_hbm.at[pl.ds(pl.multiple_of(b * blk_rows, blk_rows), blk_rows)], zsem)
            cp.start()
            cp.wait()
            return c
        lax.fori_loop(nused_ref[0], n_blocks, zero_block, 0)

    @pl.when(i >= 2)
    def _():
        wait_stage(slot)

    tiles = range(TILES_PER_STEP)
    hs = [_load_rows(h_ref.at[pl.ds(j * tm * ROW_SUB, tm * ROW_SUB)], tm).astype(BF16)
          for j in tiles]
    row_id = lax.broadcasted_iota(jnp.int32, (n_rows, tm), 0)
    onehots = []
    for j in tiles:
        hit = row_id == pos_ref[0:1, j * tm:(j + 1) * tm]
        for k in range(1, TOP_K):
            hit = hit | (row_id == pos_ref[k:k + 1, j * tm:(j + 1) * tm])
        onehots.append(jnp.where(hit, 1.0, 0.0).astype(BF16))
    rows = [jnp.dot(onehots[j], hs[j], preferred_element_type=F32) for j in tiles]
    for j in tiles:
        dh = rows[j].shape[1] // 2
        _store_rows(stage.at[slot, j], _pack_bf16_pair(rows[j][:, 0:dh], rows[j][:, dh:]))

    for j in tiles:
        def body(e, c, j=j):
            _copy_pieces(tab_ref[j, 0, e], stage.at[slot, j], tab_ref[j, 2, e], xs_hbm,
                         tab_ref[j, 1, e], sem.at[slot], RUN_BITS)
            return c
        lax.fori_loop(0, n_exp, body, 0)

    @pl.when(i == n - 1)
    def _():
        wait_stage(slot)
        wait_stage(1 - slot)


def _dispatch_call(tabs, tail, n_used, h2p, pos_t, *, n_xs_rows):
    n_tiles, _, n_exp = tabs.shape
    tm = TM_FIN
    tps = TILES_PER_STEP
    assert n_tiles % tps == 0
    nt = n_tiles // tps
    assert nt >= 2 and tm < (1 << RUN_BITS) and MOE_BLK <= (1 << TAIL_BITS)
    n_rows = TOP_K * tm
    kern = functools.partial(_dispatch_kernel, tm=tm, n_exp=n_exp, n_blocks=n_xs_rows // MOE_BLK)
    return pl.pallas_call(
        kern,
        grid=(nt,),
        in_specs=[
            pl.BlockSpec((tps, 3, n_exp), lambda i: (i, 0, 0), memory_space=pltpu.SMEM),
            pl.BlockSpec(memory_space=pltpu.SMEM),
            pl.BlockSpec(memory_space=pltpu.SMEM),
            pl.BlockSpec((tps * tm * ROW_SUB, 128), lambda i: (i, 0)),
            pl.BlockSpec((TOP_K, tps * tm), lambda i: (0, i)),
        ],
        out_specs=pl.BlockSpec(memory_space=pl.ANY),
        out_shape=jax.ShapeDtypeStruct((n_xs_rows * ROW_SUB, 128), jnp.uint32),
        scratch_shapes=[pltpu.VMEM((2, tps, n_rows * ROW_SUB, 128), jnp.uint32),
                        pltpu.VMEM((MOE_BLK * ROW_SUB, 128), jnp.uint32),
                        pltpu.SemaphoreType.DMA((2,)),
                        pltpu.SemaphoreType.DMA(())],
        compiler_params=pltpu.CompilerParams(dimension_semantics=("arbitrary",),
                                             vmem_limit_bytes=VMEM_LIMIT),
        name="moe_dispatch",
    )(tabs, tail, n_used, h2p, pos_t)


def _moe_kernel(blk_e_ref, nused_ref, x_ref, perm_ref, wgu_ref, bgu_ref, wd_ref, bd_ref, o_ref,
                wgu_scr, wd_scr, *, blk, f):
    s = pl.program_id(0)
    e_cur = blk_e_ref[s]
    e_prev = blk_e_ref[jnp.maximum(s - 1, 0)]

    @pl.when(jnp.logical_or(s == 0, e_cur != e_prev))
    def _():
        perm = perm_ref[...]
        for c in range(2 * f // 256):
            r = jnp.dot(wgu_ref[0, :, c * 256:(c + 1) * 256].astype(BF16), perm,
                        preferred_element_type=F32)
            wgu_scr[:, c * 128:(c + 1) * 128] = r[:, 0:128].astype(BF16)
            wgu_scr[:, f + c * 128:f + (c + 1) * 128] = r[:, 128:256].astype(BF16)
        wd_scr[...] = wd_ref[0].astype(BF16)

    @pl.when(s >= nused_ref[0])
    def _():
        o_ref[...] = jnp.zeros(o_ref.shape, o_ref.dtype)

    @pl.when(s < nused_ref[0])
    def _():
        x = _load_rows(x_ref, blk).astype(BF16)
        gu = jnp.dot(x, wgu_scr[...], preferred_element_type=F32) + bgu_ref[0]
        gate = jnp.minimum(gu[:, 0:f], SWIGLU_LIMIT)
        up = jnp.clip(gu[:, f:2 * f], -SWIGLU_LIMIT, SWIGLU_LIMIT)
        act = (up + 1.0) * (gate * jax.nn.sigmoid(SWIGLU_ALPHA * gate))
        y = jnp.dot(act.astype(BF16), wd_scr[...], preferred_element_type=F32) + bd_ref[0]
        dh = y.shape[1] // 2
        _store_rows(o_ref, _pack_bf16_pair(y[:, 0:dh], y[:, dh:]))


def _moe_call(blk_e, n_used, xs, perm, w_gu, b_gu_p, w_down, b_down):
    n_steps = blk_e.shape[0]
    blk = MOE_BLK
    _, d, f2 = w_gu.shape
    f = f2 // 2
    rows = blk * ROW_SUB
    kern = functools.partial(_moe_kernel, blk=blk, f=f)
    row_blk = lambda i, be, nu: (i, 0)
    grid_spec = pltpu.PrefetchScalarGridSpec(
        num_scalar_prefetch=2,
        grid=(n_steps,),
        in_specs=[
            pl.BlockSpec((rows, 128), row_blk),
            pl.BlockSpec((256, 256), lambda i, be, nu: (0, 0)),
            pl.BlockSpec((1, d, f2), lambda i, be, nu: (be[i], 0, 0)),
            pl.BlockSpec((1, 1, f2), lambda i, be, nu: (be[i], 0, 0)),
            pl.BlockSpec((1, f, d), lambda i, be, nu: (be[i], 0, 0)),
            pl.BlockSpec((1, 1, d), lambda i, be, nu: (be[i], 0, 0)),
        ],
        out_specs=pl.BlockSpec((rows, 128), row_blk),
        scratch_shapes=[
            pltpu.VMEM((d, f2), BF16),
            pltpu.VMEM((f, d), BF16),
        ],
    )
    return pl.pallas_call(
        kern,
        grid_spec=grid_spec,
        out_shape=jax.ShapeDtypeStruct((n_steps * rows, 128), jnp.uint32),
        compiler_params=pltpu.CompilerParams(dimension_semantics=("arbitrary",),
                                             vmem_limit_bytes=VMEM_LIMIT),
        name="moe_experts",
    )(blk_e, n_used, xs, perm, w_gu, b_gu_p, w_down, b_down)


def _final_kernel(tab_cur_ref, tab_nxt_ref, x1_ref, pos_ref, w_ref, mod_ref, fmod_ref, gfin_ref, ys_hbm,
                  o_ref, stage, sem, *, tm, n_exp):
    i = pl.program_id(0)
    n = pl.num_programs(0)
    slot = i % 2
    n_rows = TOP_K * tm

    tiles = range(TILES_PER_STEP)

    def issue_tiles(tab_ref, slot_):
        for j in tiles:
            def body(e, c, j=j):
                _copy_pieces(tab_ref[j, 0, e], ys_hbm, tab_ref[j, 1, e], stage.at[slot_, j],
                             tab_ref[j, 2, e], sem.at[slot_], RUN_BITS)
                return c
            lax.fori_loop(0, n_exp, body, 0)

    @pl.when(i == 0)
    def _():
        issue_tiles(tab_cur_ref, 0)

    @pl.when(i + 1 < n)
    def _():
        issue_tiles(tab_nxt_ref, 1 - slot)

    for j in tiles:
        pltpu.make_async_copy(ys_hbm.at[pl.ds(0, n_rows * ROW_SUB)], stage.at[slot, j],
                              sem.at[slot]).wait()

    ys = [_load_rows(stage.at[slot, j], n_rows).astype(BF16) for j in tiles]
    col_id = lax.broadcasted_iota(jnp.int32, (tm, n_rows), 1)
    wmats = []
    for j in tiles:
        pos = pos_ref[j * tm:(j + 1) * tm, :]
        w = w_ref[j * tm:(j + 1) * tm, :]
        a = jnp.where(col_id == pos[:, 0:1], w[:, 0:1], 0.0)
        for k in range(1, TOP_K):
            a = a + jnp.where(col_id == pos[:, k:k + 1], w[:, k:k + 1], 0.0)
        wmats.append(a.astype(BF16))
    moes = [jnp.dot(wmats[j], ys[j], preferred_element_type=F32) for j in tiles]
    gate2 = mod_ref[0, 5:6, :]
    fshift = fmod_ref[0, 0:1, :]
    fscale = fmod_ref[0, 1:2, :]
    for j in tiles:
        x2 = x1_ref[j * tm:(j + 1) * tm, :] + gate2 * moes[j]
        o_ref[j * tm:(j + 1) * tm, :] = _rms(x2, gfin_ref[...]) * (1.0 + fscale) + fshift


def _final_call(tabs, x1, pos_col, w_col, mod, fmod, g_final, ys, *, seq):
    t, d = x1.shape
    tm = TM_FIN
    assert tm < (1 << RUN_BITS)
    tls = TILES_PER_STEP
    tms = tls * tm
    assert seq % tms == 0
    tps = seq // tms
    nt = t // tms
    n_exp = tabs.shape[2]
    n_rows = TOP_K * tm
    kern = functools.partial(_final_kernel, tm=tm, n_exp=n_exp)
    tab_spec = lambda ahead: pl.BlockSpec((tls, 3, n_exp), lambda i: (jnp.minimum(i + ahead, nt - 1), 0, 0),
                                          memory_space=pltpu.SMEM)
    return pl.pallas_call(
        kern,
        grid=(nt,),
        in_specs=[
            tab_spec(0),
            tab_spec(1),
            pl.BlockSpec((tms, d), lambda i: (i, 0)),
            pl.BlockSpec((tms, TOP_K), lambda i: (i, 0)),
            pl.BlockSpec((tms, TOP_K), lambda i: (i, 0)),
            pl.BlockSpec((1, 6, d), lambda i: (i // tps, 0, 0)),
            pl.BlockSpec((1, 2, d), lambda i: (i // tps, 0, 0)),
            _const_spec((1, d)),
            pl.BlockSpec(memory_space=pl.ANY),
        ],
        out_specs=pl.BlockSpec((tms, d), lambda i: (i, 0)),
        out_shape=jax.ShapeDtypeStruct((t, d), F32),
        scratch_shapes=[pltpu.VMEM((2, tls, n_rows * ROW_SUB, 128), jnp.uint32),
                        pltpu.SemaphoreType.DMA((2,))],
        compiler_params=pltpu.CompilerParams(dimension_semantics=("arbitrary",),
                                             vmem_limit_bytes=VMEM_LIMIT),
        name="combine_final",
    )(tabs, tabs, x1, pos_col, w_col, mod, fmod, g_final, ys)


def _prep_w_in(w_in, d, q_lora, kv_lora):
    o_kpe = d + q_lora + kv_lora
    o_g = o_kpe + QK_ROPE
    half = QK_ROPE // 2
    kpe = w_in[:, o_kpe:o_kpe + QK_ROPE]
    zpad = jnp.zeros((d, 128 - QK_ROPE), w_in.dtype)
    ksw = jnp.concatenate([-kpe[:, half:], kpe[:, :half]], axis=1)
    return jnp.concatenate([w_in[:, :o_kpe], kpe, zpad, ksw, zpad, w_in[:, o_g:]], axis=1).astype(BF16)


def _prep_w_q(w_q_b):
    ql = w_q_b.shape[0]
    hd = QK_NOPE + QK_ROPE
    half = QK_ROPE // 2
    w = w_q_b.reshape(ql, N_HEADS, hd)
    nope = w[:, :, :QK_NOPE]
    pe = w[:, :, QK_NOPE:]
    sw = jnp.concatenate([-pe[:, :, half:], pe[:, :, :half]], axis=2)
    return jnp.concatenate([nope.reshape(ql, -1), pe.reshape(ql, -1), sw.reshape(ql, -1)],
                           axis=1).astype(BF16)


def _prep_w_kv(w_kv_b):
    kvl = w_kv_b.shape[0]
    w = w_kv_b.reshape(kvl, N_HEADS, QK_NOPE + V_HEAD)
    w_kn = w[:, :, :QK_NOPE].reshape(kvl, -1).astype(BF16)
    w_vt = w[:, :, QK_NOPE:].reshape(kvl, -1).T.astype(BF16)
    return w_kn, w_vt


def kernel(x, c, positions, w_mod, b_mod, g_mix, w_in, b_gate, w_pool_grp, pool_scale, w_pool_out,
           g_q_a, w_q_b, g_kv_a, w_kv_b, w_mla_out, w_out, g_ffn, w_router, b_router, w_gu, b_gu,
           w_down, b_down, g_final, w_fmod, b_fmod):
    bsz, seq, d = x.shape
    t = bsz * seq
    depth = w_mod.shape[0]
    assert depth == 1
    assert seq % TQ == 0 and seq % TM_IN == 0 and seq % TM_POST == 0 and seq % TM_FIN == 0
    q_lora = g_q_a.shape[-1]
    kv_lora = g_kv_a.shape[-1]
    n_exp = w_gu.shape[1]
    f = w_gu.shape[-1] // 2
    blk = MOE_BLK

    x2d = x.reshape(t, d)
    pos_col = positions.astype(F32).reshape(t, 1)
    inv_freq = 1.0 / (ROPE_THETA ** (jnp.arange(0, QK_ROPE, 2, dtype=F32) / QK_ROPE))
    invf2 = jnp.tile(inv_freq, 128 // (QK_ROPE // 2)).reshape(1, 128)

    mod, fmod = _mod_call(c, w_mod[0], b_mod[0], w_fmod, b_fmod)
    mod = mod.reshape(bsz, 6, d)
    fmod = fmod.reshape(bsz, 2, d)

    w_in_p = _prep_w_in(w_in[0], d, q_lora, kv_lora)
    w_q_p = _prep_w_q(w_q_b[0])
    w_kn, w_vt = _prep_w_kv(w_kv_b[0])
    ga, g1, q2, k2, vt3 = _mixer_in_call(
        x2d, pos_col, mod, g_mix[0].reshape(1, d), w_in_p, b_gate[0].reshape(1, 2 * d),
        w_pool_grp[0].astype(BF16), pool_scale[0].reshape(1, d), w_pool_out[0].astype(BF16),
        g_q_a[0].reshape(1, q_lora), w_q_p, g_kv_a[0].reshape(1, kv_lora), w_kn, w_vt, invf2,
        bsz=bsz, seq=seq)

    hp = N_HEADS * HEAD_PAD
    o = _attn_call(q2.reshape(bsz, seq, hp), k2.reshape(bsz, seq, hp), vt3)

    w_r = w_router[0]
    w_r_hi = w_r.astype(BF16)
    w_r_lo = (w_r - w_r_hi.astype(F32)).astype(BF16)
    lane_pad = lambda a: jnp.pad(a, ((0, 0), (0, 128 - n_exp)))
    w_rt = jnp.concatenate(
        [jnp.concatenate([lane_pad(w_r_hi), lane_pad(w_r_lo)], axis=1),
         jnp.concatenate([lane_pad(w_r_hi), jnp.zeros((d, 128), BF16)], axis=1)], axis=0)
    x1, h2p, pos_t, wgt_t, cnt3 = _post_call(
        o.reshape(t, d), ga, g1, x2d, mod, w_mla_out[0].astype(BF16), w_out[0].astype(BF16),
        g_ffn[0].reshape(1, d), w_rt, b_router[0].reshape(n_exp, 1), seq=seq)

    n_slots = t * TOP_K
    n_rows = n_slots + n_exp * blk
    n_blocks = n_rows // blk
    cnt = cnt3[:, 0:TM_POST // TM_FIN, 0:n_exp].reshape(-1, n_exp)
    counts = jnp.sum(cnt, axis=0)
    padded = (counts + blk - 1) // blk * blk
    pad_end = jnp.cumsum(padded)
    pad_start = pad_end - padded
    blk_start = jnp.arange(n_blocks, dtype=jnp.int32) * blk
    blk_e = jnp.minimum(jnp.sum(pad_end[None, :] <= blk_start[:, None], axis=1, dtype=jnp.int32),
                        n_exp - 1)
    n_used = (pad_end[-1:] // blk).astype(jnp.int32)
    tail = jnp.stack([(pad_start + counts) * ROW_SUB, padded - counts], axis=0).astype(jnp.int32)

    before = jnp.cumsum(cnt, axis=0) - cnt
    run_start = jnp.cumsum(cnt, axis=1) - cnt
    tabs = jnp.stack([cnt, (pad_start[None, :] + before) * ROW_SUB, run_start * ROW_SUB], axis=1)

    col = jnp.arange(256, dtype=jnp.int32)[None, :]
    row = jnp.arange(256, dtype=jnp.int32)[:, None]
    perm = (row == jnp.where(col < 128, 2 * col, 2 * (col - 128) + 1)).astype(BF16)
    b_gu_p = jnp.concatenate([b_gu[0][:, 0::2], b_gu[0][:, 1::2]], axis=-1).reshape(n_exp, 1, 2 * f)
    xs = _dispatch_call(tabs, tail, n_used, h2p, pos_t, n_xs_rows=n_rows)
    ys = _moe_call(blk_e, n_used, xs, perm, w_gu[0], b_gu_p, w_down[0], b_down[0].reshape(n_exp, 1, d))

    out = _final_call(tabs, x1, pos_t.T, wgt_t.T, mod, fmod, g_final.reshape(1, d), ys, seq=seq)
    return out.reshape(bsz, seq, d)
```
